```python
import jax, jax.numpy as jnp
from jax import lax
import numpy as np

D_MODEL = 2048
BATCH = 2
SEQ = 4096
DEPTH = 4

GRID_W = 64
CTX_LEN = 256
N_MIXERS = 3
HEAD_DIM = 128
A_Q_HEADS = 16
A_KV_HEADS = 4
A_GROUPS = A_Q_HEADS // A_KV_HEADS
A_Q_DIM = A_Q_HEADS * HEAD_DIM
A_KV_DIM = A_KV_HEADS * HEAD_DIM
Q_BLOCK = 128
ROPE_BASE = 10000.0
B_HEADS = D_MODEL // HEAD_DIM
WIN_R = 8
WIN_C = 16
POOL_WINDOWS = (2, 4, 8, 16)
POOL_GROUP = D_MODEL // len(POOL_WINDOWS)
PEER_HEADS = 8
PEER_QDIM = 256
PEER_NKEYS = 128
PEER_N = PEER_NKEYS * PEER_NKEYS
PEER_TOPK = 16
PEER_CHUNK = 128
N_MOD = 6
EPS = 1e-6
N_A = (DEPTH + 2) // 3
N_B = (DEPTH + 1) // 3
N_C = DEPTH // 3

kernel_name = "hybrid_dit_gqa_natten_pool_peer"


def rms_norm(x, w):
    x32 = x.astype(jnp.float32)
    y = x32 * lax.rsqrt(jnp.mean(x32 * x32, axis=-1, keepdims=True) + EPS)
    return (y * w.astype(jnp.float32)).astype(x.dtype)


def modulate(h, shift, scale):
    return h * (1.0 + scale) + shift


def softmax32(s, dtype):
    return jax.nn.softmax(s.astype(jnp.float32), axis=-1).astype(dtype)


def grid_angles(S):
    t = jnp.arange(S)
    row = (t // GRID_W).astype(jnp.float32)
    col = (t % GRID_W).astype(jnp.float32)
    half = HEAD_DIM // 2
    inv = ROPE_BASE ** (-jnp.arange(0, half, 2, dtype=jnp.float32) / half)
    return row[:, None] * inv[None, :], col[:, None] * inv[None, :]


def rope_rotate(x, ang):
    shape = (ang.shape[0],) + (1,) * (x.ndim - 3) + (ang.shape[1],)
    cos = jnp.cos(ang).reshape(shape)
    sin = jnp.sin(ang).reshape(shape)
    x32 = x.astype(jnp.float32)
    x1, x2 = jnp.split(x32, 2, axis=-1)
    return jnp.concatenate([x1 * cos - x2 * sin, x2 * cos + x1 * sin], axis=-1).astype(x.dtype)


def apply_rope_2d(x, ang_row, ang_col):
    half = HEAD_DIM // 2
    return jnp.concatenate([rope_rotate(x[..., :half], ang_row), rope_rotate(x[..., half:], ang_col)], axis=-1)


def gqa_axial_attention(h_ctx, h_lat, wqkv, q_gain, k_gain, wo, need_ctx_out):
    B, S, _ = h_lat.shape
    C = h_ctx.shape[1]
    scale = HEAD_DIM ** -0.5

    def split_q(q):
        return rms_norm(q.reshape(q.shape[:2] + (A_KV_HEADS, A_GROUPS, HEAD_DIM)), q_gain)

    def split_kv(kv):
        k = rms_norm(kv[..., :A_KV_DIM].reshape(kv.shape[:2] + (A_KV_HEADS, HEAD_DIM)), k_gain)
        v = kv[..., A_KV_DIM:].reshape(kv.shape[:2] + (A_KV_HEADS, HEAD_DIM))
        return k, v

    def attend(q, k, v):
        s = jnp.einsum('bqkgd,bskd->bkgqs', q, k)
        p = softmax32(s, v.dtype)
        return jnp.einsum('bkgqs,bskd->bqkgd', p, v)

    ang_r, ang_c = grid_angles(S)
    qkv = h_lat @ wqkv
    q_lat = apply_rope_2d(split_q(qkv[..., :A_Q_DIM]), ang_r, ang_c) * scale
    k_lat, v_lat = split_kv(qkv[..., A_Q_DIM:])
    k_lat = apply_rope_2d(k_lat, ang_r, ang_c)
    k_ctx, v_ctx = split_kv(h_ctx @ wqkv[:, A_Q_DIM:])
    k_all = jnp.concatenate([k_ctx, k_lat], axis=1)
    v_all = jnp.concatenate([v_ctx, v_lat], axis=1)

    nb = S // Q_BLOCK
    q_blocks = jnp.moveaxis(q_lat.reshape(B, nb, Q_BLOCK, A_KV_HEADS, A_GROUPS, HEAD_DIM), 1, 0)
    o = lax.map(lambda qb: attend(qb, k_all, v_all), q_blocks)
    o_lat = jnp.moveaxis(o, 0, 1).reshape(B, S, A_Q_DIM) @ wo
    o_ctx = None
    if need_ctx_out:
        q_ctx = split_q(h_ctx @ wqkv[:, :A_Q_DIM]) * scale
        o_ctx = attend(q_ctx, k_ctx, v_ctx).reshape(B, C, A_Q_DIM) @ wo
    return o_ctx, o_lat


def neighborhood_attention(h_ctx, h_lat, wqkv, rpb, wo, need_ctx_out):
    B, S, D = h_lat.shape
    C = h_ctx.shape[1]
    rows = S // GRID_W
    wr = min(WIN_R, rows)
    nwin = wr * WIN_C
    scale = HEAD_DIM ** -0.5

    def heads(t):
        return t.reshape(t.shape[:2] + (B_HEADS, HEAD_DIM))

    qkv = h_lat @ wqkv
    q_lat = heads(qkv[..., :D]) * scale
    k_grid = heads(qkv[..., D:2 * D]).reshape(B, rows, GRID_W, B_HEADS, HEAD_DIM)
    v_grid = heads(qkv[..., 2 * D:]).reshape(B, rows, GRID_W, B_HEADS, HEAD_DIM)
    kv_ctx = h_ctx @ wqkv[:, D:]
    k_ctx = heads(kv_ctx[..., :D])
    v_ctx = heads(kv_ctx[..., D:])

    cols = np.arange(GRID_W)
    col_start = np.clip(cols - WIN_C // 2, 0, GRID_W - WIN_C)
    key_cols = col_start[:, None] + np.arange(WIN_C)[None, :]
    col_rel = key_cols - cols[:, None] + (WIN_C - 1)

    def row_block(args):
        r, q_row = args
        r0 = jnp.clip(r - wr // 2, 0, rows - wr)
        kw = lax.dynamic_slice_in_dim(k_grid, r0, wr, axis=1)[:, :, key_cols]
        vw = lax.dynamic_slice_in_dim(v_grid, r0, wr, axis=1)[:, :, key_cols]
        rel_r = r0 + jnp.arange(wr) - r + (WIN_R - 1)
        bias = rpb[:, rel_r][:, :, col_rel]
        bias = jnp.transpose(bias, (0, 2, 1, 3)).reshape(B_HEADS, GRID_W, nwin)
        s_win = jnp.einsum('bqhd,brqjhd->bhqrj', q_row, kw).reshape(B, B_HEADS, GRID_W, nwin) + bias
        s_ctx = jnp.einsum('bqhd,bchd->bhqc', q_row, k_ctx)
        p = softmax32(jnp.concatenate([s_win, s_ctx], axis=-1), v_ctx.dtype)
        p_win = p[..., :nwin].reshape(B, B_HEADS, GRID_W, wr, WIN_C)
        return (jnp.einsum('bhqrj,brqjhd->bqhd', p_win, vw)
                + jnp.einsum('bhqc,bchd->bqhd', p[..., nwin:], v_ctx))

    q_rows = jnp.moveaxis(q_lat.reshape(B, rows, GRID_W, B_HEADS, HEAD_DIM), 1, 0)
    o = lax.map(row_block, (jnp.arange(rows), q_rows))
    o_lat = jnp.moveaxis(o, 0, 1).reshape(B, S, D) @ wo
    o_ctx = None
    if need_ctx_out:
        q_ctx = heads(h_ctx @ wqkv[:, :D]) * scale
        p = softmax32(jnp.einsum('bqhd,bkhd->bhqk', q_ctx, k_ctx), v_ctx.dtype)
        o_ctx = jnp.einsum('bhqk,bkhd->bqhd', p, v_ctx).reshape(B, C, D) @ wo
    return o_ctx, o_lat


def multiscale_pool_mixer(h_ctx, h_lat, w_grp, ls, need_ctx_out):
    def mix(h):
        B, L, D = h.shape
        h32 = h.astype(jnp.float32)
        cs = jnp.concatenate([jnp.zeros((B, 1, D), jnp.float32), jnp.cumsum(h32, axis=1)], axis=1)
        t = np.arange(L)
        outs = []
        for g, w in enumerate(POOL_WINDOWS):
            lo = np.maximum(t - w // 2, 0)
            hi = np.minimum(t + w - w // 2, L)
            inv_cnt = (1.0 / (hi - lo)).astype(np.float32)[None, :, None]
            sl = slice(g * POOL_GROUP, (g + 1) * POOL_GROUP)
            outs.append((cs[:, hi, sl] - cs[:, lo, sl]) * inv_cnt - h32[:, :, sl])
        y = jnp.stack(outs, axis=2).astype(h.dtype)
        return jnp.einsum('blgc,gce->blge', y, w_grp).reshape(B, L, D) * ls

    o_ctx = mix(h_ctx) if need_ctx_out else None
    return o_ctx, mix(h_lat)


def peer_ffn(h, wq, sub_keys, u, v):
    T, D = h.shape
    hq = PEER_QDIM // 2

    def chunk(xc):
        q = (xc @ wq).reshape(PEER_CHUNK, PEER_HEADS, 2, hq)
        s1 = jnp.einsum('thd,kd->thk', q[:, :, 0], sub_keys[0]).astype(jnp.float32)
        s2 = jnp.einsum('thd,kd->thk', q[:, :, 1], sub_keys[1]).astype(jnp.float32)
        v1, i1 = lax.top_k(s1, PEER_TOPK)
        v2, i2 = lax.top_k(s2, PEER_TOPK)
        cand = (v1[..., :, None] + v2[..., None, :]).reshape(PEER_CHUNK, PEER_HEADS, PEER_TOPK * PEER_TOPK)
        sc, ci = lax.top_k(cand, PEER_TOPK)
        e1 = jnp.take_along_axis(i1, ci // PEER_TOPK, axis=-1)
        e2 = jnp.take_along_axis(i2, ci % PEER_TOPK, axis=-1)
        idx = e1 * PEER_NKEYS + e2
        g = jax.nn.softmax(sc, axis=-1).astype(xc.dtype)
        act = jax.nn.gelu(jnp.einsum('td,thkd->thk', xc, u[idx]))
        return jnp.einsum('thk,thkd->td', g * act, v[idx])

    out = lax.map(chunk, h.reshape(T // PEER_CHUNK, PEER_CHUNK, D))
    return out.reshape(T, D)


def setup_inputs(seed: int = 0) -> dict:
    key = jax.random.key(seed)
    ks = jax.random.split(key, 21)
    f32 = jnp.float32
    D = D_MODEL

    def nrm(k, shape, std):
        return jax.random.normal(k, shape, f32) * std

    return {
        "x": nrm(ks[0], (BATCH, SEQ, D), 1.0),
        "c": nrm(ks[1], (BATCH, D), 1.0),
        "ctx": nrm(ks[2], (BATCH, CTX_LEN, D), 1.0),
        "c_ctx": nrm(ks[3], (D,), 1.0),
        "mod_w": nrm(ks[4], (DEPTH, D, N_MOD * D), 0.5 * D ** -0.5),
        "mod_b": nrm(ks[5], (DEPTH, N_MOD * D), 0.02),
        "norm_w": 1.0 + nrm(ks[6], (DEPTH, 2, D), 0.05),
        "final_norm_w": 1.0 + nrm(ks[7], (D,), 0.05),
        "a_wqkv": nrm(ks[8], (N_A, D, A_Q_DIM + 2 * A_KV_DIM), D ** -0.5),
        "a_q_gain": 1.0 + nrm(ks[9], (N_A, HEAD_DIM), 0.05),
        "a_k_gain": 1.0 + nrm(ks[10], (N_A, HEAD_DIM), 0.05),
        "a_wo": nrm(ks[11], (N_A, A_Q_DIM, D), A_Q_DIM ** -0.5),
        "b_wqkv": nrm(ks[12], (N_B, D, 3 * D), D ** -0.5),
        "b_rpb": nrm(ks[13], (N_B, B_HEADS, 2 * WIN_R - 1, 2 * WIN_C - 1), 0.1),
        "b_wo": nrm(ks[14], (N_B, D, D), D ** -0.5),
        "pool_w": nrm(ks[15], (N_C, len(POOL_WINDOWS), POOL_GROUP, POOL_GROUP), POOL_GROUP ** -0.5),
        "pool_scale": 0.5 + nrm(ks[16], (N_C, D), 0.05),
        "peer_wq": nrm(ks[17], (DEPTH, D, PEER_HEADS * PEER_QDIM), D ** -0.5),
        "peer_sub_keys": nrm(ks[18], (DEPTH, 2, PEER_NKEYS, PEER_QDIM // 2), (PEER_QDIM // 2) ** -0.5),
        "peer_u": nrm(ks[19], (DEPTH, PEER_N, D), D ** -0.5),
        "peer_v": nrm(ks[20], (DEPTH, PEER_N, D), 0.3),
    }


def reference(x, c, ctx, c_ctx, mod_w, mod_b, norm_w, final_norm_w, a_wqkv, a_q_gain, a_k_gain, a_wo,
              b_wqkv, b_rpb, b_wo, pool_w, pool_scale, peer_wq, peer_sub_keys, peer_u, peer_v):
    B, S, D = x.shape
    C = ctx.shape[1]
    xc = ctx
    sc_lat = jax.nn.silu(c)
    sc_ctx = jax.nn.silu(c_ctx)
    for i in range(DEPTH):
        last = i == DEPTH - 1
        kind, j = i % N_MIXERS, i // N_MIXERS
        m_lat = [m[:, None, :] for m in jnp.split(sc_lat @ mod_w[i] + mod_b[i], N_MOD, axis=-1)]
        h_lat = modulate(rms_norm(x, norm_w[i, 0]), m_lat[0], m_lat[1])
        h_ctx = None
        if not (last and kind == 2):
            m_ctx = jnp.split(sc_ctx @ mod_w[i] + mod_b[i], N_MOD, axis=-1)
            h_ctx = modulate(rms_norm(xc, norm_w[i, 0]), m_ctx[0], m_ctx[1])
        if kind == 0:
            o_ctx, o_lat = gqa_axial_attention(h_ctx, h_lat, a_wqkv[j], a_q_gain[j], a_k_gain[j], a_wo[j], not last)
        elif kind == 1:
            o_ctx, o_lat = neighborhood_attention(h_ctx, h_lat, b_wqkv[j], b_rpb[j], b_wo[j], not last)
        else:
            o_ctx, o_lat = multiscale_pool_mixer(h_ctx, h_lat, pool_w[j], pool_scale[j], not last)
        x = x + m_lat[2] * o_lat
        f_lat = modulate(rms_norm(x, norm_w[i, 1]), m_lat[3], m_lat[4]).reshape(B * S, D)
        if last:
            f = peer_ffn(f_lat, peer_wq[i], peer_sub_keys[i], peer_u[i], peer_v[i])
            x = x + m_lat[5] * f.reshape(B, S, D)
        else:
            xc = xc + m_ctx[2] * o_ctx
            f_ctx = modulate(rms_norm(xc, norm_w[i, 1]), m_ctx[3], m_ctx[4]).reshape(B * C, D)
            f = peer_ffn(jnp.concatenate([f_ctx, f_lat], axis=0), peer_wq[i], peer_sub_keys[i], peer_u[i], peer_v[i])
            xc = xc + m_ctx[5] * f[:B * C].reshape(B, C, D)
            x = x + m_lat[5] * f[B * C:].reshape(B, S, D)
    return rms_norm(x, final_norm_w)
```

```python
import functools

import numpy as np
import jax
import jax.numpy as jnp
from jax import lax
from jax.experimental import pallas as pl
from jax.experimental.pallas import tpu as pltpu

GRID_W = 64
ROPE_BASE = 10000.0
POOL_WINDOWS = (2, 4, 8, 16)
PEER_TOPK = 16
EPS = 1e-6
N_MIXERS = 3
N_MOD = 6

LANES = 128
SUBLANES = 8
ROW_BLOCK = 512
POOL_ROW_BLOCK = 256
POOL_HALO = 8
VMEM_LIMIT = 56 * 1024 * 1024
NEG = -1e30

f32 = jnp.float32
bf16 = jnp.bfloat16
_NT = (((1,), (1,)), ((), ()))


def _cparams(n_axes):
    return pltpu.CompilerParams(dimension_semantics=("arbitrary",) * n_axes, vmem_limit_bytes=VMEM_LIMIT)


def _group_of(i, blocks_per_seq, n_batch):
    return jnp.minimum(i // blocks_per_seq, n_batch)


def _mods_kernel(cv_ref, w_ref, b_ref, o_ref):
    cv = cv_ref[...]
    a = (cv / (1.0 + jnp.exp(-cv))).astype(bf16)
    o_ref[...] = jnp.dot(a, w_ref[...].astype(bf16), preferred_element_type=f32) + b_ref[...]


def _mods(cvec, mod_w, mod_b):
    depth, d, n = mod_w.shape
    tn = 1024
    return pl.pallas_call(
        _mods_kernel,
        grid=(depth, n // tn),
        in_specs=[pl.BlockSpec((SUBLANES, d), lambda l, j: (0, 0)),
                  pl.BlockSpec((None, d, tn), lambda l, j: (l, 0, j)),
                  pl.BlockSpec((None, 1, tn), lambda l, j: (l, 0, j))],
        out_specs=pl.BlockSpec((None, SUBLANES, tn), lambda l, j: (l, 0, j)),
        out_shape=jax.ShapeDtypeStruct((depth, SUBLANES, n), f32),
        compiler_params=_cparams(2),
        name="adaln_mods",
    )(cvec, mod_w, mod_b.reshape(depth, 1, n))


def _norm_mod_kernel(x_ref, nw_ref, m_ref, o_ref, *, shift_row, scale_row):
    x = x_ref[...]
    y = x * lax.rsqrt(jnp.mean(x * x, axis=-1, keepdims=True) + EPS) * nw_ref[...]
    y = y * (1.0 + m_ref[scale_row:scale_row + 1, :]) + m_ref[shift_row:shift_row + 1, :]
    o_ref[...] = y.astype(o_ref.dtype)


def _norm_mod(x, nw, mods, shift_row, scale_row, n_blocks, bps, n_batch, out_dtype):
    t, d = x.shape
    return pl.pallas_call(
        functools.partial(_norm_mod_kernel, shift_row=shift_row, scale_row=scale_row),
        grid=(n_blocks,),
        in_specs=[pl.BlockSpec((ROW_BLOCK, d), lambda i: (i, 0)),
                  pl.BlockSpec((1, d), lambda i: (0, 0)),
                  pl.BlockSpec((None, SUBLANES, d), lambda i: (_group_of(i, bps, n_batch), 0, 0))],
        out_specs=pl.BlockSpec((ROW_BLOCK, d), lambda i: (i, 0)),
        out_shape=jax.ShapeDtypeStruct((t, d), out_dtype),
        compiler_params=_cparams(1),
        name="norm_mod",
    )(x, nw.reshape(1, d), mods)


def _final_norm_kernel(x_ref, nw_ref, o_ref):
    x = x_ref[...]
    o_ref[...] = x * lax.rsqrt(jnp.mean(x * x, axis=-1, keepdims=True) + EPS) * nw_ref[...]


def _final_norm(x, nw, n_blocks):
    t, d = x.shape
    return pl.pallas_call(
        _final_norm_kernel,
        grid=(n_blocks,),
        in_specs=[pl.BlockSpec((ROW_BLOCK, d), lambda i: (i, 0)),
                  pl.BlockSpec((1, d), lambda i: (0, 0))],
        out_specs=pl.BlockSpec((ROW_BLOCK, d), lambda i: (i, 0)),
        out_shape=jax.ShapeDtypeStruct((n_blocks * ROW_BLOCK, d), f32),
        compiler_params=_cparams(1),
        name="final_norm",
    )(x, nw.reshape(1, d))


def _mm_kernel(a_ref, w_ref, o_ref):
    o_ref[...] = jnp.dot(a_ref[...], w_ref[...], preferred_element_type=f32).astype(o_ref.dtype)


def _mm(a, w, n_blocks, tn, out_dtype):
    t, k = a.shape
    n = w.shape[1]
    return pl.pallas_call(
        _mm_kernel,
        grid=(n // tn, n_blocks),
        in_specs=[pl.BlockSpec((ROW_BLOCK, k), lambda j, i: (i, 0)),
                  pl.BlockSpec((k, tn), lambda j, i: (0, j))],
        out_specs=pl.BlockSpec((ROW_BLOCK, tn), lambda j, i: (i, j)),
        out_shape=jax.ShapeDtypeStruct((t, n), out_dtype),
        compiler_params=_cparams(2),
        name="matmul",
    )(a, w)


def _mm_res_kernel(a_ref, w_ref, x_ref, m_ref, o_ref, *, gate_row):
    y = jnp.dot(a_ref[...], w_ref[...], preferred_element_type=f32)
    o_ref[...] = x_ref[...] + m_ref[gate_row:gate_row + 1, :] * y


def _mm_res(a, w, x, mods, gate_row, n_blocks, bps, n_batch, tn):
    t, k = a.shape
    n = w.shape[1]
    return pl.pallas_call(
        functools.partial(_mm_res_kernel, gate_row=gate_row),
        grid=(n // tn, n_blocks),
        in_specs=[pl.BlockSpec((ROW_BLOCK, k), lambda j, i: (i, 0)),
                  pl.BlockSpec((k, tn), lambda j, i: (0, j)),
                  pl.BlockSpec((ROW_BLOCK, tn), lambda j, i: (i, j)),
                  pl.BlockSpec((None, SUBLANES, tn), lambda j, i: (_group_of(i, bps, n_batch), 0, j))],
        out_specs=pl.BlockSpec((ROW_BLOCK, tn), lambda j, i: (i, j)),
        out_shape=jax.ShapeDtypeStruct((t, n), f32),
        compiler_params=_cparams(2),
        name="matmul_residual",
    )(a, w, x, mods)


def _qkv_gqa_kernel(a_ref, w_ref, qg_ref, kg_ref, cos_ref, sa_ref, sb_ref, o_ref, *, nq_tiles, nk_tiles, scale):
    j = pl.program_id(0)
    acc = jnp.dot(a_ref[...], w_ref[...], preferred_element_type=f32)
    heads = acc.shape[1] // LANES

    def prep(gain, post):
        cos, sa, sb = cos_ref[...], sa_ref[...], sb_ref[...]
        for h in range(heads):
            y = acc[:, h * LANES:(h + 1) * LANES]
            y = y * lax.rsqrt(jnp.mean(y * y, axis=-1, keepdims=True) + EPS) * gain
            y = y * cos + pltpu.roll(y, LANES - 32, 1) * sa + pltpu.roll(y, 32, 1) * sb
            o_ref[:, h * LANES:(h + 1) * LANES] = (y * post).astype(o_ref.dtype)

    @pl.when(j < nq_tiles)
    def _():
        prep(qg_ref[...], scale)

    @pl.when(jnp.logical_and(j >= nq_tiles, j < nq_tiles + nk_tiles))
    def _():
        prep(kg_ref[...], 1.0)

    @pl.when(j >= nq_tiles + nk_tiles)
    def _():
        o_ref[...] = acc.astype(o_ref.dtype)


def _rope_tables(s):
    t = np.arange(s)
    half = LANES // 2
    inv = ROPE_BASE ** (-jnp.arange(0, half, 2, dtype=f32) / half)
    ang_r = (t // GRID_W).astype(np.float32)[:, None] * inv[None, :]
    ang_c = (t % GRID_W).astype(np.float32)[:, None] * inv[None, :]
    ang = jnp.concatenate([ang_r, ang_r, ang_c, ang_c], axis=-1)
    cos, sin = jnp.cos(ang), jnp.sin(ang)
    first = (np.arange(LANES) % half) < (half // 2)
    sa = jnp.where(first[None, :], -sin, 0.0)
    sb = jnp.where(first[None, :], 0.0, sin)
    pad1 = jnp.ones((ROW_BLOCK, LANES), f32)
    pad0 = jnp.zeros((ROW_BLOCK, LANES), f32)
    return (jnp.concatenate([cos, pad1], 0), jnp.concatenate([sa, pad0], 0), jnp.concatenate([sb, pad0], 0))


def _qkv_gqa(h, wqkv, q_gain, k_gain, tables, q_dim, kv_dim, n_blocks, bps, n_batch):
    t, k = h.shape
    n = wqkv.shape[1]
    tn = min(512, kv_dim)
    cos, sa, sb = tables

    def pos_map(j, i):
        return (jnp.where(i < bps * n_batch, i % bps, bps), 0)

    return pl.pallas_call(
        functools.partial(_qkv_gqa_kernel, nq_tiles=q_dim // tn, nk_tiles=kv_dim // tn, scale=float(LANES) ** -0.5),
        grid=(n // tn, n_blocks),
        in_specs=[pl.BlockSpec((ROW_BLOCK, k), lambda j, i: (i, 0)),
                  pl.BlockSpec((k, tn), lambda j, i: (0, j)),
                  pl.BlockSpec((1, LANES), lambda j, i: (0, 0)),
                  pl.BlockSpec((1, LANES), lambda j, i: (0, 0)),
                  pl.BlockSpec((ROW_BLOCK, LANES), pos_map),
                  pl.BlockSpec((ROW_BLOCK, LANES), pos_map),
                  pl.BlockSpec((ROW_BLOCK, LANES), pos_map)],
        out_specs=pl.BlockSpec((ROW_BLOCK, tn), lambda j, i: (i, j)),
        out_shape=jax.ShapeDtypeStruct((t, n), bf16),
        compiler_params=_cparams(2),
        name="qkv_gqa",
    )(h, wqkv, q_gain.reshape(1, LANES), k_gain.reshape(1, LANES), cos, sa, sb)


def _flash_kernel(*refs, groups, n_lat_chunks, ck, scale):
    if n_lat_chunks:
        q_ref, kc_ref, vc_ref, kl_ref, vl_ref, o_ref = refs
    else:
        q_ref, kc_ref, vc_ref, o_ref = refs
    tq = q_ref.shape[0]
    q = jnp.concatenate([q_ref[:, g * LANES:(g + 1) * LANES] for g in range(groups)], axis=0)
    rows = groups * tq

    def step(k, v, m, l, acc):
        s = lax.dot_general(q, k, _NT, preferred_element_type=f32)
        if scale != 1.0:
            s = s * scale
        m_new = jnp.maximum(m, jnp.max(s, axis=-1, keepdims=True))
        p = jnp.exp(s - m_new)
        alpha = jnp.exp(m - m_new)
        l = alpha * l + jnp.sum(p, axis=-1, keepdims=True)
        acc = alpha * acc + jnp.dot(p.astype(bf16), v, preferred_element_type=f32)
        return m_new, l, acc

    carry = step(kc_ref[...], vc_ref[...], jnp.full((rows, 1), NEG, f32), jnp.zeros((rows, 1), f32),
                 jnp.zeros((rows, LANES), f32))
    if n_lat_chunks:
        def body(c, carry):
            off = pl.multiple_of(c * ck, ck)
            return step(kl_ref[pl.ds(off, ck), :], vl_ref[pl.ds(off, ck), :], *carry)
        carry = lax.fori_loop(0, n_lat_chunks, body, carry)
    _, l, acc = carry
    o = acc / l
    for g in range(groups):
        o_ref[:, g * LANES:(g + 1) * LANES] = o[g * tq:(g + 1) * tq].astype(o_ref.dtype)


def _attn_lat(q, k, v, q_col0, k_col0, v_col0, groups, n_kv, n_batch, s, c, scale, out_cols):
    t = q.shape[0]
    tq = 128
    ck = min(512, s)
    nqb = s // tq
    ctx_blk0 = n_batch * s // c
    qw = groups * LANES
    return pl.pallas_call(
        functools.partial(_flash_kernel, groups=groups, n_lat_chunks=s // ck, ck=ck, scale=scale),
        grid=(n_batch, n_kv, nqb),
        in_specs=[pl.BlockSpec((tq, qw), lambda b, h, i: (b * nqb + i, q_col0 // qw + h)),
                  pl.BlockSpec((c, LANES), lambda b, h, i: (ctx_blk0 + b, k_col0 // LANES + h)),
                  pl.BlockSpec((c, LANES), lambda b, h, i: (ctx_blk0 + b, v_col0 // LANES + h)),
                  pl.BlockSpec((s, LANES), lambda b, h, i: (b, k_col0 // LANES + h)),
                  pl.BlockSpec((s, LANES), lambda b, h, i: (b, v_col0 // LANES + h))],
        out_specs=pl.BlockSpec((tq, qw), lambda b, h, i: (b * nqb + i, h)),
        out_shape=jax.ShapeDtypeStruct((t, out_cols), bf16),
        compiler_params=_cparams(3),
        name="attn_latent",
    )(q, k, v, k, v)


def _attn_ctx_kernel(q_ref, kc_ref, vc_ref, o_in_ref, o_ref, **kw):
    del o_in_ref
    _flash_kernel(q_ref, kc_ref, vc_ref, o_ref, **kw)


def _attn_ctx(q, k, v, o, q_col0, k_col0, v_col0, groups, n_kv, n_batch, s, c, scale):
    ctx_blk0 = n_batch * s // c
    qw = groups * LANES
    return pl.pallas_call(
        functools.partial(_attn_ctx_kernel, groups=groups, n_lat_chunks=0, ck=0, scale=scale),
        grid=(n_batch, n_kv),
        in_specs=[pl.BlockSpec((c, qw), lambda b, h: (ctx_blk0 + b, q_col0 // qw + h)),
                  pl.BlockSpec((c, LANES), lambda b, h: (ctx_blk0 + b, k_col0 // LANES + h)),
                  pl.BlockSpec((c, LANES), lambda b, h: (ctx_blk0 + b, v_col0 // LANES + h)),
                  pl.BlockSpec(memory_space=pl.ANY)],
        out_specs=pl.BlockSpec((c, qw), lambda b, h: (ctx_blk0 + b, h)),
        out_shape=jax.ShapeDtypeStruct(o.shape, o.dtype),
        input_output_aliases={3: 0},
        compiler_params=_cparams(2),
        name="attn_context",
    )(q, k, v, o)


def _natten_kernel(q_ref, k_ref, v_ref, kc_ref, vc_ref, bias_ref, o_ref, *, rows, w, wr, scale):
    kc = kc_ref[...]
    vc = vc_ref[...]

    def body(r, carry):
        r0 = jnp.clip(r - wr // 2, 0, rows - wr)
        d = r - r0
        q = q_ref[pl.ds(pl.multiple_of(r * w, w), w), :]
        off = pl.multiple_of(r0 * w, w)
        kb = k_ref[pl.ds(off, wr * w), :]
        vb = v_ref[pl.ds(off, wr * w), :]
        sw = lax.dot_general(q, kb, _NT, preferred_element_type=f32) * scale + bias_ref[d]
        sc = lax.dot_general(q, kc, _NT, preferred_element_type=f32) * scale
        m = jnp.maximum(jnp.max(sw, axis=-1, keepdims=True), jnp.max(sc, axis=-1, keepdims=True))
        pw = jnp.exp(sw - m)
        pc = jnp.exp(sc - m)
        l = jnp.sum(pw, axis=-1, keepdims=True) + jnp.sum(pc, axis=-1, keepdims=True)
        o = (jnp.dot(pw.astype(bf16), vb, preferred_element_type=f32)
             + jnp.dot(pc.astype(bf16), vc, preferred_element_type=f32)) / l
        o_ref[pl.ds(pl.multiple_of(r * w, w), w), :] = o.astype(o_ref.dtype)
        return carry

    lax.fori_loop(0, rows, body, 0)


def _natten_bias(rpb, rows, wr, win_r, win_c):
    cols = np.arange(GRID_W)
    col_start = np.clip(cols - win_c // 2, 0, GRID_W - win_c)
    kc = np.arange(GRID_W)
    inside = (kc[None, :] >= col_start[:, None]) & (kc[None, :] < col_start[:, None] + win_c)
    rel_c = np.clip(kc[None, :] - cols[:, None] + (win_c - 1), 0, 2 * win_c - 2)
    d = np.arange(wr)
    j = np.arange(wr)
    rel_r = np.clip(j[None, :] - d[:, None] + (win_r - 1), 0, 2 * win_r - 2)
    tab = rpb[:, rel_r[:, None, :, None], rel_c[None, :, None, :]]
    tab = jnp.where(inside[None, None, :, None, :], tab, NEG)
    return tab.reshape(rpb.shape[0], wr, GRID_W, wr * GRID_W)


def _natten_lat(qkv, bias, n_heads, n_batch, s, c, d_model):
    t = qkv.shape[0]
    rows = s // GRID_W
    wr = bias.shape[1]
    ctx_blk0 = n_batch * s // c
    hq, hk, hv = 0, d_model // LANES, 2 * d_model // LANES
    return pl.pallas_call(
        functools.partial(_natten_kernel, rows=rows, w=GRID_W, wr=wr, scale=float(LANES) ** -0.5),
        grid=(n_batch, n_heads),
        in_specs=[pl.BlockSpec((s, LANES), lambda b, h: (b, hq + h)),
                  pl.BlockSpec((s, LANES), lambda b, h: (b, hk + h)),
                  pl.BlockSpec((s, LANES), lambda b, h: (b, hv + h)),
                  pl.BlockSpec((c, LANES), lambda b, h: (ctx_blk0 + b, hk + h)),
                  pl.BlockSpec((c, LANES), lambda b, h: (ctx_blk0 + b, hv + h)),
                  pl.BlockSpec((None, wr, GRID_W, wr * GRID_W), lambda b, h: (h, 0, 0, 0))],
        out_specs=pl.BlockSpec((s, LANES), lambda b, h: (b, h)),
        out_shape=jax.ShapeDtypeStruct((t, d_model), bf16),
        compiler_params=_cparams(2),
        name="natten_latent",
    )(qkv, qkv, qkv, qkv, qkv, bias)


def _pool_kernel(h_ref, hp_ref, hn_ref, x_ref, w_ref, ls_ref, m_ref, o_ref, pad_ref, *,
                 bps_lat, bps_ctx, n_lat_blocks, s, c, gate_row):
    i = pl.program_id(0)
    tm = h_ref.shape[0]
    is_lat = i < n_lat_blocks
    blk = jnp.where(is_lat, i % bps_lat, (i - n_lat_blocks) % bps_ctx)
    nblk = jnp.where(is_lat, bps_lat, bps_ctx)
    length = jnp.where(is_lat, s, c)
    pad_ref[0:POOL_HALO, :] = jnp.where(blk == 0, 0.0, hp_ref[...])
    pad_ref[POOL_HALO:POOL_HALO + tm, :] = h_ref[...]
    pad_ref[POOL_HALO + tm:2 * POOL_HALO + tm, :] = jnp.where(blk == nblk - 1, 0.0, hn_ref[...])
    pos = blk * tm + lax.broadcasted_iota(jnp.int32, (tm, 1), 0)
    pg = w_ref.shape[1]
    for g, win in enumerate(POOL_WINDOWS):
        cs = slice(g * pg, (g + 1) * pg)
        lo_off, hi_off = win // 2, win - win // 2
        acc = pad_ref[POOL_HALO - lo_off:POOL_HALO - lo_off + tm, cs]
        for k in range(-lo_off + 1, hi_off):
            acc = acc + pad_ref[POOL_HALO + k:POOL_HALO + k + tm, cs]
        cnt = jnp.minimum(pos + hi_off, length) - jnp.maximum(pos - lo_off, 0)
        y = acc * (1.0 / cnt.astype(f32)) - h_ref[:, cs]
        z = jnp.dot(y.astype(bf16), w_ref[g], preferred_element_type=f32) * ls_ref[:, cs]
        o_ref[:, cs] = x_ref[:, cs] + m_ref[gate_row:gate_row + 1, cs] * z


def _pool(h, x, pool_w, pool_scale, mods, gate_row, n_batch, s, c):
    t, d = h.shape
    tm = POOL_ROW_BLOCK
    nb = t // tm
    bps_lat, bps_ctx = s // tm, c // tm
    hb = tm // POOL_HALO
    last_halo = t // POOL_HALO - 1
    return pl.pallas_call(
        functools.partial(_pool_kernel, bps_lat=bps_lat, bps_ctx=bps_ctx, n_lat_blocks=n_batch * bps_lat,
                          s=s, c=c, gate_row=gate_row),
        grid=(nb,),
        in_specs=[pl.BlockSpec((tm, d), lambda i: (i, 0)),
                  pl.BlockSpec((POOL_HALO, d), lambda i: (jnp.maximum(i * hb - 1, 0), 0)),
                  pl.BlockSpec((POOL_HALO, d), lambda i: (jnp.minimum((i + 1) * hb, last_halo), 0)),
                  pl.BlockSpec((tm, d), lambda i: (i, 0)),
                  pl.BlockSpec(pool_w.shape, lambda i: (0, 0, 0)),
                  pl.BlockSpec((1, d), lambda i: (0, 0)),
                  pl.BlockSpec((None, SUBLANES, d), lambda i: (_group_of(i, bps_lat, n_batch), 0, 0))],
        out_specs=pl.BlockSpec((tm, d), lambda i: (i, 0)),
        out_shape=jax.ShapeDtypeStruct((t, d), f32),
        scratch_shapes=[pltpu.VMEM((tm + 2 * POOL_HALO, d), f32)],
        compiler_params=_cparams(1),
        name="pool_mixer",
    )(h, h, h, x, pool_w, pool_scale.reshape(1, d), mods)


def _topk_rows(s_ref, n, tb):
    iota = lax.broadcasted_iota(jnp.int32, (n, tb), 0)
    rows = lax.broadcasted_iota(jnp.int32, (PEER_TOPK, tb), 0)

    def body(k, carry):
        vals, idxs = carry
        s = s_ref[...]
        m = jnp.max(s, axis=0, keepdims=True)
        am = jnp.min(jnp.where(s == m, iota, n), axis=0, keepdims=True)
        s_ref[...] = jnp.where(iota == am, NEG, s)
        return jnp.where(rows == k, m, vals), jnp.where(rows == k, am, idxs)

    return lax.fori_loop(0, PEER_TOPK, body, (jnp.zeros((PEER_TOPK, tb), f32), jnp.zeros((PEER_TOPK, tb), jnp.int32)))


def _peer_topk_kernel(q_ref, sk_ref, e_ref, g_ref, s_ref, cand_ref, es_ref, gs_ref, *, n_heads, nkeys):
    tb = q_ref.shape[0]

    def head(h, carry):
        col = pl.multiple_of(h * 2 * LANES, 2 * LANES)
        s_ref[...] = lax.dot_general(sk_ref[0], q_ref[:, pl.ds(col, LANES)], _NT, preferred_element_type=f32)
        v1, i1 = _topk_rows(s_ref, nkeys, tb)
        s_ref[...] = lax.dot_general(sk_ref[1], q_ref[:, pl.ds(col + LANES, LANES)], _NT, preferred_element_type=f32)
        v2, i2 = _topk_rows(s_ref, nkeys, tb)
        for a in range(PEER_TOPK):
            cand_ref[a * PEER_TOPK:(a + 1) * PEER_TOPK, :] = v1[a:a + 1, :] + v2
        sc, ci = _topk_rows(cand_ref, PEER_TOPK * PEER_TOPK, tb)
        ca = lax.shift_right_logical(ci, PEER_TOPK.bit_length() - 1)
        cb = lax.bitwise_and(ci, PEER_TOPK - 1)
        e1 = jnp.zeros((PEER_TOPK, tb), jnp.int32)
        e2 = jnp.zeros((PEER_TOPK, tb), jnp.int32)
        for a in range(PEER_TOPK):
            e1 = jnp.where(ca == a, i1[a:a + 1, :], e1)
            e2 = jnp.where(cb == a, i2[a:a + 1, :], e2)
        p = jnp.exp(sc - sc[0:1, :])
        gate = p / jnp.sum(p, axis=0, keepdims=True)
        row = pl.multiple_of(h * PEER_TOPK, PEER_TOPK)
        es_ref[pl.ds(row, PEER_TOPK), :] = e1 * nkeys + e2
        gs_ref[pl.ds(row, PEER_TOPK), :] = gate
        return carry

    lax.fori_loop(0, n_heads, head, 0)
    e_ref[...] = es_ref[...].T
    g_ref[...] = gs_ref[...].T


def _peer_topk(q, sub_keys, n_blocks_rows):
    t, qd = q.shape
    nkeys = sub_keys.shape[1]
    n_heads = qd // (2 * LANES)
    slots = n_heads * PEER_TOPK
    tb = 256
    return pl.pallas_call(
        functools.partial(_peer_topk_kernel, n_heads=n_heads, nkeys=nkeys),
        grid=(n_blocks_rows * ROW_BLOCK // tb,),
        in_specs=[pl.BlockSpec((tb, qd), lambda i: (i, 0)),
                  pl.BlockSpec(sub_keys.shape, lambda i: (0, 0, 0))],
        out_specs=[pl.BlockSpec((tb, slots), lambda i: (i, 0)),
                   pl.BlockSpec((tb, slots), lambda i: (i, 0))],
        out_shape=[jax.ShapeDtypeStruct((t, slots), jnp.int32), jax.ShapeDtypeStruct((t, slots), f32)],
        scratch_shapes=[pltpu.VMEM((nkeys, tb), f32), pltpu.VMEM((PEER_TOPK * PEER_TOPK, tb), f32),
                        pltpu.VMEM((slots, tb), jnp.int32), pltpu.VMEM((slots, tb), f32)],
        compiler_params=_cparams(1),
        name="peer_topk",
    )(q, sub_keys)


def _peer_gates_kernel(e_ref, g_ref, o_ref, *, nkeys):
    tb, slots = e_ref.shape
    iota = lax.broadcasted_iota(jnp.int32, (nkeys, slots), 0)

    def body(t, carry):
        e = e_ref[pl.ds(t, 1), :]
        g = g_ref[pl.ds(t, 1), :]
        e1 = lax.shift_right_logical(e, nkeys.bit_length() - 1)
        e2 = lax.bitwise_and(e, nkeys - 1)
        w1 = jnp.where(e1 == iota, g, 0.0).astype(bf16)
        o2 = jnp.where(e2 == iota, 1.0, 0.0).astype(bf16)
        o_ref[t] = lax.dot_general(w1, o2, _NT, preferred_element_type=f32)
        return carry

    lax.fori_loop(0, tb, body, 0)


def _peer_gates(e, g, nkeys, n_blocks_rows):
    t, slots = e.shape
    tb = 128
    return pl.pallas_call(
        functools.partial(_peer_gates_kernel, nkeys=nkeys),
        grid=(n_blocks_rows * ROW_BLOCK // tb,),
        in_specs=[pl.BlockSpec((tb, slots), lambda i: (i, 0)),
                  pl.BlockSpec((tb, slots), lambda i: (i, 0))],
        out_specs=pl.BlockSpec((tb, nkeys, nkeys), lambda i: (i, 0, 0)),
        out_shape=jax.ShapeDtypeStruct((t, nkeys, nkeys), f32),
        compiler_params=_cparams(1),
        name="peer_gates",
    )(e, g)


def _gelu_tanh(x):
    return 0.5 * x * (1.0 + jnp.tanh(0.7978845608028654 * (x + 0.044715 * (x * x * x))))


def _peer_dense_kernel(f_ref, u_ref, v_ref, g_ref, x_ref, m_ref, o_ref, acc_ref, *, gate_row, nkeys):
    j = pl.program_id(1)

    @pl.when(j == 0)
    def _():
        acc_ref[...] = jnp.zeros_like(acc_ref)

    a = lax.dot_general(f_ref[...], u_ref[...], _NT, preferred_element_type=f32)
    n_first = g_ref.shape[1]
    parts = []
    for k in range(n_first):
        parts.append((_gelu_tanh(a[:, k * nkeys:(k + 1) * nkeys]) * g_ref[:, k, :]).astype(bf16))
    ga = jnp.concatenate(parts, axis=1)
    acc_ref[...] += jnp.dot(ga, v_ref[...], preferred_element_type=f32)

    @pl.when(j == pl.num_programs(1) - 1)
    def _():
        o_ref[...] = x_ref[...] + m_ref[gate_row:gate_row + 1, :] * acc_ref[...]


def _peer_dense(f, u, v, gates, x, mods, gate_row, n_blocks, bps, n_batch):
    t, d = f.shape
    ne = u.shape[0]
    nkeys = gates.shape[2]
    te = SUBLANES * nkeys
    return pl.pallas_call(
        functools.partial(_peer_dense_kernel, gate_row=gate_row, nkeys=nkeys),
        grid=(n_blocks, ne // te),
        in_specs=[pl.BlockSpec((ROW_BLOCK, d), lambda i, j: (i, 0)),
                  pl.BlockSpec((te, d), lambda i, j: (j, 0)),
                  pl.BlockSpec((te, d), lambda i, j: (j, 0)),
                  pl.BlockSpec((ROW_BLOCK, SUBLANES, nkeys), lambda i, j: (i, j, 0)),
                  pl.BlockSpec((ROW_BLOCK, d), lambda i, j: (i, 0)),
                  pl.BlockSpec((None, SUBLANES, d), lambda i, j: (_group_of(i, bps, n_batch), 0, 0))],
        out_specs=pl.BlockSpec((ROW_BLOCK, d), lambda i, j: (i, 0)),
        out_shape=jax.ShapeDtypeStruct((t, d), f32),
        scratch_shapes=[pltpu.VMEM((ROW_BLOCK, d), f32)],
        compiler_params=_cparams(2),
        name="peer_dense",
    )(f, u, v, gates, x, mods)


def kernel(x, c, ctx, c_ctx, mod_w, mod_b, norm_w, final_norm_w, a_wqkv, a_q_gain, a_k_gain, a_wo,
           b_wqkv, b_rpb, b_wo, pool_w, pool_scale, peer_wq, peer_sub_keys, peer_u, peer_v):
    n_batch, s, d = x.shape
    c_len = ctx.shape[1]
    depth = mod_w.shape[0]
    assert s % ROW_BLOCK == 0 and (n_batch * c_len) % ROW_BLOCK == 0 and n_batch * c_len <= s
    assert s % c_len == 0 and c_len % POOL_ROW_BLOCK == 0 and s % GRID_W == 0
    assert a_q_gain.shape[1] == LANES and peer_sub_keys.shape[2] == LANES and peer_sub_keys.shape[3] == LANES
    assert PEER_TOPK & (PEER_TOPK - 1) == 0 and peer_sub_keys.shape[2] & (peer_sub_keys.shape[2] - 1) == 0
    assert depth % N_MIXERS != 0

    bps = s // ROW_BLOCK
    nb_lat = n_batch * bps
    nb_all = nb_lat + n_batch * c_len // ROW_BLOCK
    t_lat = n_batch * s
    q_dim = a_wo.shape[1]
    kv_dim = (a_wqkv.shape[2] - q_dim) // 2
    n_kv = kv_dim // LANES
    groups = q_dim // kv_dim
    b_heads = b_rpb.shape[1]
    win_r, win_c = (b_rpb.shape[2] + 1) // 2, (b_rpb.shape[3] + 1) // 2
    nkeys = peer_sub_keys.shape[2]
    scale = float(LANES) ** -0.5

    xs = jnp.concatenate([x.reshape(t_lat, d), ctx.reshape(n_batch * c_len, d)], axis=0)
    cvec = jnp.concatenate([c, c_ctx[None, :], jnp.zeros((SUBLANES - n_batch - 1, d), f32)], axis=0)
    mods_all = _mods(cvec, mod_w, mod_b)
    tables = _rope_tables(s)

    for i in range(depth):
        last = i == depth - 1
        kind, j = i % N_MIXERS, i // N_MIXERS
        nb = nb_lat if last else nb_all
        mods = mods_all[i, :n_batch + 1].reshape(n_batch + 1, N_MOD, d)
        mods = jnp.pad(mods, ((0, 0), (0, SUBLANES - N_MOD), (0, 0)))

        if kind == 2:
            h = _norm_mod(xs, norm_w[i, 0], mods, 0, 1, nb_all, bps, n_batch, f32)
            xs = _pool(h, xs, pool_w[j].astype(bf16), pool_scale[j], mods, 2, n_batch, s, c_len)
        else:
            h = _norm_mod(xs, norm_w[i, 0], mods, 0, 1, nb_all, bps, n_batch, bf16)
            if kind == 0:
                qkv = _qkv_gqa(h, a_wqkv[j].astype(bf16), a_q_gain[j], a_k_gain[j], tables, q_dim, kv_dim,
                               nb_all, bps, n_batch)
                o = _attn_lat(qkv, qkv, qkv, 0, q_dim, q_dim + kv_dim, groups, n_kv, n_batch, s, c_len, 1.0, q_dim)
                if not last:
                    o = _attn_ctx(qkv, qkv, qkv, o, 0, q_dim, q_dim + kv_dim, groups, n_kv, n_batch, s, c_len, 1.0)
                wo = a_wo[j]
            else:
                qkv = _mm(h, b_wqkv[j].astype(bf16), nb_all, 512, bf16)
                rows = s // GRID_W
                bias = _natten_bias(b_rpb[j], rows, min(win_r, rows), win_r, win_c)
                o = _natten_lat(qkv, bias, b_heads, n_batch, s, c_len, d)
                if not last:
                    o = _attn_ctx(qkv, qkv, qkv, o, 0, d, 2 * d, 1, b_heads, n_batch, s, c_len, scale)
                wo = b_wo[j]
            xs = _mm_res(o, wo.astype(bf16), xs, mods, 2, nb, bps, n_batch, 512)

        f = _norm_mod(xs, norm_w[i, 1], mods, 3, 4, nb, bps, n_batch, bf16)
        qp = _mm(f, peer_wq[i].astype(bf16), nb, 512, f32)
        e, g = _peer_topk(qp, peer_sub_keys[i], nb)
        gates = _peer_gates(e, g, nkeys, nb)
        xs = _peer_dense(f, peer_u[i].astype(bf16), peer_v[i].astype(bf16), gates, xs, mods, 5, nb, bps, n_batch)

    return _final_norm(xs, final_norm_w, nb_lat).reshape(n_batch, s, d)
```

```python
import functools

import numpy as np
import jax
import jax.numpy as jnp
from jax import lax
from jax.experimental import pallas as pl
from jax.experimental.pallas import tpu as pltpu

GRID_W = 64
ROPE_BASE = 10000.0
POOL_WINDOWS = (2, 4, 8, 16)
PEER_TOPK = 16
EPS = 1e-6
N_MIXERS = 3
N_MOD = 6

LANES = 128
SUBLANES = 8
ROW_BLOCK = 512
POOL_ROW_BLOCK = 256
POOL_HALO = 8
GATES_UNROLL = 8
VMEM_LIMIT = 56 * 1024 * 1024
NEG = -1e30

f32 = jnp.float32
bf16 = jnp.bfloat16
_NT = (((1,), (1,)), ((), ()))


def _cparams(n_axes):
    return pltpu.CompilerParams(dimension_semantics=("arbitrary",) * n_axes, vmem_limit_bytes=VMEM_LIMIT)


def _group_of(i, blocks_per_seq, n_batch):
    return jnp.minimum(i // blocks_per_seq, n_batch)


def _mods_kernel(cv_ref, w_ref, b_ref, o_ref):
    cv = cv_ref[...]
    a = (cv / (1.0 + jnp.exp(-cv))).astype(bf16)
    o_ref[...] = jnp.dot(a, w_ref[...].astype(bf16), preferred_element_type=f32) + b_ref[...]


def _mods(cvec, mod_w, mod_b):
    depth, d, n = mod_w.shape
    tn = 1024
    return pl.pallas_call(
        _mods_kernel,
        grid=(depth, n // tn),
        in_specs=[pl.BlockSpec((SUBLANES, d), lambda l, j: (0, 0)),
                  pl.BlockSpec((None, d, tn), lambda l, j: (l, 0, j)),
                  pl.BlockSpec((None, 1, tn), lambda l, j: (l, 0, j))],
        out_specs=pl.BlockSpec((None, SUBLANES, tn), lambda l, j: (l, 0, j)),
        out_shape=jax.ShapeDtypeStruct((depth, SUBLANES, n), f32),
        compiler_params=_cparams(2),
        name="adaln_mods",
    )(cvec, mod_w, mod_b.reshape(depth, 1, n))


def _norm_mod_kernel(x_ref, nw_ref, m_ref, o_ref, *, shift_row, scale_row):
    x = x_ref[...]
    y = x * lax.rsqrt(jnp.mean(x * x, axis=-1, keepdims=True) + EPS) * nw_ref[...]
    y = y * (1.0 + m_ref[scale_row:scale_row + 1, :]) + m_ref[shift_row:shift_row + 1, :]
    o_ref[...] = y.astype(o_ref.dtype)


def _norm_mod(x, nw, mods, shift_row, scale_row, n_blocks, bps, n_batch, out_dtype):
    t, d = x.shape
    return pl.pallas_call(
        functools.partial(_norm_mod_kernel, shift_row=shift_row, scale_row=scale_row),
        grid=(n_blocks,),
        in_specs=[pl.BlockSpec((ROW_BLOCK, d), lambda i: (i, 0)),
                  pl.BlockSpec((1, d), lambda i: (0, 0)),
                  pl.BlockSpec((None, SUBLANES, d), lambda i: (_group_of(i, bps, n_batch), 0, 0))],
        out_specs=pl.BlockSpec((ROW_BLOCK, d), lambda i: (i, 0)),
        out_shape=jax.ShapeDtypeStruct((t, d), out_dtype),
        compiler_params=_cparams(1),
        name="norm_mod",
    )(x, nw.reshape(1, d), mods)


def _final_norm_kernel(x_ref, nw_ref, o_ref):
    x = x_ref[...]
    o_ref[...] = x * lax.rsqrt(jnp.mean(x * x, axis=-1, keepdims=True) + EPS) * nw_ref[...]


def _final_norm(x, nw, n_blocks):
    t, d = x.shape
    return pl.pallas_call(
        _final_norm_kernel,
        grid=(n_blocks,),
        in_specs=[pl.BlockSpec((ROW_BLOCK, d), lambda i: (i, 0)),
                  pl.BlockSpec((1, d), lambda i: (0, 0))],
        out_specs=pl.BlockSpec((ROW_BLOCK, d), lambda i: (i, 0)),
        out_shape=jax.ShapeDtypeStruct((n_blocks * ROW_BLOCK, d), f32),
        compiler_params=_cparams(1),
        name="final_norm",
    )(x, nw.reshape(1, d))


def _mm_kernel(a_ref, w_ref, o_ref):
    o_ref[...] = jnp.dot(a_ref[...], w_ref[...], preferred_element_type=f32).astype(o_ref.dtype)


def _mm(a, w, n_blocks, tn, out_dtype):
    t, k = a.shape
    n = w.shape[1]
    return pl.pallas_call(
        _mm_kernel,
        grid=(n // tn, n_blocks),
        in_specs=[pl.BlockSpec((ROW_BLOCK, k), lambda j, i: (i, 0)),
                  pl.BlockSpec((k, tn), lambda j, i: (0, j))],
        out_specs=pl.BlockSpec((ROW_BLOCK, tn), lambda j, i: (i, j)),
        out_shape=jax.ShapeDtypeStruct((t, n), out_dtype),
        compiler_params=_cparams(2),
        name="matmul",
    )(a, w)


def _mm_res_kernel(a_ref, w_ref, x_ref, m_ref, o_ref, *, gate_row):
    y = jnp.dot(a_ref[...], w_ref[...], preferred_element_type=f32)
    o_ref[...] = x_ref[...] + m_ref[gate_row:gate_row + 1, :] * y


def _mm_res(a, w, x, mods, gate_row, n_blocks, bps, n_batch, tn):
    t, k = a.shape
    n = w.shape[1]
    return pl.pallas_call(
        functools.partial(_mm_res_kernel, gate_row=gate_row),
        grid=(n // tn, n_blocks),
        in_specs=[pl.BlockSpec((ROW_BLOCK, k), lambda j, i: (i, 0)),
                  pl.BlockSpec((k, tn), lambda j, i: (0, j)),
                  pl.BlockSpec((ROW_BLOCK, tn), lambda j, i: (i, j)),
                  pl.BlockSpec((None, SUBLANES, tn), lambda j, i: (_group_of(i, bps, n_batch), 0, j))],
        out_specs=pl.BlockSpec((ROW_BLOCK, tn), lambda j, i: (i, j)),
        out_shape=jax.ShapeDtypeStruct((t, n), f32),
        compiler_params=_cparams(2),
        name="matmul_residual",
    )(a, w, x, mods)


def _qkv_gqa_kernel(a_ref, w_ref, qg_ref, kg_ref, cos_ref, sa_ref, sb_ref, o_ref, *, nq_tiles, nk_tiles, scale):
    j = pl.program_id(0)
    acc = jnp.dot(a_ref[...], w_ref[...], preferred_element_type=f32)
    heads = acc.shape[1] // LANES

    def prep(gain, post):
        cos, sa, sb = cos_ref[...], sa_ref[...], sb_ref[...]
        for h in range(heads):
            y = acc[:, h * LANES:(h + 1) * LANES]
            y = y * lax.rsqrt(jnp.mean(y * y, axis=-1, keepdims=True) + EPS) * gain
            y = y * cos + pltpu.roll(y, LANES - 32, 1) * sa + pltpu.roll(y, 32, 1) * sb
            o_ref[:, h * LANES:(h + 1) * LANES] = (y * post).astype(o_ref.dtype)

    @pl.when(j < nq_tiles)
    def _():
        prep(qg_ref[...], scale)

    @pl.when(jnp.logical_and(j >= nq_tiles, j < nq_tiles + nk_tiles))
    def _():
        prep(kg_ref[...], 1.0)

    @pl.when(j >= nq_tiles + nk_tiles)
    def _():
        o_ref[...] = acc.astype(o_ref.dtype)


def _rope_tables(s):
    t = np.arange(s)
    half = LANES // 2
    inv = ROPE_BASE ** (-jnp.arange(0, half, 2, dtype=f32) / half)
    ang_r = (t // GRID_W).astype(np.float32)[:, None] * inv[None, :]
    ang_c = (t % GRID_W).astype(np.float32)[:, None] * inv[None, :]
    ang = jnp.concatenate([ang_r, ang_r, ang_c, ang_c], axis=-1)
    cos, sin = jnp.cos(ang), jnp.sin(ang)
    first = (np.arange(LANES) % half) < (half // 2)
    sa = jnp.where(first[None, :], -sin, 0.0)
    sb = jnp.where(first[None, :], 0.0, sin)
    pad1 = jnp.ones((ROW_BLOCK, LANES), f32)
    pad0 = jnp.zeros((ROW_BLOCK, LANES), f32)
    return (jnp.concatenate([cos, pad1], 0), jnp.concatenate([sa, pad0], 0), jnp.concatenate([sb, pad0], 0))


def _qkv_gqa(h, wqkv, q_gain, k_gain, tables, q_dim, kv_dim, n_blocks, bps, n_batch):
    t, k = h.shape
    n = wqkv.shape[1]
    tn = min(512, kv_dim)
    cos, sa, sb = tables

    def pos_map(j, i):
        return (jnp.where(i < bps * n_batch, i % bps, bps), 0)

    return pl.pallas_call(
        functools.partial(_qkv_gqa_kernel, nq_tiles=q_dim // tn, nk_tiles=kv_dim // tn, scale=float(LANES) ** -0.5),
        grid=(n // tn, n_blocks),
        in_specs=[pl.BlockSpec((ROW_BLOCK, k), lambda j, i: (i, 0)),
                  pl.BlockSpec((k, tn), lambda j, i: (0, j)),
                  pl.BlockSpec((1, LANES), lambda j, i: (0, 0)),
                  pl.BlockSpec((1, LANES), lambda j, i: (0, 0)),
                  pl.BlockSpec((ROW_BLOCK, LANES), pos_map),
                  pl.BlockSpec((ROW_BLOCK, LANES), pos_map),
                  pl.BlockSpec((ROW_BLOCK, LANES), pos_map)],
        out_specs=pl.BlockSpec((ROW_BLOCK, tn), lambda j, i: (i, j)),
        out_shape=jax.ShapeDtypeStruct((t, n), bf16),
        compiler_params=_cparams(2),
        name="qkv_gqa",
    )(h, wqkv, q_gain.reshape(1, LANES), k_gain.reshape(1, LANES), cos, sa, sb)


def _flash_kernel(*refs, groups, n_lat_chunks, ck, scale):
    if n_lat_chunks:
        q_ref, kc_ref, vc_ref, kl_ref, vl_ref, o_ref = refs
    else:
        q_ref, kc_ref, vc_ref, o_ref = refs
    tq = q_ref.shape[0]
    q = jnp.concatenate([q_ref[:, g * LANES:(g + 1) * LANES] for g in range(groups)], axis=0)
    rows = groups * tq

    def step(k, v, m, l, acc):
        s = lax.dot_general(q, k, _NT, preferred_element_type=f32)
        if scale != 1.0:
            s = s * scale
        m_new = jnp.maximum(m, jnp.max(s, axis=-1, keepdims=True))
        p = jnp.exp(s - m_new)
        alpha = jnp.exp(m - m_new)
        l = alpha * l + jnp.sum(p, axis=-1, keepdims=True)
        acc = alpha * acc + jnp.dot(p.astype(bf16), v, preferred_element_type=f32)
        return m_new, l, acc

    carry = step(kc_ref[...], vc_ref[...], jnp.full((rows, 1), NEG, f32), jnp.zeros((rows, 1), f32),
                 jnp.zeros((rows, LANES), f32))
    if n_lat_chunks:
        def body(c, carry):
            off = pl.multiple_of(c * ck, ck)
            return step(kl_ref[pl.ds(off, ck), :], vl_ref[pl.ds(off, ck), :], *carry)
        carry = lax.fori_loop(0, n_lat_chunks, body, carry)
    _, l, acc = carry
    o = acc / l
    for g in range(groups):
        o_ref[:, g * LANES:(g + 1) * LANES] = o[g * tq:(g + 1) * tq].astype(o_ref.dtype)


def _attn_lat(q, k, v, q_col0, k_col0, v_col0, groups, n_kv, n_batch, s, c, scale, out_cols):
    t = q.shape[0]
    tq = 128
    ck = min(512, s)
    nqb = s // tq
    ctx_blk0 = n_batch * s // c
    qw = groups * LANES
    return pl.pallas_call(
        functools.partial(_flash_kernel, groups=groups, n_lat_chunks=s // ck, ck=ck, scale=scale),
        grid=(n_batch, n_kv, nqb),
        in_specs=[pl.BlockSpec((tq, qw), lambda b, h, i: (b * nqb + i, q_col0 // qw + h)),
                  pl.BlockSpec((c, LANES), lambda b, h, i: (ctx_blk0 + b, k_col0 // LANES + h)),
                  pl.BlockSpec((c, LANES), lambda b, h, i: (ctx_blk0 + b, v_col0 // LANES + h)),
                  pl.BlockSpec((s, LANES), lambda b, h, i: (b, k_col0 // LANES + h)),
                  pl.BlockSpec((s, LANES), lambda b, h, i: (b, v_col0 // LANES + h))],
        out_specs=pl.BlockSpec((tq, qw), lambda b, h, i: (b * nqb + i, h)),
        out_shape=jax.ShapeDtypeStruct((t, out_cols), bf16),
        compiler_params=_cparams(3),
        name="attn_latent",
    )(q, k, v, k, v)


def _attn_ctx_kernel(q_ref, kc_ref, vc_ref, o_in_ref, o_ref, **kw):
    del o_in_ref
    _flash_kernel(q_ref, kc_ref, vc_ref, o_ref, **kw)


def _attn_ctx(q, k, v, o, q_col0, k_col0, v_col0, groups, n_kv, n_batch, s, c, scale):
    ctx_blk0 = n_batch * s // c
    qw = groups * LANES
    return pl.pallas_call(
        functools.partial(_attn_ctx_kernel, groups=groups, n_lat_chunks=0, ck=0, scale=scale),
        grid=(n_batch, n_kv),
        in_specs=[pl.BlockSpec((c, qw), lambda b, h: (ctx_blk0 + b, q_col0 // qw + h)),
                  pl.BlockSpec((c, LANES), lambda b, h: (ctx_blk0 + b, k_col0 // LANES + h)),
                  pl.BlockSpec((c, LANES), lambda b, h: (ctx_blk0 + b, v_col0 // LANES + h)),
                  pl.BlockSpec(memory_space=pl.ANY)],
        out_specs=pl.BlockSpec((c, qw), lambda b, h: (ctx_blk0 + b, h)),
        out_shape=jax.ShapeDtypeStruct(o.shape, o.dtype),
        input_output_aliases={3: 0},
        compiler_params=_cparams(2),
        name="attn_context",
    )(q, k, v, o)


def _natten_kernel(q_ref, k_ref, v_ref, kc_ref, vc_ref, bias_ref, o_ref, *, rows, w, wr, scale):
    kc = kc_ref[...]
    vc = vc_ref[...]

    def body(r, carry):
        r0 = jnp.clip(r - wr // 2, 0, rows - wr)
        d = r - r0
        q = q_ref[pl.ds(pl.multiple_of(r * w, w), w), :]
        off = pl.multiple_of(r0 * w, w)
        kb = k_ref[pl.ds(off, wr * w), :]
        vb = v_ref[pl.ds(off, wr * w), :]
        sw = lax.dot_general(q, kb, _NT, preferred_element_type=f32) * scale + bias_ref[d]
        sc = lax.dot_general(q, kc, _NT, preferred_element_type=f32) * scale
        m = jnp.maximum(jnp.max(sw, axis=-1, keepdims=True), jnp.max(sc, axis=-1, keepdims=True))
        pw = jnp.exp(sw - m)
        pc = jnp.exp(sc - m)
        l = jnp.sum(pw, axis=-1, keepdims=True) + jnp.sum(pc, axis=-1, keepdims=True)
        o = (jnp.dot(pw.astype(bf16), vb, preferred_element_type=f32)
             + jnp.dot(pc.astype(bf16), vc, preferred_element_type=f32)) / l
        o_ref[pl.ds(pl.multiple_of(r * w, w), w), :] = o.astype(o_ref.dtype)
        return carry

    lax.fori_loop(0, rows, body, 0)


def _natten_bias(rpb, rows, wr, win_r, win_c):
    w = GRID_W
    cols = np.arange(w)
    col_start = np.clip(cols - win_c // 2, 0, w - win_c)
    inside = (cols[None, :] >= col_start[:, None]) & (cols[None, :] < col_start[:, None] + win_c)
    rpbp = jnp.pad(rpb, ((0, 0), (0, 0), (w - win_c, w - win_c)))
    p = jnp.stack([rpbp[:, :, w - 1 - qc:2 * w - 1 - qc] for qc in range(w)], axis=2)
    p = jnp.where(inside[None, None], p, NEG)
    tab = jnp.stack([p[:, win_r - 1 - d:win_r - 1 - d + wr] for d in range(wr)], axis=1)
    return jnp.transpose(tab, (0, 1, 3, 2, 4)).reshape(rpb.shape[0], wr, w, wr * w)


def _natten_lat(qkv, bias, n_heads, n_batch, s, c, d_model):
    t = qkv.shape[0]
    rows = s // GRID_W
    wr = bias.shape[1]
    ctx_blk0 = n_batch * s // c
    hq, hk, hv = 0, d_model // LANES, 2 * d_model // LANES
    return pl.pallas_call(
        functools.partial(_natten_kernel, rows=rows, w=GRID_W, wr=wr, scale=float(LANES) ** -0.5),
        grid=(n_batch, n_heads),
        in_specs=[pl.BlockSpec((s, LANES), lambda b, h: (b, hq + h)),
                  pl.BlockSpec((s, LANES), lambda b, h: (b, hk + h)),
                  pl.BlockSpec((s, LANES), lambda b, h: (b, hv + h)),
                  pl.BlockSpec((c, LANES), lambda b, h: (ctx_blk0 + b, hk + h)),
                  pl.BlockSpec((c, LANES), lambda b, h: (ctx_blk0 + b, hv + h)),
                  pl.BlockSpec((None, wr, GRID_W, wr * GRID_W), lambda b, h: (h, 0, 0, 0))],
        out_specs=pl.BlockSpec((s, LANES), lambda b, h: (b, h)),
        out_shape=jax.ShapeDtypeStruct((t, d_model), bf16),
        compiler_params=_cparams(2),
        name="natten_latent",
    )(qkv, qkv, qkv, qkv, qkv, bias)


def _pool_kernel(h_ref, hp_ref, hn_ref, x_ref, w_ref, ls_ref, m_ref, o_ref, pad_ref, *,
                 bps_lat, bps_ctx, n_lat_blocks, s, c, gate_row):
    i = pl.program_id(0)
    tm = h_ref.shape[0]
    is_lat = i < n_lat_blocks
    blk = jnp.where(is_lat, i % bps_lat, (i - n_lat_blocks) % bps_ctx)
    nblk = jnp.where(is_lat, bps_lat, bps_ctx)
    length = jnp.where(is_lat, s, c)
    pad_ref[0:POOL_HALO, :] = jnp.where(blk == 0, 0.0, hp_ref[...])
    pad_ref[POOL_HALO:POOL_HALO + tm, :] = h_ref[...]
    pad_ref[POOL_HALO + tm:2 * POOL_HALO + tm, :] = jnp.where(blk == nblk - 1, 0.0, hn_ref[...])
    pos = blk * tm + lax.broadcasted_iota(jnp.int32, (tm, 1), 0)
    pg = w_ref.shape[1]
    for g, win in enumerate(POOL_WINDOWS):
        cs = slice(g * pg, (g + 1) * pg)
        lo_off, hi_off = win // 2, win - win // 2
        acc = pad_ref[POOL_HALO - lo_off:POOL_HALO - lo_off + tm, cs]
        for k in range(-lo_off + 1, hi_off):
            acc = acc + pad_ref[POOL_HALO + k:POOL_HALO + k + tm, cs]
        cnt = jnp.minimum(pos + hi_off, length) - jnp.maximum(pos - lo_off, 0)
        y = acc * (1.0 / cnt.astype(f32)) - h_ref[:, cs]
        z = jnp.dot(y.astype(bf16), w_ref[g], preferred_element_type=f32) * ls_ref[:, cs]
        o_ref[:, cs] = x_ref[:, cs] + m_ref[gate_row:gate_row + 1, cs] * z


def _pool(h, x, pool_w, pool_scale, mods, gate_row, n_batch, s, c):
    t, d = h.shape
    tm = POOL_ROW_BLOCK
    nb = t // tm
    bps_lat, bps_ctx = s // tm, c // tm
    hb = tm // POOL_HALO
    last_halo = t // POOL_HALO - 1
    return pl.pallas_call(
        functools.partial(_pool_kernel, bps_lat=bps_lat, bps_ctx=bps_ctx, n_lat_blocks=n_batch * bps_lat,
                          s=s, c=c, gate_row=gate_row),
        grid=(nb,),
        in_specs=[pl.BlockSpec((tm, d), lambda i: (i, 0)),
                  pl.BlockSpec((POOL_HALO, d), lambda i: (jnp.maximum(i * hb - 1, 0), 0)),
                  pl.BlockSpec((POOL_HALO, d), lambda i: (jnp.minimum((i + 1) * hb, last_halo), 0)),
                  pl.BlockSpec((tm, d), lambda i: (i, 0)),
                  pl.BlockSpec(pool_w.shape, lambda i: (0, 0, 0)),
                  pl.BlockSpec((1, d), lambda i: (0, 0)),
                  pl.BlockSpec((None, SUBLANES, d), lambda i: (_group_of(i, bps_lat, n_batch), 0, 0))],
        out_specs=pl.BlockSpec((tm, d), lambda i: (i, 0)),
        out_shape=jax.ShapeDtypeStruct((t, d), f32),
        scratch_shapes=[pltpu.VMEM((tm + 2 * POOL_HALO, d), f32)],
        compiler_params=_cparams(1),
        name="pool_mixer",
    )(h, h, h, x, pool_w, pool_scale.reshape(1, d), mods)


def _topk_rows(s_ref, order, tb):
    rows = lax.broadcasted_iota(jnp.int32, (PEER_TOPK, tb), 0)

    def body(k, carry):
        vals, idxs = carry
        s = s_ref[...]
        m = jnp.max(s, axis=0, keepdims=True)
        am = jnp.min(jnp.where(s == m, order, float(2 ** 23)), axis=0, keepdims=True)
        s_ref[...] = jnp.where(order == am, NEG, s)
        return jnp.where(rows == k, m, vals), jnp.where(rows == k, am, idxs)

    vals, idxs = lax.fori_loop(0, PEER_TOPK, body, (jnp.zeros((PEER_TOPK, tb), f32), jnp.zeros((PEER_TOPK, tb), f32)))
    return vals, idxs.astype(jnp.int32)


def _cand_blocks():
    blocks = [(0, PEER_TOPK)] + [(a, SUBLANES) for a in range(1, SUBLANES)]
    assert all((a + 1) * (nb + 1) > PEER_TOPK for a, nb in blocks[1:]) and 2 * SUBLANES == PEER_TOPK
    return blocks


def _peer_topk_kernel(q_ref, sk_ref, e_ref, g_ref, s_ref, cand_ref, es_ref, gs_ref, *, n_heads, nkeys):
    tb = q_ref.shape[0]
    key_order = lax.broadcasted_iota(jnp.int32, (nkeys, tb), 0).astype(f32)
    blocks = _cand_blocks()
    n_mid = (len(blocks) - 1) * SUBLANES
    r = lax.broadcasted_iota(jnp.int32, cand_ref.shape, 0)
    rm = r - PEER_TOPK
    mid = (lax.shift_right_logical(rm, 3) + 1) * PEER_TOPK + lax.bitwise_and(rm, SUBLANES - 1)
    tail = (rm - n_mid + SUBLANES) * PEER_TOPK
    cand_order = jnp.where(r < PEER_TOPK, r, jnp.where(rm < n_mid, mid, tail)).astype(f32)

    def head(h, carry):
        col = pl.multiple_of(h * 2 * LANES, 2 * LANES)
        s_ref[...] = lax.dot_general(sk_ref[0], q_ref[:, pl.ds(col, LANES)], _NT, preferred_element_type=f32)
        v1, i1 = _topk_rows(s_ref, key_order, tb)
        s_ref[...] = lax.dot_general(sk_ref[1], q_ref[:, pl.ds(col + LANES, LANES)], _NT, preferred_element_type=f32)
        v2, i2 = _topk_rows(s_ref, key_order, tb)
        r0 = 0
        for a, nb in blocks:
            cand_ref[r0:r0 + nb, :] = v1[a:a + 1, :] + v2[:nb]
            r0 += nb
        cand_ref[r0:r0 + SUBLANES, :] = v1[SUBLANES:] + v2[0:1, :]
        sc, ci = _topk_rows(cand_ref, cand_order, tb)
        ca = lax.shift_right_logical(ci, PEER_TOPK.bit_length() - 1)
        cb = lax.bitwise_and(ci, PEER_TOPK - 1)
        e1 = jnp.zeros((PEER_TOPK, tb), jnp.int32)
        e2 = jnp.zeros((PEER_TOPK, tb), jnp.int32)
        for a in range(PEER_TOPK):
            e1 = jnp.where(ca == a, i1[a:a + 1, :], e1)
            e2 = jnp.where(cb == a, i2[a:a + 1, :], e2)
        p = jnp.exp(sc - sc[0:1, :])
        gate = p / jnp.sum(p, axis=0, keepdims=True)
        row = pl.multiple_of(h * PEER_TOPK, PEER_TOPK)
        es_ref[pl.ds(row, PEER_TOPK), :] = e1 * nkeys + e2
        gs_ref[pl.ds(row, PEER_TOPK), :] = gate
        return carry

    lax.fori_loop(0, n_heads, head, 0)
    e_ref[...] = es_ref[...].T
    g_ref[...] = gs_ref[...].T


def _peer_topk(q, sub_keys, n_blocks_rows):
    t, qd = q.shape
    nkeys = sub_keys.shape[1]
    n_heads = qd // (2 * LANES)
    slots = n_heads * PEER_TOPK
    tb = 256
    return pl.pallas_call(
        functools.partial(_peer_topk_kernel, n_heads=n_heads, nkeys=nkeys),
        grid=(n_blocks_rows * ROW_BLOCK // tb,),
        in_specs=[pl.BlockSpec((tb, qd), lambda i: (i, 0)),
                  pl.BlockSpec(sub_keys.shape, lambda i: (0, 0, 0))],
        out_specs=[pl.BlockSpec((tb, slots), lambda i: (i, 0)),
                   pl.BlockSpec((tb, slots), lambda i: (i, 0))],
        out_shape=[jax.ShapeDtypeStruct((t, slots), jnp.int32), jax.ShapeDtypeStruct((t, slots), f32)],
        scratch_shapes=[pltpu.VMEM((nkeys, tb), f32),
                        pltpu.VMEM((sum(nb for _, nb in _cand_blocks()) + SUBLANES, tb), f32),
                        pltpu.VMEM((slots, tb), jnp.int32), pltpu.VMEM((slots, tb), f32)],
        compiler_params=_cparams(1),
        name="peer_topk",
    )(q, sub_keys)


def _peer_gates_kernel(e_ref, g_ref, o_ref, *, nkeys):
    tb, slots = e_ref.shape
    iota = lax.broadcasted_iota(jnp.int32, (nkeys, slots), 0)

    def body(t, carry):
        e = e_ref[pl.ds(t, 1), :]
        g = g_ref[pl.ds(t, 1), :]
        e1 = lax.shift_right_logical(e, nkeys.bit_length() - 1)
        e2 = lax.bitwise_and(e, nkeys - 1)
        w1 = jnp.where(e1 == iota, g, 0.0).astype(bf16)
        o2 = jnp.where(e2 == iota, 1.0, 0.0).astype(bf16)
        res = lax.dot_general(w1, o2, _NT, preferred_element_type=f32)
        o_ref[:, t] = res.reshape(nkeys // SUBLANES, SUBLANES, nkeys)
        return carry

    lax.fori_loop(0, tb, body, 0, unroll=GATES_UNROLL)


def _peer_gates(e, g, nkeys, n_blocks_rows):
    t, slots = e.shape
    tb = 128
    return pl.pallas_call(
        functools.partial(_peer_gates_kernel, nkeys=nkeys),
        grid=(n_blocks_rows * ROW_BLOCK // tb,),
        in_specs=[pl.BlockSpec((tb, slots), lambda i: (i, 0)),
                  pl.BlockSpec((tb, slots), lambda i: (i, 0))],
        out_specs=pl.BlockSpec((nkeys // SUBLANES, tb, SUBLANES, nkeys), lambda i: (0, i, 0, 0)),
        out_shape=jax.ShapeDtypeStruct((nkeys // SUBLANES, t, SUBLANES, nkeys), f32),
        compiler_params=_cparams(1),
        name="peer_gates",
    )(e, g)


def _gelu_tanh(x):
    k0 = -2.0 * 0.7978845608028654
    return x / (1.0 + jnp.exp(x * (k0 + (k0 * 0.044715) * (x * x))))


def _peer_dense_kernel(f_ref, u_ref, v_ref, g_ref, x_ref, m_ref, o_ref, acc_ref, *, gate_row, nkeys):
    j = pl.program_id(1)

    @pl.when(j == 0)
    def _():
        acc_ref[...] = jnp.zeros_like(acc_ref)

    a = jnp.dot(f_ref[...], u_ref[...], preferred_element_type=f32)
    tb = f_ref.shape[0]
    n_first = g_ref.shape[0] // tb
    parts = []
    for k in range(n_first):
        gk = g_ref[pl.ds(k, tb, stride=n_first), :]
        parts.append((_gelu_tanh(a[:, k * nkeys:(k + 1) * nkeys]) * gk).astype(bf16))
    ga = jnp.concatenate(parts, axis=1)
    acc_ref[...] += jnp.dot(ga, v_ref[...], preferred_element_type=f32)

    @pl.when(j == pl.num_programs(1) - 1)
    def _():
        o_ref[...] = x_ref[...] + m_ref[gate_row:gate_row + 1, :] * acc_ref[...]


def _peer_dense(f, u_all, v_all, layer, gates, x, mods, gate_row, n_blocks, bps, n_batch):
    t, d = f.shape
    ne = v_all.shape[1]
    nkeys = gates.shape[3]
    te = SUBLANES * nkeys
    gates = gates.reshape(gates.shape[0], gates.shape[1] * SUBLANES, nkeys)
    return pl.pallas_call(
        functools.partial(_peer_dense_kernel, gate_row=gate_row, nkeys=nkeys),
        grid=(n_blocks, ne // te),
        in_specs=[pl.BlockSpec((ROW_BLOCK, d), lambda i, j: (i, 0)),
                  pl.BlockSpec((None, d, te), lambda i, j: (layer, 0, j)),
                  pl.BlockSpec((None, te, d), lambda i, j: (layer, j, 0)),
                  pl.BlockSpec((None, ROW_BLOCK * SUBLANES, nkeys), lambda i, j: (j, i, 0)),
                  pl.BlockSpec((ROW_BLOCK, d), lambda i, j: (i, 0)),
                  pl.BlockSpec((None, SUBLANES, d), lambda i, j: (_group_of(i, bps, n_batch), 0, 0))],
        out_specs=pl.BlockSpec((ROW_BLOCK, d), lambda i, j: (i, 0)),
        out_shape=jax.ShapeDtypeStruct((t, d), f32),
        scratch_shapes=[pltpu.VMEM((ROW_BLOCK, d), f32)],
        compiler_params=_cparams(2),
        name="peer_dense",
    )(f, u_all, v_all, gates, x, mods)


def kernel(x, c, ctx, c_ctx, mod_w, mod_b, norm_w, final_norm_w, a_wqkv, a_q_gain, a_k_gain, a_wo,
           b_wqkv, b_rpb, b_wo, pool_w, pool_scale, peer_wq, peer_sub_keys, peer_u, peer_v):
    n_batch, s, d = x.shape
    c_len = ctx.shape[1]
    depth = mod_w.shape[0]
    assert s % ROW_BLOCK == 0 and (n_batch * c_len) % ROW_BLOCK == 0 and n_batch * c_len <= s
    assert s % c_len == 0 and c_len % POOL_ROW_BLOCK == 0 and s % GRID_W == 0
    assert a_q_gain.shape[1] == LANES and peer_sub_keys.shape[2] == LANES and peer_sub_keys.shape[3] == LANES
    assert PEER_TOPK & (PEER_TOPK - 1) == 0 and peer_sub_keys.shape[2] & (peer_sub_keys.shape[2] - 1) == 0
    assert depth % N_MIXERS != 0

    bps = s // ROW_BLOCK
    nb_lat = n_batch * bps
    nb_all = nb_lat + n_batch * c_len // ROW_BLOCK
    t_lat = n_batch * s
    q_dim = a_wo.shape[1]
    kv_dim = (a_wqkv.shape[2] - q_dim) // 2
    n_kv = kv_dim // LANES
    groups = q_dim // kv_dim
    b_heads = b_rpb.shape[1]
    win_r, win_c = (b_rpb.shape[2] + 1) // 2, (b_rpb.shape[3] + 1) // 2
    nkeys = peer_sub_keys.shape[2]
    scale = float(LANES) ** -0.5

    xs = jnp.concatenate([x.reshape(t_lat, d), ctx.reshape(n_batch * c_len, d)], axis=0)
    cvec = jnp.concatenate([c, c_ctx[None, :], jnp.zeros((SUBLANES - n_batch - 1, d), f32)], axis=0)
    mods_all = _mods(cvec, mod_w, mod_b)
    tables = _rope_tables(s)
    u_bf, v_bf = jnp.swapaxes(peer_u.astype(bf16), 1, 2), peer_v.astype(bf16)

    for i in range(depth):
        last = i == depth - 1
        kind, j = i % N_MIXERS, i // N_MIXERS
        nb = nb_lat if last else nb_all
        mods = mods_all[i, :n_batch + 1].reshape(n_batch + 1, N_MOD, d)
        mods = jnp.pad(mods, ((0, 0), (0, SUBLANES - N_MOD), (0, 0)))

        if kind == 2:
            h = _norm_mod(xs, norm_w[i, 0], mods, 0, 1, nb_all, bps, n_batch, f32)
            xs = _pool(h, xs, pool_w[j].astype(bf16), pool_scale[j], mods, 2, n_batch, s, c_len)
        else:
            h = _norm_mod(xs, norm_w[i, 0], mods, 0, 1, nb_all, bps, n_batch, bf16)
            if kind == 0:
                qkv = _qkv_gqa(h, a_wqkv[j].astype(bf16), a_q_gain[j], a_k_gain[j], tables, q_dim, kv_dim,
                               nb_all, bps, n_batch)
                o = _attn_lat(qkv, qkv, qkv, 0, q_dim, q_dim + kv_dim, groups, n_kv, n_batch, s, c_len, 1.0, q_dim)
                if not last:
                    o = _attn_ctx(qkv, qkv, qkv, o, 0, q_dim, q_dim + kv_dim, groups, n_kv, n_batch, s, c_len, 1.0)
                wo = a_wo[j]
            else:
                qkv = _mm(h, b_wqkv[j].astype(bf16), nb_all, 512, bf16)
                rows = s // GRID_W
                bias = _natten_bias(b_rpb[j], rows, min(win_r, rows), win_r, win_c)
                o = _natten_lat(qkv, bias, b_heads, n_batch, s, c_len, d)
                if not last:
                    o = _attn_ctx(qkv, qkv, qkv, o, 0, d, 2 * d, 1, b_heads, n_batch, s, c_len, scale)
                wo = b_wo[j]
            xs = _mm_res(o, wo.astype(bf16), xs, mods, 2, nb, bps, n_batch, 512)

        f = _norm_mod(xs, norm_w[i, 1], mods, 3, 4, nb, bps, n_batch, bf16)
        qp = _mm(f, peer_wq[i].astype(bf16), nb, 512, f32)
        e, g = _peer_topk(qp, peer_sub_keys[i], nb)
        gates = _peer_gates(e, g, nkeys, nb)
        xs = _peer_dense(f, u_bf, v_bf, i, gates, xs, mods, 5, nb, bps, n_batch)

    return _final_norm(xs, final_norm_w, nb_lat).reshape(n_batch, s, d)
```

```python
import functools

import numpy as np
import jax
import jax.numpy as jnp
from jax import lax
from jax.experimental import pallas as pl
from jax.experimental.pallas import tpu as pltpu

GRID_W = 64
ROPE_BASE = 10000.0
POOL_WINDOWS = (2, 4, 8, 16)
PEER_TOPK = 16
EPS = 1e-6
N_MIXERS = 3
N_MOD = 6

LANES = 128
SUBLANES = 8
ROW_BLOCK = 512
POOL_ROW_BLOCK = 256
POOL_HALO = 8
ATTN_KV_CHUNK = 4096
NATTEN_UNROLL = 8
GATES_UNROLL = 32
VMEM_LIMIT = 56 * 1024 * 1024
NEG = -1e30
LOG2E = 1.4426950408889634
SM_SCALE_LOG2 = float(LANES) ** -0.5 * LOG2E

f32 = jnp.float32
bf16 = jnp.bfloat16
_NT = (((1,), (1,)), ((), ()))


def _cparams(n_axes):
    return pltpu.CompilerParams(dimension_semantics=("arbitrary",) * n_axes, vmem_limit_bytes=VMEM_LIMIT)


def _group_of(i, blocks_per_seq, n_batch):
    return jnp.minimum(i // blocks_per_seq, n_batch)


def _mods_kernel(cv_ref, w_ref, b_ref, o_ref):
    cv = cv_ref[...]
    a = (cv / (1.0 + jnp.exp(-cv))).astype(bf16)
    o_ref[...] = jnp.dot(a, w_ref[...].astype(bf16), preferred_element_type=f32) + b_ref[...]


def _mods(cvec, mod_w, mod_b):
    depth, d, n = mod_w.shape
    tn = 1024
    return pl.pallas_call(
        _mods_kernel,
        grid=(depth, n // tn),
        in_specs=[pl.BlockSpec((SUBLANES, d), lambda l, j: (0, 0)),
                  pl.BlockSpec((None, d, tn), lambda l, j: (l, 0, j)),
                  pl.BlockSpec((None, 1, tn), lambda l, j: (l, 0, j))],
        out_specs=pl.BlockSpec((None, SUBLANES, tn), lambda l, j: (l, 0, j)),
        out_shape=jax.ShapeDtypeStruct((depth, SUBLANES, n), f32),
        compiler_params=_cparams(2),
        name="adaln_mods",
    )(cvec, mod_w, mod_b.reshape(depth, 1, n))


def _norm_mod_kernel(x_ref, nw_ref, m_ref, o_ref, *, shift_row, scale_row):
    x = x_ref[...]
    y = x * lax.rsqrt(jnp.mean(x * x, axis=-1, keepdims=True) + EPS) * nw_ref[...]
    y = y * (1.0 + m_ref[scale_row:scale_row + 1, :]) + m_ref[shift_row:shift_row + 1, :]
    o_ref[...] = y.astype(o_ref.dtype)


def _norm_mod(x, nw, mods, shift_row, scale_row, n_blocks, bps, n_batch, out_dtype):
    t, d = x.shape
    return pl.pallas_call(
        functools.partial(_norm_mod_kernel, shift_row=shift_row, scale_row=scale_row),
        grid=(n_blocks,),
        in_specs=[pl.BlockSpec((ROW_BLOCK, d), lambda i: (i, 0)),
                  pl.BlockSpec((1, d), lambda i: (0, 0)),
                  pl.BlockSpec((None, SUBLANES, d), lambda i: (_group_of(i, bps, n_batch), 0, 0))],
        out_specs=pl.BlockSpec((ROW_BLOCK, d), lambda i: (i, 0)),
        out_shape=jax.ShapeDtypeStruct((n_blocks * ROW_BLOCK, d), out_dtype),
        compiler_params=_cparams(1),
        name="norm_mod",
    )(x, nw.reshape(1, d), mods)


def _final_norm_kernel(x_ref, nw_ref, o_ref):
    x = x_ref[...]
    o_ref[...] = x * lax.rsqrt(jnp.mean(x * x, axis=-1, keepdims=True) + EPS) * nw_ref[...]


def _final_norm(x, nw, n_blocks):
    t, d = x.shape
    return pl.pallas_call(
        _final_norm_kernel,
        grid=(n_blocks,),
        in_specs=[pl.BlockSpec((ROW_BLOCK, d), lambda i: (i, 0)),
                  pl.BlockSpec((1, d), lambda i: (0, 0))],
        out_specs=pl.BlockSpec((ROW_BLOCK, d), lambda i: (i, 0)),
        out_shape=jax.ShapeDtypeStruct((n_blocks * ROW_BLOCK, d), f32),
        compiler_params=_cparams(1),
        name="final_norm",
    )(x, nw.reshape(1, d))


def _mm_kernel(a_ref, w_ref, o_ref):
    o_ref[...] = jnp.dot(a_ref[...], w_ref[...], preferred_element_type=f32).astype(o_ref.dtype)


def _mm(a, w, n_blocks, tn, out_dtype):
    t, k = a.shape
    n = w.shape[1]
    return pl.pallas_call(
        _mm_kernel,
        grid=(n // tn, n_blocks),
        in_specs=[pl.BlockSpec((ROW_BLOCK, k), lambda j, i: (i, 0)),
                  pl.BlockSpec((k, tn), lambda j, i: (0, j))],
        out_specs=pl.BlockSpec((ROW_BLOCK, tn), lambda j, i: (i, j)),
        out_shape=jax.ShapeDtypeStruct((n_blocks * ROW_BLOCK, n), out_dtype),
        compiler_params=_cparams(2),
        name="matmul",
    )(a, w)


def _mm_res_kernel(a_ref, w_ref, x_ref, m_ref, o_ref, *, gate_row):
    y = jnp.dot(a_ref[...], w_ref[...], preferred_element_type=f32)
    o_ref[...] = x_ref[...] + m_ref[gate_row:gate_row + 1, :] * y


def _mm_res(a, w, x, mods, gate_row, n_blocks, bps, n_batch, tn):
    t, k = a.shape
    n = w.shape[1]
    return pl.pallas_call(
        functools.partial(_mm_res_kernel, gate_row=gate_row),
        grid=(n // tn, n_blocks),
        in_specs=[pl.BlockSpec((ROW_BLOCK, k), lambda j, i: (i, 0)),
                  pl.BlockSpec((k, tn), lambda j, i: (0, j)),
                  pl.BlockSpec((ROW_BLOCK, tn), lambda j, i: (i, j)),
                  pl.BlockSpec((None, SUBLANES, tn), lambda j, i: (_group_of(i, bps, n_batch), 0, j))],
        out_specs=pl.BlockSpec((ROW_BLOCK, tn), lambda j, i: (i, j)),
        out_shape=jax.ShapeDtypeStruct((n_blocks * ROW_BLOCK, n), f32),
        compiler_params=_cparams(2),
        name="matmul_residual",
    )(a, w, x, mods)


def _qkv_gqa_kernel(a_ref, w_ref, qg_ref, kg_ref, cos_ref, sa_ref, sb_ref, o_ref, *, nq_tiles, nk_tiles, scale):
    j = pl.program_id(0)
    acc = jnp.dot(a_ref[...], w_ref[...], preferred_element_type=f32)
    heads = acc.shape[1] // LANES

    def prep(gain, post):
        cos, sa, sb = cos_ref[...], sa_ref[...], sb_ref[...]
        for h in range(heads):
            y = acc[:, h * LANES:(h + 1) * LANES]
            y = y * lax.rsqrt(jnp.mean(y * y, axis=-1, keepdims=True) + EPS) * gain
            y = y * cos + pltpu.roll(y, LANES - 32, 1) * sa + pltpu.roll(y, 32, 1) * sb
            o_ref[:, h * LANES:(h + 1) * LANES] = (y * post).astype(o_ref.dtype)

    @pl.when(j < nq_tiles)
    def _():
        prep(qg_ref[...], scale)

    @pl.when(jnp.logical_and(j >= nq_tiles, j < nq_tiles + nk_tiles))
    def _():
        prep(kg_ref[...], 1.0)

    @pl.when(j >= nq_tiles + nk_tiles)
    def _():
        o_ref[...] = acc.astype(o_ref.dtype)


def _rope_tables(s):
    t = np.arange(s)
    half = LANES // 2
    inv = ROPE_BASE ** (-jnp.arange(0, half, 2, dtype=f32) / half)
    ang_r = (t // GRID_W).astype(np.float32)[:, None] * inv[None, :]
    ang_c = (t % GRID_W).astype(np.float32)[:, None] * inv[None, :]
    ang = jnp.concatenate([ang_r, ang_r, ang_c, ang_c], axis=-1)
    cos, sin = jnp.cos(ang), jnp.sin(ang)
    first = (np.arange(LANES) % half) < (half // 2)
    sa = jnp.where(first[None, :], -sin, 0.0)
    sb = jnp.where(first[None, :], 0.0, sin)
    pad1 = jnp.ones((ROW_BLOCK, LANES), f32)
    pad0 = jnp.zeros((ROW_BLOCK, LANES), f32)
    return (jnp.concatenate([cos, pad1], 0), jnp.concatenate([sa, pad0], 0), jnp.concatenate([sb, pad0], 0))


def _qkv_gqa(h, wqkv, q_gain, k_gain, tables, q_dim, kv_dim, n_blocks, bps, n_batch):
    t, k = h.shape
    n = wqkv.shape[1]
    tn = min(512, kv_dim)
    cos, sa, sb = tables

    def pos_map(j, i):
        return (jnp.where(i < bps * n_batch, i % bps, bps), 0)

    return pl.pallas_call(
        functools.partial(_qkv_gqa_kernel, nq_tiles=q_dim // tn, nk_tiles=kv_dim // tn, scale=SM_SCALE_LOG2),
        grid=(n // tn, n_blocks),
        in_specs=[pl.BlockSpec((ROW_BLOCK, k), lambda j, i: (i, 0)),
                  pl.BlockSpec((k, tn), lambda j, i: (0, j)),
                  pl.BlockSpec((1, LANES), lambda j, i: (0, 0)),
                  pl.BlockSpec((1, LANES), lambda j, i: (0, 0)),
                  pl.BlockSpec((ROW_BLOCK, LANES), pos_map),
                  pl.BlockSpec((ROW_BLOCK, LANES), pos_map),
                  pl.BlockSpec((ROW_BLOCK, LANES), pos_map)],
        out_specs=pl.BlockSpec((ROW_BLOCK, tn), lambda j, i: (i, j)),
        out_shape=jax.ShapeDtypeStruct((t, n), bf16),
        compiler_params=_cparams(2),
        name="qkv_gqa",
    )(h, wqkv, q_gain.reshape(1, LANES), k_gain.reshape(1, LANES), cos, sa, sb)


def _flash_kernel(*refs, groups, n_lat_chunks, ck, scale):
    if n_lat_chunks:
        q_ref, kc_ref, vc_ref, kl_ref, vl_ref, o_ref = refs
    else:
        q_ref, kc_ref, vc_ref, o_ref = refs
    tq = q_ref.shape[0]
    q = jnp.concatenate([q_ref[:, g * LANES:(g + 1) * LANES] for g in range(groups)], axis=0)
    rows = groups * tq

    def step(k, v, m, l, acc):
        s = lax.dot_general(q, k, _NT, preferred_element_type=f32)
        if scale != 1.0:
            s = s * scale
        m_new = jnp.maximum(m, jnp.max(s, axis=-1, keepdims=True))
        p = jnp.exp2(s - m_new)
        alpha = jnp.exp2(m - m_new)
        l = alpha * l + jnp.sum(p, axis=-1, keepdims=True)
        acc = alpha * acc + jnp.dot(p.astype(bf16), v, preferred_element_type=f32)
        return m_new, l, acc

    carry = step(kc_ref[...], vc_ref[...], jnp.full((rows, 1), NEG, f32), jnp.zeros((rows, 1), f32),
                 jnp.zeros((rows, LANES), f32))
    if n_lat_chunks:
        def body(c, carry):
            off = pl.multiple_of(c * ck, ck)
            return step(kl_ref[pl.ds(off, ck), :], vl_ref[pl.ds(off, ck), :], *carry)
        carry = lax.fori_loop(0, n_lat_chunks, body, carry)
    _, l, acc = carry
    o = acc / l
    for g in range(groups):
        o_ref[:, g * LANES:(g + 1) * LANES] = o[g * tq:(g + 1) * tq].astype(o_ref.dtype)


def _attn_lat(q, k, v, q_col0, k_col0, v_col0, groups, n_kv, n_batch, s, c, scale, out_cols):
    t = q.shape[0]
    tq = 128
    ck = min(ATTN_KV_CHUNK, s)
    nqb = s // tq
    ctx_blk0 = n_batch * s // c
    qw = groups * LANES
    return pl.pallas_call(
        functools.partial(_flash_kernel, groups=groups, n_lat_chunks=s // ck, ck=ck, scale=scale),
        grid=(n_batch, n_kv, nqb),
        in_specs=[pl.BlockSpec((tq, qw), lambda b, h, i: (b * nqb + i, q_col0 // qw + h)),
                  pl.BlockSpec((c, LANES), lambda b, h, i: (ctx_blk0 + b, k_col0 // LANES + h)),
                  pl.BlockSpec((c, LANES), lambda b, h, i: (ctx_blk0 + b, v_col0 // LANES + h)),
                  pl.BlockSpec((s, LANES), lambda b, h, i: (b, k_col0 // LANES + h)),
                  pl.BlockSpec((s, LANES), lambda b, h, i: (b, v_col0 // LANES + h))],
        out_specs=pl.BlockSpec((tq, qw), lambda b, h, i: (b * nqb + i, h)),
        out_shape=jax.ShapeDtypeStruct((n_batch * s, out_cols), bf16),
        compiler_params=_cparams(3),
        name="attn_latent",
    )(q, k, v, k, v)


def _attn_ctx(q, k, v, q_col0, k_col0, v_col0, groups, n_kv, n_batch, s, c, scale):
    ctx_blk0 = n_batch * s // c
    qw = groups * LANES
    return pl.pallas_call(
        functools.partial(_flash_kernel, groups=groups, n_lat_chunks=0, ck=0, scale=scale),
        grid=(n_batch, n_kv),
        in_specs=[pl.BlockSpec((c, qw), lambda b, h: (ctx_blk0 + b, q_col0 // qw + h)),
                  pl.BlockSpec((c, LANES), lambda b, h: (ctx_blk0 + b, k_col0 // LANES + h)),
                  pl.BlockSpec((c, LANES), lambda b, h: (ctx_blk0 + b, v_col0 // LANES + h))],
        out_specs=pl.BlockSpec((c, qw), lambda b, h: (b, h)),
        out_shape=jax.ShapeDtypeStruct((n_batch * c, n_kv * qw), bf16),
        compiler_params=_cparams(2),
        name="attn_context",
    )(q, k, v)


def _natten_kernel(q_ref, k_ref, v_ref, kc_ref, vc_ref, bias_ref, o_ref, *, rows, w, wr, scale):
    kc = kc_ref[...]
    vc = vc_ref[...]

    def body(r, carry):
        r0 = jnp.clip(r - wr // 2, 0, rows - wr)
        d = r - r0
        q = q_ref[pl.ds(pl.multiple_of(r * w, w), w), :]
        off = pl.multiple_of(r0 * w, w)
        kb = k_ref[pl.ds(off, wr * w), :]
        vb = v_ref[pl.ds(off, wr * w), :]
        sw = lax.dot_general(q, kb, _NT, preferred_element_type=f32) * scale + bias_ref[d]
        sc = lax.dot_general(q, kc, _NT, preferred_element_type=f32) * scale
        m = jnp.maximum(jnp.max(sw, axis=-1, keepdims=True), jnp.max(sc, axis=-1, keepdims=True))
        pw = jnp.exp2(sw - m)
        pc = jnp.exp2(sc - m)
        l = jnp.sum(pw, axis=-1, keepdims=True) + jnp.sum(pc, axis=-1, keepdims=True)
        o = (jnp.dot(pw.astype(bf16), vb, preferred_element_type=f32)
             + jnp.dot(pc.astype(bf16), vc, preferred_element_type=f32)) / l
        o_ref[pl.ds(pl.multiple_of(r * w, w), w), :] = o.astype(o_ref.dtype)
        return carry

    lax.fori_loop(0, rows, body, 0, unroll=NATTEN_UNROLL)


def _natten_bias(rpb, rows, wr, win_r, win_c):
    w = GRID_W
    cols = np.arange(w)
    col_start = np.clip(cols - win_c // 2, 0, w - win_c)
    inside = (cols[None, :] >= col_start[:, None]) & (cols[None, :] < col_start[:, None] + win_c)
    rpbp = jnp.pad(rpb, ((0, 0), (0, 0), (w - win_c, w - win_c)))
    p = jnp.stack([rpbp[:, :, w - 1 - qc:2 * w - 1 - qc] for qc in range(w)], axis=2)
    p = jnp.where(inside[None, None], p * LOG2E, NEG)
    tab = jnp.stack([p[:, win_r - 1 - d:win_r - 1 - d + wr] for d in range(wr)], axis=1)
    return jnp.transpose(tab, (0, 1, 3, 2, 4)).reshape(rpb.shape[0], wr, w, wr * w)


def _natten_lat(qkv, bias, n_heads, n_batch, s, c, d_model):
    t = qkv.shape[0]
    rows = s // GRID_W
    wr = bias.shape[1]
    ctx_blk0 = n_batch * s // c
    hq, hk, hv = 0, d_model // LANES, 2 * d_model // LANES
    return pl.pallas_call(
        functools.partial(_natten_kernel, rows=rows, w=GRID_W, wr=wr, scale=SM_SCALE_LOG2),
        grid=(n_batch, n_heads),
        in_specs=[pl.BlockSpec((s, LANES), lambda b, h: (b, hq + h)),
                  pl.BlockSpec((s, LANES), lambda b, h: (b, hk + h)),
                  pl.BlockSpec((s, LANES), lambda b, h: (b, hv + h)),
                  pl.BlockSpec((c, LANES), lambda b, h: (ctx_blk0 + b, hk + h)),
                  pl.BlockSpec((c, LANES), lambda b, h: (ctx_blk0 + b, hv + h)),
                  pl.BlockSpec((None, wr, GRID_W, wr * GRID_W), lambda b, h: (h, 0, 0, 0))],
        out_specs=pl.BlockSpec((s, LANES), lambda b, h: (b, h)),
        out_shape=jax.ShapeDtypeStruct((n_batch * s, d_model), bf16),
        compiler_params=_cparams(2),
        name="natten_latent",
    )(qkv, qkv, qkv, qkv, qkv, bias)


def _pool_kernel(h_ref, hp_ref, hn_ref, x_ref, w_ref, ls_ref, m_ref, o_ref, pad_ref, *,
                 bps_lat, bps_ctx, n_lat_blocks, s, c, gate_row):
    i = pl.program_id(0)
    tm = h_ref.shape[0]
    is_lat = i < n_lat_blocks
    blk = jnp.where(is_lat, i % bps_lat, (i - n_lat_blocks) % bps_ctx)
    nblk = jnp.where(is_lat, bps_lat, bps_ctx)
    length = jnp.where(is_lat, s, c)
    pad_ref[0:POOL_HALO, :] = jnp.where(blk == 0, 0.0, hp_ref[...])
    pad_ref[POOL_HALO:POOL_HALO + tm, :] = h_ref[...]
    pad_ref[POOL_HALO + tm:2 * POOL_HALO + tm, :] = jnp.where(blk == nblk - 1, 0.0, hn_ref[...])
    pos = blk * tm + lax.broadcasted_iota(jnp.int32, (tm, 1), 0)
    pg = w_ref.shape[1]
    for g, win in enumerate(POOL_WINDOWS):
        cs = slice(g * pg, (g + 1) * pg)
        lo_off, hi_off = win // 2, win - win // 2
        acc = pad_ref[POOL_HALO - lo_off:POOL_HALO - lo_off + tm, cs]
        for k in range(-lo_off + 1, hi_off):
            acc = acc + pad_ref[POOL_HALO + k:POOL_HALO + k + tm, cs]
        cnt = jnp.minimum(pos + hi_off, length) - jnp.maximum(pos - lo_off, 0)
        y = acc * (1.0 / cnt.astype(f32)) - h_ref[:, cs]
        z = jnp.dot(y.astype(bf16), w_ref[g], preferred_element_type=f32) * ls_ref[:, cs]
        o_ref[:, cs] = x_ref[:, cs] + m_ref[gate_row:gate_row + 1, cs] * z


def _pool(h, x, pool_w, pool_scale, mods, gate_row, n_batch, s, c):
    t, d = h.shape
    tm = POOL_ROW_BLOCK
    nb = t // tm
    bps_lat, bps_ctx = s // tm, c // tm
    hb = tm // POOL_HALO
    last_halo = t // POOL_HALO - 1
    return pl.pallas_call(
        functools.partial(_pool_kernel, bps_lat=bps_lat, bps_ctx=bps_ctx, n_lat_blocks=n_batch * bps_lat,
                          s=s, c=c, gate_row=gate_row),
        grid=(nb,),
        in_specs=[pl.BlockSpec((tm, d), lambda i: (i, 0)),
                  pl.BlockSpec((POOL_HALO, d), lambda i: (jnp.maximum(i * hb - 1, 0), 0)),
                  pl.BlockSpec((POOL_HALO, d), lambda i: (jnp.minimum((i + 1) * hb, last_halo), 0)),
                  pl.BlockSpec((tm, d), lambda i: (i, 0)),
                  pl.BlockSpec(pool_w.shape, lambda i: (0, 0, 0)),
                  pl.BlockSpec((1, d), lambda i: (0, 0)),
                  pl.BlockSpec((None, SUBLANES, d), lambda i: (_group_of(i, bps_lat, n_batch), 0, 0))],
        out_specs=pl.BlockSpec((tm, d), lambda i: (i, 0)),
        out_shape=jax.ShapeDtypeStruct((t, d), f32),
        scratch_shapes=[pltpu.VMEM((tm + 2 * POOL_HALO, d), f32)],
        compiler_params=_cparams(1),
        name="pool_mixer",
    )(h, h, h, x, pool_w, pool_scale.reshape(1, d), mods)


def _topk_rows(s_ref, order, tb):
    rows = lax.broadcasted_iota(jnp.int32, (PEER_TOPK, tb), 0)

    def body(k, carry):
        vals, idxs = carry
        s = s_ref[...]
        m = jnp.max(s, axis=0, keepdims=True)
        am = jnp.min(jnp.where(s == m, order, float(2 ** 23)), axis=0, keepdims=True)
        s_ref[...] = jnp.where(order == am, NEG, s)
        return jnp.where(rows == k, m, vals), jnp.where(rows == k, am, idxs)

    vals, idxs = lax.fori_loop(0, PEER_TOPK, body, (jnp.zeros((PEER_TOPK, tb), f32), jnp.zeros((PEER_TOPK, tb), f32)))
    return vals, idxs.astype(jnp.int32)


def _cand_blocks():
    blocks = [(0, PEER_TOPK)] + [(a, SUBLANES) for a in range(1, SUBLANES)]
    assert all((a + 1) * (nb + 1) > PEER_TOPK for a, nb in blocks[1:]) and 2 * SUBLANES == PEER_TOPK
    return blocks


def _peer_topk_kernel(q_ref, sk_ref, e_ref, g_ref, s_ref, cand_ref, es_ref, gs_ref, *, n_heads, nkeys):
    tb = q_ref.shape[0]
    key_order = lax.broadcasted_iota(jnp.int32, (nkeys, tb), 0).astype(f32)
    blocks = _cand_blocks()
    n_mid = (len(blocks) - 1) * SUBLANES
    r = lax.broadcasted_iota(jnp.int32, cand_ref.shape, 0)
    rm = r - PEER_TOPK
    mid = (lax.shift_right_logical(rm, 3) + 1) * PEER_TOPK + lax.bitwise_and(rm, SUBLANES - 1)
    tail = (rm - n_mid + SUBLANES) * PEER_TOPK
    cand_order = jnp.where(r < PEER_TOPK, r, jnp.where(rm < n_mid, mid, tail)).astype(f32)

    def head(h, carry):
        col = pl.multiple_of(h * 2 * LANES, 2 * LANES)
        s_ref[...] = lax.dot_general(sk_ref[0], q_ref[:, pl.ds(col, LANES)], _NT, preferred_element_type=f32)
        v1, i1 = _topk_rows(s_ref, key_order, tb)
        s_ref[...] = lax.dot_general(sk_ref[1], q_ref[:, pl.ds(col + LANES, LANES)], _NT, preferred_element_type=f32)
        v2, i2 = _topk_rows(s_ref, key_order, tb)
        r0 = 0
        for a, nb in blocks:
            cand_ref[r0:r0 + nb, :] = v1[a:a + 1, :] + v2[:nb]
            r0 += nb
        cand_ref[r0:r0 + SUBLANES, :] = v1[SUBLANES:] + v2[0:1, :]
        sc, ci = _topk_rows(cand_ref, cand_order, tb)
        ca = lax.shift_right_logical(ci, PEER_TOPK.bit_length() - 1)
        cb = lax.bitwise_and(ci, PEER_TOPK - 1)
        e1 = jnp.zeros((PEER_TOPK, tb), jnp.int32)
        e2 = jnp.zeros((PEER_TOPK, tb), jnp.int32)
        for a in range(PEER_TOPK):
            e1 = jnp.where(ca == a, i1[a:a + 1, :], e1)
            e2 = jnp.where(cb == a, i2[a:a + 1, :], e2)
        p = jnp.exp(sc - sc[0:1, :])
        gate = p / jnp.sum(p, axis=0, keepdims=True)
        row = pl.multiple_of(h * PEER_TOPK, PEER_TOPK)
        es_ref[pl.ds(row, PEER_TOPK), :] = e1 * nkeys + e2
        gs_ref[pl.ds(row, PEER_TOPK), :] = gate
        return carry

    lax.fori_loop(0, n_heads, head, 0)
    e_ref[...] = es_ref[...].T
    g_ref[...] = gs_ref[...].T


def _peer_topk(q, sub_keys, n_blocks_rows):
    t, qd = q.shape
    nkeys = sub_keys.shape[1]
    n_heads = qd // (2 * LANES)
    slots = n_heads * PEER_TOPK
    tb = 256
    return pl.pallas_call(
        functools.partial(_peer_topk_kernel, n_heads=n_heads, nkeys=nkeys),
        grid=(n_blocks_rows * ROW_BLOCK // tb,),
        in_specs=[pl.BlockSpec((tb, qd), lambda i: (i, 0)),
                  pl.BlockSpec(sub_keys.shape, lambda i: (0, 0, 0))],
        out_specs=[pl.BlockSpec((tb, slots), lambda i: (i, 0)),
                   pl.BlockSpec((tb, slots), lambda i: (i, 0))],
        out_shape=[jax.ShapeDtypeStruct((n_blocks_rows * ROW_BLOCK, slots), jnp.int32),
                   jax.ShapeDtypeStruct((n_blocks_rows * ROW_BLOCK, slots), f32)],
        scratch_shapes=[pltpu.VMEM((nkeys, tb), f32),
                        pltpu.VMEM((sum(nb for _, nb in _cand_blocks()) + SUBLANES, tb), f32),
                        pltpu.VMEM((slots, tb), jnp.int32), pltpu.VMEM((slots, tb), f32)],
        compiler_params=_cparams(1),
        name="peer_topk",
    )(q, sub_keys)


def _peer_gates_kernel(e_ref, g_ref, o_ref, *, nkeys):
    tb, slots = e_ref.shape
    iota = lax.broadcasted_iota(jnp.int32, (nkeys, slots), 0)

    def body(t, carry):
        e = e_ref[pl.ds(t, 1), :]
        g = g_ref[pl.ds(t, 1), :]
        e1 = lax.shift_right_logical(e, nkeys.bit_length() - 1)
        e2 = lax.bitwise_and(e, nkeys - 1)
        w1 = jnp.where(e1 == iota, g, 0.0).astype(bf16)
        o2 = jnp.where(e2 == iota, 1.0, 0.0).astype(bf16)
        res = lax.dot_general(w1, o2, _NT, preferred_element_type=f32)
        o_ref[:, t] = res.reshape(nkeys // SUBLANES, SUBLANES, nkeys)
        return carry

    lax.fori_loop(0, tb, body, 0, unroll=GATES_UNROLL)


def _peer_gates(e, g, nkeys, n_blocks_rows):
    t, slots = e.shape
    tb = 128
    return pl.pallas_call(
        functools.partial(_peer_gates_kernel, nkeys=nkeys),
        grid=(n_blocks_rows * ROW_BLOCK // tb,),
        in_specs=[pl.BlockSpec((tb, slots), lambda i: (i, 0)),
                  pl.BlockSpec((tb, slots), lambda i: (i, 0))],
        out_specs=pl.BlockSpec((nkeys // SUBLANES, tb, SUBLANES, nkeys), lambda i: (0, i, 0, 0)),
        out_shape=jax.ShapeDtypeStruct((nkeys // SUBLANES, t, SUBLANES, nkeys), f32),
        compiler_params=_cparams(1),
        name="peer_gates",
    )(e, g)


def _gelu_tanh(x):
    k0 = -2.0 * 0.7978845608028654 * 1.4426950408889634
    return x / (1.0 + jnp.exp2(x * (k0 + (k0 * 0.044715) * (x * x))))


def _peer_dense_kernel(f_ref, u_ref, v_ref, g_ref, x_ref, m_ref, o_ref, acc_ref, *, gate_row, nkeys):
    j = pl.program_id(1)

    @pl.when(j == 0)
    def _():
        acc_ref[...] = jnp.zeros_like(acc_ref)

    a = jnp.dot(f_ref[...], u_ref[...], preferred_element_type=f32)
    tb = f_ref.shape[0]
    n_first = g_ref.shape[0] // tb
    parts = []
    for k in range(n_first):
        gk = g_ref[pl.ds(k, tb, stride=n_first), :]
        parts.append((_gelu_tanh(a[:, k * nkeys:(k + 1) * nkeys]) * gk).astype(bf16))
    ga = jnp.concatenate(parts, axis=1)
    acc_ref[...] += jnp.dot(ga, v_ref[...], preferred_element_type=f32)

    @pl.when(j == pl.num_programs(1) - 1)
    def _():
        o_ref[...] = x_ref[...] + m_ref[gate_row:gate_row + 1, :] * acc_ref[...]


def _peer_dense(f, u_all, v_all, layer, gates, x, mods, gate_row, n_blocks, bps, n_batch):
    t, d = f.shape
    ne = v_all.shape[1]
    nkeys = gates.shape[3]
    te = SUBLANES * nkeys
    gates = gates.reshape(gates.shape[0], gates.shape[1] * SUBLANES, nkeys)
    return pl.pallas_call(
        functools.partial(_peer_dense_kernel, gate_row=gate_row, nkeys=nkeys),
        grid=(n_blocks, ne // te),
        in_specs=[pl.BlockSpec((ROW_BLOCK, d), lambda i, j: (i, 0)),
                  pl.BlockSpec((None, d, te), lambda i, j: (layer, 0, j)),
                  pl.BlockSpec((None, te, d), lambda i, j: (layer, j, 0)),
                  pl.BlockSpec((None, ROW_BLOCK * SUBLANES, nkeys), lambda i, j: (j, i, 0)),
                  pl.BlockSpec((ROW_BLOCK, d), lambda i, j: (i, 0)),
                  pl.BlockSpec((None, SUBLANES, d), lambda i, j: (_group_of(i, bps, n_batch), 0, 0))],
        out_specs=pl.BlockSpec((ROW_BLOCK, d), lambda i, j: (i, 0)),
        out_shape=jax.ShapeDtypeStruct((t, d), f32),
        scratch_shapes=[pltpu.VMEM((ROW_BLOCK, d), f32)],
        compiler_params=_cparams(2),
        name="peer_dense",
    )(f, u_all, v_all, gates, x, mods)


def kernel(x, c, ctx, c_ctx, mod_w, mod_b, norm_w, final_norm_w, a_wqkv, a_q_gain, a_k_gain, a_wo,
           b_wqkv, b_rpb, b_wo, pool_w, pool_scale, peer_wq, peer_sub_keys, peer_u, peer_v):
    n_batch, s, d = x.shape
    c_len = ctx.shape[1]
    depth = mod_w.shape[0]
    assert s % ROW_BLOCK == 0 and (n_batch * c_len) % ROW_BLOCK == 0 and n_batch * c_len <= s
    assert s % c_len == 0 and c_len % POOL_ROW_BLOCK == 0 and s % GRID_W == 0
    assert a_q_gain.shape[1] == LANES and peer_sub_keys.shape[2] == LANES and peer_sub_keys.shape[3] == LANES
    assert PEER_TOPK & (PEER_TOPK - 1) == 0 and peer_sub_keys.shape[2] & (peer_sub_keys.shape[2] - 1) == 0
    assert depth % N_MIXERS != 0

    bps = s // ROW_BLOCK
    nb_lat = n_batch * bps
    nb_all = nb_lat + n_batch * c_len // ROW_BLOCK
    t_lat = n_batch * s
    q_dim = a_wo.shape[1]
    kv_dim = (a_wqkv.shape[2] - q_dim) // 2
    n_kv = kv_dim // LANES
    groups = q_dim // kv_dim
    b_heads = b_rpb.shape[1]
    win_r, win_c = (b_rpb.shape[2] + 1) // 2, (b_rpb.shape[3] + 1) // 2
    nkeys = peer_sub_keys.shape[2]
    scale = SM_SCALE_LOG2

    xs = jnp.concatenate([x.reshape(t_lat, d), ctx.reshape(n_batch * c_len, d)], axis=0)
    cvec = jnp.concatenate([c, c_ctx[None, :], jnp.zeros((SUBLANES - n_batch - 1, d), f32)], axis=0)
    mods_all = _mods(cvec, mod_w, mod_b)
    tables = _rope_tables(s)
    u_bf, v_bf = jnp.swapaxes(peer_u.astype(bf16), 1, 2), peer_v.astype(bf16)

    for i in range(depth):
        last = i == depth - 1
        kind, j = i % N_MIXERS, i // N_MIXERS
        nb = nb_lat if last else nb_all
        mods = mods_all[i, :n_batch + 1].reshape(n_batch + 1, N_MOD, d)
        mods = jnp.pad(mods, ((0, 0), (0, SUBLANES - N_MOD), (0, 0)))

        if kind == 2:
            h = _norm_mod(xs, norm_w[i, 0], mods, 0, 1, nb_all, bps, n_batch, f32)
            xs = _pool(h, xs, pool_w[j].astype(bf16), pool_scale[j], mods, 2, n_batch, s, c_len)
        else:
            h = _norm_mod(xs, norm_w[i, 0], mods, 0, 1, nb_all, bps, n_batch, bf16)
            if kind == 0:
                qkv = _qkv_gqa(h, a_wqkv[j].astype(bf16), a_q_gain[j], a_k_gain[j], tables, q_dim, kv_dim,
                               nb_all, bps, n_batch)
                o = _attn_lat(qkv, qkv, qkv, 0, q_dim, q_dim + kv_dim, groups, n_kv, n_batch, s, c_len, 1.0, q_dim)
                if not last:
                    o_ctx = _attn_ctx(qkv, qkv, qkv, 0, q_dim, q_dim + kv_dim, groups, n_kv, n_batch, s, c_len, 1.0)
                    o = jnp.concatenate([o, o_ctx], axis=0)
                wo = a_wo[j]
            else:
                qkv = _mm(h, b_wqkv[j].astype(bf16), nb_all, 512, bf16)
                rows = s // GRID_W
                bias = _natten_bias(b_rpb[j], rows, min(win_r, rows), win_r, win_c)
                o = _natten_lat(qkv, bias, b_heads, n_batch, s, c_len, d)
                if not last:
                    o_ctx = _attn_ctx(qkv, qkv, qkv, 0, d, 2 * d, 1, b_heads, n_batch, s, c_len, scale)
                    o = jnp.concatenate([o, o_ctx], axis=0)
                wo = b_wo[j]
            xs = _mm_res(o, wo.astype(bf16), xs, mods, 2, nb, bps, n_batch, 512)

        f = _norm_mod(xs, norm_w[i, 1], mods, 3, 4, nb, bps, n_batch, bf16)
        qp = _mm(f, peer_wq[i].astype(bf16), nb, 512, f32)
        e, g = _peer_topk(qp, peer_sub_keys[i], nb)
        gates = _peer_gates(e, g, nkeys, nb)
        xs = _peer_dense(f, u_bf, v_bf, i, gates, xs, mods, 5, nb, bps, n_batch)

    return _final_norm(xs, final_norm_w, nb_lat).reshape(n_batch, s, d)
```

```python
import functools

import numpy as np
import jax
import jax.numpy as jnp
from jax import lax
from jax.experimental import pallas as pl
from jax.experimental.pallas import tpu as pltpu

GRID_W = 64
ROPE_BASE = 10000.0
POOL_WINDOWS = (2, 4, 8, 16)
PEER_TOPK = 16
EPS = 1e-6
N_MIXERS = 3
N_MOD = 6

LANES = 128
SUBLANES = 8
ROW_BLOCK = 512
POOL_ROW_BLOCK = 256
POOL_HALO = 8
MM_COL_TILE = 2048
ATTN_KV_CHUNK = 4096
NATTEN_UNROLL = 8
GATES_UNROLL = 32
VMEM_LIMIT = 56 * 1024 * 1024
NEG = -1e30
LOG2E = 1.4426950408889634
SM_SCALE_LOG2 = float(LANES) ** -0.5 * LOG2E

f32 = jnp.float32
bf16 = jnp.bfloat16
_NT = (((1,), (1,)), ((), ()))


def _cparams(n_axes):
    return pltpu.CompilerParams(dimension_semantics=("arbitrary",) * n_axes, vmem_limit_bytes=VMEM_LIMIT)


def _group_of(i, blocks_per_seq, n_batch):
    return jnp.minimum(i // blocks_per_seq, n_batch)


def _mods_kernel(cv_ref, w_ref, b_ref, o_ref):
    cv = cv_ref[...]
    a = (cv / (1.0 + jnp.exp(-cv))).astype(bf16)
    o_ref[...] = jnp.dot(a, w_ref[...].astype(bf16), preferred_element_type=f32) + b_ref[...]


def _mods(cvec, mod_w, mod_b):
    depth, d, n = mod_w.shape
    tn = 1024
    return pl.pallas_call(
        _mods_kernel,
        grid=(depth, n // tn),
        in_specs=[pl.BlockSpec((SUBLANES, d), lambda l, j: (0, 0)),
                  pl.BlockSpec((None, d, tn), lambda l, j: (l, 0, j)),
                  pl.BlockSpec((None, 1, tn), lambda l, j: (l, 0, j))],
        out_specs=pl.BlockSpec((None, SUBLANES, tn), lambda l, j: (l, 0, j)),
        out_shape=jax.ShapeDtypeStruct((depth, SUBLANES, n), f32),
        compiler_params=_cparams(2),
        name="adaln_mods",
    )(cvec, mod_w, mod_b.reshape(depth, 1, n))


def _norm_mod_kernel(x_ref, nw_ref, m_ref, o_ref, *, shift_row, scale_row):
    x = x_ref[...]
    y = x * lax.rsqrt(jnp.mean(x * x, axis=-1, keepdims=True) + EPS) * nw_ref[...]
    y = y * (1.0 + m_ref[scale_row:scale_row + 1, :]) + m_ref[shift_row:shift_row + 1, :]
    o_ref[...] = y.astype(o_ref.dtype)


def _norm_mod(x, nw, mods, shift_row, scale_row, n_blocks, bps, n_batch, out_dtype):
    t, d = x.shape
    return pl.pallas_call(
        functools.partial(_norm_mod_kernel, shift_row=shift_row, scale_row=scale_row),
        grid=(n_blocks,),
        in_specs=[pl.BlockSpec((ROW_BLOCK, d), lambda i: (i, 0)),
                  pl.BlockSpec((1, d), lambda i: (0, 0)),
                  pl.BlockSpec((None, SUBLANES, d), lambda i: (_group_of(i, bps, n_batch), 0, 0))],
        out_specs=pl.BlockSpec((ROW_BLOCK, d), lambda i: (i, 0)),
        out_shape=jax.ShapeDtypeStruct((n_blocks * ROW_BLOCK, d), out_dtype),
        compiler_params=_cparams(1),
        name="norm_mod",
    )(x, nw.reshape(1, d), mods)


def _final_norm_kernel(x_ref, nw_ref, o_ref):
    x = x_ref[...]
    o_ref[...] = x * lax.rsqrt(jnp.mean(x * x, axis=-1, keepdims=True) + EPS) * nw_ref[...]


def _final_norm(x, nw, n_blocks):
    t, d = x.shape
    return pl.pallas_call(
        _final_norm_kernel,
        grid=(n_blocks,),
        in_specs=[pl.BlockSpec((ROW_BLOCK, d), lambda i: (i, 0)),
                  pl.BlockSpec((1, d), lambda i: (0, 0))],
        out_specs=pl.BlockSpec((ROW_BLOCK, d), lambda i: (i, 0)),
        out_shape=jax.ShapeDtypeStruct((n_blocks * ROW_BLOCK, d), f32),
        compiler_params=_cparams(1),
        name="final_norm",
    )(x, nw.reshape(1, d))


def _mm_kernel(a_ref, w_ref, o_ref):
    o_ref[...] = jnp.dot(a_ref[...], w_ref[...], preferred_element_type=f32).astype(o_ref.dtype)


def _col_tile(n):
    return max(t for t in range(LANES, min(n, MM_COL_TILE) + 1, LANES) if n % t == 0)


def _mm(a, w, n_blocks, tn, out_dtype):
    t, k = a.shape
    n = w.shape[1]
    return pl.pallas_call(
        _mm_kernel,
        grid=(n // tn, n_blocks),
        in_specs=[pl.BlockSpec((ROW_BLOCK, k), lambda j, i: (i, 0)),
                  pl.BlockSpec((k, tn), lambda j, i: (0, j))],
        out_specs=pl.BlockSpec((ROW_BLOCK, tn), lambda j, i: (i, j)),
        out_shape=jax.ShapeDtypeStruct((n_blocks * ROW_BLOCK, n), out_dtype),
        compiler_params=_cparams(2),
        name="matmul",
    )(a, w)


def _mm_res_kernel(a_ref, w_ref, x_ref, m_ref, o_ref, *, gate_row):
    y = jnp.dot(a_ref[...], w_ref[...], preferred_element_type=f32)
    o_ref[...] = x_ref[...] + m_ref[gate_row:gate_row + 1, :] * y


def _mm_res(a, w, x, mods, gate_row, n_blocks, bps, n_batch, tn):
    t, k = a.shape
    n = w.shape[1]
    return pl.pallas_call(
        functools.partial(_mm_res_kernel, gate_row=gate_row),
        grid=(n // tn, n_blocks),
        in_specs=[pl.BlockSpec((ROW_BLOCK, k), lambda j, i: (i, 0)),
                  pl.BlockSpec((k, tn), lambda j, i: (0, j)),
                  pl.BlockSpec((ROW_BLOCK, tn), lambda j, i: (i, j)),
                  pl.BlockSpec((None, SUBLANES, tn), lambda j, i: (_group_of(i, bps, n_batch), 0, j))],
        out_specs=pl.BlockSpec((ROW_BLOCK, tn), lambda j, i: (i, j)),
        out_shape=jax.ShapeDtypeStruct((n_blocks * ROW_BLOCK, n), f32),
        compiler_params=_cparams(2),
        name="matmul_residual",
    )(a, w, x, mods)


def _qkv_gqa_kernel(a_ref, w_ref, qg_ref, kg_ref, cos_ref, sa_ref, sb_ref, o_ref, *, nq_tiles, nk_tiles, scale):
    j = pl.program_id(0)
    acc = jnp.dot(a_ref[...], w_ref[...], preferred_element_type=f32)
    heads = acc.shape[1] // LANES

    def prep(gain, post):
        cos, sa, sb = cos_ref[...], sa_ref[...], sb_ref[...]
        for h in range(heads):
            y = acc[:, h * LANES:(h + 1) * LANES]
            y = y * lax.rsqrt(jnp.mean(y * y, axis=-1, keepdims=True) + EPS) * gain
            y = y * cos + pltpu.roll(y, LANES - 32, 1) * sa + pltpu.roll(y, 32, 1) * sb
            o_ref[:, h * LANES:(h + 1) * LANES] = (y * post).astype(o_ref.dtype)

    @pl.when(j < nq_tiles)
    def _():
        prep(qg_ref[...], scale)

    @pl.when(jnp.logical_and(j >= nq_tiles, j < nq_tiles + nk_tiles))
    def _():
        prep(kg_ref[...], 1.0)

    @pl.when(j >= nq_tiles + nk_tiles)
    def _():
        o_ref[...] = acc.astype(o_ref.dtype)


def _rope_tables(s):
    t = np.arange(s)
    half = LANES // 2
    inv = ROPE_BASE ** (-jnp.arange(0, half, 2, dtype=f32) / half)
    ang_r = (t // GRID_W).astype(np.float32)[:, None] * inv[None, :]
    ang_c = (t % GRID_W).astype(np.float32)[:, None] * inv[None, :]
    ang = jnp.concatenate([ang_r, ang_r, ang_c, ang_c], axis=-1)
    cos, sin = jnp.cos(ang), jnp.sin(ang)
    first = (np.arange(LANES) % half) < (half // 2)
    sa = jnp.where(first[None, :], -sin, 0.0)
    sb = jnp.where(first[None, :], 0.0, sin)
    pad1 = jnp.ones((ROW_BLOCK, LANES), f32)
    pad0 = jnp.zeros((ROW_BLOCK, LANES), f32)
    return (jnp.concatenate([cos, pad1], 0), jnp.concatenate([sa, pad0], 0), jnp.concatenate([sb, pad0], 0))


def _qkv_gqa(h, wqkv, q_gain, k_gain, tables, q_dim, kv_dim, n_blocks, bps, n_batch):
    t, k = h.shape
    n = wqkv.shape[1]
    tn = min(512, kv_dim)
    cos, sa, sb = tables

    def pos_map(j, i):
        return (jnp.where(i < bps * n_batch, i % bps, bps), 0)

    return pl.pallas_call(
        functools.partial(_qkv_gqa_kernel, nq_tiles=q_dim // tn, nk_tiles=kv_dim // tn, scale=SM_SCALE_LOG2),
        grid=(n // tn, n_blocks),
        in_specs=[pl.BlockSpec((ROW_BLOCK, k), lambda j, i: (i, 0)),
                  pl.BlockSpec((k, tn), lambda j, i: (0, j)),
                  pl.BlockSpec((1, LANES), lambda j, i: (0, 0)),
                  pl.BlockSpec((1, LANES), lambda j, i: (0, 0)),
                  pl.BlockSpec((ROW_BLOCK, LANES), pos_map),
                  pl.BlockSpec((ROW_BLOCK, LANES), pos_map),
                  pl.BlockSpec((ROW_BLOCK, LANES), pos_map)],
        out_specs=pl.BlockSpec((ROW_BLOCK, tn), lambda j, i: (i, j)),
        out_shape=jax.ShapeDtypeStruct((t, n), bf16),
        compiler_params=_cparams(2),
        name="qkv_gqa",
    )(h, wqkv, q_gain.reshape(1, LANES), k_gain.reshape(1, LANES), cos, sa, sb)


def _flash_kernel(*refs, groups, n_lat_chunks, ck, scale):
    if n_lat_chunks:
        q_ref, kc_ref, vc_ref, kl_ref, vl_ref, o_ref = refs
    else:
        q_ref, kc_ref, vc_ref, o_ref = refs
    tq = q_ref.shape[0]
    q = jnp.concatenate([q_ref[:, g * LANES:(g + 1) * LANES] for g in range(groups)], axis=0)
    rows = groups * tq

    def step(k, v, m, l, acc):
        s = lax.dot_general(q, k, _NT, preferred_element_type=f32)
        if scale != 1.0:
            s = s * scale
        m_new = jnp.maximum(m, jnp.max(s, axis=-1, keepdims=True))
        p = jnp.exp2(s - m_new)
        alpha = jnp.exp2(m - m_new)
        l = alpha * l + jnp.sum(p, axis=-1, keepdims=True)
        acc = alpha * acc + jnp.dot(p.astype(bf16), v, preferred_element_type=f32)
        return m_new, l, acc

    carry = step(kc_ref[...], vc_ref[...], jnp.full((rows, 1), NEG, f32), jnp.zeros((rows, 1), f32),
                 jnp.zeros((rows, LANES), f32))
    if n_lat_chunks:
        def body(c, carry):
            off = pl.multiple_of(c * ck, ck)
            return step(kl_ref[pl.ds(off, ck), :], vl_ref[pl.ds(off, ck), :], *carry)
        carry = lax.fori_loop(0, n_lat_chunks, body, carry)
    _, l, acc = carry
    o = acc / l
    for g in range(groups):
        o_ref[:, g * LANES:(g + 1) * LANES] = o[g * tq:(g + 1) * tq].astype(o_ref.dtype)


def _attn_lat(q, k, v, q_col0, k_col0, v_col0, groups, n_kv, n_batch, s, c, scale, out_cols):
    t = q.shape[0]
    tq = 128
    ck = min(ATTN_KV_CHUNK, s)
    nqb = s // tq
    ctx_blk0 = n_batch * s // c
    qw = groups * LANES
    return pl.pallas_call(
        functools.partial(_flash_kernel, groups=groups, n_lat_chunks=s // ck, ck=ck, scale=scale),
        grid=(n_batch, n_kv, nqb),
        in_specs=[pl.BlockSpec((tq, qw), lambda b, h, i: (b * nqb + i, q_col0 // qw + h)),
                  pl.BlockSpec((c, LANES), lambda b, h, i: (ctx_blk0 + b, k_col0 // LANES + h)),
                  pl.BlockSpec((c, LANES), lambda b, h, i: (ctx_blk0 + b, v_col0 // LANES + h)),
                  pl.BlockSpec((s, LANES), lambda b, h, i: (b, k_col0 // LANES + h)),
                  pl.BlockSpec((s, LANES), lambda b, h, i: (b, v_col0 // LANES + h))],
        out_specs=pl.BlockSpec((tq, qw), lambda b, h, i: (b * nqb + i, h)),
        out_shape=jax.ShapeDtypeStruct((n_batch * s, out_cols), bf16),
        compiler_params=_cparams(3),
        name="attn_latent",
    )(q, k, v, k, v)


def _attn_ctx(q, k, v, q_col0, k_col0, v_col0, groups, n_kv, n_batch, s, c, scale):
    ctx_blk0 = n_batch * s // c
    qw = groups * LANES
    return pl.pallas_call(
        functools.partial(_flash_kernel, groups=groups, n_lat_chunks=0, ck=0, scale=scale),
        grid=(n_batch, n_kv),
        in_specs=[pl.BlockSpec((c, qw), lambda b, h: (ctx_blk0 + b, q_col0 // qw + h)),
                  pl.BlockSpec((c, LANES), lambda b, h: (ctx_blk0 + b, k_col0 // LANES + h)),
                  pl.BlockSpec((c, LANES), lambda b, h: (ctx_blk0 + b, v_col0 // LANES + h))],
        out_specs=pl.BlockSpec((c, qw), lambda b, h: (b, h)),
        out_shape=jax.ShapeDtypeStruct((n_batch * c, n_kv * qw), bf16),
        compiler_params=_cparams(2),
        name="attn_context",
    )(q, k, v)


def _natten_kernel(q_ref, k_ref, v_ref, kc_ref, vc_ref, bias_ref, o_ref, *, rows, w, wr, scale):
    kc = kc_ref[...]
    vc = vc_ref[...]

    def body(r, carry):
        r0 = jnp.clip(r - wr // 2, 0, rows - wr)
        d = r - r0
        q = q_ref[pl.ds(pl.multiple_of(r * w, w), w), :]
        off = pl.multiple_of(r0 * w, w)
        kb = k_ref[pl.ds(off, wr * w), :]
        vb = v_ref[pl.ds(off, wr * w), :]
        sw = lax.dot_general(q, kb, _NT, preferred_element_type=f32) * scale + bias_ref[d]
        sc = lax.dot_general(q, kc, _NT, preferred_element_type=f32) * scale
        m = jnp.maximum(jnp.max(sw, axis=-1, keepdims=True), jnp.max(sc, axis=-1, keepdims=True))
        pw = jnp.exp2(sw - m)
        pc = jnp.exp2(sc - m)
        l = jnp.sum(pw, axis=-1, keepdims=True) + jnp.sum(pc, axis=-1, keepdims=True)
        o = (jnp.dot(pw.astype(bf16), vb, preferred_element_type=f32)
             + jnp.dot(pc.astype(bf16), vc, preferred_element_type=f32)) / l
        o_ref[pl.ds(pl.multiple_of(r * w, w), w), :] = o.astype(o_ref.dtype)
        return carry

    lax.fori_loop(0, rows, body, 0, unroll=NATTEN_UNROLL)


def _natten_bias(rpb, rows, wr, win_r, win_c):
    w = GRID_W
    cols = np.arange(w)
    col_start = np.clip(cols - win_c // 2, 0, w - win_c)
    inside = (cols[None, :] >= col_start[:, None]) & (cols[None, :] < col_start[:, None] + win_c)
    rpbp = jnp.pad(rpb, ((0, 0), (0, 0), (w - win_c, w - win_c)))
    p = jnp.stack([rpbp[:, :, w - 1 - qc:2 * w - 1 - qc] for qc in range(w)], axis=2)
    p = jnp.where(inside[None, None], p * LOG2E, NEG)
    tab = jnp.stack([p[:, win_r - 1 - d:win_r - 1 - d + wr] for d in range(wr)], axis=1)
    return jnp.transpose(tab, (0, 1, 3, 2, 4)).reshape(rpb.shape[0], wr, w, wr * w)


def _natten_lat(qkv, bias, n_heads, n_batch, s, c, d_model):
    t = qkv.shape[0]
    rows = s // GRID_W
    wr = bias.shape[1]
    ctx_blk0 = n_batch * s // c
    hq, hk, hv = 0, d_model // LANES, 2 * d_model // LANES
    return pl.pallas_call(
        functools.partial(_natten_kernel, rows=rows, w=GRID_W, wr=wr, scale=SM_SCALE_LOG2),
        grid=(n_batch, n_heads),
        in_specs=[pl.BlockSpec((s, LANES), lambda b, h: (b, hq + h)),
                  pl.BlockSpec((s, LANES), lambda b, h: (b, hk + h)),
                  pl.BlockSpec((s, LANES), lambda b, h: (b, hv + h)),
                  pl.BlockSpec((c, LANES), lambda b, h: (ctx_blk0 + b, hk + h)),
                  pl.BlockSpec((c, LANES), lambda b, h: (ctx_blk0 + b, hv + h)),
                  pl.BlockSpec((None, wr, GRID_W, wr * GRID_W), lambda b, h: (h, 0, 0, 0))],
        out_specs=pl.BlockSpec((s, LANES), lambda b, h: (b, h)),
        out_shape=jax.ShapeDtypeStruct((n_batch * s, d_model), bf16),
        compiler_params=_cparams(2),
        name="natten_latent",
    )(qkv, qkv, qkv, qkv, qkv, bias)


def _pool_kernel(h_ref, hp_ref, hn_ref, x_ref, w_ref, ls_ref, m_ref, o_ref, pad_ref, *,
                 bps_lat, bps_ctx, n_lat_blocks, s, c, gate_row):
    i = pl.program_id(0)
    tm = h_ref.shape[0]
    is_lat = i < n_lat_blocks
    blk = jnp.where(is_lat, i % bps_lat, (i - n_lat_blocks) % bps_ctx)
    nblk = jnp.where(is_lat, bps_lat, bps_ctx)
    length = jnp.where(is_lat, s, c)
    pad_ref[0:POOL_HALO, :] = jnp.where(blk == 0, 0.0, hp_ref[...])
    pad_ref[POOL_HALO:POOL_HALO + tm, :] = h_ref[...]
    pad_ref[POOL_HALO + tm:2 * POOL_HALO + tm, :] = jnp.where(blk == nblk - 1, 0.0, hn_ref[...])
    pos = blk * tm + lax.broadcasted_iota(jnp.int32, (tm, 1), 0)
    pg = w_ref.shape[1]
    for g, win in enumerate(POOL_WINDOWS):
        cs = slice(g * pg, (g + 1) * pg)
        lo_off, hi_off = win // 2, win - win // 2
        acc = pad_ref[POOL_HALO - lo_off:POOL_HALO - lo_off + tm, cs]
        for k in range(-lo_off + 1, hi_off):
            acc = acc + pad_ref[POOL_HALO + k:POOL_HALO + k + tm, cs]
        cnt = jnp.minimum(pos + hi_off, length) - jnp.maximum(pos - lo_off, 0)
        y = acc * (1.0 / cnt.astype(f32)) - h_ref[:, cs]
        z = jnp.dot(y.astype(bf16), w_ref[g], preferred_element_type=f32) * ls_ref[:, cs]
        o_ref[:, cs] = x_ref[:, cs] + m_ref[gate_row:gate_row + 1, cs] * z


def _pool(h, x, pool_w, pool_scale, mods, gate_row, n_batch, s, c):
    t, d = h.shape
    tm = POOL_ROW_BLOCK
    nb = t // tm
    bps_lat, bps_ctx = s // tm, c // tm
    hb = tm // POOL_HALO
    last_halo = t // POOL_HALO - 1
    return pl.pallas_call(
        functools.partial(_pool_kernel, bps_lat=bps_lat, bps_ctx=bps_ctx, n_lat_blocks=n_batch * bps_lat,
                          s=s, c=c, gate_row=gate_row),
        grid=(nb,),
        in_specs=[pl.BlockSpec((tm, d), lambda i: (i, 0)),
                  pl.BlockSpec((POOL_HALO, d), lambda i: (jnp.maximum(i * hb - 1, 0), 0)),
                  pl.BlockSpec((POOL_HALO, d), lambda i: (jnp.minimum((i + 1) * hb, last_halo), 0)),
                  pl.BlockSpec((tm, d), lambda i: (i, 0)),
                  pl.BlockSpec(pool_w.shape, lambda i: (0, 0, 0)),
                  pl.BlockSpec((1, d), lambda i: (0, 0)),
                  pl.BlockSpec((None, SUBLANES, d), lambda i: (_group_of(i, bps_lat, n_batch), 0, 0))],
        out_specs=pl.BlockSpec((tm, d), lambda i: (i, 0)),
        out_shape=jax.ShapeDtypeStruct((t, d), f32),
        scratch_shapes=[pltpu.VMEM((tm + 2 * POOL_HALO, d), f32)],
        compiler_params=_cparams(1),
        name="pool_mixer",
    )(h, h, h, x, pool_w, pool_scale.reshape(1, d), mods)


def _topk_rows(chains, tb):
    rows = lax.broadcasted_iota(jnp.int32, (PEER_TOPK, tb), 0)

    def body(k, carry):
        out = []
        for (s_ref, order), (vals, idxs) in zip(chains, carry):
            s = s_ref[...]
            m = jnp.max(s, axis=0, keepdims=True)
            am = jnp.min(jnp.where(s == m, order, float(2 ** 23)), axis=0, keepdims=True)
            s_ref[...] = jnp.where(order == am, NEG, s)
            out.append((jnp.where(rows == k, m, vals), jnp.where(rows == k, am, idxs)))
        return tuple(out)

    zero = jnp.zeros((PEER_TOPK, tb), f32)
    res = lax.fori_loop(0, PEER_TOPK, body, tuple((zero, zero) for _ in chains))
    return [(vals, idxs.astype(jnp.int32)) for vals, idxs in res]


def _cand_blocks():
    blocks = [(0, PEER_TOPK)] + [(a, SUBLANES) for a in range(1, SUBLANES)]
    assert all((a + 1) * (nb + 1) > PEER_TOPK for a, nb in blocks[1:]) and 2 * SUBLANES == PEER_TOPK
    return blocks


def _peer_topk_kernel(q_ref, sk_ref, e_ref, g_ref, s1_ref, s2_ref, cand_ref, es_ref, gs_ref, *, n_heads, nkeys):
    tb = q_ref.shape[0]
    key_order = lax.broadcasted_iota(jnp.int32, (nkeys, tb), 0).astype(f32)
    blocks = _cand_blocks()
    n_mid = (len(blocks) - 1) * SUBLANES
    r = lax.broadcasted_iota(jnp.int32, cand_ref.shape, 0)
    rm = r - PEER_TOPK
    mid = (lax.shift_right_logical(rm, 3) + 1) * PEER_TOPK + lax.bitwise_and(rm, SUBLANES - 1)
    tail = (rm - n_mid + SUBLANES) * PEER_TOPK
    cand_order = jnp.where(r < PEER_TOPK, r, jnp.where(rm < n_mid, mid, tail)).astype(f32)

    def scores(h):
        col = pl.multiple_of(h * 2 * LANES, 2 * LANES)
        s1_ref[...] = lax.dot_general(sk_ref[0], q_ref[:, pl.ds(col, LANES)], _NT, preferred_element_type=f32)
        s2_ref[...] = lax.dot_general(sk_ref[1], q_ref[:, pl.ds(col + LANES, LANES)], _NT, preferred_element_type=f32)

    sub_chains = [(s1_ref, key_order), (s2_ref, key_order)]
    scores(0)
    (v1, i1), (v2, i2) = _topk_rows(sub_chains, tb)

    def head(h, carry):
        v1, i1, v2, i2 = carry
        r0 = 0
        for a, nb in blocks:
            cand_ref[r0:r0 + nb, :] = v1[a:a + 1, :] + v2[:nb]
            r0 += nb
        cand_ref[r0:r0 + SUBLANES, :] = v1[SUBLANES:] + v2[0:1, :]
        scores(jnp.minimum(h + 1, n_heads - 1))
        nxt1, nxt2, (sc, ci) = _topk_rows(sub_chains + [(cand_ref, cand_order)], tb)
        ca = lax.shift_right_logical(ci, PEER_TOPK.bit_length() - 1)
        cb = lax.bitwise_and(ci, PEER_TOPK - 1)
        e1 = jnp.zeros((PEER_TOPK, tb), jnp.int32)
        e2 = jnp.zeros((PEER_TOPK, tb), jnp.int32)
        for a in range(PEER_TOPK):
            e1 = jnp.where(ca == a, i1[a:a + 1, :], e1)
            e2 = jnp.where(cb == a, i2[a:a + 1, :], e2)
        p = jnp.exp(sc - sc[0:1, :])
        gate = p / jnp.sum(p, axis=0, keepdims=True)
        row = pl.multiple_of(h * PEER_TOPK, PEER_TOPK)
        es_ref[pl.ds(row, PEER_TOPK), :] = e1 * nkeys + e2
        gs_ref[pl.ds(row, PEER_TOPK), :] = gate
        return nxt1 + nxt2

    lax.fori_loop(0, n_heads, head, (v1, i1, v2, i2))
    e_ref[...] = es_ref[...].T
    g_ref[...] = gs_ref[...].T


def _peer_topk(q, sub_keys, n_blocks_rows):
    t, qd = q.shape
    nkeys = sub_keys.shape[1]
    n_heads = qd // (2 * LANES)
    slots = n_heads * PEER_TOPK
    tb = 256
    return pl.pallas_call(
        functools.partial(_peer_topk_kernel, n_heads=n_heads, nkeys=nkeys),
        grid=(n_blocks_rows * ROW_BLOCK // tb,),
        in_specs=[pl.BlockSpec((tb, qd), lambda i: (i, 0)),
                  pl.BlockSpec(sub_keys.shape, lambda i: (0, 0, 0))],
        out_specs=[pl.BlockSpec((tb, slots), lambda i: (i, 0)),
                   pl.BlockSpec((tb, slots), lambda i: (i, 0))],
        out_shape=[jax.ShapeDtypeStruct((n_blocks_rows * ROW_BLOCK, slots), jnp.int32),
                   jax.ShapeDtypeStruct((n_blocks_rows * ROW_BLOCK, slots), f32)],
        scratch_shapes=[pltpu.VMEM((nkeys, tb), f32), pltpu.VMEM((nkeys, tb), f32),
                        pltpu.VMEM((sum(nb for _, nb in _cand_blocks()) + SUBLANES, tb), f32),
                        pltpu.VMEM((slots, tb), jnp.int32), pltpu.VMEM((slots, tb), f32)],
        compiler_params=_cparams(1),
        name="peer_topk",
    )(q, sub_keys)


def _peer_gates_kernel(e_ref, g_ref, o_ref, *, nkeys):
    tb, slots = e_ref.shape
    iota = lax.broadcasted_iota(jnp.int32, (nkeys, slots), 0)

    def body(t, carry):
        e = e_ref[pl.ds(t, 1), :]
        g = g_ref[pl.ds(t, 1), :]
        e1 = lax.shift_right_logical(e, nkeys.bit_length() - 1)
        e2 = lax.bitwise_and(e, nkeys - 1)
        w1 = jnp.where(e1 == iota, g, 0.0).astype(bf16)
        o2 = jnp.where(e2 == iota, 1.0, 0.0).astype(bf16)
        res = lax.dot_general(w1, o2, _NT, preferred_element_type=f32)
        o_ref[:, t] = res.reshape(nkeys // SUBLANES, SUBLANES, nkeys)
        return carry

    lax.fori_loop(0, tb, body, 0, unroll=GATES_UNROLL)


def _peer_gates(e, g, nkeys, n_blocks_rows):
    t, slots = e.shape
    tb = 128
    return pl.pallas_call(
        functools.partial(_peer_gates_kernel, nkeys=nkeys),
        grid=(n_blocks_rows * ROW_BLOCK // tb,),
        in_specs=[pl.BlockSpec((tb, slots), lambda i: (i, 0)),
                  pl.BlockSpec((tb, slots), lambda i: (i, 0))],
        out_specs=pl.BlockSpec((nkeys // SUBLANES, tb, SUBLANES, nkeys), lambda i: (0, i, 0, 0)),
        out_shape=jax.ShapeDtypeStruct((nkeys // SUBLANES, t, SUBLANES, nkeys), f32),
        compiler_params=_cparams(1),
        name="peer_gates",
    )(e, g)


def _gelu_tanh(x):
    k0 = -2.0 * 0.7978845608028654 * 1.4426950408889634
    return x / (1.0 + jnp.exp2(x * (k0 + (k0 * 0.044715) * (x * x))))


def _peer_dense_kernel(f_ref, u_ref, v_ref, g_ref, x_ref, m_ref, o_ref, acc_ref, *, gate_row, nkeys):
    j = pl.program_id(1)

    @pl.when(j == 0)
    def _():
        acc_ref[...] = jnp.zeros_like(acc_ref)

    a = lax.dot_general(f_ref[...], u_ref[...], _NT, preferred_element_type=f32)
    tb = f_ref.shape[0]
    n_first = g_ref.shape[0] // tb
    parts = []
    for k in range(n_first):
        gk = g_ref[pl.ds(k, tb, stride=n_first), :]
        parts.append((_gelu_tanh(a[:, k * nkeys:(k + 1) * nkeys]) * gk).astype(bf16))
    ga = jnp.concatenate(parts, axis=1)
    acc_ref[...] += jnp.dot(ga, v_ref[...], preferred_element_type=f32)

    @pl.when(j == pl.num_programs(1) - 1)
    def _():
        o_ref[...] = x_ref[...] + m_ref[gate_row:gate_row + 1, :] * acc_ref[...]


def _peer_dense(f, u_all, v_all, layer, gates, x, mods, gate_row, n_blocks, bps, n_batch):
    t, d = f.shape
    ne = v_all.shape[1]
    nkeys = gates.shape[3]
    te = SUBLANES * nkeys
    gates = gates.reshape(gates.shape[0], gates.shape[1] * SUBLANES, nkeys)
    return pl.pallas_call(
        functools.partial(_peer_dense_kernel, gate_row=gate_row, nkeys=nkeys),
        grid=(n_blocks, ne // te),
        in_specs=[pl.BlockSpec((ROW_BLOCK, d), lambda i, j: (i, 0)),
                  pl.BlockSpec((None, te, d), lambda i, j: (layer, j, 0)),
                  pl.BlockSpec((None, te, d), lambda i, j: (layer, j, 0)),
                  pl.BlockSpec((None, ROW_BLOCK * SUBLANES, nkeys), lambda i, j: (j, i, 0)),
                  pl.BlockSpec((ROW_BLOCK, d), lambda i, j: (i, 0)),
                  pl.BlockSpec((None, SUBLANES, d), lambda i, j: (_group_of(i, bps, n_batch), 0, 0))],
        out_specs=pl.BlockSpec((ROW_BLOCK, d), lambda i, j: (i, 0)),
        out_shape=jax.ShapeDtypeStruct((t, d), f32),
        scratch_shapes=[pltpu.VMEM((ROW_BLOCK, d), f32)],
        compiler_params=_cparams(2),
        name="peer_dense",
    )(f, u_all, v_all, gates, x, mods)


def kernel(x, c, ctx, c_ctx, mod_w, mod_b, norm_w, final_norm_w, a_wqkv, a_q_gain, a_k_gain, a_wo,
           b_wqkv, b_rpb, b_wo, pool_w, pool_scale, peer_wq, peer_sub_keys, peer_u, peer_v):
    n_batch, s, d = x.shape
    c_len = ctx.shape[1]
    depth = mod_w.shape[0]
    assert s % ROW_BLOCK == 0 and (n_batch * c_len) % ROW_BLOCK == 0 and n_batch * c_len <= s
    assert s % c_len == 0 and c_len % POOL_ROW_BLOCK == 0 and s % GRID_W == 0
    assert a_q_gain.shape[1] == LANES and peer_sub_keys.shape[2] == LANES and peer_sub_keys.shape[3] == LANES
    assert PEER_TOPK & (PEER_TOPK - 1) == 0 and peer_sub_keys.shape[2] & (peer_sub_keys.shape[2] - 1) == 0
    assert depth % N_MIXERS != 0

    bps = s // ROW_BLOCK
    nb_lat = n_batch * bps
    nb_all = nb_lat + n_batch * c_len // ROW_BLOCK
    t_lat = n_batch * s
    q_dim = a_wo.shape[1]
    kv_dim = (a_wqkv.shape[2] - q_dim) // 2
    n_kv = kv_dim // LANES
    groups = q_dim // kv_dim
    b_heads = b_rpb.shape[1]
    win_r, win_c = (b_rpb.shape[2] + 1) // 2, (b_rpb.shape[3] + 1) // 2
    nkeys = peer_sub_keys.shape[2]
    scale = SM_SCALE_LOG2

    xs = jnp.concatenate([x.reshape(t_lat, d), ctx.reshape(n_batch * c_len, d)], axis=0)
    cvec = jnp.concatenate([c, c_ctx[None, :], jnp.zeros((SUBLANES - n_batch - 1, d), f32)], axis=0)
    mods_all = _mods(cvec, mod_w, mod_b)
    tables = _rope_tables(s)
    u_bf, v_bf = peer_u.astype(bf16), peer_v.astype(bf16)

    for i in range(depth):
        last = i == depth - 1
        kind, j = i % N_MIXERS, i // N_MIXERS
        nb = nb_lat if last else nb_all
        mods = mods_all[i, :n_batch + 1].reshape(n_batch + 1, N_MOD, d)
        mods = jnp.pad(mods, ((0, 0), (0, SUBLANES - N_MOD), (0, 0)))

        if kind == 2:
            h = _norm_mod(xs, norm_w[i, 0], mods, 0, 1, nb_all, bps, n_batch, f32)
            xs = _pool(h, xs, pool_w[j].astype(bf16), pool_scale[j], mods, 2, n_batch, s, c_len)
        else:
            h = _norm_mod(xs, norm_w[i, 0], mods, 0, 1, nb_all, bps, n_batch, bf16)
            if kind == 0:
                qkv = _qkv_gqa(h, a_wqkv[j].astype(bf16), a_q_gain[j], a_k_gain[j], tables, q_dim, kv_dim,
                               nb_all, bps, n_batch)
                o = _attn_lat(qkv, qkv, qkv, 0, q_dim, q_dim + kv_dim, groups, n_kv, n_batch, s, c_len, 1.0, q_dim)
                if not last:
                    o_ctx = _attn_ctx(qkv, qkv, qkv, 0, q_dim, q_dim + kv_dim, groups, n_kv, n_batch, s, c_len, 1.0)
                    o = jnp.concatenate([o, o_ctx], axis=0)
                wo = a_wo[j]
            else:
                qkv = _mm(h, b_wqkv[j].astype(bf16), nb_all, _col_tile(3 * d), bf16)
                rows = s // GRID_W
                bias = _natten_bias(b_rpb[j], rows, min(win_r, rows), win_r, win_c)
                o = _natten_lat(qkv, bias, b_heads, n_batch, s, c_len, d)
                if not last:
                    o_ctx = _attn_ctx(qkv, qkv, qkv, 0, d, 2 * d, 1, b_heads, n_batch, s, c_len, scale)
                    o = jnp.concatenate([o, o_ctx], axis=0)
                wo = b_wo[j]
            xs = _mm_res(o, wo.astype(bf16), xs, mods, 2, nb, bps, n_batch, _col_tile(d))

        f = _norm_mod(xs, norm_w[i, 1], mods, 3, 4, nb, bps, n_batch, bf16)
        qp = _mm(f, peer_wq[i].astype(bf16), nb, _col_tile(peer_wq.shape[2]), f32)
        e, g = _peer_topk(qp, peer_sub_keys[i], nb)
        gates = _peer_gates(e, g, nkeys, nb)
        xs = _peer_dense(f, u_bf, v_bf, i, gates, xs, mods, 5, nb, bps, n_batch)

    return _final_norm(xs, final_norm_w, nb_lat).reshape(n_batch, s, d)
```

```python
import functools

import numpy as np
import jax
import jax.numpy as jnp
from jax import lax
from jax.experimental import pallas as pl
from jax.experimental.pallas import tpu as pltpu

GRID_W = 64
ROPE_BASE = 10000.0
POOL_WINDOWS = (2, 4, 8, 16)
PEER_TOPK = 16
EPS = 1e-6
N_MIXERS = 3
N_MOD = 6

LANES = 128
SUBLANES = 8
ROW_BLOCK = 512
POOL_ROW_BLOCK = 256
POOL_HALO = 8
MM_COL_TILE = 2048
ATTN_KV_CHUNK = 4096
NATTEN_UNROLL = 8
GATES_UNROLL = 32
VMEM_LIMIT = 56 * 1024 * 1024
NEG = -1e30
LOG2E = 1.4426950408889634
SM_SCALE_LOG2 = float(LANES) ** -0.5 * LOG2E

f32 = jnp.float32
bf16 = jnp.bfloat16
_NT = (((1,), (1,)), ((), ()))


def _cparams(n_axes):
    return pltpu.CompilerParams(dimension_semantics=("arbitrary",) * n_axes, vmem_limit_bytes=VMEM_LIMIT)


def _group_of(i, blocks_per_seq, n_batch):
    return jnp.minimum(i // blocks_per_seq, n_batch)


def _mods_kernel(cv_ref, w_ref, b_ref, o_ref):
    cv = cv_ref[...]
    a = (cv / (1.0 + jnp.exp(-cv))).astype(bf16)
    o_ref[...] = jnp.dot(a, w_ref[...].astype(bf16), preferred_element_type=f32) + b_ref[...]


def _mods(cvec, mod_w, mod_b):
    depth, d, n = mod_w.shape
    tn = 1024
    return pl.pallas_call(
        _mods_kernel,
        grid=(depth, n // tn),
        in_specs=[pl.BlockSpec((SUBLANES, d), lambda l, j: (0, 0)),
                  pl.BlockSpec((None, d, tn), lambda l, j: (l, 0, j)),
                  pl.BlockSpec((None, 1, tn), lambda l, j: (l, 0, j))],
        out_specs=pl.BlockSpec((None, SUBLANES, tn), lambda l, j: (l, 0, j)),
        out_shape=jax.ShapeDtypeStruct((depth, SUBLANES, n), f32),
        compiler_params=_cparams(2),
        name="adaln_mods",
    )(cvec, mod_w, mod_b.reshape(depth, 1, n))


def _norm_mod_kernel(x_ref, nw_ref, m_ref, o_ref, *, shift_row, scale_row):
    x = x_ref[...]
    y = x * lax.rsqrt(jnp.mean(x * x, axis=-1, keepdims=True) + EPS) * nw_ref[...]
    y = y * (1.0 + m_ref[scale_row:scale_row + 1, :]) + m_ref[shift_row:shift_row + 1, :]
    o_ref[...] = y.astype(o_ref.dtype)


def _norm_mod(x, nw, mods, shift_row, scale_row, n_blocks, bps, n_batch, out_dtype):
    t, d = x.shape
    return pl.pallas_call(
        functools.partial(_norm_mod_kernel, shift_row=shift_row, scale_row=scale_row),
        grid=(n_blocks,),
        in_specs=[pl.BlockSpec((ROW_BLOCK, d), lambda i: (i, 0)),
                  pl.BlockSpec((1, d), lambda i: (0, 0)),
                  pl.BlockSpec((None, SUBLANES, d), lambda i: (_group_of(i, bps, n_batch), 0, 0))],
        out_specs=pl.BlockSpec((ROW_BLOCK, d), lambda i: (i, 0)),
        out_shape=jax.ShapeDtypeStruct((n_blocks * ROW_BLOCK, d), out_dtype),
        compiler_params=_cparams(1),
        name="norm_mod",
    )(x, nw.reshape(1, d), mods)


def _final_norm_kernel(x_ref, nw_ref, o_ref):
    x = x_ref[...]
    o_ref[...] = x * lax.rsqrt(jnp.mean(x * x, axis=-1, keepdims=True) + EPS) * nw_ref[...]


def _final_norm(x, nw, n_blocks):
    t, d = x.shape
    return pl.pallas_call(
        _final_norm_kernel,
        grid=(n_blocks,),
        in_specs=[pl.BlockSpec((ROW_BLOCK, d), lambda i: (i, 0)),
                  pl.BlockSpec((1, d), lambda i: (0, 0))],
        out_specs=pl.BlockSpec((ROW_BLOCK, d), lambda i: (i, 0)),
        out_shape=jax.ShapeDtypeStruct((n_blocks * ROW_BLOCK, d), f32),
        compiler_params=_cparams(1),
        name="final_norm",
    )(x, nw.reshape(1, d))


def _mm_kernel(a_ref, w_ref, o_ref):
    o_ref[...] = jnp.dot(a_ref[...], w_ref[...], preferred_element_type=f32).astype(o_ref.dtype)


def _col_tile(n):
    return max(t for t in range(LANES, min(n, MM_COL_TILE) + 1, LANES) if n % t == 0)


def _mm(a, w, n_blocks, tn, out_dtype):
    t, k = a.shape
    n = w.shape[1]
    return pl.pallas_call(
        _mm_kernel,
        grid=(n // tn, n_blocks),
        in_specs=[pl.BlockSpec((ROW_BLOCK, k), lambda j, i: (i, 0)),
                  pl.BlockSpec((k, tn), lambda j, i: (0, j))],
        out_specs=pl.BlockSpec((ROW_BLOCK, tn), lambda j, i: (i, j)),
        out_shape=jax.ShapeDtypeStruct((n_blocks * ROW_BLOCK, n), out_dtype),
        compiler_params=_cparams(2),
        name="matmul",
    )(a, w)


def _mm_res_kernel(a_ref, w_ref, x_ref, m_ref, o_ref, *, gate_row):
    y = jnp.dot(a_ref[...], w_ref[...], preferred_element_type=f32)
    o_ref[...] = x_ref[...] + m_ref[gate_row:gate_row + 1, :] * y


def _mm_res(a, w, x, mods, gate_row, n_blocks, bps, n_batch, tn):
    t, k = a.shape
    n = w.shape[1]
    return pl.pallas_call(
        functools.partial(_mm_res_kernel, gate_row=gate_row),
        grid=(n // tn, n_blocks),
        in_specs=[pl.BlockSpec((ROW_BLOCK, k), lambda j, i: (i, 0)),
                  pl.BlockSpec((k, tn), lambda j, i: (0, j)),
                  pl.BlockSpec((ROW_BLOCK, tn), lambda j, i: (i, j)),
                  pl.BlockSpec((None, SUBLANES, tn), lambda j, i: (_group_of(i, bps, n_batch), 0, j))],
        out_specs=pl.BlockSpec((ROW_BLOCK, tn), lambda j, i: (i, j)),
        out_shape=jax.ShapeDtypeStruct((n_blocks * ROW_BLOCK, n), f32),
        compiler_params=_cparams(2),
        name="matmul_residual",
    )(a, w, x, mods)


def _qkv_gqa_kernel(a_ref, w_ref, qg_ref, kg_ref, cos_ref, sa_ref, sb_ref, o_ref, *, nq_tiles, nk_tiles, scale):
    j = pl.program_id(0)
    acc = jnp.dot(a_ref[...], w_ref[...], preferred_element_type=f32)
    heads = acc.shape[1] // LANES

    def prep(gain, post):
        cos, sa, sb = cos_ref[...], sa_ref[...], sb_ref[...]
        for h in range(heads):
            y = acc[:, h * LANES:(h + 1) * LANES]
            y = y * lax.rsqrt(jnp.mean(y * y, axis=-1, keepdims=True) + EPS) * gain
            y = y * cos + pltpu.roll(y, LANES - 32, 1) * sa + pltpu.roll(y, 32, 1) * sb
            o_ref[:, h * LANES:(h + 1) * LANES] = (y * post).astype(o_ref.dtype)

    @pl.when(j < nq_tiles)
    def _():
        prep(qg_ref[...], scale)

    @pl.when(jnp.logical_and(j >= nq_tiles, j < nq_tiles + nk_tiles))
    def _():
        prep(kg_ref[...], 1.0)

    @pl.when(j >= nq_tiles + nk_tiles)
    def _():
        o_ref[...] = acc.astype(o_ref.dtype)


def _rope_tables(s):
    t = np.arange(s)
    half = LANES // 2
    inv = ROPE_BASE ** (-jnp.arange(0, half, 2, dtype=f32) / half)
    ang_r = (t // GRID_W).astype(np.float32)[:, None] * inv[None, :]
    ang_c = (t % GRID_W).astype(np.float32)[:, None] * inv[None, :]
    ang = jnp.concatenate([ang_r, ang_r, ang_c, ang_c], axis=-1)
    cos, sin = jnp.cos(ang), jnp.sin(ang)
    first = (np.arange(LANES) % half) < (half // 2)
    sa = jnp.where(first[None, :], -sin, 0.0)
    sb = jnp.where(first[None, :], 0.0, sin)
    pad1 = jnp.ones((ROW_BLOCK, LANES), f32)
    pad0 = jnp.zeros((ROW_BLOCK, LANES), f32)
    return (jnp.concatenate([cos, pad1], 0), jnp.concatenate([sa, pad0], 0), jnp.concatenate([sb, pad0], 0))


def _qkv_gqa(h, wqkv, q_gain, k_gain, tables, q_dim, kv_dim, n_blocks, bps, n_batch):
    t, k = h.shape
    n = wqkv.shape[1]
    tn = min(512, kv_dim)
    cos, sa, sb = tables

    def pos_map(j, i):
        return (jnp.where(i < bps * n_batch, i % bps, bps), 0)

    return pl.pallas_call(
        functools.partial(_qkv_gqa_kernel, nq_tiles=q_dim // tn, nk_tiles=kv_dim // tn, scale=SM_SCALE_LOG2),
        grid=(n // tn, n_blocks),
        in_specs=[pl.BlockSpec((ROW_BLOCK, k), lambda j, i: (i, 0)),
                  pl.BlockSpec((k, tn), lambda j, i: (0, j)),
                  pl.BlockSpec((1, LANES), lambda j, i: (0, 0)),
                  pl.BlockSpec((1, LANES), lambda j, i: (0, 0)),
                  pl.BlockSpec((ROW_BLOCK, LANES), pos_map),
                  pl.BlockSpec((ROW_BLOCK, LANES), pos_map),
                  pl.BlockSpec((ROW_BLOCK, LANES), pos_map)],
        out_specs=pl.BlockSpec((ROW_BLOCK, tn), lambda j, i: (i, j)),
        out_shape=jax.ShapeDtypeStruct((t, n), bf16),
        compiler_params=_cparams(2),
        name="qkv_gqa",
    )(h, wqkv, q_gain.reshape(1, LANES), k_gain.reshape(1, LANES), cos, sa, sb)


def _flash_kernel(*refs, groups, n_lat_chunks, ck, scale):
    if n_lat_chunks:
        q_ref, kc_ref, vc_ref, kl_ref, vl_ref, o_ref = refs
    else:
        q_ref, kc_ref, vc_ref, o_ref = refs
    tq = q_ref.shape[0]
    q = jnp.concatenate([q_ref[:, g * LANES:(g + 1) * LANES] for g in range(groups)], axis=0)
    rows = groups * tq

    def step(k, v, m, l, acc):
        s = lax.dot_general(q, k, _NT, preferred_element_type=f32)
        if scale != 1.0:
            s = s * scale
        m_new = jnp.maximum(m, jnp.max(s, axis=-1, keepdims=True))
        p = jnp.exp2(s - m_new)
        alpha = jnp.exp2(m - m_new)
        l = alpha * l + jnp.sum(p, axis=-1, keepdims=True)
        acc = alpha * acc + jnp.dot(p.astype(bf16), v, preferred_element_type=f32)
        return m_new, l, acc

    carry = step(kc_ref[...], vc_ref[...], jnp.full((rows, 1), NEG, f32), jnp.zeros((rows, 1), f32),
                 jnp.zeros((rows, LANES), f32))
    if n_lat_chunks:
        def body(c, carry):
            off = pl.multiple_of(c * ck, ck)
            return step(kl_ref[pl.ds(off, ck), :], vl_ref[pl.ds(off, ck), :], *carry)
        carry = lax.fori_loop(0, n_lat_chunks, body, carry)
    _, l, acc = carry
    o = acc / l
    for g in range(groups):
        o_ref[:, g * LANES:(g + 1) * LANES] = o[g * tq:(g + 1) * tq].astype(o_ref.dtype)


def _attn_lat(q, k, v, q_col0, k_col0, v_col0, groups, n_kv, n_batch, s, c, scale, out_cols):
    t = q.shape[0]
    tq = 128
    ck = min(ATTN_KV_CHUNK, s)
    nqb = s // tq
    ctx_blk0 = n_batch * s // c
    qw = groups * LANES
    return pl.pallas_call(
        functools.partial(_flash_kernel, groups=groups, n_lat_chunks=s // ck, ck=ck, scale=scale),
        grid=(n_batch, n_kv, nqb),
        in_specs=[pl.BlockSpec((tq, qw), lambda b, h, i: (b * nqb + i, q_col0 // qw + h)),
                  pl.BlockSpec((c, LANES), lambda b, h, i: (ctx_blk0 + b, k_col0 // LANES + h)),
                  pl.BlockSpec((c, LANES), lambda b, h, i: (ctx_blk0 + b, v_col0 // LANES + h)),
                  pl.BlockSpec((s, LANES), lambda b, h, i: (b, k_col0 // LANES + h)),
                  pl.BlockSpec((s, LANES), lambda b, h, i: (b, v_col0 // LANES + h))],
        out_specs=pl.BlockSpec((tq, qw), lambda b, h, i: (b * nqb + i, h)),
        out_shape=jax.ShapeDtypeStruct((n_batch * s, out_cols), bf16),
        compiler_params=_cparams(3),
        name="attn_latent",
    )(q, k, v, k, v)


def _attn_ctx(q, k, v, q_col0, k_col0, v_col0, groups, n_kv, n_batch, s, c, scale):
    ctx_blk0 = n_batch * s // c
    qw = groups * LANES
    return pl.pallas_call(
        functools.partial(_flash_kernel, groups=groups, n_lat_chunks=0, ck=0, scale=scale),
        grid=(n_batch, n_kv),
        in_specs=[pl.BlockSpec((c, qw), lambda b, h: (ctx_blk0 + b, q_col0 // qw + h)),
                  pl.BlockSpec((c, LANES), lambda b, h: (ctx_blk0 + b, k_col0 // LANES + h)),
                  pl.BlockSpec((c, LANES), lambda b, h: (ctx_blk0 + b, v_col0 // LANES + h))],
        out_specs=pl.BlockSpec((c, qw), lambda b, h: (b, h)),
        out_shape=jax.ShapeDtypeStruct((n_batch * c, n_kv * qw), bf16),
        compiler_params=_cparams(2),
        name="attn_context",
    )(q, k, v)


def _natten_kernel(q_ref, k_ref, v_ref, kc_ref, vc_ref, bias_ref, o_ref, *, rows, w, wr, scale):
    kc = kc_ref[...]
    vc = vc_ref[...]

    nr = NATTEN_UNROLL

    def body(g, carry):
        base = pl.multiple_of(g * nr * w, nr * w)
        qg = q_ref[pl.ds(base, nr * w), :]
        scg = lax.dot_general(qg, kc, _NT, preferred_element_type=f32) * scale
        offs, sws = [], []
        for i in range(nr):
            r = g * nr + i
            r0 = jnp.clip(r - wr // 2, 0, rows - wr)
            off = pl.multiple_of(r0 * w, w)
            kb = k_ref[pl.ds(off, wr * w), :]
            sws.append(lax.dot_general(qg[i * w:(i + 1) * w], kb, _NT, preferred_element_type=f32) * scale
                       + bias_ref[r - r0])
            offs.append(off)
        pws, pcs, ls = [], [], []
        for i in range(nr):
            sc = scg[i * w:(i + 1) * w]
            m = jnp.maximum(jnp.max(sws[i], axis=-1, keepdims=True), jnp.max(sc, axis=-1, keepdims=True))
            pw = jnp.exp2(sws[i] - m)
            pc = jnp.exp2(sc - m)
            ls.append(jnp.sum(pw, axis=-1, keepdims=True) + jnp.sum(pc, axis=-1, keepdims=True))
            pws.append(pw.astype(bf16))
            pcs.append(pc.astype(bf16))
        ocg = jnp.dot(jnp.concatenate(pcs, axis=0), vc, preferred_element_type=f32)
        for i in range(nr):
            vb = v_ref[pl.ds(offs[i], wr * w), :]
            o = (jnp.dot(pws[i], vb, preferred_element_type=f32) + ocg[i * w:(i + 1) * w]) / ls[i]
            o_ref[pl.ds(pl.multiple_of(base + i * w, w), w), :] = o.astype(o_ref.dtype)
        return carry

    lax.fori_loop(0, rows // nr, body, 0)


def _natten_bias(rpb, rows, wr, win_r, win_c):
    w = GRID_W
    cols = np.arange(w)
    col_start = np.clip(cols - win_c // 2, 0, w - win_c)
    inside = (cols[None, :] >= col_start[:, None]) & (cols[None, :] < col_start[:, None] + win_c)
    rpbp = jnp.pad(rpb, ((0, 0), (0, 0), (w - win_c, w - win_c)))
    p = jnp.stack([rpbp[:, :, w - 1 - qc:2 * w - 1 - qc] for qc in range(w)], axis=2)
    p = jnp.where(inside[None, None], p * LOG2E, NEG)
    tab = jnp.stack([p[:, win_r - 1 - d:win_r - 1 - d + wr] for d in range(wr)], axis=1)
    return jnp.transpose(tab, (0, 1, 3, 2, 4)).reshape(rpb.shape[0], wr, w, wr * w)


def _natten_lat(qkv, bias, n_heads, n_batch, s, c, d_model):
    t = qkv.shape[0]
    rows = s // GRID_W
    wr = bias.shape[1]
    ctx_blk0 = n_batch * s // c
    hq, hk, hv = 0, d_model // LANES, 2 * d_model // LANES
    return pl.pallas_call(
        functools.partial(_natten_kernel, rows=rows, w=GRID_W, wr=wr, scale=SM_SCALE_LOG2),
        grid=(n_batch, n_heads),
        in_specs=[pl.BlockSpec((s, LANES), lambda b, h: (b, hq + h)),
                  pl.BlockSpec((s, LANES), lambda b, h: (b, hk + h)),
                  pl.BlockSpec((s, LANES), lambda b, h: (b, hv + h)),
                  pl.BlockSpec((c, LANES), lambda b, h: (ctx_blk0 + b, hk + h)),
                  pl.BlockSpec((c, LANES), lambda b, h: (ctx_blk0 + b, hv + h)),
                  pl.BlockSpec((None, wr, GRID_W, wr * GRID_W), lambda b, h: (h, 0, 0, 0))],
        out_specs=pl.BlockSpec((s, LANES), lambda b, h: (b, h)),
        out_shape=jax.ShapeDtypeStruct((n_batch * s, d_model), bf16),
        compiler_params=_cparams(2),
        name="natten_latent",
    )(qkv, qkv, qkv, qkv, qkv, bias)


def _pool_kernel(h_ref, hp_ref, hn_ref, x_ref, w_ref, ls_ref, m_ref, o_ref, pad_ref, *,
                 bps_lat, bps_ctx, n_lat_blocks, s, c, gate_row):
    i = pl.program_id(0)
    tm = h_ref.shape[0]
    is_lat = i < n_lat_blocks
    blk = jnp.where(is_lat, i % bps_lat, (i - n_lat_blocks) % bps_ctx)
    nblk = jnp.where(is_lat, bps_lat, bps_ctx)
    length = jnp.where(is_lat, s, c)
    pad_ref[0:POOL_HALO, :] = jnp.where(blk == 0, 0.0, hp_ref[...])
    pad_ref[POOL_HALO:POOL_HALO + tm, :] = h_ref[...]
    pad_ref[POOL_HALO + tm:2 * POOL_HALO + tm, :] = jnp.where(blk == nblk - 1, 0.0, hn_ref[...])
    pos = blk * tm + lax.broadcasted_iota(jnp.int32, (tm, 1), 0)
    pg = w_ref.shape[1]
    for g, win in enumerate(POOL_WINDOWS):
        cs = slice(g * pg, (g + 1) * pg)
        lo_off, hi_off = win // 2, win - win // 2
        acc = pad_ref[POOL_HALO - lo_off:POOL_HALO - lo_off + tm, cs]
        for k in range(-lo_off + 1, hi_off):
            acc = acc + pad_ref[POOL_HALO + k:POOL_HALO + k + tm, cs]
        cnt = jnp.minimum(pos + hi_off, length) - jnp.maximum(pos - lo_off, 0)
        y = acc * (1.0 / cnt.astype(f32)) - h_ref[:, cs]
        z = jnp.dot(y.astype(bf16), w_ref[g], preferred_element_type=f32) * ls_ref[:, cs]
        o_ref[:, cs] = x_ref[:, cs] + m_ref[gate_row:gate_row + 1, cs] * z


def _pool(h, x, pool_w, pool_scale, mods, gate_row, n_batch, s, c):
    t, d = h.shape
    tm = POOL_ROW_BLOCK
    nb = t // tm
    bps_lat, bps_ctx = s // tm, c // tm
    hb = tm // POOL_HALO
    last_halo = t // POOL_HALO - 1
    return pl.pallas_call(
        functools.partial(_pool_kernel, bps_lat=bps_lat, bps_ctx=bps_ctx, n_lat_blocks=n_batch * bps_lat,
                          s=s, c=c, gate_row=gate_row),
        grid=(nb,),
        in_specs=[pl.BlockSpec((tm, d), lambda i: (i, 0)),
                  pl.BlockSpec((POOL_HALO, d), lambda i: (jnp.maximum(i * hb - 1, 0), 0)),
                  pl.BlockSpec((POOL_HALO, d), lambda i: (jnp.minimum((i + 1) * hb, last_halo), 0)),
                  pl.BlockSpec((tm, d), lambda i: (i, 0)),
                  pl.BlockSpec(pool_w.shape, lambda i: (0, 0, 0)),
                  pl.BlockSpec((1, d), lambda i: (0, 0)),
                  pl.BlockSpec((None, SUBLANES, d), lambda i: (_group_of(i, bps_lat, n_batch), 0, 0))],
        out_specs=pl.BlockSpec((tm, d), lambda i: (i, 0)),
        out_shape=jax.ShapeDtypeStruct((t, d), f32),
        scratch_shapes=[pltpu.VMEM((tm + 2 * POOL_HALO, d), f32)],
        compiler_params=_cparams(1),
        name="pool_mixer",
    )(h, h, h, x, pool_w, pool_scale.reshape(1, d), mods)


def _topk_rows(chains, tb):
    rows = lax.broadcasted_iota(jnp.int32, (PEER_TOPK, tb), 0)

    def body(k, carry):
        out = []
        for (s_ref, order), (vals, idxs) in zip(chains, carry):
            s = s_ref[...]
            m = jnp.max(s, axis=0, keepdims=True)
            am = jnp.min(jnp.where(s == m, order, float(2 ** 23)), axis=0, keepdims=True)
            s_ref[...] = jnp.where(order == am, NEG, s)
            out.append((jnp.where(rows == k, m, vals), jnp.where(rows == k, am, idxs)))
        return tuple(out)

    zero = jnp.zeros((PEER_TOPK, tb), f32)
    res = lax.fori_loop(0, PEER_TOPK, body, tuple((zero, zero) for _ in chains))
    return [(vals, idxs.astype(jnp.int32)) for vals, idxs in res]


def _cand_blocks():
    blocks = [(0, PEER_TOPK)] + [(a, SUBLANES) for a in range(1, SUBLANES)]
    assert all((a + 1) * (nb + 1) > PEER_TOPK for a, nb in blocks[1:]) and 2 * SUBLANES == PEER_TOPK
    return blocks


def _peer_topk_kernel(q_ref, sk_ref, e_ref, g_ref, s1_ref, s2_ref, cand_ref, es_ref, gs_ref, *, n_heads, nkeys):
    tb = q_ref.shape[0]
    key_order = lax.broadcasted_iota(jnp.int32, (nkeys, tb), 0).astype(f32)
    blocks = _cand_blocks()
    n_mid = (len(blocks) - 1) * SUBLANES
    r = lax.broadcasted_iota(jnp.int32, cand_ref.shape, 0)
    rm = r - PEER_TOPK
    mid = (lax.shift_right_logical(rm, 3) + 1) * PEER_TOPK + lax.bitwise_and(rm, SUBLANES - 1)
    tail = (rm - n_mid + SUBLANES) * PEER_TOPK
    cand_order = jnp.where(r < PEER_TOPK, r, jnp.where(rm < n_mid, mid, tail)).astype(f32)

    def scores(h):
        col = pl.multiple_of(h * 2 * LANES, 2 * LANES)
        s1_ref[...] = lax.dot_general(sk_ref[0], q_ref[:, pl.ds(col, LANES)], _NT, preferred_element_type=f32)
        s2_ref[...] = lax.dot_general(sk_ref[1], q_ref[:, pl.ds(col + LANES, LANES)], _NT, preferred_element_type=f32)

    sub_chains = [(s1_ref, key_order), (s2_ref, key_order)]
    scores(0)
    (v1, i1), (v2, i2) = _topk_rows(sub_chains, tb)

    def head(h, carry):
        v1, i1, v2, i2 = carry
        r0 = 0
        for a, nb in blocks:
            cand_ref[r0:r0 + nb, :] = v1[a:a + 1, :] + v2[:nb]
            r0 += nb
        cand_ref[r0:r0 + SUBLANES, :] = v1[SUBLANES:] + v2[0:1, :]
        scores(jnp.minimum(h + 1, n_heads - 1))
        nxt1, nxt2, (sc, ci) = _topk_rows(sub_chains + [(cand_ref, cand_order)], tb)
        ca = lax.shift_right_logical(ci, PEER_TOPK.bit_length() - 1)
        cb = lax.bitwise_and(ci, PEER_TOPK - 1)
        e1 = jnp.zeros((PEER_TOPK, tb), jnp.int32)
        e2 = jnp.zeros((PEER_TOPK, tb), jnp.int32)
        for a in range(PEER_TOPK):
            e1 = jnp.where(ca == a, i1[a:a + 1, :], e1)
            e2 = jnp.where(cb == a, i2[a:a + 1, :], e2)
        p = jnp.exp(sc - sc[0:1, :])
        gate = p / jnp.sum(p, axis=0, keepdims=True)
        row = pl.multiple_of(h * PEER_TOPK, PEER_TOPK)
        es_ref[pl.ds(row, PEER_TOPK), :] = e1 * nkeys + e2
        gs_ref[pl.ds(row, PEER_TOPK), :] = gate
        return nxt1 + nxt2

    lax.fori_loop(0, n_heads, head, (v1, i1, v2, i2))
    e_ref[...] = es_ref[...].T
    g_ref[...] = gs_ref[...].T


def _peer_topk(q, sub_keys, n_blocks_rows):
    t, qd = q.shape
    nkeys = sub_keys.shape[1]
    n_heads = qd // (2 * LANES)
    slots = n_heads * PEER_TOPK
    tb = 256
    return pl.pallas_call(
        functools.partial(_peer_topk_kernel, n_heads=n_heads, nkeys=nkeys),
        grid=(n_blocks_rows * ROW_BLOCK // tb,),
        in_specs=[pl.BlockSpec((tb, qd), lambda i: (i, 0)),
                  pl.BlockSpec(sub_keys.shape, lambda i: (0, 0, 0))],
        out_specs=[pl.BlockSpec((tb, slots), lambda i: (i, 0)),
                   pl.BlockSpec((tb, slots), lambda i: (i, 0))],
        out_shape=[jax.ShapeDtypeStruct((n_blocks_rows * ROW_BLOCK, slots), jnp.int32),
                   jax.ShapeDtypeStruct((n_blocks_rows * ROW_BLOCK, slots), f32)],
        scratch_shapes=[pltpu.VMEM((nkeys, tb), f32), pltpu.VMEM((nkeys, tb), f32),
                        pltpu.VMEM((sum(nb for _, nb in _cand_blocks()) + SUBLANES, tb), f32),
                        pltpu.VMEM((slots, tb), jnp.int32), pltpu.VMEM((slots, tb), f32)],
        compiler_params=_cparams(1),
        name="peer_topk",
    )(q, sub_keys)


def _peer_gates_kernel(e_ref, g_ref, o_ref, *, nkeys):
    tb, slots = e_ref.shape
    iota = lax.broadcasted_iota(jnp.int32, (nkeys, slots), 0)

    def body(t, carry):
        e = e_ref[pl.ds(t, 1), :]
        g = g_ref[pl.ds(t, 1), :]
        e1 = lax.shift_right_logical(e, nkeys.bit_length() - 1)
        e2 = lax.bitwise_and(e, nkeys - 1)
        w1 = jnp.where(e1 == iota, g, 0.0).astype(bf16)
        o2 = jnp.where(e2 == iota, 1.0, 0.0).astype(bf16)
        res = lax.dot_general(w1, o2, _NT, preferred_element_type=f32)
        o_ref[:, t] = res.reshape(nkeys // SUBLANES, SUBLANES, nkeys)
        return carry

    lax.fori_loop(0, tb, body, 0, unroll=GATES_UNROLL)


def _peer_gates(e, g, nkeys, n_blocks_rows):
    t, slots = e.shape
    tb = 128
    return pl.pallas_call(
        functools.partial(_peer_gates_kernel, nkeys=nkeys),
        grid=(n_blocks_rows * ROW_BLOCK // tb,),
        in_specs=[pl.BlockSpec((tb, slots), lambda i: (i, 0)),
                  pl.BlockSpec((tb, slots), lambda i: (i, 0))],
        out_specs=pl.BlockSpec((nkeys // SUBLANES, tb, SUBLANES, nkeys), lambda i: (0, i, 0, 0)),
        out_shape=jax.ShapeDtypeStruct((nkeys // SUBLANES, t, SUBLANES, nkeys), f32),
        compiler_params=_cparams(1),
        name="peer_gates",
    )(e, g)


def _gelu_tanh(x):
    k0 = -2.0 * 0.7978845608028654 * 1.4426950408889634
    return x / (1.0 + jnp.exp2(x * (k0 + (k0 * 0.044715) * (x * x))))


def _peer_dense_kernel(f_ref, u_ref, v_ref, g_ref, x_ref, m_ref, o_ref, acc_ref, a_ref, *, gate_row, nkeys):
    j = pl.program_id(1)

    @pl.when(j == 0)
    def _():
        acc_ref[...] = jnp.zeros_like(acc_ref)

    slot = j % 2
    a_ref[slot] = lax.dot_general(f_ref[...], u_ref[...], _NT, preferred_element_type=f32)
    tb = f_ref.shape[0]
    n_first = g_ref.shape[0] // tb
    parts = []
    for k in range(n_first):
        gk = g_ref[pl.ds(k, tb, stride=n_first), :]
        parts.append((_gelu_tanh(a_ref[slot, :, k * nkeys:(k + 1) * nkeys]) * gk).astype(bf16))
    ga = jnp.concatenate(parts, axis=1)
    acc_ref[...] += jnp.dot(ga, v_ref[...], preferred_element_type=f32)

    @pl.when(j == pl.num_programs(1) - 1)
    def _():
        o_ref[...] = x_ref[...] + m_ref[gate_row:gate_row + 1, :] * acc_ref[...]


def _peer_dense(f, u_all, v_all, layer, gates, x, mods, gate_row, n_blocks, bps, n_batch):
    t, d = f.shape
    ne = v_all.shape[1]
    nkeys = gates.shape[3]
    te = SUBLANES * nkeys
    gates = gates.reshape(gates.shape[0], gates.shape[1] * SUBLANES, nkeys)
    return pl.pallas_call(
        functools.partial(_peer_dense_kernel, gate_row=gate_row, nkeys=nkeys),
        grid=(n_blocks, ne // te),
        in_specs=[pl.BlockSpec((ROW_BLOCK, d), lambda i, j: (i, 0)),
                  pl.BlockSpec((None, te, d), lambda i, j: (layer, j, 0)),
                  pl.BlockSpec((None, te, d), lambda i, j: (layer, j, 0)),
                  pl.BlockSpec((None, ROW_BLOCK * SUBLANES, nkeys), lambda i, j: (j, i, 0)),
                  pl.BlockSpec((ROW_BLOCK, d), lambda i, j: (i, 0)),
                  pl.BlockSpec((None, SUBLANES, d), lambda i, j: (_group_of(i, bps, n_batch), 0, 0))],
        out_specs=pl.BlockSpec((ROW_BLOCK, d), lambda i, j: (i, 0)),
        out_shape=jax.ShapeDtypeStruct((t, d), f32),
        scratch_shapes=[pltpu.VMEM((ROW_BLOCK, d), f32), pltpu.VMEM((2, ROW_BLOCK, te), f32)],
        compiler_params=_cparams(2),
        name="peer_dense",
    )(f, u_all, v_all, gates, x, mods)


def kernel(x, c, ctx, c_ctx, mod_w, mod_b, norm_w, final_norm_w, a_wqkv, a_q_gain, a_k_gain, a_wo,
           b_wqkv, b_rpb, b_wo, pool_w, pool_scale, peer_wq, peer_sub_keys, peer_u, peer_v):
    n_batch, s, d = x.shape
    c_len = ctx.shape[1]
    depth = mod_w.shape[0]
    assert s % ROW_BLOCK == 0 and (n_batch * c_len) % ROW_BLOCK == 0 and n_batch * c_len <= s
    assert s % c_len == 0 and c_len % POOL_ROW_BLOCK == 0 and s % GRID_W == 0
    assert a_q_gain.shape[1] == LANES and peer_sub_keys.shape[2] == LANES and peer_sub_keys.shape[3] == LANES
    assert PEER_TOPK & (PEER_TOPK - 1) == 0 and peer_sub_keys.shape[2] & (peer_sub_keys.shape[2] - 1) == 0
    assert depth % N_MIXERS != 0

    bps = s // ROW_BLOCK
    nb_lat = n_batch * bps
    nb_all = nb_lat + n_batch * c_len // ROW_BLOCK
    t_lat = n_batch * s
    q_dim = a_wo.shape[1]
    kv_dim = (a_wqkv.shape[2] - q_dim) // 2
    n_kv = kv_dim // LANES
    groups = q_dim // kv_dim
    b_heads = b_rpb.shape[1]
    win_r, win_c = (b_rpb.shape[2] + 1) // 2, (b_rpb.shape[3] + 1) // 2
    nkeys = peer_sub_keys.shape[2]
    scale = SM_SCALE_LOG2

    xs = jnp.concatenate([x.reshape(t_lat, d), ctx.reshape(n_batch * c_len, d)], axis=0)
    cvec = jnp.concatenate([c, c_ctx[None, :], jnp.zeros((SUBLANES - n_batch - 1, d), f32)], axis=0)
    mods_all = _mods(cvec, mod_w, mod_b)
    tables = _rope_tables(s)
    u_bf, v_bf = peer_u.astype(bf16), peer_v.astype(bf16)

    for i in range(depth):
        last = i == depth - 1
        kind, j = i % N_MIXERS, i // N_MIXERS
        nb = nb_lat if last else nb_all
        mods = mods_all[i, :n_batch + 1].reshape(n_batch + 1, N_MOD, d)
        mods = jnp.pad(mods, ((0, 0), (0, SUBLANES - N_MOD), (0, 0)))

        if kind == 2:
            h = _norm_mod(xs, norm_w[i, 0], mods, 0, 1, nb_all, bps, n_batch, f32)
            xs = _pool(h, xs, pool_w[j].astype(bf16), pool_scale[j], mods, 2, n_batch, s, c_len)
        else:
            h = _norm_mod(xs, norm_w[i, 0], mods, 0, 1, nb_all, bps, n_batch, bf16)
            if kind == 0:
                qkv = _qkv_gqa(h, a_wqkv[j].astype(bf16), a_q_gain[j], a_k_gain[j], tables, q_dim, kv_dim,
                               nb_all, bps, n_batch)
                o = _attn_lat(qkv, qkv, qkv, 0, q_dim, q_dim + kv_dim, groups, n_kv, n_batch, s, c_len, 1.0, q_dim)
                if not last:
                    o_ctx = _attn_ctx(qkv, qkv, qkv, 0, q_dim, q_dim + kv_dim, groups, n_kv, n_batch, s, c_len, 1.0)
                    o = jnp.concatenate([o, o_ctx], axis=0)
                wo = a_wo[j]
            else:
                qkv = _mm(h, b_wqkv[j].astype(bf16), nb_all, _col_tile(3 * d), bf16)
                rows = s // GRID_W
                bias = _natten_bias(b_rpb[j], rows, min(win_r, rows), win_r, win_c)
                o = _natten_lat(qkv, bias, b_heads, n_batch, s, c_len, d)
                if not last:
                    o_ctx = _attn_ctx(qkv, qkv, qkv, 0, d, 2 * d, 1, b_heads, n_batch, s, c_len, scale)
                    o = jnp.concatenate([o, o_ctx], axis=0)
                wo = b_wo[j]
            xs = _mm_res(o, wo.astype(bf16), xs, mods, 2, nb, bps, n_batch, _col_tile(d))

        f = _norm_mod(xs, norm_w[i, 1], mods, 3, 4, nb, bps, n_batch, bf16)
        qp = _mm(f, peer_wq[i].astype(bf16), nb, _col_tile(peer_wq.shape[2]), f32)
        e, g = _peer_topk(qp, peer_sub_keys[i], nb)
        gates = _peer_gates(e, g, nkeys, nb)
        xs = _peer_dense(f, u_bf, v_bf, i, gates, xs, mods, 5, nb, bps, n_batch)

    return _final_norm(xs, final_norm_w, nb_lat).reshape(n_batch, s, d)
```

```python
import functools

import numpy as np
import jax
import jax.numpy as jnp
from jax import lax
from jax.experimental import pallas as pl
from jax.experimental.pallas import tpu as pltpu

GRID_W = 64
ROPE_BASE = 10000.0
POOL_WINDOWS = (2, 4, 8, 16)
PEER_TOPK = 16
EPS = 1e-6
N_MIXERS = 3
N_MOD = 6

LANES = 128
SUBLANES = 8
ROW_BLOCK = 512
POOL_ROW_BLOCK = 256
POOL_HALO = 8
MM_COL_TILE = 2048
ATTN_Q_ROWS = 128
ATTN_Q_SUBBLOCKS = 2
ATTN_KV_CHUNK = 4096
NATTEN_UNROLL = 8
GATES_UNROLL = 32
VMEM_LIMIT = 56 * 1024 * 1024
NEG = -1e30
LOG2E = 1.4426950408889634
SM_SCALE_LOG2 = float(LANES) ** -0.5 * LOG2E

f32 = jnp.float32
bf16 = jnp.bfloat16
_NT = (((1,), (1,)), ((), ()))


def _cparams(n_axes):
    return pltpu.CompilerParams(dimension_semantics=("arbitrary",) * n_axes, vmem_limit_bytes=VMEM_LIMIT)


def _group_of(i, blocks_per_seq, n_batch):
    return jnp.minimum(i // blocks_per_seq, n_batch)


def _mods_kernel(cv_ref, w_ref, b_ref, o_ref):
    cv = cv_ref[...]
    a = (cv / (1.0 + jnp.exp(-cv))).astype(bf16)
    o_ref[...] = jnp.dot(a, w_ref[...].astype(bf16), preferred_element_type=f32) + b_ref[...]


def _mods(cvec, mod_w, mod_b):
    depth, d, n = mod_w.shape
    tn = 1024
    return pl.pallas_call(
        _mods_kernel,
        grid=(depth, n // tn),
        in_specs=[pl.BlockSpec((SUBLANES, d), lambda l, j: (0, 0)),
                  pl.BlockSpec((None, d, tn), lambda l, j: (l, 0, j)),
                  pl.BlockSpec((None, 1, tn), lambda l, j: (l, 0, j))],
        out_specs=pl.BlockSpec((None, SUBLANES, tn), lambda l, j: (l, 0, j)),
        out_shape=jax.ShapeDtypeStruct((depth, SUBLANES, n), f32),
        compiler_params=_cparams(2),
        name="adaln_mods",
    )(cvec, mod_w, mod_b.reshape(depth, 1, n))


def _norm_mod_kernel(x_ref, nw_ref, m_ref, o_ref, *, shift_row, scale_row):
    x = x_ref[...]
    y = x * lax.rsqrt(jnp.mean(x * x, axis=-1, keepdims=True) + EPS) * nw_ref[...]
    y = y * (1.0 + m_ref[scale_row:scale_row + 1, :]) + m_ref[shift_row:shift_row + 1, :]
    o_ref[...] = y.astype(o_ref.dtype)


def _norm_mod(x, nw, mods, shift_row, scale_row, n_blocks, bps, n_batch, out_dtype):
    t, d = x.shape
    return pl.pallas_call(
        functools.partial(_norm_mod_kernel, shift_row=shift_row, scale_row=scale_row),
        grid=(n_blocks,),
        in_specs=[pl.BlockSpec((ROW_BLOCK, d), lambda i: (i, 0)),
                  pl.BlockSpec((1, d), lambda i: (0, 0)),
                  pl.BlockSpec((None, SUBLANES, d), lambda i: (_group_of(i, bps, n_batch), 0, 0))],
        out_specs=pl.BlockSpec((ROW_BLOCK, d), lambda i: (i, 0)),
        out_shape=jax.ShapeDtypeStruct((n_blocks * ROW_BLOCK, d), out_dtype),
        compiler_params=_cparams(1),
        name="norm_mod",
    )(x, nw.reshape(1, d), mods)


def _final_norm_kernel(x_ref, nw_ref, o_ref):
    x = x_ref[...]
    o_ref[...] = x * lax.rsqrt(jnp.mean(x * x, axis=-1, keepdims=True) + EPS) * nw_ref[...]


def _final_norm(x, nw, n_blocks):
    t, d = x.shape
    return pl.pallas_call(
        _final_norm_kernel,
        grid=(n_blocks,),
        in_specs=[pl.BlockSpec((ROW_BLOCK, d), lambda i: (i, 0)),
                  pl.BlockSpec((1, d), lambda i: (0, 0))],
        out_specs=pl.BlockSpec((ROW_BLOCK, d), lambda i: (i, 0)),
        out_shape=jax.ShapeDtypeStruct((n_blocks * ROW_BLOCK, d), f32),
        compiler_params=_cparams(1),
        name="final_norm",
    )(x, nw.reshape(1, d))


def _mm_kernel(a_ref, w_ref, o_ref):
    o_ref[...] = jnp.dot(a_ref[...], w_ref[...], preferred_element_type=f32).astype(o_ref.dtype)


def _col_tile(n):
    return max(t for t in range(LANES, min(n, MM_COL_TILE) + 1, LANES) if n % t == 0)


def _mm(a, w, n_blocks, tn, out_dtype):
    t, k = a.shape
    n = w.shape[1]
    return pl.pallas_call(
        _mm_kernel,
        grid=(n // tn, n_blocks),
        in_specs=[pl.BlockSpec((ROW_BLOCK, k), lambda j, i: (i, 0)),
                  pl.BlockSpec((k, tn), lambda j, i: (0, j))],
        out_specs=pl.BlockSpec((ROW_BLOCK, tn), lambda j, i: (i, j)),
        out_shape=jax.ShapeDtypeStruct((n_blocks * ROW_BLOCK, n), out_dtype),
        compiler_params=_cparams(2),
        name="matmul",
    )(a, w)


def _mm_res_kernel(a_ref, w_ref, x_ref, m_ref, o_ref, *, gate_row):
    y = jnp.dot(a_ref[...], w_ref[...], preferred_element_type=f32)
    o_ref[...] = x_ref[...] + m_ref[gate_row:gate_row + 1, :] * y


def _mm_res(a, w, x, mods, gate_row, n_blocks, bps, n_batch, tn):
    t, k = a.shape
    n = w.shape[1]
    return pl.pallas_call(
        functools.partial(_mm_res_kernel, gate_row=gate_row),
        grid=(n // tn, n_blocks),
        in_specs=[pl.BlockSpec((ROW_BLOCK, k), lambda j, i: (i, 0)),
                  pl.BlockSpec((k, tn), lambda j, i: (0, j)),
                  pl.BlockSpec((ROW_BLOCK, tn), lambda j, i: (i, j)),
                  pl.BlockSpec((None, SUBLANES, tn), lambda j, i: (_group_of(i, bps, n_batch), 0, j))],
        out_specs=pl.BlockSpec((ROW_BLOCK, tn), lambda j, i: (i, j)),
        out_shape=jax.ShapeDtypeStruct((n_blocks * ROW_BLOCK, n), f32),
        compiler_params=_cparams(2),
        name="matmul_residual",
    )(a, w, x, mods)


def _qkv_gqa_kernel(a_ref, w_ref, qg_ref, kg_ref, cos_ref, sa_ref, sb_ref, o_ref, *, nq_tiles, nk_tiles, scale):
    j = pl.program_id(0)
    acc = jnp.dot(a_ref[...], w_ref[...], preferred_element_type=f32)
    heads = acc.shape[1] // LANES

    def prep(gain, post):
        cos, sa, sb = cos_ref[...], sa_ref[...], sb_ref[...]
        for h in range(heads):
            y = acc[:, h * LANES:(h + 1) * LANES]
            y = y * lax.rsqrt(jnp.mean(y * y, axis=-1, keepdims=True) + EPS) * gain
            y = y * cos + pltpu.roll(y, LANES - 32, 1) * sa + pltpu.roll(y, 32, 1) * sb
            o_ref[:, h * LANES:(h + 1) * LANES] = (y * post).astype(o_ref.dtype)

    @pl.when(j < nq_tiles)
    def _():
        prep(qg_ref[...], scale)

    @pl.when(jnp.logical_and(j >= nq_tiles, j < nq_tiles + nk_tiles))
    def _():
        prep(kg_ref[...], 1.0)

    @pl.when(j >= nq_tiles + nk_tiles)
    def _():
        o_ref[...] = acc.astype(o_ref.dtype)


def _rope_tables(s):
    t = np.arange(s)
    half = LANES // 2
    inv = ROPE_BASE ** (-jnp.arange(0, half, 2, dtype=f32) / half)
    ang_r = (t // GRID_W).astype(np.float32)[:, None] * inv[None, :]
    ang_c = (t % GRID_W).astype(np.float32)[:, None] * inv[None, :]
    ang = jnp.concatenate([ang_r, ang_r, ang_c, ang_c], axis=-1)
    cos, sin = jnp.cos(ang), jnp.sin(ang)
    first = (np.arange(LANES) % half) < (half // 2)
    sa = jnp.where(first[None, :], -sin, 0.0)
    sb = jnp.where(first[None, :], 0.0, sin)
    pad1 = jnp.ones((ROW_BLOCK, LANES), f32)
    pad0 = jnp.zeros((ROW_BLOCK, LANES), f32)
    return (jnp.concatenate([cos, pad1], 0), jnp.concatenate([sa, pad0], 0), jnp.concatenate([sb, pad0], 0))


def _qkv_gqa(h, wqkv, q_gain, k_gain, tables, q_dim, kv_dim, n_blocks, bps, n_batch):
    t, k = h.shape
    n = wqkv.shape[1]
    tn = min(512, kv_dim)
    cos, sa, sb = tables

    def pos_map(j, i):
        return (jnp.where(i < bps * n_batch, i % bps, bps), 0)

    return pl.pallas_call(
        functools.partial(_qkv_gqa_kernel, nq_tiles=q_dim // tn, nk_tiles=kv_dim // tn, scale=SM_SCALE_LOG2),
        grid=(n // tn, n_blocks),
        in_specs=[pl.BlockSpec((ROW_BLOCK, k), lambda j, i: (i, 0)),
                  pl.BlockSpec((k, tn), lambda j, i: (0, j)),
                  pl.BlockSpec((1, LANES), lambda j, i: (0, 0)),
                  pl.BlockSpec((1, LANES), lambda j, i: (0, 0)),
                  pl.BlockSpec((ROW_BLOCK, LANES), pos_map),
                  pl.BlockSpec((ROW_BLOCK, LANES), pos_map),
                  pl.BlockSpec((ROW_BLOCK, LANES), pos_map)],
        out_specs=pl.BlockSpec((ROW_BLOCK, tn), lambda j, i: (i, j)),
        out_shape=jax.ShapeDtypeStruct((t, n), bf16),
        compiler_params=_cparams(2),
        name="qkv_gqa",
    )(h, wqkv, q_gain.reshape(1, LANES), k_gain.reshape(1, LANES), cos, sa, sb)


def _flash_kernel(*refs, groups, n_lat_chunks, ck, scale):
    if n_lat_chunks:
        q_ref, kc_ref, vc_ref, kl_ref, vl_ref, o_ref = refs
    else:
        q_ref, kc_ref, vc_ref, o_ref = refs
    tq = q_ref.shape[0]
    q = jnp.concatenate([q_ref[:, g * LANES:(g + 1) * LANES] for g in range(groups)], axis=0)
    rows = groups * tq

    def step(k, v, m, l, acc):
        s = lax.dot_general(q, k, _NT, preferred_element_type=f32)
        if scale != 1.0:
            s = s * scale
        m_new = jnp.maximum(m, jnp.max(s, axis=-1, keepdims=True))
        p = jnp.exp2(s - m_new)
        alpha = jnp.exp2(m - m_new)
        l = alpha * l + jnp.sum(p, axis=-1, keepdims=True)
        acc = alpha * acc + jnp.dot(p.astype(bf16), v, preferred_element_type=f32)
        return m_new, l, acc

    carry = step(kc_ref[...], vc_ref[...], jnp.full((rows, 1), NEG, f32), jnp.zeros((rows, 1), f32),
                 jnp.zeros((rows, LANES), f32))
    if n_lat_chunks:
        def body(c, carry):
            off = pl.multiple_of(c * ck, ck)
            return step(kl_ref[pl.ds(off, ck), :], vl_ref[pl.ds(off, ck), :], *carry)
        carry = lax.fori_loop(0, n_lat_chunks, body, carry)
    _, l, acc = carry
    o = acc / l
    for g in range(groups):
        o_ref[:, g * LANES:(g + 1) * LANES] = o[g * tq:(g + 1) * tq].astype(o_ref.dtype)


def _gqa_lat_kernel(q_ref, kc_ref, vc_ref, kl_ref, vl_ref, o_ref, *scratch, groups, n_sub):
    tq = q_ref.shape[0] // n_sub
    ls = []
    for h in range(n_sub):
        sc_ref, sl_ref, pc_ref, pl_ref = scratch[4 * h:4 * h + 4]
        q = jnp.concatenate([q_ref[h * tq:(h + 1) * tq, g * LANES:(g + 1) * LANES] for g in range(groups)], axis=0)
        sc_ref[...] = lax.dot_general(q, kc_ref[...], _NT, preferred_element_type=f32)
        sl_ref[...] = lax.dot_general(q, kl_ref[...], _NT, preferred_element_type=f32)
        m = jnp.maximum(jnp.max(sc_ref[...], axis=-1, keepdims=True), jnp.max(sl_ref[...], axis=-1, keepdims=True))
        pc = jnp.exp2(sc_ref[...] - m)
        pl_ = jnp.exp2(sl_ref[...] - m)
        ls.append(jnp.sum(pc, axis=-1, keepdims=True) + jnp.sum(pl_, axis=-1, keepdims=True))
        pc_ref[...] = pc.astype(bf16)
        pl_ref[...] = pl_.astype(bf16)
    for h in range(n_sub):
        pc_ref, pl_ref = scratch[4 * h + 2:4 * h + 4]
        o = (jnp.dot(pc_ref[...], vc_ref[...], preferred_element_type=f32)
             + jnp.dot(pl_ref[...], vl_ref[...], preferred_element_type=f32)) / ls[h]
        for g in range(groups):
            o_ref[h * tq:(h + 1) * tq, g * LANES:(g + 1) * LANES] = o[g * tq:(g + 1) * tq].astype(o_ref.dtype)


def _attn_lat(q, k, v, q_col0, k_col0, v_col0, groups, n_kv, n_batch, s, c, out_cols):
    n_sub = ATTN_Q_SUBBLOCKS
    tq = n_sub * ATTN_Q_ROWS
    nqb = s // tq
    ctx_blk0 = n_batch * s // c
    qw = groups * LANES
    return pl.pallas_call(
        functools.partial(_gqa_lat_kernel, groups=groups, n_sub=n_sub),
        grid=(n_batch, n_kv, nqb),
        in_specs=[pl.BlockSpec((tq, qw), lambda b, h, i: (b * nqb + i, q_col0 // qw + h)),
                  pl.BlockSpec((c, LANES), lambda b, h, i: (ctx_blk0 + b, k_col0 // LANES + h)),
                  pl.BlockSpec((c, LANES), lambda b, h, i: (ctx_blk0 + b, v_col0 // LANES + h)),
                  pl.BlockSpec((s, LANES), lambda b, h, i: (b, k_col0 // LANES + h)),
                  pl.BlockSpec((s, LANES), lambda b, h, i: (b, v_col0 // LANES + h))],
        out_specs=pl.BlockSpec((tq, qw), lambda b, h, i: (b * nqb + i, h)),
        out_shape=jax.ShapeDtypeStruct((n_batch * s, out_cols), bf16),
        scratch_shapes=[pltpu.VMEM((groups * ATTN_Q_ROWS, n), dt)
                        for _ in range(n_sub) for dt in (f32, bf16) for n in (c, s)],
        compiler_params=_cparams(3),
        name="attn_latent",
    )(q, k, v, k, v)


def _attn_ctx(q, k, v, q_col0, k_col0, v_col0, groups, n_kv, n_batch, s, c, scale):
    ctx_blk0 = n_batch * s // c
    qw = groups * LANES
    return pl.pallas_call(
        functools.partial(_flash_kernel, groups=groups, n_lat_chunks=0, ck=0, scale=scale),
        grid=(n_batch, n_kv),
        in_specs=[pl.BlockSpec((c, qw), lambda b, h: (ctx_blk0 + b, q_col0 // qw + h)),
                  pl.BlockSpec((c, LANES), lambda b, h: (ctx_blk0 + b, k_col0 // LANES + h)),
                  pl.BlockSpec((c, LANES), lambda b, h: (ctx_blk0 + b, v_col0 // LANES + h))],
        out_specs=pl.BlockSpec((c, qw), lambda b, h: (b, h)),
        out_shape=jax.ShapeDtypeStruct((n_batch * c, n_kv * qw), bf16),
        compiler_params=_cparams(2),
        name="attn_context",
    )(q, k, v)


def _natten_kernel(q_ref, k_ref, v_ref, kc_ref, vc_ref, bias_ref, o_ref, *, rows, w, wr, scale):
    kc = kc_ref[...]
    vc = vc_ref[...]

    nr = NATTEN_UNROLL

    def body(g, carry):
        base = pl.multiple_of(g * nr * w, nr * w)
        qg = q_ref[pl.ds(base, nr * w), :]
        scg = lax.dot_general(qg, kc, _NT, preferred_element_type=f32) * scale
        offs, sws = [], []
        for i in range(nr):
            r = g * nr + i
            r0 = jnp.clip(r - wr // 2, 0, rows - wr)
            off = pl.multiple_of(r0 * w, w)
            kb = k_ref[pl.ds(off, wr * w), :]
            sws.append(lax.dot_general(qg[i * w:(i + 1) * w], kb, _NT, preferred_element_type=f32) * scale
                       + bias_ref[r - r0])
            offs.append(off)
        pws, pcs, ls = [], [], []
        for i in range(nr):
            sc = scg[i * w:(i + 1) * w]
            m = jnp.maximum(jnp.max(sws[i], axis=-1, keepdims=True), jnp.max(sc, axis=-1, keepdims=True))
            pw = jnp.exp2(sws[i] - m)
            pc = jnp.exp2(sc - m)
            ls.append(jnp.sum(pw, axis=-1, keepdims=True) + jnp.sum(pc, axis=-1, keepdims=True))
            pws.append(pw.astype(bf16))
            pcs.append(pc.astype(bf16))
        ocg = jnp.dot(jnp.concatenate(pcs, axis=0), vc, preferred_element_type=f32)
        for i in range(nr):
            vb = v_ref[pl.ds(offs[i], wr * w), :]
            o = (jnp.dot(pws[i], vb, preferred_element_type=f32) + ocg[i * w:(i + 1) * w]) / ls[i]
            o_ref[pl.ds(pl.multiple_of(base + i * w, w), w), :] = o.astype(o_ref.dtype)
        return carry

    lax.fori_loop(0, rows // nr, body, 0)


def _natten_bias(rpb, rows, wr, win_r, win_c):
    w = GRID_W
    cols = np.arange(w)
    col_start = np.clip(cols - win_c // 2, 0, w - win_c)
    inside = (cols[None, :] >= col_start[:, None]) & (cols[None, :] < col_start[:, None] + win_c)
    rpbp = jnp.pad(rpb, ((0, 0), (0, 0), (w - win_c, w - win_c)))
    p = jnp.stack([rpbp[:, :, w - 1 - qc:2 * w - 1 - qc] for qc in range(w)], axis=2)
    p = jnp.where(inside[None, None], p * LOG2E, NEG)
    tab = jnp.stack([p[:, win_r - 1 - d:win_r - 1 - d + wr] for d in range(wr)], axis=1)
    return jnp.transpose(tab, (0, 1, 3, 2, 4)).reshape(rpb.shape[0], wr, w, wr * w)


def _natten_lat(qkv, bias, n_heads, n_batch, s, c, d_model):
    t = qkv.shape[0]
    rows = s // GRID_W
    wr = bias.shape[1]
    ctx_blk0 = n_batch * s // c
    hq, hk, hv = 0, d_model // LANES, 2 * d_model // LANES
    return pl.pallas_call(
        functools.partial(_natten_kernel, rows=rows, w=GRID_W, wr=wr, scale=SM_SCALE_LOG2),
        grid=(n_batch, n_heads),
        in_specs=[pl.BlockSpec((s, LANES), lambda b, h: (b, hq + h)),
                  pl.BlockSpec((s, LANES), lambda b, h: (b, hk + h)),
                  pl.BlockSpec((s, LANES), lambda b, h: (b, hv + h)),
                  pl.BlockSpec((c, LANES), lambda b, h: (ctx_blk0 + b, hk + h)),
                  pl.BlockSpec((c, LANES), lambda b, h: (ctx_blk0 + b, hv + h)),
                  pl.BlockSpec((None, wr, GRID_W, wr * GRID_W), lambda b, h: (h, 0, 0, 0))],
        out_specs=pl.BlockSpec((s, LANES), lambda b, h: (b, h)),
        out_shape=jax.ShapeDtypeStruct((n_batch * s, d_model), bf16),
        compiler_params=_cparams(2),
        name="natten_latent",
    )(qkv, qkv, qkv, qkv, qkv, bias)


def _pool_kernel(h_ref, hp_ref, hn_ref, x_ref, w_ref, ls_ref, m_ref, o_ref, pad_ref, *,
                 bps_lat, bps_ctx, n_lat_blocks, s, c, gate_row):
    i = pl.program_id(0)
    tm = h_ref.shape[0]
    is_lat = i < n_lat_blocks
    blk = jnp.where(is_lat, i % bps_lat, (i - n_lat_blocks) % bps_ctx)
    nblk = jnp.where(is_lat, bps_lat, bps_ctx)
    length = jnp.where(is_lat, s, c)
    pad_ref[0:POOL_HALO, :] = jnp.where(blk == 0, 0.0, hp_ref[...])
    pad_ref[POOL_HALO:POOL_HALO + tm, :] = h_ref[...]
    pad_ref[POOL_HALO + tm:2 * POOL_HALO + tm, :] = jnp.where(blk == nblk - 1, 0.0, hn_ref[...])
    pos = blk * tm + lax.broadcasted_iota(jnp.int32, (tm, 1), 0)
    pg = w_ref.shape[1]
    for g, win in enumerate(POOL_WINDOWS):
        cs = slice(g * pg, (g + 1) * pg)
        lo_off, hi_off = win // 2, win - win // 2
        acc = pad_ref[POOL_HALO - lo_off:POOL_HALO - lo_off + tm, cs]
        for k in range(-lo_off + 1, hi_off):
            acc = acc + pad_ref[POOL_HALO + k:POOL_HALO + k + tm, cs]
        cnt = jnp.minimum(pos + hi_off, length) - jnp.maximum(pos - lo_off, 0)
        y = acc * (1.0 / cnt.astype(f32)) - h_ref[:, cs]
        z = jnp.dot(y.astype(bf16), w_ref[g], preferred_element_type=f32) * ls_ref[:, cs]
        o_ref[:, cs] = x_ref[:, cs] + m_ref[gate_row:gate_row + 1, cs] * z


def _pool(h, x, pool_w, pool_scale, mods, gate_row, n_batch, s, c):
    t, d = h.shape
    tm = POOL_ROW_BLOCK
    nb = t // tm
    bps_lat, bps_ctx = s // tm, c // tm
    hb = tm // POOL_HALO
    last_halo = t // POOL_HALO - 1
    return pl.pallas_call(
        functools.partial(_pool_kernel, bps_lat=bps_lat, bps_ctx=bps_ctx, n_lat_blocks=n_batch * bps_lat,
                          s=s, c=c, gate_row=gate_row),
        grid=(nb,),
        in_specs=[pl.BlockSpec((tm, d), lambda i: (i, 0)),
                  pl.BlockSpec((POOL_HALO, d), lambda i: (jnp.maximum(i * hb - 1, 0), 0)),
                  pl.BlockSpec((POOL_HALO, d), lambda i: (jnp.minimum((i + 1) * hb, last_halo), 0)),
                  pl.BlockSpec((tm, d), lambda i: (i, 0)),
                  pl.BlockSpec(pool_w.shape, lambda i: (0, 0, 0)),
                  pl.BlockSpec((1, d), lambda i: (0, 0)),
                  pl.BlockSpec((None, SUBLANES, d), lambda i: (_group_of(i, bps_lat, n_batch), 0, 0))],
        out_specs=pl.BlockSpec((tm, d), lambda i: (i, 0)),
        out_shape=jax.ShapeDtypeStruct((t, d), f32),
        scratch_shapes=[pltpu.VMEM((tm + 2 * POOL_HALO, d), f32)],
        compiler_params=_cparams(1),
        name="pool_mixer",
    )(h, h, h, x, pool_w, pool_scale.reshape(1, d), mods)


def _topk_rows(chains, tb):
    rows = lax.broadcasted_iota(jnp.int32, (PEER_TOPK, tb), 0)

    def body(k, carry):
        out = []
        for (s_ref, order), (vals, idxs) in zip(chains, carry):
            s = s_ref[...]
            m = jnp.max(s, axis=0, keepdims=True)
            am = jnp.min(jnp.where(s == m, order, float(2 ** 23)), axis=0, keepdims=True)
            s_ref[...] = jnp.where(order == am, NEG, s)
            out.append((jnp.where(rows == k, m, vals), jnp.where(rows == k, am, idxs)))
        return tuple(out)

    zero = jnp.zeros((PEER_TOPK, tb), f32)
    res = lax.fori_loop(0, PEER_TOPK, body, tuple((zero, zero) for _ in chains))
    return [(vals, idxs.astype(jnp.int32)) for vals, idxs in res]


def _cand_blocks():
    blocks = [(0, PEER_TOPK)] + [(a, SUBLANES) for a in range(1, SUBLANES)]
    assert all((a + 1) * (nb + 1) > PEER_TOPK for a, nb in blocks[1:]) and 2 * SUBLANES == PEER_TOPK
    return blocks


def _peer_topk_kernel(q_ref, sk_ref, e_ref, g_ref, s1_ref, s2_ref, cand_ref, es_ref, gs_ref, *, n_heads, nkeys):
    tb = q_ref.shape[0]
    key_order = lax.broadcasted_iota(jnp.int32, (nkeys, tb), 0).astype(f32)
    blocks = _cand_blocks()
    n_mid = (len(blocks) - 1) * SUBLANES
    r = lax.broadcasted_iota(jnp.int32, cand_ref.shape, 0)
    rm = r - PEER_TOPK
    mid = (lax.shift_right_logical(rm, 3) + 1) * PEER_TOPK + lax.bitwise_and(rm, SUBLANES - 1)
    tail = (rm - n_mid + SUBLANES) * PEER_TOPK
    cand_order = jnp.where(r < PEER_TOPK, r, jnp.where(rm < n_mid, mid, tail)).astype(f32)

    def scores(h):
        col = pl.multiple_of(h * 2 * LANES, 2 * LANES)
        s1_ref[...] = lax.dot_general(sk_ref[0], q_ref[:, pl.ds(col, LANES)], _NT, preferred_element_type=f32)
        s2_ref[...] = lax.dot_general(sk_ref[1], q_ref[:, pl.ds(col + LANES, LANES)], _NT, preferred_element_type=f32)

    sub_chains = [(s1_ref, key_order), (s2_ref, key_order)]
    scores(0)
    (v1, i1), (v2, i2) = _topk_rows(sub_chains, tb)

    def head(h, carry):
        v1, i1, v2, i2 = carry
        r0 = 0
        for a, nb in blocks:
            cand_ref[r0:r0 + nb, :] = v1[a:a + 1, :] + v2[:nb]
            r0 += nb
        cand_ref[r0:r0 + SUBLANES, :] = v1[SUBLANES:] + v2[0:1, :]
        scores(jnp.minimum(h + 1, n_heads - 1))
        nxt1, nxt2, (sc, ci) = _topk_rows(sub_chains + [(cand_ref, cand_order)], tb)
        ca = lax.shift_right_logical(ci, PEER_TOPK.bit_length() - 1)
        cb = lax.bitwise_and(ci, PEER_TOPK - 1)
        e1 = jnp.zeros((PEER_TOPK, tb), jnp.int32)
        e2 = jnp.zeros((PEER_TOPK, tb), jnp.int32)
        for a in range(PEER_TOPK):
            e1 = jnp.where(ca == a, i1[a:a + 1, :], e1)
            e2 = jnp.where(cb == a, i2[a:a + 1, :], e2)
        p = jnp.exp(sc - sc[0:1, :])
        gate = p / jnp.sum(p, axis=0, keepdims=True)
        row = pl.multiple_of(h * PEER_TOPK, PEER_TOPK)
        es_ref[pl.ds(row, PEER_TOPK), :] = e1 * nkeys + e2
        gs_ref[pl.ds(row, PEER_TOPK), :] = gate
        return nxt1 + nxt2

    lax.fori_loop(0, n_heads, head, (v1, i1, v2, i2))
    e_ref[...] = es_ref[...].T
    g_ref[...] = gs_ref[...].T


def _peer_topk(q, sub_keys, n_blocks_rows):
    t, qd = q.shape
    nkeys = sub_keys.shape[1]
    n_heads = qd // (2 * LANES)
    slots = n_heads * PEER_TOPK
    tb = 256
    return pl.pallas_call(
        functools.partial(_peer_topk_kernel, n_heads=n_heads, nkeys=nkeys),
        grid=(n_blocks_rows * ROW_BLOCK // tb,),
        in_specs=[pl.BlockSpec((tb, qd), lambda i: (i, 0)),
                  pl.BlockSpec(sub_keys.shape, lambda i: (0, 0, 0))],
        out_specs=[pl.BlockSpec((tb, slots), lambda i: (i, 0)),
                   pl.BlockSpec((tb, slots), lambda i: (i, 0))],
        out_shape=[jax.ShapeDtypeStruct((n_blocks_rows * ROW_BLOCK, slots), jnp.int32),
                   jax.ShapeDtypeStruct((n_blocks_rows * ROW_BLOCK, slots), f32)],
        scratch_shapes=[pltpu.VMEM((nkeys, tb), f32), pltpu.VMEM((nkeys, tb), f32),
                        pltpu.VMEM((sum(nb for _, nb in _cand_blocks()) + SUBLANES, tb), f32),
                        pltpu.VMEM((slots, tb), jnp.int32), pltpu.VMEM((slots, tb), f32)],
        compiler_params=_cparams(1),
        name="peer_topk",
    )(q, sub_keys)


def _peer_gates_kernel(e_ref, g_ref, o_ref, *, nkeys):
    tb, slots = e_ref.shape
    iota = lax.broadcasted_iota(jnp.int32, (nkeys, slots), 0)

    def body(t, carry):
        e = e_ref[pl.ds(t, 1), :]
        g = g_ref[pl.ds(t, 1), :]
        e1 = lax.shift_right_logical(e, nkeys.bit_length() - 1)
        e2 = lax.bitwise_and(e, nkeys - 1)
        w1 = jnp.where(e1 == iota, g, 0.0).astype(bf16)
        o2 = jnp.where(e2 == iota, 1.0, 0.0).astype(bf16)
        res = lax.dot_general(w1, o2, _NT, preferred_element_type=f32)
        bits = lax.bitcast_convert_type(res.astype(bf16).astype(f32), jnp.uint32)
        packed = jnp.bitwise_or(jnp.right_shift(bits[:nkeys // 2], jnp.uint32(16)), bits[nkeys // 2:])
        o_ref[:, t] = packed.reshape(nkeys // (2 * SUBLANES), SUBLANES, nkeys)
        return carry

    lax.fori_loop(0, tb, body, 0, unroll=GATES_UNROLL)


def _peer_gates(e, g, nkeys, n_blocks_rows):
    t, slots = e.shape
    tb = 128
    nblk = nkeys // (2 * SUBLANES)
    return pl.pallas_call(
        functools.partial(_peer_gates_kernel, nkeys=nkeys),
        grid=(n_blocks_rows * ROW_BLOCK // tb,),
        in_specs=[pl.BlockSpec((tb, slots), lambda i: (i, 0)),
                  pl.BlockSpec((tb, slots), lambda i: (i, 0))],
        out_specs=pl.BlockSpec((nblk, tb, SUBLANES, nkeys), lambda i: (0, i, 0, 0)),
        out_shape=jax.ShapeDtypeStruct((nblk, t, SUBLANES, nkeys), jnp.uint32),
        compiler_params=_cparams(1),
        name="peer_gates",
    )(e, g)


def _gelu_tanh(x):
    k0 = -2.0 * 0.7978845608028654 * 1.4426950408889634
    return x / (1.0 + jnp.exp2(x * (k0 + (k0 * 0.044715) * (x * x))))


def _peer_dense_kernel(f_ref, u_ref, v_ref, g_ref, x_ref, m_ref, o_ref, acc_ref, a_ref, *, gate_row, nkeys):
    j = pl.program_id(1)

    @pl.when(j == 0)
    def _():
        acc_ref[...] = jnp.zeros_like(acc_ref)

    slot = j % 2
    a_ref[slot] = lax.dot_general(f_ref[...], u_ref[...], _NT, preferred_element_type=f32)
    tb = f_ref.shape[0]
    n_first = g_ref.shape[0] // tb
    shift = jnp.where(j < pl.num_programs(1) // 2, 16, 0).astype(jnp.uint32)
    parts = []
    for k in range(n_first):
        word = g_ref[pl.ds(k, tb, stride=n_first), :]
        gbits = jnp.bitwise_and(jnp.left_shift(word, shift), jnp.uint32(0xFFFF0000))
        gk = lax.bitcast_convert_type(gbits, f32)
        parts.append((_gelu_tanh(a_ref[slot, :, k * nkeys:(k + 1) * nkeys]) * gk).astype(bf16))
    ga = jnp.concatenate(parts, axis=1)
    acc_ref[...] += jnp.dot(ga, v_ref[...], preferred_element_type=f32)

    @pl.when(j == pl.num_programs(1) - 1)
    def _():
        o_ref[...] = x_ref[...] + m_ref[gate_row:gate_row + 1, :] * acc_ref[...]


def _peer_dense(f, u_all, v_all, layer, gates, x, mods, gate_row, n_blocks, bps, n_batch):
    t, d = f.shape
    ne = v_all.shape[1]
    nkeys = gates.shape[3]
    te = SUBLANES * nkeys
    n_gblk = gates.shape[0]
    assert ne // te == 2 * n_gblk
    gates = gates.reshape(n_gblk, gates.shape[1] * SUBLANES, nkeys)
    return pl.pallas_call(
        functools.partial(_peer_dense_kernel, gate_row=gate_row, nkeys=nkeys),
        grid=(n_blocks, ne // te),
        in_specs=[pl.BlockSpec((ROW_BLOCK, d), lambda i, j: (i, 0)),
                  pl.BlockSpec((None, te, d), lambda i, j: (layer, j, 0)),
                  pl.BlockSpec((None, te, d), lambda i, j: (layer, j, 0)),
                  pl.BlockSpec((None, ROW_BLOCK * SUBLANES, nkeys), lambda i, j: (j % n_gblk, i, 0)),
                  pl.BlockSpec((ROW_BLOCK, d), lambda i, j: (i, 0)),
                  pl.BlockSpec((None, SUBLANES, d), lambda i, j: (_group_of(i, bps, n_batch), 0, 0))],
        out_specs=pl.BlockSpec((ROW_BLOCK, d), lambda i, j: (i, 0)),
        out_shape=jax.ShapeDtypeStruct((t, d), f32),
        scratch_shapes=[pltpu.VMEM((ROW_BLOCK, d), f32), pltpu.VMEM((2, ROW_BLOCK, te), f32)],
        compiler_params=_cparams(2),
        name="peer_dense",
    )(f, u_all, v_all, gates, x, mods)


def kernel(x, c, ctx, c_ctx, mod_w, mod_b, norm_w, final_norm_w, a_wqkv, a_q_gain, a_k_gain, a_wo,
           b_wqkv, b_rpb, b_wo, pool_w, pool_scale, peer_wq, peer_sub_keys, peer_u, peer_v):
    n_batch, s, d = x.shape
    c_len = ctx.shape[1]
    depth = mod_w.shape[0]
    assert s % ROW_BLOCK == 0 and (n_batch * c_len) % ROW_BLOCK == 0 and n_batch * c_len <= s
    assert s % c_len == 0 and c_len % POOL_ROW_BLOCK == 0 and s % GRID_W == 0
    assert a_q_gain.shape[1] == LANES and peer_sub_keys.shape[2] == LANES and peer_sub_keys.shape[3] == LANES
    assert PEER_TOPK & (PEER_TOPK - 1) == 0 and peer_sub_keys.shape[2] & (peer_sub_keys.shape[2] - 1) == 0
    assert depth % N_MIXERS != 0

    bps = s // ROW_BLOCK
    nb_lat = n_batch * bps
    nb_all = nb_lat + n_batch * c_len // ROW_BLOCK
    t_lat = n_batch * s
    q_dim = a_wo.shape[1]
    kv_dim = (a_wqkv.shape[2] - q_dim) // 2
    n_kv = kv_dim // LANES
    groups = q_dim // kv_dim
    b_heads = b_rpb.shape[1]
    win_r, win_c = (b_rpb.shape[2] + 1) // 2, (b_rpb.shape[3] + 1) // 2
    nkeys = peer_sub_keys.shape[2]
    scale = SM_SCALE_LOG2

    xs = jnp.concatenate([x.reshape(t_lat, d), ctx.reshape(n_batch * c_len, d)], axis=0)
    cvec = jnp.concatenate([c, c_ctx[None, :], jnp.zeros((SUBLANES - n_batch - 1, d), f32)], axis=0)
    mods_all = _mods(cvec, mod_w, mod_b)
    tables = _rope_tables(s)
    u_bf, v_bf = peer_u.astype(bf16), peer_v.astype(bf16)

    for i in range(depth):
        last = i == depth - 1
        kind, j = i % N_MIXERS, i // N_MIXERS
        nb = nb_lat if last else nb_all
        mods = mods_all[i, :n_batch + 1].reshape(n_batch + 1, N_MOD, d)
        mods = jnp.pad(mods, ((0, 0), (0, SUBLANES - N_MOD), (0, 0)))

        if kind == 2:
            h = _norm_mod(xs, norm_w[i, 0], mods, 0, 1, nb_all, bps, n_batch, f32)
            xs = _pool(h, xs, pool_w[j].astype(bf16), pool_scale[j], mods, 2, n_batch, s, c_len)
        else:
            h = _norm_mod(xs, norm_w[i, 0], mods, 0, 1, nb_all, bps, n_batch, bf16)
            if kind == 0:
                qkv = _qkv_gqa(h, a_wqkv[j].astype(bf16), a_q_gain[j], a_k_gain[j], tables, q_dim, kv_dim,
                               nb_all, bps, n_batch)
                o = _attn_lat(qkv, qkv, qkv, 0, q_dim, q_dim + kv_dim, groups, n_kv, n_batch, s, c_len, q_dim)
                if not last:
                    o_ctx = _attn_ctx(qkv, qkv, qkv, 0, q_dim, q_dim + kv_dim, groups, n_kv, n_batch, s, c_len, 1.0)
                    o = jnp.concatenate([o, o_ctx], axis=0)
                wo = a_wo[j]
            else:
                qkv = _mm(h, b_wqkv[j].astype(bf16), nb_all, _col_tile(3 * d), bf16)
                rows = s // GRID_W
                bias = _natten_bias(b_rpb[j], rows, min(win_r, rows), win_r, win_c)
                o = _natten_lat(qkv, bias, b_heads, n_batch, s, c_len, d)
                if not last:
                    o_ctx = _attn_ctx(qkv, qkv, qkv, 0, d, 2 * d, 1, b_heads, n_batch, s, c_len, scale)
                    o = jnp.concatenate([o, o_ctx], axis=0)
                wo = b_wo[j]
            xs = _mm_res(o, wo.astype(bf16), xs, mods, 2, nb, bps, n_batch, _col_tile(d))

        f = _norm_mod(xs, norm_w[i, 1], mods, 3, 4, nb, bps, n_batch, bf16)
        qp = _mm(f, peer_wq[i].astype(bf16), nb, _col_tile(peer_wq.shape[2]), f32)
        e, g = _peer_topk(qp, peer_sub_keys[i], nb)
        gates = _peer_gates(e, g, nkeys, nb)
        xs = _peer_dense(f, u_bf, v_bf, i, gates, xs, mods, 5, nb, bps, n_batch)

    return _final_norm(xs, final_norm_w, nb_lat).reshape(n_batch, s, d)
```

```python
import functools

import numpy as np
import jax
import jax.numpy as jnp
from jax import lax
from jax.experimental import pallas as pl
from jax.experimental.pallas import tpu as pltpu

GRID_W = 64
ROPE_BASE = 10000.0
POOL_WINDOWS = (2, 4, 8, 16)
PEER_TOPK = 16
EPS = 1e-6
N_MIXERS = 3
N_MOD = 6

LANES = 128
SUBLANES = 8
ROW_BLOCK = 512
POOL_ROW_BLOCK = 256
POOL_HALO = 8
MM_COL_TILE = 2048
ATTN_Q_ROWS = 128
ATTN_Q_SUBBLOCKS = 2
NATTEN_UNROLL = 8
GATES_UNROLL = 32
VMEM_LIMIT = 56 * 1024 * 1024
NEG = -1e30
LOG2E = 1.4426950408889634
SM_SCALE_LOG2 = float(LANES) ** -0.5 * LOG2E

f32 = jnp.float32
bf16 = jnp.bfloat16
_NT = (((1,), (1,)), ((), ()))


def _cparams(n_axes):
    return pltpu.CompilerParams(dimension_semantics=("arbitrary",) * n_axes, vmem_limit_bytes=VMEM_LIMIT)


def _group_of(i, blocks_per_seq, n_batch):
    return jnp.minimum(i // blocks_per_seq, n_batch)


def _mods_kernel(cv_ref, w_ref, b_ref, o_ref):
    cv = cv_ref[...]
    a = (cv / (1.0 + jnp.exp(-cv))).astype(bf16)
    o_ref[...] = jnp.dot(a, w_ref[...].astype(bf16), preferred_element_type=f32) + b_ref[...]


def _mods(cvec, mod_w, mod_b):
    depth, d, n = mod_w.shape
    tn = 1024
    return pl.pallas_call(
        _mods_kernel,
        grid=(depth, n // tn),
        in_specs=[pl.BlockSpec((SUBLANES, d), lambda l, j: (0, 0)),
                  pl.BlockSpec((None, d, tn), lambda l, j: (l, 0, j)),
                  pl.BlockSpec((None, 1, tn), lambda l, j: (l, 0, j))],
        out_specs=pl.BlockSpec((None, SUBLANES, tn), lambda l, j: (l, 0, j)),
        out_shape=jax.ShapeDtypeStruct((depth, SUBLANES, n), f32),
        compiler_params=_cparams(2),
        name="adaln_mods",
    )(cvec, mod_w, mod_b.reshape(depth, 1, n))


def _norm_mod_value(x, nw, m_ref, shift_row, scale_row):
    y = x * lax.rsqrt(jnp.mean(x * x, axis=-1, keepdims=True) + EPS) * nw
    return y * (1.0 + m_ref[scale_row:scale_row + 1, :]) + m_ref[shift_row:shift_row + 1, :]


def _norm_mod_kernel(x_ref, nw_ref, m_ref, o_ref, *, shift_row, scale_row):
    o_ref[...] = _norm_mod_value(x_ref[...], nw_ref[...], m_ref, shift_row, scale_row).astype(o_ref.dtype)


def _norm_mod(x, nw, mods, shift_row, scale_row, n_blocks, bps, n_batch, out_dtype):
    t, d = x.shape
    return pl.pallas_call(
        functools.partial(_norm_mod_kernel, shift_row=shift_row, scale_row=scale_row),
        grid=(n_blocks,),
        in_specs=[pl.BlockSpec((ROW_BLOCK, d), lambda i: (i, 0)),
                  pl.BlockSpec((1, d), lambda i: (0, 0)),
                  pl.BlockSpec((None, SUBLANES, d), lambda i: (_group_of(i, bps, n_batch), 0, 0))],
        out_specs=pl.BlockSpec((ROW_BLOCK, d), lambda i: (i, 0)),
        out_shape=jax.ShapeDtypeStruct((n_blocks * ROW_BLOCK, d), out_dtype),
        compiler_params=_cparams(1),
        name="norm_mod",
    )(x, nw.reshape(1, d), mods)


def _final_norm_kernel(x_ref, nw_ref, o_ref):
    x = x_ref[...]
    o_ref[...] = x * lax.rsqrt(jnp.mean(x * x, axis=-1, keepdims=True) + EPS) * nw_ref[...]


def _final_norm(x, nw, n_blocks):
    t, d = x.shape
    return pl.pallas_call(
        _final_norm_kernel,
        grid=(n_blocks,),
        in_specs=[pl.BlockSpec((ROW_BLOCK, d), lambda i: (i, 0)),
                  pl.BlockSpec((1, d), lambda i: (0, 0))],
        out_specs=pl.BlockSpec((ROW_BLOCK, d), lambda i: (i, 0)),
        out_shape=jax.ShapeDtypeStruct((n_blocks * ROW_BLOCK, d), f32),
        compiler_params=_cparams(1),
        name="final_norm",
    )(x, nw.reshape(1, d))


def _mm_kernel(a_ref, w_ref, o_ref):
    o_ref[...] = jnp.dot(a_ref[...], w_ref[...], preferred_element_type=f32).astype(o_ref.dtype)


def _col_tile(n):
    return max(t for t in range(LANES, min(n, MM_COL_TILE) + 1, LANES) if n % t == 0)


def _mm(a, w, n_blocks, tn, out_dtype):
    t, k = a.shape
    n = w.shape[1]
    return pl.pallas_call(
        _mm_kernel,
        grid=(n // tn, n_blocks),
        in_specs=[pl.BlockSpec((ROW_BLOCK, k), lambda j, i: (i, 0)),
                  pl.BlockSpec((k, tn), lambda j, i: (0, j))],
        out_specs=pl.BlockSpec((ROW_BLOCK, tn), lambda j, i: (i, j)),
        out_shape=jax.ShapeDtypeStruct((n_blocks * ROW_BLOCK, n), out_dtype),
        compiler_params=_cparams(2),
        name="matmul",
    )(a, w)


def _mm_res_kernel(a_ref, w_ref, x_ref, m_ref, nw_ref, o_ref, f_ref, *, gate_row, shift_row, scale_row):
    y = x_ref[...] + m_ref[gate_row:gate_row + 1, :] * jnp.dot(a_ref[...], w_ref[...], preferred_element_type=f32)
    o_ref[...] = y
    f_ref[...] = _norm_mod_value(y, nw_ref[...], m_ref, shift_row, scale_row).astype(f_ref.dtype)


def _mm_res(a, w, x, mods, nw, gate_row, shift_row, scale_row, n_blocks, bps, n_batch):
    t, k = a.shape
    n = w.shape[1]
    rows = n_blocks * ROW_BLOCK
    return pl.pallas_call(
        functools.partial(_mm_res_kernel, gate_row=gate_row, shift_row=shift_row, scale_row=scale_row),
        grid=(n_blocks,),
        in_specs=[pl.BlockSpec((ROW_BLOCK, k), lambda i: (i, 0)),
                  pl.BlockSpec((k, n), lambda i: (0, 0)),
                  pl.BlockSpec((ROW_BLOCK, n), lambda i: (i, 0)),
                  pl.BlockSpec((None, SUBLANES, n), lambda i: (_group_of(i, bps, n_batch), 0, 0)),
                  pl.BlockSpec((1, n), lambda i: (0, 0))],
        out_specs=[pl.BlockSpec((ROW_BLOCK, n), lambda i: (i, 0)), pl.BlockSpec((ROW_BLOCK, n), lambda i: (i, 0))],
        out_shape=[jax.ShapeDtypeStruct((rows, n), f32), jax.ShapeDtypeStruct((rows, n), bf16)],
        compiler_params=_cparams(1),
        name="matmul_residual",
    )(a, w, x, mods, nw.reshape(1, n))


def _qkv_gqa_kernel(a_ref, w_ref, qg_ref, kg_ref, cos_ref, sa_ref, sb_ref, o_ref, *, nq_tiles, nk_tiles, scale):
    j = pl.program_id(0)
    acc = jnp.dot(a_ref[...], w_ref[...], preferred_element_type=f32)
    heads = acc.shape[1] // LANES

    def prep(gain, post):
        cos, sa, sb = cos_ref[...], sa_ref[...], sb_ref[...]
        for h in range(heads):
            y = acc[:, h * LANES:(h + 1) * LANES]
            y = y * lax.rsqrt(jnp.mean(y * y, axis=-1, keepdims=True) + EPS) * gain
            y = y * cos + pltpu.roll(y, LANES - 32, 1) * sa + pltpu.roll(y, 32, 1) * sb
            o_ref[:, h * LANES:(h + 1) * LANES] = (y * post).astype(o_ref.dtype)

    @pl.when(j < nq_tiles)
    def _():
        prep(qg_ref[...], scale)

    @pl.when(jnp.logical_and(j >= nq_tiles, j < nq_tiles + nk_tiles))
    def _():
        prep(kg_ref[...], 1.0)

    @pl.when(j >= nq_tiles + nk_tiles)
    def _():
        o_ref[...] = acc.astype(o_ref.dtype)


def _rope_tables(s):
    t = np.arange(s)
    half = LANES // 2
    inv = ROPE_BASE ** (-jnp.arange(0, half, 2, dtype=f32) / half)
    ang_r = (t // GRID_W).astype(np.float32)[:, None] * inv[None, :]
    ang_c = (t % GRID_W).astype(np.float32)[:, None] * inv[None, :]
    ang = jnp.concatenate([ang_r, ang_r, ang_c, ang_c], axis=-1)
    cos, sin = jnp.cos(ang), jnp.sin(ang)
    first = (np.arange(LANES) % half) < (half // 2)
    sa = jnp.where(first[None, :], -sin, 0.0)
    sb = jnp.where(first[None, :], 0.0, sin)
    pad1 = jnp.ones((ROW_BLOCK, LANES), f32)
    pad0 = jnp.zeros((ROW_BLOCK, LANES), f32)
    return (jnp.concatenate([cos, pad1], 0), jnp.concatenate([sa, pad0], 0), jnp.concatenate([sb, pad0], 0))


def _qkv_gqa(h, wqkv, q_gain, k_gain, tables, q_dim, kv_dim, n_blocks, bps, n_batch):
    t, k = h.shape
    n = wqkv.shape[1]
    tn = min(512, kv_dim)
    cos, sa, sb = tables

    def pos_map(j, i):
        return (jnp.where(i < bps * n_batch, i % bps, bps), 0)

    return pl.pallas_call(
        functools.partial(_qkv_gqa_kernel, nq_tiles=q_dim // tn, nk_tiles=kv_dim // tn, scale=SM_SCALE_LOG2),
        grid=(n // tn, n_blocks),
        in_specs=[pl.BlockSpec((ROW_BLOCK, k), lambda j, i: (i, 0)),
                  pl.BlockSpec((k, tn), lambda j, i: (0, j)),
                  pl.BlockSpec((1, LANES), lambda j, i: (0, 0)),
                  pl.BlockSpec((1, LANES), lambda j, i: (0, 0)),
                  pl.BlockSpec((ROW_BLOCK, LANES), pos_map),
                  pl.BlockSpec((ROW_BLOCK, LANES), pos_map),
                  pl.BlockSpec((ROW_BLOCK, LANES), pos_map)],
        out_specs=pl.BlockSpec((ROW_BLOCK, tn), lambda j, i: (i, j)),
        out_shape=jax.ShapeDtypeStruct((t, n), bf16),
        compiler_params=_cparams(2),
        name="qkv_gqa",
    )(h, wqkv, q_gain.reshape(1, LANES), k_gain.reshape(1, LANES), cos, sa, sb)


def _flash_kernel(*refs, groups, n_lat_chunks, ck, scale):
    if n_lat_chunks:
        q_ref, kc_ref, vc_ref, kl_ref, vl_ref, o_ref = refs
    else:
        q_ref, kc_ref, vc_ref, o_ref = refs
    tq = q_ref.shape[0]
    q = jnp.concatenate([q_ref[:, g * LANES:(g + 1) * LANES] for g in range(groups)], axis=0)
    rows = groups * tq

    def step(k, v, m, l, acc):
        s = lax.dot_general(q, k, _NT, preferred_element_type=f32)
        if scale != 1.0:
            s = s * scale
        m_new = jnp.maximum(m, jnp.max(s, axis=-1, keepdims=True))
        p = jnp.exp2(s - m_new)
        alpha = jnp.exp2(m - m_new)
        l = alpha * l + jnp.sum(p, axis=-1, keepdims=True)
        acc = alpha * acc + jnp.dot(p.astype(bf16), v, preferred_element_type=f32)
        return m_new, l, acc

    carry = step(kc_ref[...], vc_ref[...], jnp.full((rows, 1), NEG, f32), jnp.zeros((rows, 1), f32),
                 jnp.zeros((rows, LANES), f32))
    if n_lat_chunks:
        def body(c, carry):
            off = pl.multiple_of(c * ck, ck)
            return step(kl_ref[pl.ds(off, ck), :], vl_ref[pl.ds(off, ck), :], *carry)
        carry = lax.fori_loop(0, n_lat_chunks, body, carry)
    _, l, acc = carry
    o = acc / l
    for g in range(groups):
        o_ref[:, g * LANES:(g + 1) * LANES] = o[g * tq:(g + 1) * tq].astype(o_ref.dtype)


def _gqa_lat_kernel(q_ref, kc_ref, vc_ref, kl_ref, vl_ref, o_ref, *scratch, groups, n_sub):
    tq = q_ref.shape[0] // n_sub
    vcx_ref, vlx_ref = scratch[4 * n_sub:]

    @pl.when(pl.program_id(2) == 0)
    def _():
        for src, dst in ((vc_ref, vcx_ref), (vl_ref, vlx_ref)):
            dst[:, :LANES] = src[...]
            dst[:, LANES:] = jnp.ones((src.shape[0], LANES), bf16)

    for h in range(n_sub):
        sc_ref, sl_ref, pc_ref, pl_ref = scratch[4 * h:4 * h + 4]
        q = jnp.concatenate([q_ref[h * tq:(h + 1) * tq, g * LANES:(g + 1) * LANES] for g in range(groups)], axis=0)
        sc_ref[...] = lax.dot_general(q, kc_ref[...], _NT, preferred_element_type=f32)
        sl_ref[...] = lax.dot_general(q, kl_ref[...], _NT, preferred_element_type=f32)
        m = jnp.maximum(jnp.max(sc_ref[...], axis=-1, keepdims=True), jnp.max(sl_ref[...], axis=-1, keepdims=True))
        pc_ref[...] = jnp.exp2(sc_ref[...] - m).astype(bf16)
        pl_ref[...] = jnp.exp2(sl_ref[...] - m).astype(bf16)
    for h in range(n_sub):
        pc_ref, pl_ref = scratch[4 * h + 2:4 * h + 4]
        ox = (jnp.dot(pc_ref[...], vcx_ref[...], preferred_element_type=f32)
              + jnp.dot(pl_ref[...], vlx_ref[...], preferred_element_type=f32))
        o = ox[:, :LANES] / ox[:, LANES:]
        for g in range(groups):
            o_ref[h * tq:(h + 1) * tq, g * LANES:(g + 1) * LANES] = o[g * tq:(g + 1) * tq].astype(o_ref.dtype)


def _attn_lat(q, k, v, q_col0, k_col0, v_col0, groups, n_kv, n_batch, s, c, out_cols):
    n_sub = ATTN_Q_SUBBLOCKS
    tq = n_sub * ATTN_Q_ROWS
    nqb = s // tq
    ctx_blk0 = n_batch * s // c
    qw = groups * LANES
    return pl.pallas_call(
        functools.partial(_gqa_lat_kernel, groups=groups, n_sub=n_sub),
        grid=(n_batch, n_kv, nqb),
        in_specs=[pl.BlockSpec((tq, qw), lambda b, h, i: (b * nqb + i, q_col0 // qw + h)),
                  pl.BlockSpec((c, LANES), lambda b, h, i: (ctx_blk0 + b, k_col0 // LANES + h)),
                  pl.BlockSpec((c, LANES), lambda b, h, i: (ctx_blk0 + b, v_col0 // LANES + h)),
                  pl.BlockSpec((s, LANES), lambda b, h, i: (b, k_col0 // LANES + h)),
                  pl.BlockSpec((s, LANES), lambda b, h, i: (b, v_col0 // LANES + h))],
        out_specs=pl.BlockSpec((tq, qw), lambda b, h, i: (b * nqb + i, h)),
        out_shape=jax.ShapeDtypeStruct((n_batch * s, out_cols), bf16),
        scratch_shapes=[pltpu.VMEM((groups * ATTN_Q_ROWS, n), dt)
                        for _ in range(n_sub) for dt in (f32, bf16) for n in (c, s)]
                       + [pltpu.VMEM((c, 2 * LANES), bf16), pltpu.VMEM((s, 2 * LANES), bf16)],
        compiler_params=_cparams(3),
        name="attn_latent",
    )(q, k, v, k, v)


def _attn_ctx(q, k, v, q_col0, k_col0, v_col0, groups, n_kv, n_batch, s, c, scale):
    ctx_blk0 = n_batch * s // c
    qw = groups * LANES
    return pl.pallas_call(
        functools.partial(_flash_kernel, groups=groups, n_lat_chunks=0, ck=0, scale=scale),
        grid=(n_batch, n_kv),
        in_specs=[pl.BlockSpec((c, qw), lambda b, h: (ctx_blk0 + b, q_col0 // qw + h)),
                  pl.BlockSpec((c, LANES), lambda b, h: (ctx_blk0 + b, k_col0 // LANES + h)),
                  pl.BlockSpec((c, LANES), lambda b, h: (ctx_blk0 + b, v_col0 // LANES + h))],
        out_specs=pl.BlockSpec((c, qw), lambda b, h: (b, h)),
        out_shape=jax.ShapeDtypeStruct((n_batch * c, n_kv * qw), bf16),
        compiler_params=_cparams(2),
        name="attn_context",
    )(q, k, v)


def _natten_kernel(q_ref, k_ref, v_ref, kc_ref, vc_ref, bias_ref, o_ref, *, rows, w, wr, scale):
    kc = kc_ref[...]
    vc = vc_ref[...]

    nr = NATTEN_UNROLL

    def body(g, carry):
        base = pl.multiple_of(g * nr * w, nr * w)
        qg = q_ref[pl.ds(base, nr * w), :]
        scg = lax.dot_general(qg, kc, _NT, preferred_element_type=f32) * scale
        offs, sws = [], []
        for i in range(nr):
            r = g * nr + i
            r0 = jnp.clip(r - wr // 2, 0, rows - wr)
            off = pl.multiple_of(r0 * w, w)
            kb = k_ref[pl.ds(off, wr * w), :]
            sws.append(lax.dot_general(qg[i * w:(i + 1) * w], kb, _NT, preferred_element_type=f32) * scale
                       + bias_ref[r - r0])
            offs.append(off)
        pws, pcs, ls = [], [], []
        for i in range(nr):
            sc = scg[i * w:(i + 1) * w]
            m = jnp.maximum(jnp.max(sws[i], axis=-1, keepdims=True), jnp.max(sc, axis=-1, keepdims=True))
            pw = jnp.exp2(sws[i] - m)
            pc = jnp.exp2(sc - m)
            ls.append(jnp.sum(pw, axis=-1, keepdims=True) + jnp.sum(pc, axis=-1, keepdims=True))
            pws.append(pw.astype(bf16))
            pcs.append(pc.astype(bf16))
        ocg = jnp.dot(jnp.concatenate(pcs, axis=0), vc, preferred_element_type=f32)
        for i in range(nr):
            vb = v_ref[pl.ds(offs[i], wr * w), :]
            o = (jnp.dot(pws[i], vb, preferred_element_type=f32) + ocg[i * w:(i + 1) * w]) / ls[i]
            o_ref[pl.ds(pl.multiple_of(base + i * w, w), w), :] = o.astype(o_ref.dtype)
        return carry

    lax.fori_loop(0, rows // nr, body, 0)


def _natten_bias(rpb, rows, wr, win_r, win_c):
    w = GRID_W
    cols = np.arange(w)
    col_start = np.clip(cols - win_c // 2, 0, w - win_c)
    inside = (cols[None, :] >= col_start[:, None]) & (cols[None, :] < col_start[:, None] + win_c)
    rpbp = jnp.pad(rpb, ((0, 0), (0, 0), (w - win_c, w - win_c)))
    p = jnp.stack([rpbp[:, :, w - 1 - qc:2 * w - 1 - qc] for qc in range(w)], axis=2)
    p = jnp.where(inside[None, None], p * LOG2E, NEG)
    tab = jnp.stack([p[:, win_r - 1 - d:win_r - 1 - d + wr] for d in range(wr)], axis=1)
    return jnp.transpose(tab, (0, 1, 3, 2, 4)).reshape(rpb.shape[0], wr, w, wr * w)


def _natten_lat(qkv, bias, n_heads, n_batch, s, c, d_model):
    t = qkv.shape[0]
    rows = s // GRID_W
    wr = bias.shape[1]
    ctx_blk0 = n_batch * s // c
    hq, hk, hv = 0, d_model // LANES, 2 * d_model // LANES
    return pl.pallas_call(
        functools.partial(_natten_kernel, rows=rows, w=GRID_W, wr=wr, scale=SM_SCALE_LOG2),
        grid=(n_batch, n_heads),
        in_specs=[pl.BlockSpec((s, LANES), lambda b, h: (b, hq + h)),
                  pl.BlockSpec((s, LANES), lambda b, h: (b, hk + h)),
                  pl.BlockSpec((s, LANES), lambda b, h: (b, hv + h)),
                  pl.BlockSpec((c, LANES), lambda b, h: (ctx_blk0 + b, hk + h)),
                  pl.BlockSpec((c, LANES), lambda b, h: (ctx_blk0 + b, hv + h)),
                  pl.BlockSpec((None, wr, GRID_W, wr * GRID_W), lambda b, h: (h, 0, 0, 0))],
        out_specs=pl.BlockSpec((s, LANES), lambda b, h: (b, h)),
        out_shape=jax.ShapeDtypeStruct((n_batch * s, d_model), bf16),
        compiler_params=_cparams(2),
        name="natten_latent",
    )(qkv, qkv, qkv, qkv, qkv, bias)


def _pool_kernel(h_ref, hp_ref, hn_ref, x_ref, w_ref, ls_ref, m_ref, nw_ref, o_ref, f_ref, pad_ref, *,
                 bps_lat, bps_ctx, n_lat_blocks, s, c, gate_row, shift_row, scale_row):
    i = pl.program_id(0)
    tm = h_ref.shape[0]
    is_lat = i < n_lat_blocks
    blk = jnp.where(is_lat, i % bps_lat, (i - n_lat_blocks) % bps_ctx)
    nblk = jnp.where(is_lat, bps_lat, bps_ctx)
    length = jnp.where(is_lat, s, c)
    pad_ref[0:POOL_HALO, :] = jnp.where(blk == 0, 0.0, hp_ref[...])
    pad_ref[POOL_HALO:POOL_HALO + tm, :] = h_ref[...]
    pad_ref[POOL_HALO + tm:2 * POOL_HALO + tm, :] = jnp.where(blk == nblk - 1, 0.0, hn_ref[...])
    pos = blk * tm + lax.broadcasted_iota(jnp.int32, (tm, 1), 0)
    pg = w_ref.shape[1]
    for g, win in enumerate(POOL_WINDOWS):
        cs = slice(g * pg, (g + 1) * pg)
        lo_off, hi_off = win // 2, win - win // 2
        acc = pad_ref[POOL_HALO - lo_off:POOL_HALO - lo_off + tm, cs]
        for k in range(-lo_off + 1, hi_off):
            acc = acc + pad_ref[POOL_HALO + k:POOL_HALO + k + tm, cs]
        cnt = jnp.minimum(pos + hi_off, length) - jnp.maximum(pos - lo_off, 0)
        y = acc * (1.0 / cnt.astype(f32)) - h_ref[:, cs]
        z = jnp.dot(y.astype(bf16), w_ref[g], preferred_element_type=f32) * ls_ref[:, cs]
        o_ref[:, cs] = x_ref[:, cs] + m_ref[gate_row:gate_row + 1, cs] * z
    f_ref[...] = _norm_mod_value(o_ref[...], nw_ref[...], m_ref, shift_row, scale_row).astype(f_ref.dtype)


def _pool(h, x, pool_w, pool_scale, mods, nw, gate_row, shift_row, scale_row, n_batch, s, c):
    t, d = h.shape
    tm = POOL_ROW_BLOCK
    nb = t // tm
    bps_lat, bps_ctx = s // tm, c // tm
    hb = tm // POOL_HALO
    last_halo = t // POOL_HALO - 1
    return pl.pallas_call(
        functools.partial(_pool_kernel, bps_lat=bps_lat, bps_ctx=bps_ctx, n_lat_blocks=n_batch * bps_lat,
                          s=s, c=c, gate_row=gate_row, shift_row=shift_row, scale_row=scale_row),
        grid=(nb,),
        in_specs=[pl.BlockSpec((tm, d), lambda i: (i, 0)),
                  pl.BlockSpec((POOL_HALO, d), lambda i: (jnp.maximum(i * hb - 1, 0), 0)),
                  pl.BlockSpec((POOL_HALO, d), lambda i: (jnp.minimum((i + 1) * hb, last_halo), 0)),
                  pl.BlockSpec((tm, d), lambda i: (i, 0)),
                  pl.BlockSpec(pool_w.shape, lambda i: (0, 0, 0)),
                  pl.BlockSpec((1, d), lambda i: (0, 0)),
                  pl.BlockSpec((None, SUBLANES, d), lambda i: (_group_of(i, bps_lat, n_batch), 0, 0)),
                  pl.BlockSpec((1, d), lambda i: (0, 0))],
        out_specs=[pl.BlockSpec((tm, d), lambda i: (i, 0)), pl.BlockSpec((tm, d), lambda i: (i, 0))],
        out_shape=[jax.ShapeDtypeStruct((t, d), f32), jax.ShapeDtypeStruct((t, d), bf16)],
        scratch_shapes=[pltpu.VMEM((tm + 2 * POOL_HALO, d), f32)],
        compiler_params=_cparams(1),
        name="pool_mixer",
    )(h, h, h, x, pool_w, pool_scale.reshape(1, d), mods, nw.reshape(1, d))


def _topk_rows(chains, tb):
    rows = lax.broadcasted_iota(jnp.int32, (PEER_TOPK, tb), 0)

    def body(k, carry):
        out = []
        for (s_ref, order), (vals, idxs) in zip(chains, carry):
            s = s_ref[...]
            m = jnp.max(s, axis=0, keepdims=True)
            am = jnp.min(jnp.where(s == m, order, float(2 ** 23)), axis=0, keepdims=True)
            s_ref[...] = jnp.where(order == am, NEG, s)
            out.append((jnp.where(rows == k, m, vals), jnp.where(rows == k, am, idxs)))
        return tuple(out)

    zero = jnp.zeros((PEER_TOPK, tb), f32)
    res = lax.fori_loop(0, PEER_TOPK, body, tuple((zero, zero) for _ in chains))
    return [(vals, idxs.astype(jnp.int32)) for vals, idxs in res]


def _cand_blocks():
    blocks = [(0, PEER_TOPK)] + [(a, SUBLANES) for a in range(1, SUBLANES)]
    assert all((a + 1) * (nb + 1) > PEER_TOPK for a, nb in blocks[1:]) and 2 * SUBLANES == PEER_TOPK
    return blocks


def _peer_topk_kernel(q_ref, sk_ref, e_ref, g_ref, s1_ref, s2_ref, cand_ref, es_ref, gs_ref, *, n_heads, nkeys):
    tb = q_ref.shape[0]
    key_order = lax.broadcasted_iota(jnp.int32, (nkeys, tb), 0).astype(f32)
    blocks = _cand_blocks()
    n_mid = (len(blocks) - 1) * SUBLANES
    r = lax.broadcasted_iota(jnp.int32, cand_ref.shape, 0)
    rm = r - PEER_TOPK
    mid = (lax.shift_right_logical(rm, 3) + 1) * PEER_TOPK + lax.bitwise_and(rm, SUBLANES - 1)
    tail = (rm - n_mid + SUBLANES) * PEER_TOPK
    cand_order = jnp.where(r < PEER_TOPK, r, jnp.where(rm < n_mid, mid, tail)).astype(f32)

    def scores(h):
        col = pl.multiple_of(h * 2 * LANES, 2 * LANES)
        s1_ref[...] = lax.dot_general(sk_ref[0], q_ref[:, pl.ds(col, LANES)], _NT, preferred_element_type=f32)
        s2_ref[...] = lax.dot_general(sk_ref[1], q_ref[:, pl.ds(col + LANES, LANES)], _NT, preferred_element_type=f32)

    sub_chains = [(s1_ref, key_order), (s2_ref, key_order)]
    scores(0)
    (v1, i1), (v2, i2) = _topk_rows(sub_chains, tb)

    def head(h, carry):
        v1, i1, v2, i2 = carry
        r0 = 0
        for a, nb in blocks:
            cand_ref[r0:r0 + nb, :] = v1[a:a + 1, :] + v2[:nb]
            r0 += nb
        cand_ref[r0:r0 + SUBLANES, :] = v1[SUBLANES:] + v2[0:1, :]
        scores(jnp.minimum(h + 1, n_heads - 1))
        nxt1, nxt2, (sc, ci) = _topk_rows(sub_chains + [(cand_ref, cand_order)], tb)
        ca = lax.shift_right_logical(ci, PEER_TOPK.bit_length() - 1)
        cb = lax.bitwise_and(ci, PEER_TOPK - 1)
        e1 = jnp.zeros((PEER_TOPK, tb), jnp.int32)
        e2 = jnp.zeros((PEER_TOPK, tb), jnp.int32)
        for a in range(PEER_TOPK):
            e1 = jnp.where(ca == a, i1[a:a + 1, :], e1)
            e2 = jnp.where(cb == a, i2[a:a + 1, :], e2)
        p = jnp.exp(sc - sc[0:1, :])
        gate = p / jnp.sum(p, axis=0, keepdims=True)
        row = pl.multiple_of(h * PEER_TOPK, PEER_TOPK)
        es_ref[pl.ds(row, PEER_TOPK), :] = e1 * nkeys + e2
        gs_ref[pl.ds(row, PEER_TOPK), :] = gate
        return nxt1 + nxt2

    lax.fori_loop(0, n_heads, head, (v1, i1, v2, i2))
    e_ref[...] = es_ref[...].T
    g_ref[...] = gs_ref[...].T


def _peer_topk(q, sub_keys, n_blocks_rows):
    t, qd = q.shape
    nkeys = sub_keys.shape[1]
    n_heads = qd // (2 * LANES)
    slots = n_heads * PEER_TOPK
    tb = 256
    return pl.pallas_call(
        functools.partial(_peer_topk_kernel, n_heads=n_heads, nkeys=nkeys),
        grid=(n_blocks_rows * ROW_BLOCK // tb,),
        in_specs=[pl.BlockSpec((tb, qd), lambda i: (i, 0)),
                  pl.BlockSpec(sub_keys.shape, lambda i: (0, 0, 0))],
        out_specs=[pl.BlockSpec((tb, slots), lambda i: (i, 0)),
                   pl.BlockSpec((tb, slots), lambda i: (i, 0))],
        out_shape=[jax.ShapeDtypeStruct((n_blocks_rows * ROW_BLOCK, slots), jnp.int32),
                   jax.ShapeDtypeStruct((n_blocks_rows * ROW_BLOCK, slots), f32)],
        scratch_shapes=[pltpu.VMEM((nkeys, tb), f32), pltpu.VMEM((nkeys, tb), f32),
                        pltpu.VMEM((sum(nb for _, nb in _cand_blocks()) + SUBLANES, tb), f32),
                        pltpu.VMEM((slots, tb), jnp.int32), pltpu.VMEM((slots, tb), f32)],
        compiler_params=_cparams(1),
        name="peer_topk",
    )(q, sub_keys)


def _peer_gates_kernel(e_ref, g_ref, o_ref, *, nkeys):
    tb, slots = e_ref.shape
    iota = lax.broadcasted_iota(jnp.int32, (nkeys, slots), 0)

    def body(t, carry):
        e = e_ref[pl.ds(t, 1), :]
        g = g_ref[pl.ds(t, 1), :]
        e1 = lax.shift_right_logical(e, nkeys.bit_length() - 1)
        e2 = lax.bitwise_and(e, nkeys - 1)
        w1 = jnp.where(e1 == iota, g, 0.0).astype(bf16)
        o2 = jnp.where(e2 == iota, 1.0, 0.0).astype(bf16)
        res = lax.dot_general(w1, o2, _NT, preferred_element_type=f32)
        bits = lax.bitcast_convert_type(res.astype(bf16).astype(f32), jnp.uint32)
        packed = jnp.bitwise_or(jnp.right_shift(bits[:nkeys // 2], jnp.uint32(16)), bits[nkeys // 2:])
        o_ref[:, t] = packed.reshape(nkeys // (2 * SUBLANES), SUBLANES, nkeys)
        return carry

    lax.fori_loop(0, tb, body, 0, unroll=GATES_UNROLL)


def _peer_gates(e, g, nkeys, n_blocks_rows):
    t, slots = e.shape
    tb = 128
    nblk = nkeys // (2 * SUBLANES)
    return pl.pallas_call(
        functools.partial(_peer_gates_kernel, nkeys=nkeys),
        grid=(n_blocks_rows * ROW_BLOCK // tb,),
        in_specs=[pl.BlockSpec((tb, slots), lambda i: (i, 0)),
                  pl.BlockSpec((tb, slots), lambda i: (i, 0))],
        out_specs=pl.BlockSpec((nblk, tb, SUBLANES, nkeys), lambda i: (0, i, 0, 0)),
        out_shape=jax.ShapeDtypeStruct((nblk, t, SUBLANES, nkeys), jnp.uint32),
        compiler_params=_cparams(1),
        name="peer_gates",
    )(e, g)


def _gelu_tanh(x):
    k0 = -2.0 * 0.7978845608028654 * 1.4426950408889634
    return x / (1.0 + jnp.exp2(x * (k0 + (k0 * 0.044715) * (x * x))))


def _peer_dense_kernel(f_ref, u_ref, v_ref, g_ref, x_ref, m_ref, o_ref, acc_ref, a_ref, *, gate_row, nkeys):
    j = pl.program_id(1)

    @pl.when(j == 0)
    def _():
        acc_ref[...] = jnp.zeros_like(acc_ref)

    slot = j % 2
    a_ref[slot] = lax.dot_general(f_ref[...], u_ref[...], _NT, preferred_element_type=f32)
    tb = f_ref.shape[0]
    n_first = g_ref.shape[0] // tb
    shift = jnp.where(j < pl.num_programs(1) // 2, 16, 0).astype(jnp.uint32)
    parts = []
    for k in range(n_first):
        word = g_ref[pl.ds(k, tb, stride=n_first), :]
        gbits = jnp.bitwise_and(jnp.left_shift(word, shift), jnp.uint32(0xFFFF0000))
        gk = lax.bitcast_convert_type(gbits, f32)
        parts.append((_gelu_tanh(a_ref[slot, :, k * nkeys:(k + 1) * nkeys]) * gk).astype(bf16))
    ga = jnp.concatenate(parts, axis=1)
    acc_ref[...] += jnp.dot(ga, v_ref[...], preferred_element_type=f32)

    @pl.when(j == pl.num_programs(1) - 1)
    def _():
        o_ref[...] = x_ref[...] + m_ref[gate_row:gate_row + 1, :] * acc_ref[...]


def _peer_dense(f, u_all, v_all, layer, gates, x, mods, gate_row, n_blocks, bps, n_batch):
    t, d = f.shape
    ne = v_all.shape[1]
    nkeys = gates.shape[3]
    te = SUBLANES * nkeys
    n_gblk = gates.shape[0]
    assert ne // te == 2 * n_gblk
    gates = gates.reshape(n_gblk, gates.shape[1] * SUBLANES, nkeys)
    return pl.pallas_call(
        functools.partial(_peer_dense_kernel, gate_row=gate_row, nkeys=nkeys),
        grid=(n_blocks, ne // te),
        in_specs=[pl.BlockSpec((ROW_BLOCK, d), lambda i, j: (i, 0)),
                  pl.BlockSpec((None, te, d), lambda i, j: (layer, j, 0)),
                  pl.BlockSpec((None, te, d), lambda i, j: (layer, j, 0)),
                  pl.BlockSpec((None, ROW_BLOCK * SUBLANES, nkeys), lambda i, j: (j % n_gblk, i, 0)),
                  pl.BlockSpec((ROW_BLOCK, d), lambda i, j: (i, 0)),
                  pl.BlockSpec((None, SUBLANES, d), lambda i, j: (_group_of(i, bps, n_batch), 0, 0))],
        out_specs=pl.BlockSpec((ROW_BLOCK, d), lambda i, j: (i, 0)),
        out_shape=jax.ShapeDtypeStruct((t, d), f32),
        scratch_shapes=[pltpu.VMEM((ROW_BLOCK, d), f32), pltpu.VMEM((2, ROW_BLOCK, te), f32)],
        compiler_params=_cparams(2),
        name="peer_dense",
    )(f, u_all, v_all, gates, x, mods)


def kernel(x, c, ctx, c_ctx, mod_w, mod_b, norm_w, final_norm_w, a_wqkv, a_q_gain, a_k_gain, a_wo,
           b_wqkv, b_rpb, b_wo, pool_w, pool_scale, peer_wq, peer_sub_keys, peer_u, peer_v):
    n_batch, s, d = x.shape
    c_len = ctx.shape[1]
    depth = mod_w.shape[0]
    assert s % ROW_BLOCK == 0 and (n_batch * c_len) % ROW_BLOCK == 0 and n_batch * c_len <= s
    assert s % c_len == 0 and c_len % POOL_ROW_BLOCK == 0 and s % GRID_W == 0
    assert a_q_gain.shape[1] == LANES and peer_sub_keys.shape[2] == LANES and peer_sub_keys.shape[3] == LANES
    assert PEER_TOPK & (PEER_TOPK - 1) == 0 and peer_sub_keys.shape[2] & (peer_sub_keys.shape[2] - 1) == 0
    assert depth % N_MIXERS != 0

    bps = s // ROW_BLOCK
    nb_lat = n_batch * bps
    nb_all = nb_lat + n_batch * c_len // ROW_BLOCK
    t_lat = n_batch * s
    q_dim = a_wo.shape[1]
    kv_dim = (a_wqkv.shape[2] - q_dim) // 2
    n_kv = kv_dim // LANES
    groups = q_dim // kv_dim
    b_heads = b_rpb.shape[1]
    win_r, win_c = (b_rpb.shape[2] + 1) // 2, (b_rpb.shape[3] + 1) // 2
    nkeys = peer_sub_keys.shape[2]
    scale = SM_SCALE_LOG2

    xs = jnp.concatenate([x.reshape(t_lat, d), ctx.reshape(n_batch * c_len, d)], axis=0)
    cvec = jnp.concatenate([c, c_ctx[None, :], jnp.zeros((SUBLANES - n_batch - 1, d), f32)], axis=0)
    mods_all = _mods(cvec, mod_w, mod_b)
    tables = _rope_tables(s)
    u_bf, v_bf = peer_u.astype(bf16), peer_v.astype(bf16)

    for i in range(depth):
        last = i == depth - 1
        kind, j = i % N_MIXERS, i // N_MIXERS
        nb = nb_lat if last else nb_all
        mods = mods_all[i, :n_batch + 1].reshape(n_batch + 1, N_MOD, d)
        mods = jnp.pad(mods, ((0, 0), (0, SUBLANES - N_MOD), (0, 0)))

        if kind == 2:
            h = _norm_mod(xs, norm_w[i, 0], mods, 0, 1, nb_all, bps, n_batch, f32)
            xs, f = _pool(h, xs, pool_w[j].astype(bf16), pool_scale[j], mods, norm_w[i, 1], 2, 3, 4, n_batch, s, c_len)
        else:
            h = _norm_mod(xs, norm_w[i, 0], mods, 0, 1, nb_all, bps, n_batch, bf16)
            if kind == 0:
                qkv = _qkv_gqa(h, a_wqkv[j].astype(bf16), a_q_gain[j], a_k_gain[j], tables, q_dim, kv_dim,
                               nb_all, bps, n_batch)
                o = _attn_lat(qkv, qkv, qkv, 0, q_dim, q_dim + kv_dim, groups, n_kv, n_batch, s, c_len, q_dim)
                if not last:
                    o_ctx = _attn_ctx(qkv, qkv, qkv, 0, q_dim, q_dim + kv_dim, groups, n_kv, n_batch, s, c_len, 1.0)
                    o = jnp.concatenate([o, o_ctx], axis=0)
                wo = a_wo[j]
            else:
                qkv = _mm(h, b_wqkv[j].astype(bf16), nb_all, _col_tile(3 * d), bf16)
                rows = s // GRID_W
                bias = _natten_bias(b_rpb[j], rows, min(win_r, rows), win_r, win_c)
                o = _natten_lat(qkv, bias, b_heads, n_batch, s, c_len, d)
                if not last:
                    o_ctx = _attn_ctx(qkv, qkv, qkv, 0, d, 2 * d, 1, b_heads, n_batch, s, c_len, scale)
                    o = jnp.concatenate([o, o_ctx], axis=0)
                wo = b_wo[j]
            xs, f = _mm_res(o, wo.astype(bf16), xs, mods, norm_w[i, 1], 2, 3, 4, nb, bps, n_batch)

        qp = _mm(f, peer_wq[i].astype(bf16), nb, _col_tile(peer_wq.shape[2]), f32)
        e, g = _peer_topk(qp, peer_sub_keys[i], nb)
        gates = _peer_gates(e, g, nkeys, nb)
        xs = _peer_dense(f, u_bf, v_bf, i, gates, xs, mods, 5, nb, bps, n_batch)

    return _final_norm(xs, final_norm_w, nb_lat).reshape(n_batch, s, d)
```

```python
import functools

import numpy as np
import jax
import jax.numpy as jnp
from jax import lax
from jax.experimental import pallas as pl
from jax.experimental.pallas import tpu as pltpu

GRID_W = 64
ROPE_BASE = 10000.0
POOL_WINDOWS = (2, 4, 8, 16)
PEER_TOPK = 16
EPS = 1e-6
N_MIXERS = 3
N_MOD = 6

LANES = 128
SUBLANES = 8
ROW_BLOCK = 512
POOL_ROW_BLOCK = 256
POOL_HALO = 8
MM_COL_TILE = 2048
ATTN_Q_ROWS = 128
ATTN_Q_SUBBLOCKS = 2
NATTEN_UNROLL = 8
TOPK_TOKENS = 512
GATES_UNROLL = 32
VMEM_LIMIT = 56 * 1024 * 1024
NEG = -1e30
LOG2E = 1.4426950408889634
SM_SCALE_LOG2 = float(LANES) ** -0.5 * LOG2E

f32 = jnp.float32
bf16 = jnp.bfloat16
_NT = (((1,), (1,)), ((), ()))


def _cparams(n_axes):
    return pltpu.CompilerParams(dimension_semantics=("arbitrary",) * n_axes, vmem_limit_bytes=VMEM_LIMIT)


def _group_of(i, blocks_per_seq, n_batch):
    return jnp.minimum(i // blocks_per_seq, n_batch)


def _mods_kernel(cv_ref, w_ref, b_ref, o_ref):
    cv = cv_ref[...]
    a = (cv / (1.0 + jnp.exp(-cv))).astype(bf16)
    o_ref[...] = jnp.dot(a, w_ref[...].astype(bf16), preferred_element_type=f32) + b_ref[...]


def _mods(cvec, mod_w, mod_b):
    depth, d, n = mod_w.shape
    tn = 1024
    return pl.pallas_call(
        _mods_kernel,
        grid=(depth, n // tn),
        in_specs=[pl.BlockSpec((SUBLANES, d), lambda l, j: (0, 0)),
                  pl.BlockSpec((None, d, tn), lambda l, j: (l, 0, j)),
                  pl.BlockSpec((None, 1, tn), lambda l, j: (l, 0, j))],
        out_specs=pl.BlockSpec((None, SUBLANES, tn), lambda l, j: (l, 0, j)),
        out_shape=jax.ShapeDtypeStruct((depth, SUBLANES, n), f32),
        compiler_params=_cparams(2),
        name="adaln_mods",
    )(cvec, mod_w, mod_b.reshape(depth, 1, n))


def _norm_mod_value(x, nw, m_ref, shift_row, scale_row):
    y = x * lax.rsqrt(jnp.mean(x * x, axis=-1, keepdims=True) + EPS) * nw
    return y * (1.0 + m_ref[scale_row:scale_row + 1, :]) + m_ref[shift_row:shift_row + 1, :]


def _norm_mod_kernel(x_ref, nw_ref, m_ref, o_ref, *, shift_row, scale_row):
    o_ref[...] = _norm_mod_value(x_ref[...], nw_ref[...], m_ref, shift_row, scale_row).astype(o_ref.dtype)


def _norm_mod(x, nw, mods, shift_row, scale_row, n_blocks, bps, n_batch, out_dtype):
    t, d = x.shape
    return pl.pallas_call(
        functools.partial(_norm_mod_kernel, shift_row=shift_row, scale_row=scale_row),
        grid=(n_blocks,),
        in_specs=[pl.BlockSpec((ROW_BLOCK, d), lambda i: (i, 0)),
                  pl.BlockSpec((1, d), lambda i: (0, 0)),
                  pl.BlockSpec((None, SUBLANES, d), lambda i: (_group_of(i, bps, n_batch), 0, 0))],
        out_specs=pl.BlockSpec((ROW_BLOCK, d), lambda i: (i, 0)),
        out_shape=jax.ShapeDtypeStruct((n_blocks * ROW_BLOCK, d), out_dtype),
        compiler_params=_cparams(1),
        name="norm_mod",
    )(x, nw.reshape(1, d), mods)


def _final_norm_kernel(x_ref, nw_ref, o_ref):
    x = x_ref[...]
    o_ref[...] = x * lax.rsqrt(jnp.mean(x * x, axis=-1, keepdims=True) + EPS) * nw_ref[...]


def _final_norm(x, nw, n_blocks):
    t, d = x.shape
    return pl.pallas_call(
        _final_norm_kernel,
        grid=(n_blocks,),
        in_specs=[pl.BlockSpec((ROW_BLOCK, d), lambda i: (i, 0)),
                  pl.BlockSpec((1, d), lambda i: (0, 0))],
        out_specs=pl.BlockSpec((ROW_BLOCK, d), lambda i: (i, 0)),
        out_shape=jax.ShapeDtypeStruct((n_blocks * ROW_BLOCK, d), f32),
        compiler_params=_cparams(1),
        name="final_norm",
    )(x, nw.reshape(1, d))


def _mm_kernel(a_ref, w_ref, o_ref):
    o_ref[...] = jnp.dot(a_ref[...], w_ref[...], preferred_element_type=f32).astype(o_ref.dtype)


def _col_tile(n):
    return max(t for t in range(LANES, min(n, MM_COL_TILE) + 1, LANES) if n % t == 0)


def _mm(a, w, n_blocks, tn, out_dtype):
    t, k = a.shape
    n = w.shape[1]
    return pl.pallas_call(
        _mm_kernel,
        grid=(n // tn, n_blocks),
        in_specs=[pl.BlockSpec((ROW_BLOCK, k), lambda j, i: (i, 0)),
                  pl.BlockSpec((k, tn), lambda j, i: (0, j))],
        out_specs=pl.BlockSpec((ROW_BLOCK, tn), lambda j, i: (i, j)),
        out_shape=jax.ShapeDtypeStruct((n_blocks * ROW_BLOCK, n), out_dtype),
        compiler_params=_cparams(2),
        name="matmul",
    )(a, w)


def _mm_res_kernel(a_ref, w_ref, x_ref, m_ref, nw_ref, o_ref, f_ref, *, gate_row, shift_row, scale_row):
    y = x_ref[...] + m_ref[gate_row:gate_row + 1, :] * jnp.dot(a_ref[...], w_ref[...], preferred_element_type=f32)
    o_ref[...] = y
    f_ref[...] = _norm_mod_value(y, nw_ref[...], m_ref, shift_row, scale_row).astype(f_ref.dtype)


def _mm_res(a, w, x, mods, nw, gate_row, shift_row, scale_row, n_blocks, bps, n_batch):
    t, k = a.shape
    n = w.shape[1]
    rows = n_blocks * ROW_BLOCK
    return pl.pallas_call(
        functools.partial(_mm_res_kernel, gate_row=gate_row, shift_row=shift_row, scale_row=scale_row),
        grid=(n_blocks,),
        in_specs=[pl.BlockSpec((ROW_BLOCK, k), lambda i: (i, 0)),
                  pl.BlockSpec((k, n), lambda i: (0, 0)),
                  pl.BlockSpec((ROW_BLOCK, n), lambda i: (i, 0)),
                  pl.BlockSpec((None, SUBLANES, n), lambda i: (_group_of(i, bps, n_batch), 0, 0)),
                  pl.BlockSpec((1, n), lambda i: (0, 0))],
        out_specs=[pl.BlockSpec((ROW_BLOCK, n), lambda i: (i, 0)), pl.BlockSpec((ROW_BLOCK, n), lambda i: (i, 0))],
        out_shape=[jax.ShapeDtypeStruct((rows, n), f32), jax.ShapeDtypeStruct((rows, n), bf16)],
        compiler_params=_cparams(1),
        name="matmul_residual",
    )(a, w, x, mods, nw.reshape(1, n))


def _qkv_gqa_kernel(a_ref, w_ref, qg_ref, kg_ref, cos_ref, sa_ref, sb_ref, o_ref, *, nq_tiles, nk_tiles, scale):
    j = pl.program_id(0)
    acc = jnp.dot(a_ref[...], w_ref[...], preferred_element_type=f32)
    heads = acc.shape[1] // LANES

    def prep(gain, post):
        cos, sa, sb = cos_ref[...], sa_ref[...], sb_ref[...]
        for h in range(heads):
            y = acc[:, h * LANES:(h + 1) * LANES]
            y = y * lax.rsqrt(jnp.mean(y * y, axis=-1, keepdims=True) + EPS) * gain
            y = y * cos + pltpu.roll(y, LANES - 32, 1) * sa + pltpu.roll(y, 32, 1) * sb
            o_ref[:, h * LANES:(h + 1) * LANES] = (y * post).astype(o_ref.dtype)

    @pl.when(j < nq_tiles)
    def _():
        prep(qg_ref[...], scale)

    @pl.when(jnp.logical_and(j >= nq_tiles, j < nq_tiles + nk_tiles))
    def _():
        prep(kg_ref[...], 1.0)

    @pl.when(j >= nq_tiles + nk_tiles)
    def _():
        o_ref[...] = acc.astype(o_ref.dtype)


def _rope_tables(s):
    t = np.arange(s)
    half = LANES // 2
    inv = ROPE_BASE ** (-jnp.arange(0, half, 2, dtype=f32) / half)
    ang_r = (t // GRID_W).astype(np.float32)[:, None] * inv[None, :]
    ang_c = (t % GRID_W).astype(np.float32)[:, None] * inv[None, :]
    ang = jnp.concatenate([ang_r, ang_r, ang_c, ang_c], axis=-1)
    cos, sin = jnp.cos(ang), jnp.sin(ang)
    first = (np.arange(LANES) % half) < (half // 2)
    sa = jnp.where(first[None, :], -sin, 0.0)
    sb = jnp.where(first[None, :], 0.0, sin)
    pad1 = jnp.ones((ROW_BLOCK, LANES), f32)
    pad0 = jnp.zeros((ROW_BLOCK, LANES), f32)
    return (jnp.concatenate([cos, pad1], 0), jnp.concatenate([sa, pad0], 0), jnp.concatenate([sb, pad0], 0))


def _qkv_gqa(h, wqkv, q_gain, k_gain, tables, q_dim, kv_dim, n_blocks, bps, n_batch):
    t, k = h.shape
    n = wqkv.shape[1]
    tn = min(512, kv_dim)
    cos, sa, sb = tables

    def pos_map(j, i):
        return (jnp.where(i < bps * n_batch, i % bps, bps), 0)

    return pl.pallas_call(
        functools.partial(_qkv_gqa_kernel, nq_tiles=q_dim // tn, nk_tiles=kv_dim // tn, scale=SM_SCALE_LOG2),
        grid=(n // tn, n_blocks),
        in_specs=[pl.BlockSpec((ROW_BLOCK, k), lambda j, i: (i, 0)),
                  pl.BlockSpec((k, tn), lambda j, i: (0, j)),
                  pl.BlockSpec((1, LANES), lambda j, i: (0, 0)),
                  pl.BlockSpec((1, LANES), lambda j, i: (0, 0)),
                  pl.BlockSpec((ROW_BLOCK, LANES), pos_map),
                  pl.BlockSpec((ROW_BLOCK, LANES), pos_map),
                  pl.BlockSpec((ROW_BLOCK, LANES), pos_map)],
        out_specs=pl.BlockSpec((ROW_BLOCK, tn), lambda j, i: (i, j)),
        out_shape=jax.ShapeDtypeStruct((t, n), bf16),
        compiler_params=_cparams(2),
        name="qkv_gqa",
    )(h, wqkv, q_gain.reshape(1, LANES), k_gain.reshape(1, LANES), cos, sa, sb)


def _flash_kernel(*refs, groups, n_lat_chunks, ck, scale):
    if n_lat_chunks:
        q_ref, kc_ref, vc_ref, kl_ref, vl_ref, o_ref = refs
    else:
        q_ref, kc_ref, vc_ref, o_ref = refs
    tq = q_ref.shape[0]
    q = jnp.concatenate([q_ref[:, g * LANES:(g + 1) * LANES] for g in range(groups)], axis=0)
    rows = groups * tq

    def step(k, v, m, l, acc):
        s = lax.dot_general(q, k, _NT, preferred_element_type=f32)
        if scale != 1.0:
            s = s * scale
        m_new = jnp.maximum(m, jnp.max(s, axis=-1, keepdims=True))
        p = jnp.exp2(s - m_new)
        alpha = jnp.exp2(m - m_new)
        l = alpha * l + jnp.sum(p, axis=-1, keepdims=True)
        acc = alpha * acc + jnp.dot(p.astype(bf16), v, preferred_element_type=f32)
        return m_new, l, acc

    carry = step(kc_ref[...], vc_ref[...], jnp.full((rows, 1), NEG, f32), jnp.zeros((rows, 1), f32),
                 jnp.zeros((rows, LANES), f32))
    if n_lat_chunks:
        def body(c, carry):
            off = pl.multiple_of(c * ck, ck)
            return step(kl_ref[pl.ds(off, ck), :], vl_ref[pl.ds(off, ck), :], *carry)
        carry = lax.fori_loop(0, n_lat_chunks, body, carry)
    _, l, acc = carry
    o = acc / l
    for g in range(groups):
        o_ref[:, g * LANES:(g + 1) * LANES] = o[g * tq:(g + 1) * tq].astype(o_ref.dtype)


def _gqa_lat_kernel(q_ref, kc_ref, vc_ref, kl_ref, vl_ref, o_ref, *scratch, groups, n_sub):
    tq = q_ref.shape[0] // n_sub
    vcx_ref, vlx_ref = scratch[4 * n_sub:]

    @pl.when(pl.program_id(2) == 0)
    def _():
        for src, dst in ((vc_ref, vcx_ref), (vl_ref, vlx_ref)):
            dst[:, :LANES] = src[...]
            dst[:, LANES:] = jnp.ones((src.shape[0], LANES), bf16)

    for h in range(n_sub):
        sc_ref, sl_ref, pc_ref, pl_ref = scratch[4 * h:4 * h + 4]
        q = jnp.concatenate([q_ref[h * tq:(h + 1) * tq, g * LANES:(g + 1) * LANES] for g in range(groups)], axis=0)
        sc_ref[...] = lax.dot_general(q, kc_ref[...], _NT, preferred_element_type=f32)
        sl_ref[...] = lax.dot_general(q, kl_ref[...], _NT, preferred_element_type=f32)
        m = jnp.maximum(jnp.max(sc_ref[...], axis=-1, keepdims=True), jnp.max(sl_ref[...], axis=-1, keepdims=True))
        pc_ref[...] = jnp.exp2(sc_ref[...] - m).astype(bf16)
        pl_ref[...] = jnp.exp2(sl_ref[...] - m).astype(bf16)
    for h in range(n_sub):
        pc_ref, pl_ref = scratch[4 * h + 2:4 * h + 4]
        ox = (jnp.dot(pc_ref[...], vcx_ref[...], preferred_element_type=f32)
              + jnp.dot(pl_ref[...], vlx_ref[...], preferred_element_type=f32))
        o = ox[:, :LANES] / ox[:, LANES:]
        for g in range(groups):
            o_ref[h * tq:(h + 1) * tq, g * LANES:(g + 1) * LANES] = o[g * tq:(g + 1) * tq].astype(o_ref.dtype)


def _attn_lat(q, k, v, q_col0, k_col0, v_col0, groups, n_kv, n_batch, s, c, out_cols):
    n_sub = ATTN_Q_SUBBLOCKS
    tq = n_sub * ATTN_Q_ROWS
    nqb = s // tq
    ctx_blk0 = n_batch * s // c
    qw = groups * LANES
    return pl.pallas_call(
        functools.partial(_gqa_lat_kernel, groups=groups, n_sub=n_sub),
        grid=(n_batch, n_kv, nqb),
        in_specs=[pl.BlockSpec((tq, qw), lambda b, h, i: (b * nqb + i, q_col0 // qw + h)),
                  pl.BlockSpec((c, LANES), lambda b, h, i: (ctx_blk0 + b, k_col0 // LANES + h)),
                  pl.BlockSpec((c, LANES), lambda b, h, i: (ctx_blk0 + b, v_col0 // LANES + h)),
                  pl.BlockSpec((s, LANES), lambda b, h, i: (b, k_col0 // LANES + h)),
                  pl.BlockSpec((s, LANES), lambda b, h, i: (b, v_col0 // LANES + h))],
        out_specs=pl.BlockSpec((tq, qw), lambda b, h, i: (b * nqb + i, h)),
        out_shape=jax.ShapeDtypeStruct((n_batch * s, out_cols), bf16),
        scratch_shapes=[pltpu.VMEM((groups * ATTN_Q_ROWS, n), dt)
                        for _ in range(n_sub) for dt in (f32, bf16) for n in (c, s)]
                       + [pltpu.VMEM((c, 2 * LANES), bf16), pltpu.VMEM((s, 2 * LANES), bf16)],
        compiler_params=_cparams(3),
        name="attn_latent",
    )(q, k, v, k, v)


def _attn_ctx(q, k, v, q_col0, k_col0, v_col0, groups, n_kv, n_batch, s, c, scale):
    ctx_blk0 = n_batch * s // c
    qw = groups * LANES
    return pl.pallas_call(
        functools.partial(_flash_kernel, groups=groups, n_lat_chunks=0, ck=0, scale=scale),
        grid=(n_batch, n_kv),
        in_specs=[pl.BlockSpec((c, qw), lambda b, h: (ctx_blk0 + b, q_col0 // qw + h)),
                  pl.BlockSpec((c, LANES), lambda b, h: (ctx_blk0 + b, k_col0 // LANES + h)),
                  pl.BlockSpec((c, LANES), lambda b, h: (ctx_blk0 + b, v_col0 // LANES + h))],
        out_specs=pl.BlockSpec((c, qw), lambda b, h: (b, h)),
        out_shape=jax.ShapeDtypeStruct((n_batch * c, n_kv * qw), bf16),
        compiler_params=_cparams(2),
        name="attn_context",
    )(q, k, v)


def _natten_kernel(q_ref, k_ref, v_ref, kc_ref, vc_ref, bias_ref, o_ref, *, rows, w, wr, scale):
    kc = kc_ref[...]
    vc = vc_ref[...]

    nr = NATTEN_UNROLL

    def body(g, carry):
        base = pl.multiple_of(g * nr * w, nr * w)
        qg = q_ref[pl.ds(base, nr * w), :]
        scg = lax.dot_general(qg, kc, _NT, preferred_element_type=f32) * scale
        offs, sws = [], []
        for i in range(nr):
            r = g * nr + i
            r0 = jnp.clip(r - wr // 2, 0, rows - wr)
            off = pl.multiple_of(r0 * w, w)
            kb = k_ref[pl.ds(off, wr * w), :]
            sws.append(lax.dot_general(qg[i * w:(i + 1) * w], kb, _NT, preferred_element_type=f32) * scale
                       + bias_ref[r - r0])
            offs.append(off)
        pws, pcs, ls = [], [], []
        for i in range(nr):
            sc = scg[i * w:(i + 1) * w]
            m = jnp.maximum(jnp.max(sws[i], axis=-1, keepdims=True), jnp.max(sc, axis=-1, keepdims=True))
            pw = jnp.exp2(sws[i] - m)
            pc = jnp.exp2(sc - m)
            ls.append(jnp.sum(pw, axis=-1, keepdims=True) + jnp.sum(pc, axis=-1, keepdims=True))
            pws.append(pw.astype(bf16))
            pcs.append(pc.astype(bf16))
        ocg = jnp.dot(jnp.concatenate(pcs, axis=0), vc, preferred_element_type=f32)
        for i in range(nr):
            vb = v_ref[pl.ds(offs[i], wr * w), :]
            o = (jnp.dot(pws[i], vb, preferred_element_type=f32) + ocg[i * w:(i + 1) * w]) / ls[i]
            o_ref[pl.ds(pl.multiple_of(base + i * w, w), w), :] = o.astype(o_ref.dtype)
        return carry

    lax.fori_loop(0, rows // nr, body, 0)


def _natten_bias(rpb, rows, wr, win_r, win_c):
    w = GRID_W
    cols = np.arange(w)
    col_start = np.clip(cols - win_c // 2, 0, w - win_c)
    inside = (cols[None, :] >= col_start[:, None]) & (cols[None, :] < col_start[:, None] + win_c)
    rpbp = jnp.pad(rpb, ((0, 0), (0, 0), (w - win_c, w - win_c)))
    p = jnp.stack([rpbp[:, :, w - 1 - qc:2 * w - 1 - qc] for qc in range(w)], axis=2)
    p = jnp.where(inside[None, None], p * LOG2E, NEG)
    tab = jnp.stack([p[:, win_r - 1 - d:win_r - 1 - d + wr] for d in range(wr)], axis=1)
    return jnp.transpose(tab, (0, 1, 3, 2, 4)).reshape(rpb.shape[0], wr, w, wr * w)


def _natten_lat(qkv, bias, n_heads, n_batch, s, c, d_model):
    t = qkv.shape[0]
    rows = s // GRID_W
    wr = bias.shape[1]
    ctx_blk0 = n_batch * s // c
    hq, hk, hv = 0, d_model // LANES, 2 * d_model // LANES
    return pl.pallas_call(
        functools.partial(_natten_kernel, rows=rows, w=GRID_W, wr=wr, scale=SM_SCALE_LOG2),
        grid=(n_batch, n_heads),
        in_specs=[pl.BlockSpec((s, LANES), lambda b, h: (b, hq + h)),
                  pl.BlockSpec((s, LANES), lambda b, h: (b, hk + h)),
                  pl.BlockSpec((s, LANES), lambda b, h: (b, hv + h)),
                  pl.BlockSpec((c, LANES), lambda b, h: (ctx_blk0 + b, hk + h)),
                  pl.BlockSpec((c, LANES), lambda b, h: (ctx_blk0 + b, hv + h)),
                  pl.BlockSpec((None, wr, GRID_W, wr * GRID_W), lambda b, h: (h, 0, 0, 0))],
        out_specs=pl.BlockSpec((s, LANES), lambda b, h: (b, h)),
        out_shape=jax.ShapeDtypeStruct((n_batch * s, d_model), bf16),
        compiler_params=_cparams(2),
        name="natten_latent",
    )(qkv, qkv, qkv, qkv, qkv, bias)


def _pool_kernel(h_ref, hp_ref, hn_ref, x_ref, w_ref, ls_ref, m_ref, nw_ref, o_ref, f_ref, pad_ref, *,
                 bps_lat, bps_ctx, n_lat_blocks, s, c, gate_row, shift_row, scale_row):
    i = pl.program_id(0)
    tm = h_ref.shape[0]
    is_lat = i < n_lat_blocks
    blk = jnp.where(is_lat, i % bps_lat, (i - n_lat_blocks) % bps_ctx)
    nblk = jnp.where(is_lat, bps_lat, bps_ctx)
    length = jnp.where(is_lat, s, c)
    pad_ref[0:POOL_HALO, :] = jnp.where(blk == 0, 0.0, hp_ref[...])
    pad_ref[POOL_HALO:POOL_HALO + tm, :] = h_ref[...]
    pad_ref[POOL_HALO + tm:2 * POOL_HALO + tm, :] = jnp.where(blk == nblk - 1, 0.0, hn_ref[...])
    pos = blk * tm + lax.broadcasted_iota(jnp.int32, (tm, 1), 0)
    pg = w_ref.shape[1]
    for g, win in enumerate(POOL_WINDOWS):
        cs = slice(g * pg, (g + 1) * pg)
        lo_off, hi_off = win // 2, win - win // 2
        acc = pad_ref[POOL_HALO - lo_off:POOL_HALO - lo_off + tm, cs]
        for k in range(-lo_off + 1, hi_off):
            acc = acc + pad_ref[POOL_HALO + k:POOL_HALO + k + tm, cs]
        cnt = jnp.minimum(pos + hi_off, length) - jnp.maximum(pos - lo_off, 0)
        y = acc * (1.0 / cnt.astype(f32)) - h_ref[:, cs]
        z = jnp.dot(y.astype(bf16), w_ref[g], preferred_element_type=f32) * ls_ref[:, cs]
        o_ref[:, cs] = x_ref[:, cs] + m_ref[gate_row:gate_row + 1, cs] * z
    f_ref[...] = _norm_mod_value(o_ref[...], nw_ref[...], m_ref, shift_row, scale_row).astype(f_ref.dtype)


def _pool(h, x, pool_w, pool_scale, mods, nw, gate_row, shift_row, scale_row, n_batch, s, c):
    t, d = h.shape
    tm = POOL_ROW_BLOCK
    nb = t // tm
    bps_lat, bps_ctx = s // tm, c // tm
    hb = tm // POOL_HALO
    last_halo = t // POOL_HALO - 1
    return pl.pallas_call(
        functools.partial(_pool_kernel, bps_lat=bps_lat, bps_ctx=bps_ctx, n_lat_blocks=n_batch * bps_lat,
                          s=s, c=c, gate_row=gate_row, shift_row=shift_row, scale_row=scale_row),
        grid=(nb,),
        in_specs=[pl.BlockSpec((tm, d), lambda i: (i, 0)),
                  pl.BlockSpec((POOL_HALO, d), lambda i: (jnp.maximum(i * hb - 1, 0), 0)),
                  pl.BlockSpec((POOL_HALO, d), lambda i: (jnp.minimum((i + 1) * hb, last_halo), 0)),
                  pl.BlockSpec((tm, d), lambda i: (i, 0)),
                  pl.BlockSpec(pool_w.shape, lambda i: (0, 0, 0)),
                  pl.BlockSpec((1, d), lambda i: (0, 0)),
                  pl.BlockSpec((None, SUBLANES, d), lambda i: (_group_of(i, bps_lat, n_batch), 0, 0)),
                  pl.BlockSpec((1, d), lambda i: (0, 0))],
        out_specs=[pl.BlockSpec((tm, d), lambda i: (i, 0)), pl.BlockSpec((tm, d), lambda i: (i, 0))],
        out_shape=[jax.ShapeDtypeStruct((t, d), f32), jax.ShapeDtypeStruct((t, d), bf16)],
        scratch_shapes=[pltpu.VMEM((tm + 2 * POOL_HALO, d), f32)],
        compiler_params=_cparams(1),
        name="pool_mixer",
    )(h, h, h, x, pool_w, pool_scale.reshape(1, d), mods, nw.reshape(1, d))


def _topk_rows(chains, tb):
    rows = lax.broadcasted_iota(jnp.int32, (PEER_TOPK, tb), 0)

    def body(k, carry):
        out = []
        for (s_ref, order), (vals, idxs) in zip(chains, carry):
            s = s_ref[...]
            m = jnp.max(s, axis=0, keepdims=True)
            am = jnp.min(jnp.where(s == m, order, float(2 ** 23)), axis=0, keepdims=True)
            s_ref[...] = jnp.where(order == am, NEG, s)
            out.append((jnp.where(rows == k, m, vals), jnp.where(rows == k, am, idxs)))
        return tuple(out)

    zero = jnp.zeros((PEER_TOPK, tb), f32)
    res = lax.fori_loop(0, PEER_TOPK, body, tuple((zero, zero) for _ in chains))
    return [(vals, idxs.astype(jnp.int32)) for vals, idxs in res]


def _cand_blocks():
    blocks = [(0, PEER_TOPK)] + [(a, SUBLANES) for a in range(1, SUBLANES)]
    assert all((a + 1) * (nb + 1) > PEER_TOPK for a, nb in blocks[1:]) and 2 * SUBLANES == PEER_TOPK
    return blocks


def _cast_chunks(n_rows, steps):
    n = 1 << (steps.bit_length() - 1)
    assert n_rows % n == 0
    return n, n_rows // n


def _peer_topk_kernel(q_ref, sk_ref, w_ref, e_ref, g_ref, wb_ref, s1_ref, s2_ref, cand_ref, es_ref, gs_ref, *,
                      n_heads, nkeys):
    tb = q_ref.shape[0]
    wb_ref[...] = w_ref[...].astype(bf16)
    key_order = lax.broadcasted_iota(jnp.int32, (nkeys, tb), 0).astype(f32)
    blocks = _cand_blocks()
    n_mid = (len(blocks) - 1) * SUBLANES
    r = lax.broadcasted_iota(jnp.int32, cand_ref.shape, 0)
    rm = r - PEER_TOPK
    mid = (lax.shift_right_logical(rm, 3) + 1) * PEER_TOPK + lax.bitwise_and(rm, SUBLANES - 1)
    tail = (rm - n_mid + SUBLANES) * PEER_TOPK
    cand_order = jnp.where(r < PEER_TOPK, r, jnp.where(rm < n_mid, mid, tail)).astype(f32)

    def scores(h):
        col = pl.multiple_of(h * 2 * LANES, 2 * LANES)
        s1_ref[...] = lax.dot_general(sk_ref[0], q_ref[:, pl.ds(col, LANES)], _NT, preferred_element_type=f32)
        s2_ref[...] = lax.dot_general(sk_ref[1], q_ref[:, pl.ds(col + LANES, LANES)], _NT, preferred_element_type=f32)

    sub_chains = [(s1_ref, key_order), (s2_ref, key_order)]
    scores(0)
    (v1, i1), (v2, i2) = _topk_rows(sub_chains, tb)

    def head(h, carry):
        v1, i1, v2, i2 = carry
        r0 = 0
        for a, nb in blocks:
            cand_ref[r0:r0 + nb, :] = v1[a:a + 1, :] + v2[:nb]
            r0 += nb
        cand_ref[r0:r0 + SUBLANES, :] = v1[SUBLANES:] + v2[0:1, :]
        scores(jnp.minimum(h + 1, n_heads - 1))
        nxt1, nxt2, (sc, ci) = _topk_rows(sub_chains + [(cand_ref, cand_order)], tb)
        ca = lax.shift_right_logical(ci, PEER_TOPK.bit_length() - 1)
        cb = lax.bitwise_and(ci, PEER_TOPK - 1)
        e1 = jnp.zeros((PEER_TOPK, tb), jnp.int32)
        e2 = jnp.zeros((PEER_TOPK, tb), jnp.int32)
        for a in range(PEER_TOPK):
            e1 = jnp.where(ca == a, i1[a:a + 1, :], e1)
            e2 = jnp.where(cb == a, i2[a:a + 1, :], e2)
        p = jnp.exp(sc - sc[0:1, :])
        gate = p / jnp.sum(p, axis=0, keepdims=True)
        row = pl.multiple_of(h * PEER_TOPK, PEER_TOPK)
        es_ref[pl.ds(row, PEER_TOPK), :] = e1 * nkeys + e2
        gs_ref[pl.ds(row, PEER_TOPK), :] = gate
        return nxt1 + nxt2

    lax.fori_loop(0, n_heads, head, (v1, i1, v2, i2))
    e_ref[...] = es_ref[...].T
    g_ref[...] = gs_ref[...].T


def _peer_topk(q, sub_keys, n_blocks_rows, w_all, layer):
    t, qd = q.shape
    nkeys = sub_keys.shape[1]
    n_heads = qd // (2 * LANES)
    slots = n_heads * PEER_TOPK
    tb = TOPK_TOKENS
    steps = n_blocks_rows * ROW_BLOCK // tb
    ne, d = w_all.shape[1:]
    n_chunks, chunk = _cast_chunks(ne, steps)
    return pl.pallas_call(
        functools.partial(_peer_topk_kernel, n_heads=n_heads, nkeys=nkeys),
        grid=(steps,),
        in_specs=[pl.BlockSpec((tb, qd), lambda i: (i, 0)),
                  pl.BlockSpec(sub_keys.shape, lambda i: (0, 0, 0)),
                  pl.BlockSpec((None, chunk, d), lambda i: (layer, jnp.minimum(i, n_chunks - 1), 0))],
        out_specs=[pl.BlockSpec((tb, slots), lambda i: (i, 0)),
                   pl.BlockSpec((tb, slots), lambda i: (i, 0)),
                   pl.BlockSpec((chunk, d), lambda i: (jnp.minimum(i, n_chunks - 1), 0))],
        out_shape=[jax.ShapeDtypeStruct((n_blocks_rows * ROW_BLOCK, slots), jnp.int32),
                   jax.ShapeDtypeStruct((n_blocks_rows * ROW_BLOCK, slots), f32),
                   jax.ShapeDtypeStruct((ne, d), bf16)],
        scratch_shapes=[pltpu.VMEM((nkeys, tb), f32), pltpu.VMEM((nkeys, tb), f32),
                        pltpu.VMEM((sum(nb for _, nb in _cand_blocks()) + SUBLANES, tb), f32),
                        pltpu.VMEM((slots, tb), jnp.int32), pltpu.VMEM((slots, tb), f32)],
        compiler_params=_cparams(1),
        name="peer_topk",
    )(q, sub_keys, w_all)


def _peer_gates_kernel(e_ref, g_ref, w_ref, o_ref, wb_ref, *, nkeys):
    tb, slots = e_ref.shape
    wb_ref[...] = w_ref[...].astype(bf16)
    iota = lax.broadcasted_iota(jnp.int32, (nkeys, slots), 0)

    def body(t, carry):
        e = e_ref[pl.ds(t, 1), :]
        g = g_ref[pl.ds(t, 1), :]
        e1 = lax.shift_right_logical(e, nkeys.bit_length() - 1)
        e2 = lax.bitwise_and(e, nkeys - 1)
        w1 = jnp.where(e1 == iota, g, 0.0).astype(bf16)
        o2 = jnp.where(e2 == iota, 1.0, 0.0).astype(bf16)
        res = lax.dot_general(w1, o2, _NT, preferred_element_type=f32)
        bits = lax.bitcast_convert_type(res.astype(bf16).astype(f32), jnp.uint32)
        packed = jnp.bitwise_or(jnp.right_shift(bits[:nkeys // 2], jnp.uint32(16)), bits[nkeys // 2:])
        o_ref[:, t] = packed.reshape(nkeys // (2 * SUBLANES), SUBLANES, nkeys)
        return carry

    lax.fori_loop(0, tb, body, 0, unroll=GATES_UNROLL)


def _peer_gates(e, g, nkeys, n_blocks_rows, w_all, layer):
    t, slots = e.shape
    tb = 128
    nblk = nkeys // (2 * SUBLANES)
    steps = n_blocks_rows * ROW_BLOCK // tb
    ne, d = w_all.shape[1:]
    n_chunks, chunk = _cast_chunks(ne, steps)
    return pl.pallas_call(
        functools.partial(_peer_gates_kernel, nkeys=nkeys),
        grid=(steps,),
        in_specs=[pl.BlockSpec((tb, slots), lambda i: (i, 0)),
                  pl.BlockSpec((tb, slots), lambda i: (i, 0)),
                  pl.BlockSpec((None, chunk, d), lambda i: (layer, jnp.minimum(i, n_chunks - 1), 0))],
        out_specs=[pl.BlockSpec((nblk, tb, SUBLANES, nkeys), lambda i: (0, i, 0, 0)),
                   pl.BlockSpec((chunk, d), lambda i: (jnp.minimum(i, n_chunks - 1), 0))],
        out_shape=[jax.ShapeDtypeStruct((nblk, t, SUBLANES, nkeys), jnp.uint32),
                   jax.ShapeDtypeStruct((ne, d), bf16)],
        compiler_params=_cparams(1),
        name="peer_gates",
    )(e, g, w_all)


def _gelu_tanh(x):
    k0 = -2.0 * 0.7978845608028654 * 1.4426950408889634
    return x / (1.0 + jnp.exp2(x * (k0 + (k0 * 0.044715) * (x * x))))


def _peer_dense_kernel(f_ref, u_ref, v_ref, g_ref, x_ref, m_ref, o_ref, acc_ref, a_ref, *, gate_row, nkeys):
    j = pl.program_id(1)

    @pl.when(j == 0)
    def _():
        acc_ref[...] = jnp.zeros_like(acc_ref)

    slot = j % 2
    a_ref[slot] = lax.dot_general(f_ref[...], u_ref[...], _NT, preferred_element_type=f32)
    tb = f_ref.shape[0]
    n_first = g_ref.shape[0] // tb
    shift = jnp.where(j < pl.num_programs(1) // 2, 16, 0).astype(jnp.uint32)
    parts = []
    for k in range(n_first):
        word = g_ref[pl.ds(k, tb, stride=n_first), :]
        gbits = jnp.bitwise_and(jnp.left_shift(word, shift), jnp.uint32(0xFFFF0000))
        gk = lax.bitcast_convert_type(gbits, f32)
        parts.append((_gelu_tanh(a_ref[slot, :, k * nkeys:(k + 1) * nkeys]) * gk).astype(bf16))
    ga = jnp.concatenate(parts, axis=1)
    acc_ref[...] += jnp.dot(ga, v_ref[...], preferred_element_type=f32)

    @pl.when(j == pl.num_programs(1) - 1)
    def _():
        o_ref[...] = x_ref[...] + m_ref[gate_row:gate_row + 1, :] * acc_ref[...]


def _peer_dense(f, u, v, gates, x, mods, gate_row, n_blocks, bps, n_batch):
    t, d = f.shape
    ne = v.shape[0]
    nkeys = gates.shape[3]
    te = SUBLANES * nkeys
    n_gblk = gates.shape[0]
    assert ne // te == 2 * n_gblk
    gates = gates.reshape(n_gblk, gates.shape[1] * SUBLANES, nkeys)
    return pl.pallas_call(
        functools.partial(_peer_dense_kernel, gate_row=gate_row, nkeys=nkeys),
        grid=(n_blocks, ne // te),
        in_specs=[pl.BlockSpec((ROW_BLOCK, d), lambda i, j: (i, 0)),
                  pl.BlockSpec((te, d), lambda i, j: (j, 0)),
                  pl.BlockSpec((te, d), lambda i, j: (j, 0)),
                  pl.BlockSpec((None, ROW_BLOCK * SUBLANES, nkeys), lambda i, j: (j % n_gblk, i, 0)),
                  pl.BlockSpec((ROW_BLOCK, d), lambda i, j: (i, 0)),
                  pl.BlockSpec((None, SUBLANES, d), lambda i, j: (_group_of(i, bps, n_batch), 0, 0))],
        out_specs=pl.BlockSpec((ROW_BLOCK, d), lambda i, j: (i, 0)),
        out_shape=jax.ShapeDtypeStruct((t, d), f32),
        scratch_shapes=[pltpu.VMEM((ROW_BLOCK, d), f32), pltpu.VMEM((2, ROW_BLOCK, te), f32)],
        compiler_params=_cparams(2),
        name="peer_dense",
    )(f, u, v, gates, x, mods)


def kernel(x, c, ctx, c_ctx, mod_w, mod_b, norm_w, final_norm_w, a_wqkv, a_q_gain, a_k_gain, a_wo,
           b_wqkv, b_rpb, b_wo, pool_w, pool_scale, peer_wq, peer_sub_keys, peer_u, peer_v):
    n_batch, s, d = x.shape
    c_len = ctx.shape[1]
    depth = mod_w.shape[0]
    assert s % ROW_BLOCK == 0 and (n_batch * c_len) % ROW_BLOCK == 0 and n_batch * c_len <= s
    assert s % c_len == 0 and c_len % POOL_ROW_BLOCK == 0 and s % GRID_W == 0
    assert a_q_gain.shape[1] == LANES and peer_sub_keys.shape[2] == LANES and peer_sub_keys.shape[3] == LANES
    assert PEER_TOPK & (PEER_TOPK - 1) == 0 and peer_sub_keys.shape[2] & (peer_sub_keys.shape[2] - 1) == 0
    assert depth % N_MIXERS != 0

    bps = s // ROW_BLOCK
    nb_lat = n_batch * bps
    nb_all = nb_lat + n_batch * c_len // ROW_BLOCK
    t_lat = n_batch * s
    q_dim = a_wo.shape[1]
    kv_dim = (a_wqkv.shape[2] - q_dim) // 2
    n_kv = kv_dim // LANES
    groups = q_dim // kv_dim
    b_heads = b_rpb.shape[1]
    win_r, win_c = (b_rpb.shape[2] + 1) // 2, (b_rpb.shape[3] + 1) // 2
    nkeys = peer_sub_keys.shape[2]
    scale = SM_SCALE_LOG2

    xs = jnp.concatenate([x.reshape(t_lat, d), ctx.reshape(n_batch * c_len, d)], axis=0)
    cvec = jnp.concatenate([c, c_ctx[None, :], jnp.zeros((SUBLANES - n_batch - 1, d), f32)], axis=0)
    mods_all = _mods(cvec, mod_w, mod_b)
    tables = _rope_tables(s)

    for i in range(depth):
        last = i == depth - 1
        kind, j = i % N_MIXERS, i // N_MIXERS
        nb = nb_lat if last else nb_all
        mods = mods_all[i, :n_batch + 1].reshape(n_batch + 1, N_MOD, d)
        mods = jnp.pad(mods, ((0, 0), (0, SUBLANES - N_MOD), (0, 0)))

        if kind == 2:
            h = _norm_mod(xs, norm_w[i, 0], mods, 0, 1, nb_all, bps, n_batch, f32)
            xs, f = _pool(h, xs, pool_w[j].astype(bf16), pool_scale[j], mods, norm_w[i, 1], 2, 3, 4, n_batch, s, c_len)
        else:
            h = _norm_mod(xs, norm_w[i, 0], mods, 0, 1, nb_all, bps, n_batch, bf16)
            if kind == 0:
                qkv = _qkv_gqa(h, a_wqkv[j].astype(bf16), a_q_gain[j], a_k_gain[j], tables, q_dim, kv_dim,
                               nb_all, bps, n_batch)
                o = _attn_lat(qkv, qkv, qkv, 0, q_dim, q_dim + kv_dim, groups, n_kv, n_batch, s, c_len, q_dim)
                if not last:
                    o_ctx = _attn_ctx(qkv, qkv, qkv, 0, q_dim, q_dim + kv_dim, groups, n_kv, n_batch, s, c_len, 1.0)
                    o = jnp.concatenate([o, o_ctx], axis=0)
                wo = a_wo[j]
            else:
                qkv = _mm(h, b_wqkv[j].astype(bf16), nb_all, _col_tile(3 * d), bf16)
                rows = s // GRID_W
                bias = _natten_bias(b_rpb[j], rows, min(win_r, rows), win_r, win_c)
                o = _natten_lat(qkv, bias, b_heads, n_batch, s, c_len, d)
                if not last:
                    o_ctx = _attn_ctx(qkv, qkv, qkv, 0, d, 2 * d, 1, b_heads, n_batch, s, c_len, scale)
                    o = jnp.concatenate([o, o_ctx], axis=0)
                wo = b_wo[j]
            xs, f = _mm_res(o, wo.astype(bf16), xs, mods, norm_w[i, 1], 2, 3, 4, nb, bps, n_batch)

        qp = _mm(f, peer_wq[i].astype(bf16), nb, _col_tile(peer_wq.shape[2]), f32)
        e, g, u_bf = _peer_topk(qp, peer_sub_keys[i], nb, peer_u, i)
        gates, v_bf = _peer_gates(e, g, nkeys, nb, peer_v, i)
        xs = _peer_dense(f, u_bf, v_bf, gates, xs, mods, 5, nb, bps, n_batch)

    return _final_norm(xs, final_norm_w, nb_lat).reshape(n_batch, s, d)
```

```python
import functools

import numpy as np
import jax
import jax.numpy as jnp
from jax import lax
from jax.experimental import pallas as pl
from jax.experimental.pallas import tpu as pltpu

GRID_W = 64
ROPE_BASE = 10000.0
POOL_WINDOWS = (2, 4, 8, 16)
PEER_TOPK = 16
EPS = 1e-6
N_MIXERS = 3
N_MOD = 6

LANES = 128
SUBLANES = 8
ROW_BLOCK = 512
POOL_ROW_BLOCK = 256
POOL_HALO = 8
MM_COL_TILE = 2048
ATTN_Q_ROWS = 128
ATTN_Q_SUBBLOCKS = 2
NATTEN_UNROLL = 8
TOPK_TOKENS = 512
GATES_UNROLL = 32
VMEM_LIMIT = 56 * 1024 * 1024
NEG = -1e30
LOG2E = 1.4426950408889634
SM_SCALE_LOG2 = float(LANES) ** -0.5 * LOG2E

f32 = jnp.float32
bf16 = jnp.bfloat16
_NT = (((1,), (1,)), ((), ()))


def _cparams(n_axes):
    return pltpu.CompilerParams(dimension_semantics=("arbitrary",) * n_axes, vmem_limit_bytes=VMEM_LIMIT)


def _group_of(i, blocks_per_seq, n_batch):
    return jnp.minimum(i // blocks_per_seq, n_batch)


def _mods_kernel(cv_ref, w_ref, b_ref, o_ref):
    cv = cv_ref[...]
    a = (cv / (1.0 + jnp.exp(-cv))).astype(bf16)
    o_ref[...] = jnp.dot(a, w_ref[...].astype(bf16), preferred_element_type=f32) + b_ref[...]


def _mods(cvec, mod_w, mod_b):
    depth, d, n = mod_w.shape
    tn = 1024
    return pl.pallas_call(
        _mods_kernel,
        grid=(depth, n // tn),
        in_specs=[pl.BlockSpec((SUBLANES, d), lambda l, j: (0, 0)),
                  pl.BlockSpec((None, d, tn), lambda l, j: (l, 0, j)),
                  pl.BlockSpec((None, 1, tn), lambda l, j: (l, 0, j))],
        out_specs=pl.BlockSpec((None, SUBLANES, tn), lambda l, j: (l, 0, j)),
        out_shape=jax.ShapeDtypeStruct((depth, SUBLANES, n), f32),
        compiler_params=_cparams(2),
        name="adaln_mods",
    )(cvec, mod_w, mod_b.reshape(depth, 1, n))


def _norm_mod_value(x, nw, m_ref, shift_row, scale_row):
    y = x * lax.rsqrt(jnp.mean(x * x, axis=-1, keepdims=True) + EPS) * nw
    return y * (1.0 + m_ref[scale_row:scale_row + 1, :]) + m_ref[shift_row:shift_row + 1, :]


def _norm_mod_kernel(x_ref, nw_ref, m_ref, o_ref, *, shift_row, scale_row):
    o_ref[...] = _norm_mod_value(x_ref[...], nw_ref[...], m_ref, shift_row, scale_row).astype(o_ref.dtype)


def _norm_mod(x, nw, mods, shift_row, scale_row, n_blocks, bps, n_batch, out_dtype):
    t, d = x.shape
    return pl.pallas_call(
        functools.partial(_norm_mod_kernel, shift_row=shift_row, scale_row=scale_row),
        grid=(n_blocks,),
        in_specs=[pl.BlockSpec((ROW_BLOCK, d), lambda i: (i, 0)),
                  pl.BlockSpec((1, d), lambda i: (0, 0)),
                  pl.BlockSpec((None, SUBLANES, d), lambda i: (_group_of(i, bps, n_batch), 0, 0))],
        out_specs=pl.BlockSpec((ROW_BLOCK, d), lambda i: (i, 0)),
        out_shape=jax.ShapeDtypeStruct((n_blocks * ROW_BLOCK, d), out_dtype),
        compiler_params=_cparams(1),
        name="norm_mod",
    )(x, nw.reshape(1, d), mods)


def _final_norm_kernel(x_ref, nw_ref, o_ref):
    x = x_ref[...]
    o_ref[...] = x * lax.rsqrt(jnp.mean(x * x, axis=-1, keepdims=True) + EPS) * nw_ref[...]


def _final_norm(x, nw, n_blocks):
    t, d = x.shape
    return pl.pallas_call(
        _final_norm_kernel,
        grid=(n_blocks,),
        in_specs=[pl.BlockSpec((ROW_BLOCK, d), lambda i: (i, 0)),
                  pl.BlockSpec((1, d), lambda i: (0, 0))],
        out_specs=pl.BlockSpec((ROW_BLOCK, d), lambda i: (i, 0)),
        out_shape=jax.ShapeDtypeStruct((n_blocks * ROW_BLOCK, d), f32),
        compiler_params=_cparams(1),
        name="final_norm",
    )(x, nw.reshape(1, d))


def _mm_kernel(a_ref, w_ref, o_ref):
    o_ref[...] = jnp.dot(a_ref[...], w_ref[...], preferred_element_type=f32).astype(o_ref.dtype)


def _col_tile(n):
    return max(t for t in range(LANES, min(n, MM_COL_TILE) + 1, LANES) if n % t == 0)


def _mm(a, w, n_blocks, tn, out_dtype):
    t, k = a.shape
    n = w.shape[1]
    return pl.pallas_call(
        _mm_kernel,
        grid=(n // tn, n_blocks),
        in_specs=[pl.BlockSpec((ROW_BLOCK, k), lambda j, i: (i, 0)),
                  pl.BlockSpec((k, tn), lambda j, i: (0, j))],
        out_specs=pl.BlockSpec((ROW_BLOCK, tn), lambda j, i: (i, j)),
        out_shape=jax.ShapeDtypeStruct((n_blocks * ROW_BLOCK, n), out_dtype),
        compiler_params=_cparams(2),
        name="matmul",
    )(a, w)


def _mm_res_kernel(a_ref, w_ref, x_ref, m_ref, nw_ref, o_ref, f_ref, *, gate_row, shift_row, scale_row):
    y = x_ref[...] + m_ref[gate_row:gate_row + 1, :] * jnp.dot(a_ref[...], w_ref[...], preferred_element_type=f32)
    o_ref[...] = y
    f_ref[...] = _norm_mod_value(y, nw_ref[...], m_ref, shift_row, scale_row).astype(f_ref.dtype)


def _mm_res(a, w, x, mods, nw, gate_row, shift_row, scale_row, n_blocks, bps, n_batch):
    t, k = a.shape
    n = w.shape[1]
    rows = n_blocks * ROW_BLOCK
    return pl.pallas_call(
        functools.partial(_mm_res_kernel, gate_row=gate_row, shift_row=shift_row, scale_row=scale_row),
        grid=(n_blocks,),
        in_specs=[pl.BlockSpec((ROW_BLOCK, k), lambda i: (i, 0)),
                  pl.BlockSpec((k, n), lambda i: (0, 0)),
                  pl.BlockSpec((ROW_BLOCK, n), lambda i: (i, 0)),
                  pl.BlockSpec((None, SUBLANES, n), lambda i: (_group_of(i, bps, n_batch), 0, 0)),
                  pl.BlockSpec((1, n), lambda i: (0, 0))],
        out_specs=[pl.BlockSpec((ROW_BLOCK, n), lambda i: (i, 0)), pl.BlockSpec((ROW_BLOCK, n), lambda i: (i, 0))],
        out_shape=[jax.ShapeDtypeStruct((rows, n), f32), jax.ShapeDtypeStruct((rows, n), bf16)],
        compiler_params=_cparams(1),
        name="matmul_residual",
    )(a, w, x, mods, nw.reshape(1, n))


def _qkv_gqa_kernel(a_ref, w_ref, qg_ref, kg_ref, cos_ref, sa_ref, sb_ref, o_ref, *, nq_tiles, nk_tiles, scale):
    j = pl.program_id(0)
    acc = jnp.dot(a_ref[...], w_ref[...], preferred_element_type=f32)
    heads = acc.shape[1] // LANES

    def prep(gain, post):
        cos, sa, sb = cos_ref[...], sa_ref[...], sb_ref[...]
        for h in range(heads):
            y = acc[:, h * LANES:(h + 1) * LANES]
            y = y * lax.rsqrt(jnp.mean(y * y, axis=-1, keepdims=True) + EPS) * gain
            y = y * cos + pltpu.roll(y, LANES - 32, 1) * sa + pltpu.roll(y, 32, 1) * sb
            o_ref[:, h * LANES:(h + 1) * LANES] = (y * post).astype(o_ref.dtype)

    @pl.when(j < nq_tiles)
    def _():
        prep(qg_ref[...], scale)

    @pl.when(jnp.logical_and(j >= nq_tiles, j < nq_tiles + nk_tiles))
    def _():
        prep(kg_ref[...], 1.0)

    @pl.when(j >= nq_tiles + nk_tiles)
    def _():
        o_ref[...] = acc.astype(o_ref.dtype)


def _rope_tables(s):
    t = np.arange(s)
    half = LANES // 2
    inv = ROPE_BASE ** (-jnp.arange(0, half, 2, dtype=f32) / half)
    ang_r = (t // GRID_W).astype(np.float32)[:, None] * inv[None, :]
    ang_c = (t % GRID_W).astype(np.float32)[:, None] * inv[None, :]
    ang = jnp.concatenate([ang_r, ang_r, ang_c, ang_c], axis=-1)
    cos, sin = jnp.cos(ang), jnp.sin(ang)
    first = (np.arange(LANES) % half) < (half // 2)
    sa = jnp.where(first[None, :], -sin, 0.0)
    sb = jnp.where(first[None, :], 0.0, sin)
    pad1 = jnp.ones((ROW_BLOCK, LANES), f32)
    pad0 = jnp.zeros((ROW_BLOCK, LANES), f32)
    return (jnp.concatenate([cos, pad1], 0), jnp.concatenate([sa, pad0], 0), jnp.concatenate([sb, pad0], 0))


def _qkv_gqa(h, wqkv, q_gain, k_gain, tables, q_dim, kv_dim, n_blocks, bps, n_batch):
    t, k = h.shape
    n = wqkv.shape[1]
    tn = min(512, kv_dim)
    cos, sa, sb = tables

    def pos_map(j, i):
        return (jnp.where(i < bps * n_batch, i % bps, bps), 0)

    return pl.pallas_call(
        functools.partial(_qkv_gqa_kernel, nq_tiles=q_dim // tn, nk_tiles=kv_dim // tn, scale=SM_SCALE_LOG2),
        grid=(n // tn, n_blocks),
        in_specs=[pl.BlockSpec((ROW_BLOCK, k), lambda j, i: (i, 0)),
                  pl.BlockSpec((k, tn), lambda j, i: (0, j)),
                  pl.BlockSpec((1, LANES), lambda j, i: (0, 0)),
                  pl.BlockSpec((1, LANES), lambda j, i: (0, 0)),
                  pl.BlockSpec((ROW_BLOCK, LANES), pos_map),
                  pl.BlockSpec((ROW_BLOCK, LANES), pos_map),
                  pl.BlockSpec((ROW_BLOCK, LANES), pos_map)],
        out_specs=pl.BlockSpec((ROW_BLOCK, tn), lambda j, i: (i, j)),
        out_shape=jax.ShapeDtypeStruct((t, n), bf16),
        compiler_params=_cparams(2),
        name="qkv_gqa",
    )(h, wqkv, q_gain.reshape(1, LANES), k_gain.reshape(1, LANES), cos, sa, sb)


def _flash_kernel(*refs, groups, n_lat_chunks, ck, scale):
    if n_lat_chunks:
        q_ref, kc_ref, vc_ref, kl_ref, vl_ref, o_ref = refs
    else:
        q_ref, kc_ref, vc_ref, o_ref = refs
    tq = q_ref.shape[0]
    q = jnp.concatenate([q_ref[:, g * LANES:(g + 1) * LANES] for g in range(groups)], axis=0)
    rows = groups * tq

    def step(k, v, m, l, acc):
        s = lax.dot_general(q, k, _NT, preferred_element_type=f32)
        if scale != 1.0:
            s = s * scale
        m_new = jnp.maximum(m, jnp.max(s, axis=-1, keepdims=True))
        p = jnp.exp2(s - m_new)
        alpha = jnp.exp2(m - m_new)
        l = alpha * l + jnp.sum(p, axis=-1, keepdims=True)
        acc = alpha * acc + jnp.dot(p.astype(bf16), v, preferred_element_type=f32)
        return m_new, l, acc

    carry = step(kc_ref[...], vc_ref[...], jnp.full((rows, 1), NEG, f32), jnp.zeros((rows, 1), f32),
                 jnp.zeros((rows, LANES), f32))
    if n_lat_chunks:
        def body(c, carry):
            off = pl.multiple_of(c * ck, ck)
            return step(kl_ref[pl.ds(off, ck), :], vl_ref[pl.ds(off, ck), :], *carry)
        carry = lax.fori_loop(0, n_lat_chunks, body, carry)
    _, l, acc = carry
    o = acc / l
    for g in range(groups):
        o_ref[:, g * LANES:(g + 1) * LANES] = o[g * tq:(g + 1) * tq].astype(o_ref.dtype)


def _gqa_lat_kernel(q_ref, kc_ref, vc_ref, kl_ref, vl_ref, o_ref, *scratch, groups, n_sub):
    tq = q_ref.shape[0] // n_sub
    vcx_ref, vlx_ref = scratch[4 * n_sub:]

    @pl.when(pl.program_id(2) == 0)
    def _():
        for src, dst in ((vc_ref, vcx_ref), (vl_ref, vlx_ref)):
            dst[:, :LANES] = src[...]
            dst[:, LANES:] = jnp.ones((src.shape[0], LANES), bf16)

    for h in range(n_sub):
        sc_ref, sl_ref, pc_ref, pl_ref = scratch[4 * h:4 * h + 4]
        q = jnp.concatenate([q_ref[h * tq:(h + 1) * tq, g * LANES:(g + 1) * LANES] for g in range(groups)], axis=0)
        sc_ref[...] = lax.dot_general(q, kc_ref[...], _NT, preferred_element_type=f32)
        sl_ref[...] = lax.dot_general(q, kl_ref[...], _NT, preferred_element_type=f32)
        m = jnp.maximum(jnp.max(sc_ref[...], axis=-1, keepdims=True), jnp.max(sl_ref[...], axis=-1, keepdims=True))
        pc_ref[...] = jnp.exp2((sc_ref[...] - m).astype(bf16))
        pl_ref[...] = jnp.exp2((sl_ref[...] - m).astype(bf16))
    for h in range(n_sub):
        pc_ref, pl_ref = scratch[4 * h + 2:4 * h + 4]
        ox = (jnp.dot(pc_ref[...], vcx_ref[...], preferred_element_type=f32)
              + jnp.dot(pl_ref[...], vlx_ref[...], preferred_element_type=f32))
        o = ox[:, :LANES] / ox[:, LANES:]
        for g in range(groups):
            o_ref[h * tq:(h + 1) * tq, g * LANES:(g + 1) * LANES] = o[g * tq:(g + 1) * tq].astype(o_ref.dtype)


def _attn_lat(q, k, v, q_col0, k_col0, v_col0, groups, n_kv, n_batch, s, c, out_cols):
    n_sub = ATTN_Q_SUBBLOCKS
    tq = n_sub * ATTN_Q_ROWS
    nqb = s // tq
    ctx_blk0 = n_batch * s // c
    qw = groups * LANES
    return pl.pallas_call(
        functools.partial(_gqa_lat_kernel, groups=groups, n_sub=n_sub),
        grid=(n_batch, n_kv, nqb),
        in_specs=[pl.BlockSpec((tq, qw), lambda b, h, i: (b * nqb + i, q_col0 // qw + h)),
                  pl.BlockSpec((c, LANES), lambda b, h, i: (ctx_blk0 + b, k_col0 // LANES + h)),
                  pl.BlockSpec((c, LANES), lambda b, h, i: (ctx_blk0 + b, v_col0 // LANES + h)),
                  pl.BlockSpec((s, LANES), lambda b, h, i: (b, k_col0 // LANES + h)),
                  pl.BlockSpec((s, LANES), lambda b, h, i: (b, v_col0 // LANES + h))],
        out_specs=pl.BlockSpec((tq, qw), lambda b, h, i: (b * nqb + i, h)),
        out_shape=jax.ShapeDtypeStruct((n_batch * s, out_cols), bf16),
        scratch_shapes=[pltpu.VMEM((groups * ATTN_Q_ROWS, n), dt)
                        for _ in range(n_sub) for dt in (f32, bf16) for n in (c, s)]
                       + [pltpu.VMEM((c, 2 * LANES), bf16), pltpu.VMEM((s, 2 * LANES), bf16)],
        compiler_params=_cparams(3),
        name="attn_latent",
    )(q, k, v, k, v)


def _attn_ctx(q, k, v, q_col0, k_col0, v_col0, groups, n_kv, n_batch, s, c, scale):
    ctx_blk0 = n_batch * s // c
    qw = groups * LANES
    return pl.pallas_call(
        functools.partial(_flash_kernel, groups=groups, n_lat_chunks=0, ck=0, scale=scale),
        grid=(n_batch, n_kv),
        in_specs=[pl.BlockSpec((c, qw), lambda b, h: (ctx_blk0 + b, q_col0 // qw + h)),
                  pl.BlockSpec((c, LANES), lambda b, h: (ctx_blk0 + b, k_col0 // LANES + h)),
                  pl.BlockSpec((c, LANES), lambda b, h: (ctx_blk0 + b, v_col0 // LANES + h))],
        out_specs=pl.BlockSpec((c, qw), lambda b, h: (b, h)),
        out_shape=jax.ShapeDtypeStruct((n_batch * c, n_kv * qw), bf16),
        compiler_params=_cparams(2),
        name="attn_context",
    )(q, k, v)


def _natten_kernel(q_ref, k_ref, v_ref, kc_ref, vc_ref, bias_ref, o_ref, *, rows, w, wr, scale):
    kc = kc_ref[...]
    vc = vc_ref[...]

    nr = NATTEN_UNROLL

    def body(g, carry):
        base = pl.multiple_of(g * nr * w, nr * w)
        qg = q_ref[pl.ds(base, nr * w), :]
        scg = lax.dot_general(qg, kc, _NT, preferred_element_type=f32) * scale
        offs, sws = [], []
        for i in range(nr):
            r = g * nr + i
            r0 = jnp.clip(r - wr // 2, 0, rows - wr)
            off = pl.multiple_of(r0 * w, w)
            kb = k_ref[pl.ds(off, wr * w), :]
            sws.append(lax.dot_general(qg[i * w:(i + 1) * w], kb, _NT, preferred_element_type=f32) * scale
                       + bias_ref[r - r0])
            offs.append(off)
        pws, pcs, ls = [], [], []
        for i in range(nr):
            sc = scg[i * w:(i + 1) * w]
            m = jnp.maximum(jnp.max(sws[i], axis=-1, keepdims=True), jnp.max(sc, axis=-1, keepdims=True))
            pw = jnp.exp2(sws[i] - m)
            pc = jnp.exp2(sc - m)
            ls.append(jnp.sum(pw, axis=-1, keepdims=True) + jnp.sum(pc, axis=-1, keepdims=True))
            pws.append(pw.astype(bf16))
            pcs.append(pc.astype(bf16))
        ocg = jnp.dot(jnp.concatenate(pcs, axis=0), vc, preferred_element_type=f32)
        for i in range(nr):
            vb = v_ref[pl.ds(offs[i], wr * w), :]
            o = (jnp.dot(pws[i], vb, preferred_element_type=f32) + ocg[i * w:(i + 1) * w]) / ls[i]
            o_ref[pl.ds(pl.multiple_of(base + i * w, w), w), :] = o.astype(o_ref.dtype)
        return carry

    lax.fori_loop(0, rows // nr, body, 0)


def _natten_bias(rpb, rows, wr, win_r, win_c):
    w = GRID_W
    cols = np.arange(w)
    col_start = np.clip(cols - win_c // 2, 0, w - win_c)
    inside = (cols[None, :] >= col_start[:, None]) & (cols[None, :] < col_start[:, None] + win_c)
    rpbp = jnp.pad(rpb, ((0, 0), (0, 0), (w - win_c, w - win_c)))
    p = jnp.stack([rpbp[:, :, w - 1 - qc:2 * w - 1 - qc] for qc in range(w)], axis=2)
    p = jnp.where(inside[None, None], p * LOG2E, NEG)
    tab = jnp.stack([p[:, win_r - 1 - d:win_r - 1 - d + wr] for d in range(wr)], axis=1)
    return jnp.transpose(tab, (0, 1, 3, 2, 4)).reshape(rpb.shape[0], wr, w, wr * w)


def _natten_lat(qkv, bias, n_heads, n_batch, s, c, d_model):
    t = qkv.shape[0]
    rows = s // GRID_W
    wr = bias.shape[1]
    ctx_blk0 = n_batch * s // c
    hq, hk, hv = 0, d_model // LANES, 2 * d_model // LANES
    return pl.pallas_call(
        functools.partial(_natten_kernel, rows=rows, w=GRID_W, wr=wr, scale=SM_SCALE_LOG2),
        grid=(n_batch, n_heads),
        in_specs=[pl.BlockSpec((s, LANES), lambda b, h: (b, hq + h)),
                  pl.BlockSpec((s, LANES), lambda b, h: (b, hk + h)),
                  pl.BlockSpec((s, LANES), lambda b, h: (b, hv + h)),
                  pl.BlockSpec((c, LANES), lambda b, h: (ctx_blk0 + b, hk + h)),
                  pl.BlockSpec((c, LANES), lambda b, h: (ctx_blk0 + b, hv + h)),
                  pl.BlockSpec((None, wr, GRID_W, wr * GRID_W), lambda b, h: (h, 0, 0, 0))],
        out_specs=pl.BlockSpec((s, LANES), lambda b, h: (b, h)),
        out_shape=jax.ShapeDtypeStruct((n_batch * s, d_model), bf16),
        compiler_params=_cparams(2),
        name="natten_latent",
    )(qkv, qkv, qkv, qkv, qkv, bias)


def _pool_kernel(h_ref, hp_ref, hn_ref, x_ref, w_ref, ls_ref, m_ref, nw_ref, o_ref, f_ref, pad_ref, *,
                 bps_lat, bps_ctx, n_lat_blocks, s, c, gate_row, shift_row, scale_row):
    i = pl.program_id(0)
    tm = h_ref.shape[0]
    is_lat = i < n_lat_blocks
    blk = jnp.where(is_lat, i % bps_lat, (i - n_lat_blocks) % bps_ctx)
    nblk = jnp.where(is_lat, bps_lat, bps_ctx)
    length = jnp.where(is_lat, s, c)
    pad_ref[0:POOL_HALO, :] = jnp.where(blk == 0, 0.0, hp_ref[...])
    pad_ref[POOL_HALO:POOL_HALO + tm, :] = h_ref[...]
    pad_ref[POOL_HALO + tm:2 * POOL_HALO + tm, :] = jnp.where(blk == nblk - 1, 0.0, hn_ref[...])
    pos = blk * tm + lax.broadcasted_iota(jnp.int32, (tm, 1), 0)
    pg = w_ref.shape[1]
    for g, win in enumerate(POOL_WINDOWS):
        cs = slice(g * pg, (g + 1) * pg)
        lo_off, hi_off = win // 2, win - win // 2
        acc = pad_ref[POOL_HALO - lo_off:POOL_HALO - lo_off + tm, cs]
        for k in range(-lo_off + 1, hi_off):
            acc = acc + pad_ref[POOL_HALO + k:POOL_HALO + k + tm, cs]
        cnt = jnp.minimum(pos + hi_off, length) - jnp.maximum(pos - lo_off, 0)
        y = acc * (1.0 / cnt.astype(f32)) - h_ref[:, cs]
        z = jnp.dot(y.astype(bf16), w_ref[g], preferred_element_type=f32) * ls_ref[:, cs]
        o_ref[:, cs] = x_ref[:, cs] + m_ref[gate_row:gate_row + 1, cs] * z
    f_ref[...] = _norm_mod_value(o_ref[...], nw_ref[...], m_ref, shift_row, scale_row).astype(f_ref.dtype)


def _pool(h, x, pool_w, pool_scale, mods, nw, gate_row, shift_row, scale_row, n_batch, s, c):
    t, d = h.shape
    tm = POOL_ROW_BLOCK
    nb = t // tm
    bps_lat, bps_ctx = s // tm, c // tm
    hb = tm // POOL_HALO
    last_halo = t // POOL_HALO - 1
    return pl.pallas_call(
        functools.partial(_pool_kernel, bps_lat=bps_lat, bps_ctx=bps_ctx, n_lat_blocks=n_batch * bps_lat,
                          s=s, c=c, gate_row=gate_row, shift_row=shift_row, scale_row=scale_row),
        grid=(nb,),
        in_specs=[pl.BlockSpec((tm, d), lambda i: (i, 0)),
                  pl.BlockSpec((POOL_HALO, d), lambda i: (jnp.maximum(i * hb - 1, 0), 0)),
                  pl.BlockSpec((POOL_HALO, d), lambda i: (jnp.minimum((i + 1) * hb, last_halo), 0)),
                  pl.BlockSpec((tm, d), lambda i: (i, 0)),
                  pl.BlockSpec(pool_w.shape, lambda i: (0, 0, 0)),
                  pl.BlockSpec((1, d), lambda i: (0, 0)),
                  pl.BlockSpec((None, SUBLANES, d), lambda i: (_group_of(i, bps_lat, n_batch), 0, 0)),
                  pl.BlockSpec((1, d), lambda i: (0, 0))],
        out_specs=[pl.BlockSpec((tm, d), lambda i: (i, 0)), pl.BlockSpec((tm, d), lambda i: (i, 0))],
        out_shape=[jax.ShapeDtypeStruct((t, d), f32), jax.ShapeDtypeStruct((t, d), bf16)],
        scratch_shapes=[pltpu.VMEM((tm + 2 * POOL_HALO, d), f32)],
        compiler_params=_cparams(1),
        name="pool_mixer",
    )(h, h, h, x, pool_w, pool_scale.reshape(1, d), mods, nw.reshape(1, d))


def _topk_rows(chains, tb):
    rows = lax.broadcasted_iota(jnp.int32, (PEER_TOPK, tb), 0)

    def body(k, carry):
        out = []
        for (s_ref, order), (vals, idxs) in zip(chains, carry):
            s = s_ref[...]
            m = jnp.max(s, axis=0, keepdims=True)
            am = jnp.min(jnp.where(s == m, order, float(2 ** 23)), axis=0, keepdims=True)
            s_ref[...] = jnp.where(order == am, NEG, s)
            out.append((jnp.where(rows == k, m, vals), jnp.where(rows == k, am, idxs)))
        return tuple(out)

    zero = jnp.zeros((PEER_TOPK, tb), f32)
    res = lax.fori_loop(0, PEER_TOPK, body, tuple((zero, zero) for _ in chains))
    return [(vals, idxs.astype(jnp.int32)) for vals, idxs in res]


def _cand_blocks():
    blocks = [(0, PEER_TOPK)] + [(a, SUBLANES) for a in range(1, SUBLANES)]
    assert all((a + 1) * (nb + 1) > PEER_TOPK for a, nb in blocks[1:]) and 2 * SUBLANES == PEER_TOPK
    return blocks


def _cast_chunks(n_rows, steps):
    n = 1 << (steps.bit_length() - 1)
    assert n_rows % n == 0
    return n, n_rows // n


def _peer_topk_kernel(q_ref, sk_ref, w_ref, e_ref, g_ref, wb_ref, s1_ref, s2_ref, cand_ref, es_ref, gs_ref, *,
                      n_heads, nkeys):
    tb = q_ref.shape[0]
    wb_ref[...] = w_ref[...].astype(bf16)
    key_order = lax.broadcasted_iota(jnp.int32, (nkeys, tb), 0).astype(f32)
    blocks = _cand_blocks()
    n_mid = (len(blocks) - 1) * SUBLANES
    r = lax.broadcasted_iota(jnp.int32, cand_ref.shape, 0)
    rm = r - PEER_TOPK
    mid = (lax.shift_right_logical(rm, 3) + 1) * PEER_TOPK + lax.bitwise_and(rm, SUBLANES - 1)
    tail = (rm - n_mid + SUBLANES) * PEER_TOPK
    cand_order = jnp.where(r < PEER_TOPK, r, jnp.where(rm < n_mid, mid, tail)).astype(f32)

    def scores(h):
        col = pl.multiple_of(h * 2 * LANES, 2 * LANES)
        s1_ref[...] = lax.dot_general(sk_ref[0], q_ref[:, pl.ds(col, LANES)], _NT, preferred_element_type=f32)
        s2_ref[...] = lax.dot_general(sk_ref[1], q_ref[:, pl.ds(col + LANES, LANES)], _NT, preferred_element_type=f32)

    sub_chains = [(s1_ref, key_order), (s2_ref, key_order)]
    scores(0)
    (v1, i1), (v2, i2) = _topk_rows(sub_chains, tb)

    def head(h, carry):
        v1, i1, v2, i2 = carry
        r0 = 0
        for a, nb in blocks:
            cand_ref[r0:r0 + nb, :] = v1[a:a + 1, :] + v2[:nb]
            r0 += nb
        cand_ref[r0:r0 + SUBLANES, :] = v1[SUBLANES:] + v2[0:1, :]
        scores(jnp.minimum(h + 1, n_heads - 1))
        nxt1, nxt2, (sc, ci) = _topk_rows(sub_chains + [(cand_ref, cand_order)], tb)
        ca = lax.shift_right_logical(ci, PEER_TOPK.bit_length() - 1)
        cb = lax.bitwise_and(ci, PEER_TOPK - 1)
        e1 = jnp.zeros((PEER_TOPK, tb), jnp.int32)
        e2 = jnp.zeros((PEER_TOPK, tb), jnp.int32)
        for a in range(PEER_TOPK):
            e1 = jnp.where(ca == a, i1[a:a + 1, :], e1)
            e2 = jnp.where(cb == a, i2[a:a + 1, :], e2)
        p = jnp.exp(sc - sc[0:1, :])
        gate = p / jnp.sum(p, axis=0, keepdims=True)
        row = pl.multiple_of(h * PEER_TOPK, PEER_TOPK)
        es_ref[pl.ds(row, PEER_TOPK), :] = e1 * nkeys + e2
        gs_ref[pl.ds(row, PEER_TOPK), :] = gate
        return nxt1 + nxt2

    lax.fori_loop(0, n_heads, head, (v1, i1, v2, i2))
    e_ref[...] = es_ref[...].T
    g_ref[...] = gs_ref[...].T


def _peer_topk(q, sub_keys, n_blocks_rows, w_all, layer):
    t, qd = q.shape
    nkeys = sub_keys.shape[1]
    n_heads = qd // (2 * LANES)
    slots = n_heads * PEER_TOPK
    tb = TOPK_TOKENS
    steps = n_blocks_rows * ROW_BLOCK // tb
    ne, d = w_all.shape[1:]
    n_chunks, chunk = _cast_chunks(ne, steps)
    return pl.pallas_call(
        functools.partial(_peer_topk_kernel, n_heads=n_heads, nkeys=nkeys),
        grid=(steps,),
        in_specs=[pl.BlockSpec((tb, qd), lambda i: (i, 0)),
                  pl.BlockSpec(sub_keys.shape, lambda i: (0, 0, 0)),
                  pl.BlockSpec((None, chunk, d), lambda i: (layer, jnp.minimum(i, n_chunks - 1), 0))],
        out_specs=[pl.BlockSpec((tb, slots), lambda i: (i, 0)),
                   pl.BlockSpec((tb, slots), lambda i: (i, 0)),
                   pl.BlockSpec((chunk, d), lambda i: (jnp.minimum(i, n_chunks - 1), 0))],
        out_shape=[jax.ShapeDtypeStruct((n_blocks_rows * ROW_BLOCK, slots), jnp.int32),
                   jax.ShapeDtypeStruct((n_blocks_rows * ROW_BLOCK, slots), f32),
                   jax.ShapeDtypeStruct((ne, d), bf16)],
        scratch_shapes=[pltpu.VMEM((nkeys, tb), f32), pltpu.VMEM((nkeys, tb), f32),
                        pltpu.VMEM((sum(nb for _, nb in _cand_blocks()) + SUBLANES, tb), f32),
                        pltpu.VMEM((slots, tb), jnp.int32), pltpu.VMEM((slots, tb), f32)],
        compiler_params=_cparams(1),
        name="peer_topk",
    )(q, sub_keys, w_all)


def _peer_gates_kernel(e_ref, g_ref, w_ref, o_ref, wb_ref, *, nkeys):
    tb, slots = e_ref.shape
    wb_ref[...] = w_ref[...].astype(bf16)
    iota = lax.broadcasted_iota(jnp.int32, (nkeys, slots), 0)

    def body(t, carry):
        e = e_ref[pl.ds(t, 1), :]
        g = g_ref[pl.ds(t, 1), :]
        e1 = lax.shift_right_logical(e, nkeys.bit_length() - 1)
        e2 = lax.bitwise_and(e, nkeys - 1)
        w1 = jnp.where(e1 == iota, g, 0.0).astype(bf16)
        o2 = jnp.where(e2 == iota, 1.0, 0.0).astype(bf16)
        res = lax.dot_general(w1, o2, _NT, preferred_element_type=f32)
        bits = lax.bitcast_convert_type(res.astype(bf16).astype(f32), jnp.uint32)
        packed = jnp.bitwise_or(jnp.right_shift(bits[:nkeys // 2], jnp.uint32(16)), bits[nkeys // 2:])
        o_ref[:, t] = packed.reshape(nkeys // (2 * SUBLANES), SUBLANES, nkeys)
        return carry

    lax.fori_loop(0, tb, body, 0, unroll=GATES_UNROLL)


def _peer_gates(e, g, nkeys, n_blocks_rows, w_all, layer):
    t, slots = e.shape
    tb = 128
    nblk = nkeys // (2 * SUBLANES)
    steps = n_blocks_rows * ROW_BLOCK // tb
    ne, d = w_all.shape[1:]
    n_chunks, chunk = _cast_chunks(ne, steps)
    return pl.pallas_call(
        functools.partial(_peer_gates_kernel, nkeys=nkeys),
        grid=(steps,),
        in_specs=[pl.BlockSpec((tb, slots), lambda i: (i, 0)),
                  pl.BlockSpec((tb, slots), lambda i: (i, 0)),
                  pl.BlockSpec((None, chunk, d), lambda i: (layer, jnp.minimum(i, n_chunks - 1), 0))],
        out_specs=[pl.BlockSpec((nblk, tb, SUBLANES, nkeys), lambda i: (0, i, 0, 0)),
                   pl.BlockSpec((chunk, d), lambda i: (jnp.minimum(i, n_chunks - 1), 0))],
        out_shape=[jax.ShapeDtypeStruct((nblk, t, SUBLANES, nkeys), jnp.uint32),
                   jax.ShapeDtypeStruct((ne, d), bf16)],
        compiler_params=_cparams(1),
        name="peer_gates",
    )(e, g, w_all)


def _gelu_tanh(x):
    k0 = -2.0 * 0.7978845608028654 * 1.4426950408889634
    return x / (1.0 + jnp.exp2(x * (k0 + (k0 * 0.044715) * (x * x))))


def _peer_dense_kernel(f_ref, u_ref, v_ref, g_ref, x_ref, m_ref, o_ref, acc_ref, a_ref, *, gate_row, nkeys):
    j = pl.program_id(1)

    @pl.when(j == 0)
    def _():
        acc_ref[...] = jnp.zeros_like(acc_ref)

    slot = j % 2
    a_ref[slot] = lax.dot_general(f_ref[...], u_ref[...], _NT, preferred_element_type=f32)
    tb = f_ref.shape[0]
    n_first = g_ref.shape[0] // tb
    shift = jnp.where(j < pl.num_programs(1) // 2, 16, 0).astype(jnp.uint32)
    parts = []
    for k in range(n_first):
        word = g_ref[pl.ds(k, tb, stride=n_first), :]
        gbits = jnp.bitwise_and(jnp.left_shift(word, shift), jnp.uint32(0xFFFF0000))
        gk = lax.bitcast_convert_type(gbits, f32)
        parts.append((_gelu_tanh(a_ref[slot, :, k * nkeys:(k + 1) * nkeys]) * gk).astype(bf16))
    ga = jnp.concatenate(parts, axis=1)
    acc_ref[...] += jnp.dot(ga, v_ref[...], preferred_element_type=f32)

    @pl.when(j == pl.num_programs(1) - 1)
    def _():
        o_ref[...] = x_ref[...] + m_ref[gate_row:gate_row + 1, :] * acc_ref[...]


def _peer_dense(f, u, v, gates, x, mods, gate_row, n_blocks, bps, n_batch):
    t, d = f.shape
    ne = v.shape[0]
    nkeys = gates.shape[3]
    te = SUBLANES * nkeys
    n_gblk = gates.shape[0]
    assert ne // te == 2 * n_gblk
    gates = gates.reshape(n_gblk, gates.shape[1] * SUBLANES, nkeys)
    return pl.pallas_call(
        functools.partial(_peer_dense_kernel, gate_row=gate_row, nkeys=nkeys),
        grid=(n_blocks, ne // te),
        in_specs=[pl.BlockSpec((ROW_BLOCK, d), lambda i, j: (i, 0)),
                  pl.BlockSpec((te, d), lambda i, j: (j, 0)),
                  pl.BlockSpec((te, d), lambda i, j: (j, 0)),
                  pl.BlockSpec((None, ROW_BLOCK * SUBLANES, nkeys), lambda i, j: (j % n_gblk, i, 0)),
                  pl.BlockSpec((ROW_BLOCK, d), lambda i, j: (i, 0)),
                  pl.BlockSpec((None, SUBLANES, d), lambda i, j: (_group_of(i, bps, n_batch), 0, 0))],
        out_specs=pl.BlockSpec((ROW_BLOCK, d), lambda i, j: (i, 0)),
        out_shape=jax.ShapeDtypeStruct((t, d), f32),
        scratch_shapes=[pltpu.VMEM((ROW_BLOCK, d), f32), pltpu.VMEM((2, ROW_BLOCK, te), f32)],
        compiler_params=_cparams(2),
        name="peer_dense",
    )(f, u, v, gates, x, mods)


def kernel(x, c, ctx, c_ctx, mod_w, mod_b, norm_w, final_norm_w, a_wqkv, a_q_gain, a_k_gain, a_wo,
           b_wqkv, b_rpb, b_wo, pool_w, pool_scale, peer_wq, peer_sub_keys, peer_u, peer_v):
    n_batch, s, d = x.shape
    c_len = ctx.shape[1]
    depth = mod_w.shape[0]
    assert s % ROW_BLOCK == 0 and (n_batch * c_len) % ROW_BLOCK == 0 and n_batch * c_len <= s
    assert s % c_len == 0 and c_len % POOL_ROW_BLOCK == 0 and s % GRID_W == 0
    assert a_q_gain.shape[1] == LANES and peer_sub_keys.shape[2] == LANES and peer_sub_keys.shape[3] == LANES
    assert PEER_TOPK & (PEER_TOPK - 1) == 0 and peer_sub_keys.shape[2] & (peer_sub_keys.shape[2] - 1) == 0
    assert depth % N_MIXERS != 0

    bps = s // ROW_BLOCK
    nb_lat = n_batch * bps
    nb_all = nb_lat + n_batch * c_len // ROW_BLOCK
    t_lat = n_batch * s
    q_dim = a_wo.shape[1]
    kv_dim = (a_wqkv.shape[2] - q_dim) // 2
    n_kv = kv_dim // LANES
    groups = q_dim // kv_dim
    b_heads = b_rpb.shape[1]
    win_r, win_c = (b_rpb.shape[2] + 1) // 2, (b_rpb.shape[3] + 1) // 2
    nkeys = peer_sub_keys.shape[2]
    scale = SM_SCALE_LOG2

    xs = jnp.concatenate([x.reshape(t_lat, d), ctx.reshape(n_batch * c_len, d)], axis=0)
    cvec = jnp.concatenate([c, c_ctx[None, :], jnp.zeros((SUBLANES - n_batch - 1, d), f32)], axis=0)
    mods_all = _mods(cvec, mod_w, mod_b)
    tables = _rope_tables(s)

    for i in range(depth):
        last = i == depth - 1
        kind, j = i % N_MIXERS, i // N_MIXERS
        nb = nb_lat if last else nb_all
        mods = mods_all[i, :n_batch + 1].reshape(n_batch + 1, N_MOD, d)
        mods = jnp.pad(mods, ((0, 0), (0, SUBLANES - N_MOD), (0, 0)))

        if kind == 2:
            h = _norm_mod(xs, norm_w[i, 0], mods, 0, 1, nb_all, bps, n_batch, f32)
            xs, f = _pool(h, xs, pool_w[j].astype(bf16), pool_scale[j], mods, norm_w[i, 1], 2, 3, 4, n_batch, s, c_len)
        else:
            h = _norm_mod(xs, norm_w[i, 0], mods, 0, 1, nb_all, bps, n_batch, bf16)
            if kind == 0:
                qkv = _qkv_gqa(h, a_wqkv[j].astype(bf16), a_q_gain[j], a_k_gain[j], tables, q_dim, kv_dim,
                               nb_all, bps, n_batch)
                o = _attn_lat(qkv, qkv, qkv, 0, q_dim, q_dim + kv_dim, groups, n_kv, n_batch, s, c_len, q_dim)
                if not last:
                    o_ctx = _attn_ctx(qkv, qkv, qkv, 0, q_dim, q_dim + kv_dim, groups, n_kv, n_batch, s, c_len, 1.0)
                    o = jnp.concatenate([o, o_ctx], axis=0)
                wo = a_wo[j]
            else:
                qkv = _mm(h, b_wqkv[j].astype(bf16), nb_all, _col_tile(3 * d), bf16)
                rows = s // GRID_W
                bias = _natten_bias(b_rpb[j], rows, min(win_r, rows), win_r, win_c)
                o = _natten_lat(qkv, bias, b_heads, n_batch, s, c_len, d)
                if not last:
                    o_ctx = _attn_ctx(qkv, qkv, qkv, 0, d, 2 * d, 1, b_heads, n_batch, s, c_len, scale)
                    o = jnp.concatenate([o, o_ctx], axis=0)
                wo = b_wo[j]
            xs, f = _mm_res(o, wo.astype(bf16), xs, mods, norm_w[i, 1], 2, 3, 4, nb, bps, n_batch)

        qp = _mm(f, peer_wq[i].astype(bf16), nb, _col_tile(peer_wq.shape[2]), f32)
        e, g, u_bf = _peer_topk(qp, peer_sub_keys[i], nb, peer_u, i)
        gates, v_bf = _peer_gates(e, g, nkeys, nb, peer_v, i)
        xs = _peer_dense(f, u_bf, v_bf, gates, xs, mods, 5, nb, bps, n_batch)

    return _final_norm(xs, final_norm_w, nb_lat).reshape(n_batch, s, d)
```

```python
import functools

import numpy as np
import jax
import jax.numpy as jnp
from jax import lax
from jax.experimental import pallas as pl
from jax.experimental.pallas import tpu as pltpu

GRID_W = 64
ROPE_BASE = 10000.0
POOL_WINDOWS = (2, 4, 8, 16)
PEER_TOPK = 16
EPS = 1e-6
N_MIXERS = 3
N_MOD = 6

LANES = 128
SUBLANES = 8
ROW_BLOCK = 512
POOL_ROW_BLOCK = 256
POOL_HALO = 8
MM_COL_TILE = 2048
ATTN_Q_ROWS = 128
ATTN_Q_SUBBLOCKS = 2
NATTEN_UNROLL = 8
TOPK_TOKENS = 512
GATES_UNROLL = 32
VMEM_LIMIT = 56 * 1024 * 1024
NEG = -1e30
LOG2E = 1.4426950408889634
SM_SCALE_LOG2 = float(LANES) ** -0.5 * LOG2E

f32 = jnp.float32
bf16 = jnp.bfloat16
_NT = (((1,), (1,)), ((), ()))


def _cparams(n_axes):
    return pltpu.CompilerParams(dimension_semantics=("arbitrary",) * n_axes, vmem_limit_bytes=VMEM_LIMIT)


def _group_of(i, blocks_per_seq, n_batch):
    return jnp.minimum(i // blocks_per_seq, n_batch)


def _mods_kernel(cv_ref, w_ref, b_ref, o_ref):
    cv = cv_ref[...]
    a = (cv / (1.0 + jnp.exp(-cv))).astype(bf16)
    o_ref[...] = jnp.dot(a, w_ref[...].astype(bf16), preferred_element_type=f32) + b_ref[...]


def _mods(cvec, mod_w, mod_b):
    depth, d, n = mod_w.shape
    tn = 1024
    return pl.pallas_call(
        _mods_kernel,
        grid=(depth, n // tn),
        in_specs=[pl.BlockSpec((SUBLANES, d), lambda l, j: (0, 0)),
                  pl.BlockSpec((None, d, tn), lambda l, j: (l, 0, j)),
                  pl.BlockSpec((None, 1, tn), lambda l, j: (l, 0, j))],
        out_specs=pl.BlockSpec((None, SUBLANES, tn), lambda l, j: (l, 0, j)),
        out_shape=jax.ShapeDtypeStruct((depth, SUBLANES, n), f32),
        compiler_params=_cparams(2),
        name="adaln_mods",
    )(cvec, mod_w, mod_b.reshape(depth, 1, n))


def _norm_mod_value(x, nw, m_ref, shift_row, scale_row):
    y = x * lax.rsqrt(jnp.mean(x * x, axis=-1, keepdims=True) + EPS) * nw
    return y * (1.0 + m_ref[scale_row:scale_row + 1, :]) + m_ref[shift_row:shift_row + 1, :]


def _norm_mod_kernel(x_ref, nw_ref, m_ref, o_ref, *, shift_row, scale_row):
    o_ref[...] = _norm_mod_value(x_ref[...], nw_ref[...], m_ref, shift_row, scale_row).astype(o_ref.dtype)


def _norm_mod(x, nw, mods, shift_row, scale_row, n_blocks, bps, n_batch, out_dtype):
    t, d = x.shape
    return pl.pallas_call(
        functools.partial(_norm_mod_kernel, shift_row=shift_row, scale_row=scale_row),
        grid=(n_blocks,),
        in_specs=[pl.BlockSpec((ROW_BLOCK, d), lambda i: (i, 0)),
                  pl.BlockSpec((1, d), lambda i: (0, 0)),
                  pl.BlockSpec((None, SUBLANES, d), lambda i: (_group_of(i, bps, n_batch), 0, 0))],
        out_specs=pl.BlockSpec((ROW_BLOCK, d), lambda i: (i, 0)),
        out_shape=jax.ShapeDtypeStruct((n_blocks * ROW_BLOCK, d), out_dtype),
        compiler_params=_cparams(1),
        name="norm_mod",
    )(x, nw.reshape(1, d), mods)


def _mm_kernel(a_ref, w_ref, o_ref):
    o_ref[...] = jnp.dot(a_ref[...], w_ref[...], preferred_element_type=f32).astype(o_ref.dtype)


def _col_tile(n):
    return max(t for t in range(LANES, min(n, MM_COL_TILE) + 1, LANES) if n % t == 0)


def _mm(a, w, n_blocks, tn, out_dtype):
    t, k = a.shape
    n = w.shape[1]
    return pl.pallas_call(
        _mm_kernel,
        grid=(n // tn, n_blocks),
        in_specs=[pl.BlockSpec((ROW_BLOCK, k), lambda j, i: (i, 0)),
                  pl.BlockSpec((k, tn), lambda j, i: (0, j))],
        out_specs=pl.BlockSpec((ROW_BLOCK, tn), lambda j, i: (i, j)),
        out_shape=jax.ShapeDtypeStruct((n_blocks * ROW_BLOCK, n), out_dtype),
        compiler_params=_cparams(2),
        name="matmul",
    )(a, w)


def _mm_res_kernel(a_ref, w_ref, x_ref, m_ref, nw_ref, o_ref, f_ref, *, gate_row, shift_row, scale_row):
    y = x_ref[...] + m_ref[gate_row:gate_row + 1, :] * jnp.dot(a_ref[...], w_ref[...], preferred_element_type=f32)
    o_ref[...] = y
    f_ref[...] = _norm_mod_value(y, nw_ref[...], m_ref, shift_row, scale_row).astype(f_ref.dtype)


def _mm_res(a, w, x, mods, nw, gate_row, shift_row, scale_row, n_blocks, bps, n_batch):
    t, k = a.shape
    n = w.shape[1]
    rows = n_blocks * ROW_BLOCK
    return pl.pallas_call(
        functools.partial(_mm_res_kernel, gate_row=gate_row, shift_row=shift_row, scale_row=scale_row),
        grid=(n_blocks,),
        in_specs=[pl.BlockSpec((ROW_BLOCK, k), lambda i: (i, 0)),
                  pl.BlockSpec((k, n), lambda i: (0, 0)),
                  pl.BlockSpec((ROW_BLOCK, n), lambda i: (i, 0)),
                  pl.BlockSpec((None, SUBLANES, n), lambda i: (_group_of(i, bps, n_batch), 0, 0)),
                  pl.BlockSpec((1, n), lambda i: (0, 0))],
        out_specs=[pl.BlockSpec((ROW_BLOCK, n), lambda i: (i, 0)), pl.BlockSpec((ROW_BLOCK, n), lambda i: (i, 0))],
        out_shape=[jax.ShapeDtypeStruct((rows, n), f32), jax.ShapeDtypeStruct((rows, n), bf16)],
        compiler_params=_cparams(1),
        name="matmul_residual",
    )(a, w, x, mods, nw.reshape(1, n))


def _qkv_gqa_kernel(a_ref, w_ref, qg_ref, kg_ref, cos_ref, sa_ref, sb_ref, o_ref, *, nq_tiles, nk_tiles, scale):
    j = pl.program_id(0)
    acc = jnp.dot(a_ref[...], w_ref[...], preferred_element_type=f32)
    heads = acc.shape[1] // LANES

    def prep(gain, post):
        cos, sa, sb = cos_ref[...], sa_ref[...], sb_ref[...]
        for h in range(heads):
            y = acc[:, h * LANES:(h + 1) * LANES]
            y = y * lax.rsqrt(jnp.mean(y * y, axis=-1, keepdims=True) + EPS) * gain
            y = y * cos + pltpu.roll(y, LANES - 32, 1) * sa + pltpu.roll(y, 32, 1) * sb
            o_ref[:, h * LANES:(h + 1) * LANES] = (y * post).astype(o_ref.dtype)

    @pl.when(j < nq_tiles)
    def _():
        prep(qg_ref[...], scale)

    @pl.when(jnp.logical_and(j >= nq_tiles, j < nq_tiles + nk_tiles))
    def _():
        prep(kg_ref[...], 1.0)

    @pl.when(j >= nq_tiles + nk_tiles)
    def _():
        o_ref[...] = acc.astype(o_ref.dtype)


def _rope_tables(s):
    t = np.arange(s)
    half = LANES // 2
    inv = ROPE_BASE ** (-jnp.arange(0, half, 2, dtype=f32) / half)
    ang_r = (t // GRID_W).astype(np.float32)[:, None] * inv[None, :]
    ang_c = (t % GRID_W).astype(np.float32)[:, None] * inv[None, :]
    ang = jnp.concatenate([ang_r, ang_r, ang_c, ang_c], axis=-1)
    cos, sin = jnp.cos(ang), jnp.sin(ang)
    first = (np.arange(LANES) % half) < (half // 2)
    sa = jnp.where(first[None, :], -sin, 0.0)
    sb = jnp.where(first[None, :], 0.0, sin)
    pad1 = jnp.ones((ROW_BLOCK, LANES), f32)
    pad0 = jnp.zeros((ROW_BLOCK, LANES), f32)
    return (jnp.concatenate([cos, pad1], 0), jnp.concatenate([sa, pad0], 0), jnp.concatenate([sb, pad0], 0))


def _qkv_gqa(h, wqkv, q_gain, k_gain, tables, q_dim, kv_dim, n_blocks, bps, n_batch):
    t, k = h.shape
    n = wqkv.shape[1]
    tn = min(512, kv_dim)
    cos, sa, sb = tables

    def pos_map(j, i):
        return (jnp.where(i < bps * n_batch, i % bps, bps), 0)

    return pl.pallas_call(
        functools.partial(_qkv_gqa_kernel, nq_tiles=q_dim // tn, nk_tiles=kv_dim // tn, scale=SM_SCALE_LOG2),
        grid=(n // tn, n_blocks),
        in_specs=[pl.BlockSpec((ROW_BLOCK, k), lambda j, i: (i, 0)),
                  pl.BlockSpec((k, tn), lambda j, i: (0, j)),
                  pl.BlockSpec((1, LANES), lambda j, i: (0, 0)),
                  pl.BlockSpec((1, LANES), lambda j, i: (0, 0)),
                  pl.BlockSpec((ROW_BLOCK, LANES), pos_map),
                  pl.BlockSpec((ROW_BLOCK, LANES), pos_map),
                  pl.BlockSpec((ROW_BLOCK, LANES), pos_map)],
        out_specs=pl.BlockSpec((ROW_BLOCK, tn), lambda j, i: (i, j)),
        out_shape=jax.ShapeDtypeStruct((t, n), bf16),
        compiler_params=_cparams(2),
        name="qkv_gqa",
    )(h, wqkv, q_gain.reshape(1, LANES), k_gain.reshape(1, LANES), cos, sa, sb)


def _flash_kernel(*refs, groups, n_lat_chunks, ck, scale):
    if n_lat_chunks:
        q_ref, kc_ref, vc_ref, kl_ref, vl_ref, o_ref = refs
    else:
        q_ref, kc_ref, vc_ref, o_ref = refs
    tq = q_ref.shape[0]
    q = jnp.concatenate([q_ref[:, g * LANES:(g + 1) * LANES] for g in range(groups)], axis=0)
    rows = groups * tq

    def step(k, v, m, l, acc):
        s = lax.dot_general(q, k, _NT, preferred_element_type=f32)
        if scale != 1.0:
            s = s * scale
        m_new = jnp.maximum(m, jnp.max(s, axis=-1, keepdims=True))
        p = jnp.exp2(s - m_new)
        alpha = jnp.exp2(m - m_new)
        l = alpha * l + jnp.sum(p, axis=-1, keepdims=True)
        acc = alpha * acc + jnp.dot(p.astype(bf16), v, preferred_element_type=f32)
        return m_new, l, acc

    carry = step(kc_ref[...], vc_ref[...], jnp.full((rows, 1), NEG, f32), jnp.zeros((rows, 1), f32),
                 jnp.zeros((rows, LANES), f32))
    if n_lat_chunks:
        def body(c, carry):
            off = pl.multiple_of(c * ck, ck)
            return step(kl_ref[pl.ds(off, ck), :], vl_ref[pl.ds(off, ck), :], *carry)
        carry = lax.fori_loop(0, n_lat_chunks, body, carry)
    _, l, acc = carry
    o = acc / l
    for g in range(groups):
        o_ref[:, g * LANES:(g + 1) * LANES] = o[g * tq:(g + 1) * tq].astype(o_ref.dtype)


def _gqa_lat_kernel(q_ref, kc_ref, vc_ref, kl_ref, vl_ref, o_ref, *scratch, groups, n_sub):
    tq = q_ref.shape[0] // n_sub
    vcx_ref, vlx_ref = scratch[4 * n_sub:]

    @pl.when(pl.program_id(2) == 0)
    def _():
        for src, dst in ((vc_ref, vcx_ref), (vl_ref, vlx_ref)):
            dst[:, :LANES] = src[...]
            dst[:, LANES:] = jnp.ones((src.shape[0], LANES), bf16)

    for h in range(n_sub):
        sc_ref, sl_ref, pc_ref, pl_ref = scratch[4 * h:4 * h + 4]
        q = jnp.concatenate([q_ref[h * tq:(h + 1) * tq, g * LANES:(g + 1) * LANES] for g in range(groups)], axis=0)
        sc_ref[...] = lax.dot_general(q, kc_ref[...], _NT, preferred_element_type=f32)
        sl_ref[...] = lax.dot_general(q, kl_ref[...], _NT, preferred_element_type=f32)
        m = jnp.maximum(jnp.max(sc_ref[...], axis=-1, keepdims=True), jnp.max(sl_ref[...], axis=-1, keepdims=True))
        pc_ref[...] = jnp.exp2((sc_ref[...] - m).astype(bf16))
        pl_ref[...] = jnp.exp2((sl_ref[...] - m).astype(bf16))
    for h in range(n_sub):
        pc_ref, pl_ref = scratch[4 * h + 2:4 * h + 4]
        ox = (jnp.dot(pc_ref[...], vcx_ref[...], preferred_element_type=f32)
              + jnp.dot(pl_ref[...], vlx_ref[...], preferred_element_type=f32))
        o = ox[:, :LANES] / ox[:, LANES:]
        for g in range(groups):
            o_ref[h * tq:(h + 1) * tq, g * LANES:(g + 1) * LANES] = o[g * tq:(g + 1) * tq].astype(o_ref.dtype)


def _attn_lat(q, k, v, q_col0, k_col0, v_col0, groups, n_kv, n_batch, s, c, out_cols):
    n_sub = ATTN_Q_SUBBLOCKS
    tq = n_sub * ATTN_Q_ROWS
    nqb = s // tq
    ctx_blk0 = n_batch * s // c
    qw = groups * LANES
    return pl.pallas_call(
        functools.partial(_gqa_lat_kernel, groups=groups, n_sub=n_sub),
        grid=(n_batch, n_kv, nqb),
        in_specs=[pl.BlockSpec((tq, qw), lambda b, h, i: (b * nqb + i, q_col0 // qw + h)),
                  pl.BlockSpec((c, LANES), lambda b, h, i: (ctx_blk0 + b, k_col0 // LANES + h)),
                  pl.BlockSpec((c, LANES), lambda b, h, i: (ctx_blk0 + b, v_col0 // LANES + h)),
                  pl.BlockSpec((s, LANES), lambda b, h, i: (b, k_col0 // LANES + h)),
                  pl.BlockSpec((s, LANES), lambda b, h, i: (b, v_col0 // LANES + h))],
        out_specs=pl.BlockSpec((tq, qw), lambda b, h, i: (b * nqb + i, h)),
        out_shape=jax.ShapeDtypeStruct((n_batch * s, out_cols), bf16),
        scratch_shapes=[pltpu.VMEM((groups * ATTN_Q_ROWS, n), dt)
                        for _ in range(n_sub) for dt in (f32, bf16) for n in (c, s)]
                       + [pltpu.VMEM((c, 2 * LANES), bf16), pltpu.VMEM((s, 2 * LANES), bf16)],
        compiler_params=_cparams(3),
        name="attn_latent",
    )(q, k, v, k, v)


def _attn_ctx(q, k, v, q_col0, k_col0, v_col0, groups, n_kv, n_batch, s, c, scale):
    ctx_blk0 = n_batch * s // c
    qw = groups * LANES
    return pl.pallas_call(
        functools.partial(_flash_kernel, groups=groups, n_lat_chunks=0, ck=0, scale=scale),
        grid=(n_batch, n_kv),
        in_specs=[pl.BlockSpec((c, qw), lambda b, h: (ctx_blk0 + b, q_col0 // qw + h)),
                  pl.BlockSpec((c, LANES), lambda b, h: (ctx_blk0 + b, k_col0 // LANES + h)),
                  pl.BlockSpec((c, LANES), lambda b, h: (ctx_blk0 + b, v_col0 // LANES + h))],
        out_specs=pl.BlockSpec((c, qw), lambda b, h: (b, h)),
        out_shape=jax.ShapeDtypeStruct((n_batch * c, n_kv * qw), bf16),
        compiler_params=_cparams(2),
        name="attn_context",
    )(q, k, v)


def _natten_kernel(q_ref, k_ref, v_ref, kc_ref, vc_ref, bias_ref, o_ref, *, rows, w, wr, scale):
    kc = kc_ref[...]
    vc = vc_ref[...]

    nr = NATTEN_UNROLL

    def body(g, carry):
        base = pl.multiple_of(g * nr * w, nr * w)
        qg = q_ref[pl.ds(base, nr * w), :]
        scg = lax.dot_general(qg, kc, _NT, preferred_element_type=f32) * scale
        offs, sws = [], []
        for i in range(nr):
            r = g * nr + i
            r0 = jnp.clip(r - wr // 2, 0, rows - wr)
            off = pl.multiple_of(r0 * w, w)
            kb = k_ref[pl.ds(off, wr * w), :]
            sws.append(lax.dot_general(qg[i * w:(i + 1) * w], kb, _NT, preferred_element_type=f32) * scale
                       + bias_ref[r - r0])
            offs.append(off)
        pws, pcs, ls = [], [], []
        for i in range(nr):
            sc = scg[i * w:(i + 1) * w]
            m = jnp.maximum(jnp.max(sws[i], axis=-1, keepdims=True), jnp.max(sc, axis=-1, keepdims=True))
            pw = jnp.exp2(sws[i] - m)
            pc = jnp.exp2(sc - m)
            ls.append(jnp.sum(pw, axis=-1, keepdims=True) + jnp.sum(pc, axis=-1, keepdims=True))
            pws.append(pw.astype(bf16))
            pcs.append(pc.astype(bf16))
        ocg = jnp.dot(jnp.concatenate(pcs, axis=0), vc, preferred_element_type=f32)
        for i in range(nr):
            vb = v_ref[pl.ds(offs[i], wr * w), :]
            o = (jnp.dot(pws[i], vb, preferred_element_type=f32) + ocg[i * w:(i + 1) * w]) / ls[i]
            o_ref[pl.ds(pl.multiple_of(base + i * w, w), w), :] = o.astype(o_ref.dtype)
        return carry

    lax.fori_loop(0, rows // nr, body, 0)


def _natten_bias(rpb, rows, wr, win_r, win_c):
    w = GRID_W
    cols = np.arange(w)
    col_start = np.clip(cols - win_c // 2, 0, w - win_c)
    inside = (cols[None, :] >= col_start[:, None]) & (cols[None, :] < col_start[:, None] + win_c)
    rpbp = jnp.pad(rpb, ((0, 0), (0, 0), (w - win_c, w - win_c)))
    p = jnp.stack([rpbp[:, :, w - 1 - qc:2 * w - 1 - qc] for qc in range(w)], axis=2)
    p = jnp.where(inside[None, None], p * LOG2E, NEG)
    tab = jnp.stack([p[:, win_r - 1 - d:win_r - 1 - d + wr] for d in range(wr)], axis=1)
    return jnp.transpose(tab, (0, 1, 3, 2, 4)).reshape(rpb.shape[0], wr, w, wr * w)


def _natten_lat(qkv, bias, n_heads, n_batch, s, c, d_model):
    t = qkv.shape[0]
    rows = s // GRID_W
    wr = bias.shape[1]
    ctx_blk0 = n_batch * s // c
    hq, hk, hv = 0, d_model // LANES, 2 * d_model // LANES
    return pl.pallas_call(
        functools.partial(_natten_kernel, rows=rows, w=GRID_W, wr=wr, scale=SM_SCALE_LOG2),
        grid=(n_batch, n_heads),
        in_specs=[pl.BlockSpec((s, LANES), lambda b, h: (b, hq + h)),
                  pl.BlockSpec((s, LANES), lambda b, h: (b, hk + h)),
                  pl.BlockSpec((s, LANES), lambda b, h: (b, hv + h)),
                  pl.BlockSpec((c, LANES), lambda b, h: (ctx_blk0 + b, hk + h)),
                  pl.BlockSpec((c, LANES), lambda b, h: (ctx_blk0 + b, hv + h)),
                  pl.BlockSpec((None, wr, GRID_W, wr * GRID_W), lambda b, h: (h, 0, 0, 0))],
        out_specs=pl.BlockSpec((s, LANES), lambda b, h: (b, h)),
        out_shape=jax.ShapeDtypeStruct((n_batch * s, d_model), bf16),
        compiler_params=_cparams(2),
        name="natten_latent",
    )(qkv, qkv, qkv, qkv, qkv, bias)


def _pool_kernel(h_ref, hp_ref, hn_ref, x_ref, w_ref, ls_ref, m_ref, nw_ref, o_ref, f_ref, pad_ref, *,
                 bps_lat, bps_ctx, n_lat_blocks, s, c, gate_row, shift_row, scale_row):
    i = pl.program_id(0)
    tm = h_ref.shape[0]
    is_lat = i < n_lat_blocks
    blk = jnp.where(is_lat, i % bps_lat, (i - n_lat_blocks) % bps_ctx)
    nblk = jnp.where(is_lat, bps_lat, bps_ctx)
    length = jnp.where(is_lat, s, c)
    pad_ref[0:POOL_HALO, :] = jnp.where(blk == 0, 0.0, hp_ref[...])
    pad_ref[POOL_HALO:POOL_HALO + tm, :] = h_ref[...]
    pad_ref[POOL_HALO + tm:2 * POOL_HALO + tm, :] = jnp.where(blk == nblk - 1, 0.0, hn_ref[...])
    pos = blk * tm + lax.broadcasted_iota(jnp.int32, (tm, 1), 0)
    pg = w_ref.shape[1]
    for g, win in enumerate(POOL_WINDOWS):
        cs = slice(g * pg, (g + 1) * pg)
        lo_off, hi_off = win // 2, win - win // 2
        acc = pad_ref[POOL_HALO - lo_off:POOL_HALO - lo_off + tm, cs]
        for k in range(-lo_off + 1, hi_off):
            acc = acc + pad_ref[POOL_HALO + k:POOL_HALO + k + tm, cs]
        cnt = jnp.minimum(pos + hi_off, length) - jnp.maximum(pos - lo_off, 0)
        y = acc * (1.0 / cnt.astype(f32)) - h_ref[:, cs]
        z = jnp.dot(y.astype(bf16), w_ref[g], preferred_element_type=f32) * ls_ref[:, cs]
        o_ref[:, cs] = x_ref[:, cs] + m_ref[gate_row:gate_row + 1, cs] * z
    f_ref[...] = _norm_mod_value(o_ref[...], nw_ref[...], m_ref, shift_row, scale_row).astype(f_ref.dtype)


def _pool(h, x, pool_w, pool_scale, mods, nw, gate_row, shift_row, scale_row, n_batch, s, c):
    t, d = h.shape
    tm = POOL_ROW_BLOCK
    nb = t // tm
    bps_lat, bps_ctx = s // tm, c // tm
    hb = tm // POOL_HALO
    last_halo = t // POOL_HALO - 1
    return pl.pallas_call(
        functools.partial(_pool_kernel, bps_lat=bps_lat, bps_ctx=bps_ctx, n_lat_blocks=n_batch * bps_lat,
                          s=s, c=c, gate_row=gate_row, shift_row=shift_row, scale_row=scale_row),
        grid=(nb,),
        in_specs=[pl.BlockSpec((tm, d), lambda i: (i, 0)),
                  pl.BlockSpec((POOL_HALO, d), lambda i: (jnp.maximum(i * hb - 1, 0), 0)),
                  pl.BlockSpec((POOL_HALO, d), lambda i: (jnp.minimum((i + 1) * hb, last_halo), 0)),
                  pl.BlockSpec((tm, d), lambda i: (i, 0)),
                  pl.BlockSpec(pool_w.shape, lambda i: (0, 0, 0)),
                  pl.BlockSpec((1, d), lambda i: (0, 0)),
                  pl.BlockSpec((None, SUBLANES, d), lambda i: (_group_of(i, bps_lat, n_batch), 0, 0)),
                  pl.BlockSpec((1, d), lambda i: (0, 0))],
        out_specs=[pl.BlockSpec((tm, d), lambda i: (i, 0)), pl.BlockSpec((tm, d), lambda i: (i, 0))],
        out_shape=[jax.ShapeDtypeStruct((t, d), f32), jax.ShapeDtypeStruct((t, d), bf16)],
        scratch_shapes=[pltpu.VMEM((tm + 2 * POOL_HALO, d), f32)],
        compiler_params=_cparams(1),
        name="pool_mixer",
    )(h, h, h, x, pool_w, pool_scale.reshape(1, d), mods, nw.reshape(1, d))


def _topk_rows(chains, tb):
    rows = lax.broadcasted_iota(jnp.int32, (PEER_TOPK, tb), 0)

    def body(k, carry):
        out = []
        for (s_ref, order), (vals, idxs) in zip(chains, carry):
            s = s_ref[...]
            m = jnp.max(s, axis=0, keepdims=True)
            am = jnp.min(jnp.where(s == m, order, float(2 ** 23)), axis=0, keepdims=True)
            s_ref[...] = jnp.where(order == am, NEG, s)
            out.append((jnp.where(rows == k, m, vals), jnp.where(rows == k, am, idxs)))
        return tuple(out)

    zero = jnp.zeros((PEER_TOPK, tb), f32)
    res = lax.fori_loop(0, PEER_TOPK, body, tuple((zero, zero) for _ in chains))
    return [(vals, idxs.astype(jnp.int32)) for vals, idxs in res]


def _cand_blocks():
    blocks = [(0, PEER_TOPK)] + [(a, SUBLANES) for a in range(1, SUBLANES)]
    assert all((a + 1) * (nb + 1) > PEER_TOPK for a, nb in blocks[1:]) and 2 * SUBLANES == PEER_TOPK
    return blocks


def _cast_chunks(n_rows, steps):
    n = 1 << (steps.bit_length() - 1)
    assert n_rows % n == 0
    return n, n_rows // n


def _peer_topk_kernel(q_ref, sk_ref, w_ref, e_ref, g_ref, wb_ref, s1_ref, s2_ref, cand_ref, es_ref, gs_ref, *,
                      n_heads, nkeys):
    tb = q_ref.shape[0]
    wb_ref[...] = w_ref[...].astype(bf16)
    key_order = lax.broadcasted_iota(jnp.int32, (nkeys, tb), 0).astype(f32)
    blocks = _cand_blocks()
    n_mid = (len(blocks) - 1) * SUBLANES
    r = lax.broadcasted_iota(jnp.int32, cand_ref.shape, 0)
    rm = r - PEER_TOPK
    mid = (lax.shift_right_logical(rm, SUBLANES.bit_length() - 1) + 1) * PEER_TOPK + lax.bitwise_and(rm, SUBLANES - 1)
    tail = (rm - n_mid + SUBLANES) * PEER_TOPK
    cand_order = jnp.where(r < PEER_TOPK, r, jnp.where(rm < n_mid, mid, tail)).astype(f32)

    def scores(h):
        col = pl.multiple_of(h * 2 * LANES, 2 * LANES)
        s1_ref[...] = lax.dot_general(sk_ref[0], q_ref[:, pl.ds(col, LANES)], _NT, preferred_element_type=f32)
        s2_ref[...] = lax.dot_general(sk_ref[1], q_ref[:, pl.ds(col + LANES, LANES)], _NT, preferred_element_type=f32)

    sub_chains = [(s1_ref, key_order), (s2_ref, key_order)]
    scores(0)
    (v1, i1), (v2, i2) = _topk_rows(sub_chains, tb)

    def head(h, carry):
        v1, i1, v2, i2 = carry
        r0 = 0
        for a, nb in blocks:
            cand_ref[r0:r0 + nb, :] = v1[a:a + 1, :] + v2[:nb]
            r0 += nb
        cand_ref[r0:r0 + SUBLANES, :] = v1[SUBLANES:] + v2[0:1, :]
        scores(jnp.minimum(h + 1, n_heads - 1))
        nxt1, nxt2, (sc, ci) = _topk_rows(sub_chains + [(cand_ref, cand_order)], tb)
        ca = lax.shift_right_logical(ci, PEER_TOPK.bit_length() - 1)
        cb = lax.bitwise_and(ci, PEER_TOPK - 1)
        e1 = jnp.zeros((PEER_TOPK, tb), jnp.int32)
        e2 = jnp.zeros((PEER_TOPK, tb), jnp.int32)
        for a in range(PEER_TOPK):
            e1 = jnp.where(ca == a, i1[a:a + 1, :], e1)
            e2 = jnp.where(cb == a, i2[a:a + 1, :], e2)
        p = jnp.exp(sc - sc[0:1, :])
        gate = p / jnp.sum(p, axis=0, keepdims=True)
        row = pl.multiple_of(h * PEER_TOPK, PEER_TOPK)
        es_ref[pl.ds(row, PEER_TOPK), :] = e1 * nkeys + e2
        gs_ref[pl.ds(row, PEER_TOPK), :] = gate
        return nxt1 + nxt2

    lax.fori_loop(0, n_heads, head, (v1, i1, v2, i2))
    e_ref[...] = es_ref[...].T
    g_ref[...] = gs_ref[...].T


def _peer_topk(q, sub_keys, n_blocks_rows, w_all, layer):
    t, qd = q.shape
    nkeys = sub_keys.shape[1]
    n_heads = qd // (2 * LANES)
    slots = n_heads * PEER_TOPK
    tb = TOPK_TOKENS
    steps = n_blocks_rows * ROW_BLOCK // tb
    ne, d = w_all.shape[1:]
    n_chunks, chunk = _cast_chunks(ne, steps)
    return pl.pallas_call(
        functools.partial(_peer_topk_kernel, n_heads=n_heads, nkeys=nkeys),
        grid=(steps,),
        in_specs=[pl.BlockSpec((tb, qd), lambda i: (i, 0)),
                  pl.BlockSpec(sub_keys.shape, lambda i: (0, 0, 0)),
                  pl.BlockSpec((None, chunk, d), lambda i: (layer, jnp.minimum(i, n_chunks - 1), 0))],
        out_specs=[pl.BlockSpec((tb, slots), lambda i: (i, 0)),
                   pl.BlockSpec((tb, slots), lambda i: (i, 0)),
                   pl.BlockSpec((chunk, d), lambda i: (jnp.minimum(i, n_chunks - 1), 0))],
        out_shape=[jax.ShapeDtypeStruct((n_blocks_rows * ROW_BLOCK, slots), jnp.int32),
                   jax.ShapeDtypeStruct((n_blocks_rows * ROW_BLOCK, slots), f32),
                   jax.ShapeDtypeStruct((ne, d), bf16)],
        scratch_shapes=[pltpu.VMEM((nkeys, tb), f32), pltpu.VMEM((nkeys, tb), f32),
                        pltpu.VMEM((sum(nb for _, nb in _cand_blocks()) + SUBLANES, tb), f32),
                        pltpu.VMEM((slots, tb), jnp.int32), pltpu.VMEM((slots, tb), f32)],
        compiler_params=_cparams(1),
        name="peer_topk",
    )(q, sub_keys, w_all)


def _peer_gates_kernel(e_ref, g_ref, w_ref, o_ref, wb_ref, *, nkeys):
    tb, slots = e_ref.shape
    wb_ref[...] = w_ref[...].astype(bf16)
    iota = lax.broadcasted_iota(jnp.int32, (nkeys, slots), 0)

    def body(t, carry):
        e = e_ref[pl.ds(t, 1), :]
        g = g_ref[pl.ds(t, 1), :]
        e1 = lax.shift_right_logical(e, nkeys.bit_length() - 1)
        e2 = lax.bitwise_and(e, nkeys - 1)
        w1 = jnp.where(e1 == iota, g, 0.0).astype(bf16)
        o2 = jnp.where(e2 == iota, 1.0, 0.0).astype(bf16)
        res = lax.dot_general(w1, o2, _NT, preferred_element_type=f32)
        bits = lax.bitcast_convert_type(res.astype(bf16).astype(f32), jnp.uint32)
        packed = jnp.bitwise_or(jnp.right_shift(bits[:nkeys // 2], jnp.uint32(16)), bits[nkeys // 2:])
        o_ref[:, t] = packed.reshape(nkeys // (2 * SUBLANES), SUBLANES, nkeys)
        return carry

    lax.fori_loop(0, tb, body, 0, unroll=GATES_UNROLL)


def _peer_gates(e, g, nkeys, n_blocks_rows, w_all, layer):
    t, slots = e.shape
    tb = 128
    nblk = nkeys // (2 * SUBLANES)
    steps = n_blocks_rows * ROW_BLOCK // tb
    ne, d = w_all.shape[1:]
    n_chunks, chunk = _cast_chunks(ne, steps)
    return pl.pallas_call(
        functools.partial(_peer_gates_kernel, nkeys=nkeys),
        grid=(steps,),
        in_specs=[pl.BlockSpec((tb, slots), lambda i: (i, 0)),
                  pl.BlockSpec((tb, slots), lambda i: (i, 0)),
                  pl.BlockSpec((None, chunk, d), lambda i: (layer, jnp.minimum(i, n_chunks - 1), 0))],
        out_specs=[pl.BlockSpec((nblk, tb, SUBLANES, nkeys), lambda i: (0, i, 0, 0)),
                   pl.BlockSpec((chunk, d), lambda i: (jnp.minimum(i, n_chunks - 1), 0))],
        out_shape=[jax.ShapeDtypeStruct((nblk, t, SUBLANES, nkeys), jnp.uint32),
                   jax.ShapeDtypeStruct((ne, d), bf16)],
        compiler_params=_cparams(1),
        name="peer_gates",
    )(e, g, w_all)


def _gelu_tanh(x):
    k0 = -2.0 * 0.7978845608028654 * 1.4426950408889634
    return x / (1.0 + jnp.exp2(x * (k0 + (k0 * 0.044715) * (x * x))))


def _peer_dense_kernel(f_ref, u_ref, v_ref, g_ref, x_ref, m_ref, fw_ref, o_ref, acc_ref, a_ref, *,
                       gate_row, nkeys, final_norm):
    j = pl.program_id(1)

    @pl.when(j == 0)
    def _():
        acc_ref[...] = jnp.zeros_like(acc_ref)

    slot = j % 2
    a_ref[slot] = lax.dot_general(f_ref[...], u_ref[...], _NT, preferred_element_type=f32)
    tb = f_ref.shape[0]
    n_first = g_ref.shape[0] // tb
    shift = jnp.where(j < pl.num_programs(1) // 2, 16, 0).astype(jnp.uint32)
    parts = []
    for k in range(n_first):
        word = g_ref[pl.ds(k, tb, stride=n_first), :]
        gbits = jnp.bitwise_and(jnp.left_shift(word, shift), jnp.uint32(0xFFFF0000))
        gk = lax.bitcast_convert_type(gbits, f32)
        parts.append((_gelu_tanh(a_ref[slot, :, k * nkeys:(k + 1) * nkeys]) * gk).astype(bf16))
    ga = jnp.concatenate(parts, axis=1)
    acc_ref[...] += jnp.dot(ga, v_ref[...], preferred_element_type=f32)

    @pl.when(j == pl.num_programs(1) - 1)
    def _():
        y = x_ref[...] + m_ref[gate_row:gate_row + 1, :] * acc_ref[...]
        if final_norm:
            y = y * lax.rsqrt(jnp.mean(y * y, axis=-1, keepdims=True) + EPS) * fw_ref[...]
        o_ref[...] = y


def _peer_dense(f, u, v, gates, x, mods, gate_row, n_blocks, bps, n_batch, final_w=None):
    t, d = f.shape
    ne = v.shape[0]
    nkeys = gates.shape[3]
    te = SUBLANES * nkeys
    n_gblk = gates.shape[0]
    assert ne // te == 2 * n_gblk
    gates = gates.reshape(n_gblk, gates.shape[1] * SUBLANES, nkeys)
    return pl.pallas_call(
        functools.partial(_peer_dense_kernel, gate_row=gate_row, nkeys=nkeys, final_norm=final_w is not None),
        grid=(n_blocks, ne // te),
        in_specs=[pl.BlockSpec((ROW_BLOCK, d), lambda i, j: (i, 0)),
                  pl.BlockSpec((te, d), lambda i, j: (j, 0)),
                  pl.BlockSpec((te, d), lambda i, j: (j, 0)),
                  pl.BlockSpec((None, ROW_BLOCK * SUBLANES, nkeys), lambda i, j: (j % n_gblk, i, 0)),
                  pl.BlockSpec((ROW_BLOCK, d), lambda i, j: (i, 0)),
                  pl.BlockSpec((None, SUBLANES, d), lambda i, j: (_group_of(i, bps, n_batch), 0, 0)),
                  pl.BlockSpec((1, d), lambda i, j: (0, 0))],
        out_specs=pl.BlockSpec((ROW_BLOCK, d), lambda i, j: (i, 0)),
        out_shape=jax.ShapeDtypeStruct((t, d), f32),
        scratch_shapes=[pltpu.VMEM((ROW_BLOCK, d), f32), pltpu.VMEM((2, ROW_BLOCK, te), f32)],
        compiler_params=_cparams(2),
        name="peer_dense",
    )(f, u, v, gates, x, mods, (jnp.ones((d,), f32) if final_w is None else final_w).reshape(1, d))


def kernel(x, c, ctx, c_ctx, mod_w, mod_b, norm_w, final_norm_w, a_wqkv, a_q_gain, a_k_gain, a_wo,
           b_wqkv, b_rpb, b_wo, pool_w, pool_scale, peer_wq, peer_sub_keys, peer_u, peer_v):
    n_batch, s, d = x.shape
    c_len = ctx.shape[1]
    depth = mod_w.shape[0]
    assert s % ROW_BLOCK == 0 and (n_batch * c_len) % ROW_BLOCK == 0 and n_batch * c_len <= s
    assert s % c_len == 0 and c_len % POOL_ROW_BLOCK == 0 and s % GRID_W == 0
    assert a_q_gain.shape[1] == LANES and peer_sub_keys.shape[2] == LANES and peer_sub_keys.shape[3] == LANES
    assert PEER_TOPK & (PEER_TOPK - 1) == 0 and peer_sub_keys.shape[2] & (peer_sub_keys.shape[2] - 1) == 0
    assert depth % N_MIXERS != 0

    bps = s // ROW_BLOCK
    nb_lat = n_batch * bps
    nb_all = nb_lat + n_batch * c_len // ROW_BLOCK
    t_lat = n_batch * s
    q_dim = a_wo.shape[1]
    kv_dim = (a_wqkv.shape[2] - q_dim) // 2
    n_kv = kv_dim // LANES
    groups = q_dim // kv_dim
    b_heads = b_rpb.shape[1]
    win_r, win_c = (b_rpb.shape[2] + 1) // 2, (b_rpb.shape[3] + 1) // 2
    nkeys = peer_sub_keys.shape[2]
    scale = SM_SCALE_LOG2

    xs = jnp.concatenate([x.reshape(t_lat, d), ctx.reshape(n_batch * c_len, d)], axis=0)
    cvec = jnp.concatenate([c, c_ctx[None, :], jnp.zeros((SUBLANES - n_batch - 1, d), f32)], axis=0)
    mods_all = _mods(cvec, mod_w, mod_b)
    tables = _rope_tables(s)

    for i in range(depth):
        last = i == depth - 1
        kind, j = i % N_MIXERS, i // N_MIXERS
        nb = nb_lat if last else nb_all
        mods = mods_all[i, :n_batch + 1].reshape(n_batch + 1, N_MOD, d)
        mods = jnp.pad(mods, ((0, 0), (0, SUBLANES - N_MOD), (0, 0)))

        if kind == 2:
            h = _norm_mod(xs, norm_w[i, 0], mods, 0, 1, nb_all, bps, n_batch, f32)
            xs, f = _pool(h, xs, pool_w[j].astype(bf16), pool_scale[j], mods, norm_w[i, 1], 2, 3, 4, n_batch, s, c_len)
        else:
            h = _norm_mod(xs, norm_w[i, 0], mods, 0, 1, nb_all, bps, n_batch, bf16)
            if kind == 0:
                qkv = _qkv_gqa(h, a_wqkv[j].astype(bf16), a_q_gain[j], a_k_gain[j], tables, q_dim, kv_dim,
                               nb_all, bps, n_batch)
                o = _attn_lat(qkv, qkv, qkv, 0, q_dim, q_dim + kv_dim, groups, n_kv, n_batch, s, c_len, q_dim)
                if not last:
                    o_ctx = _attn_ctx(qkv, qkv, qkv, 0, q_dim, q_dim + kv_dim, groups, n_kv, n_batch, s, c_len, 1.0)
                    o = jnp.concatenate([o, o_ctx], axis=0)
                wo = a_wo[j]
            else:
                qkv = _mm(h, b_wqkv[j].astype(bf16), nb_all, _col_tile(3 * d), bf16)
                rows = s // GRID_W
                bias = _natten_bias(b_rpb[j], rows, min(win_r, rows), win_r, win_c)
                o = _natten_lat(qkv, bias, b_heads, n_batch, s, c_len, d)
                if not last:
                    o_ctx = _attn_ctx(qkv, qkv, qkv, 0, d, 2 * d, 1, b_heads, n_batch, s, c_len, scale)
                    o = jnp.concatenate([o, o_ctx], axis=0)
                wo = b_wo[j]
            xs, f = _mm_res(o, wo.astype(bf16), xs, mods, norm_w[i, 1], 2, 3, 4, nb, bps, n_batch)

        qp = _mm(f, peer_wq[i].astype(bf16), nb, _col_tile(peer_wq.shape[2]), f32)
        e, g, u_bf = _peer_topk(qp, peer_sub_keys[i], nb, peer_u, i)
        gates, v_bf = _peer_gates(e, g, nkeys, nb, peer_v, i)
        xs = _peer_dense(f, u_bf, v_bf, gates, xs, mods, 5, nb, bps, n_batch, final_norm_w if last else None)

    return xs.reshape(n_batch, s, d)
```

```python
import functools

import numpy as np
import jax
import jax.numpy as jnp
from jax import lax
from jax.experimental import pallas as pl
from jax.experimental.pallas import tpu as pltpu

GRID_W = 64
ROPE_BASE = 10000.0
POOL_WINDOWS = (2, 4, 8, 16)
PEER_TOPK = 16
EPS = 1e-6
N_MIXERS = 3
N_MOD = 6

LANES = 128
SUBLANES = 8
ROW_BLOCK = 512
POOL_ROW_BLOCK = 256
POOL_HALO = 8
MM_COL_TILE = 2048
ATTN_Q_ROWS = 128
ATTN_Q_SUBBLOCKS = 2
NATTEN_UNROLL = 16
TOPK_TOKENS = 512
GATES_TOKENS = 256
GATES_UNROLL = 32
VMEM_LIMIT = 56 * 1024 * 1024
NEG = -1e30
LOG2E = 1.4426950408889634
SM_SCALE_LOG2 = float(LANES) ** -0.5 * LOG2E

f32 = jnp.float32
bf16 = jnp.bfloat16
_NT = (((1,), (1,)), ((), ()))


def _cparams(n_axes):
    return pltpu.CompilerParams(dimension_semantics=("arbitrary",) * n_axes, vmem_limit_bytes=VMEM_LIMIT)


def _group_of(i, blocks_per_seq, n_batch):
    return jnp.minimum(i // blocks_per_seq, n_batch)


def _mods_kernel(cv_ref, w_ref, b_ref, o_ref):
    cv = cv_ref[...]
    a = (cv / (1.0 + jnp.exp(-cv))).astype(bf16)
    o_ref[...] = jnp.dot(a, w_ref[...].astype(bf16), preferred_element_type=f32) + b_ref[...]


def _mods(cvec, mod_w, mod_b):
    depth, d, n = mod_w.shape
    tn = 1024
    return pl.pallas_call(
        _mods_kernel,
        grid=(depth, n // tn),
        in_specs=[pl.BlockSpec((SUBLANES, d), lambda l, j: (0, 0)),
                  pl.BlockSpec((None, d, tn), lambda l, j: (l, 0, j)),
                  pl.BlockSpec((None, 1, tn), lambda l, j: (l, 0, j))],
        out_specs=pl.BlockSpec((None, SUBLANES, tn), lambda l, j: (l, 0, j)),
        out_shape=jax.ShapeDtypeStruct((depth, SUBLANES, n), f32),
        compiler_params=_cparams(2),
        name="adaln_mods",
    )(cvec, mod_w, mod_b.reshape(depth, 1, n))


def _norm_mod_value(x, nw, m_ref, shift_row, scale_row):
    y = x * lax.rsqrt(jnp.mean(x * x, axis=-1, keepdims=True) + EPS) * nw
    return y * (1.0 + m_ref[scale_row:scale_row + 1, :]) + m_ref[shift_row:shift_row + 1, :]


def _norm_mod_kernel(x_ref, nw_ref, m_ref, o_ref, *, shift_row, scale_row):
    o_ref[...] = _norm_mod_value(x_ref[...], nw_ref[...], m_ref, shift_row, scale_row).astype(o_ref.dtype)


def _norm_mod(x, nw, mods, shift_row, scale_row, n_blocks, bps, n_batch, out_dtype):
    t, d = x.shape
    return pl.pallas_call(
        functools.partial(_norm_mod_kernel, shift_row=shift_row, scale_row=scale_row),
        grid=(n_blocks,),
        in_specs=[pl.BlockSpec((ROW_BLOCK, d), lambda i: (i, 0)),
                  pl.BlockSpec((1, d), lambda i: (0, 0)),
                  pl.BlockSpec((None, SUBLANES, d), lambda i: (_group_of(i, bps, n_batch), 0, 0))],
        out_specs=pl.BlockSpec((ROW_BLOCK, d), lambda i: (i, 0)),
        out_shape=jax.ShapeDtypeStruct((n_blocks * ROW_BLOCK, d), out_dtype),
        compiler_params=_cparams(1),
        name="norm_mod",
    )(x, nw.reshape(1, d), mods)


def _mm_kernel(a_ref, w_ref, o_ref):
    o_ref[...] = jnp.dot(a_ref[...], w_ref[...], preferred_element_type=f32).astype(o_ref.dtype)


def _col_tile(n):
    return max(t for t in range(LANES, min(n, MM_COL_TILE) + 1, LANES) if n % t == 0)


def _mm(a, w, n_blocks, tn, out_dtype):
    t, k = a.shape
    n = w.shape[1]
    return pl.pallas_call(
        _mm_kernel,
        grid=(n // tn, n_blocks),
        in_specs=[pl.BlockSpec((ROW_BLOCK, k), lambda j, i: (i, 0)),
                  pl.BlockSpec((k, tn), lambda j, i: (0, j))],
        out_specs=pl.BlockSpec((ROW_BLOCK, tn), lambda j, i: (i, j)),
        out_shape=jax.ShapeDtypeStruct((n_blocks * ROW_BLOCK, n), out_dtype),
        compiler_params=_cparams(2),
        name="matmul",
    )(a, w)


def _mm_res_kernel(al_ref, ac_ref, w_ref, x_ref, m_ref, nw_ref, o_ref, f_ref, *,
                   gate_row, shift_row, scale_row, n_lat_blocks):
    a = jnp.where(pl.program_id(0) < n_lat_blocks, al_ref[...], ac_ref[...])
    y = x_ref[...] + m_ref[gate_row:gate_row + 1, :] * jnp.dot(a, w_ref[...], preferred_element_type=f32)
    o_ref[...] = y
    f_ref[...] = _norm_mod_value(y, nw_ref[...], m_ref, shift_row, scale_row).astype(f_ref.dtype)


def _mm_res(a_lat, a_ctx, w, x, mods, nw, gate_row, shift_row, scale_row, n_blocks, bps, n_batch):
    k = a_lat.shape[1]
    n = w.shape[1]
    rows = n_blocks * ROW_BLOCK
    n_lat_blocks = a_lat.shape[0] // ROW_BLOCK
    assert n_blocks == n_lat_blocks or n_blocks == n_lat_blocks + a_ctx.shape[0] // ROW_BLOCK
    return pl.pallas_call(
        functools.partial(_mm_res_kernel, gate_row=gate_row, shift_row=shift_row, scale_row=scale_row,
                          n_lat_blocks=n_lat_blocks),
        grid=(n_blocks,),
        in_specs=[pl.BlockSpec((ROW_BLOCK, k), lambda i: (jnp.minimum(i, n_lat_blocks - 1), 0)),
                  pl.BlockSpec((ROW_BLOCK, k), lambda i: (jnp.maximum(i - n_lat_blocks, 0), 0)),
                  pl.BlockSpec((k, n), lambda i: (0, 0)),
                  pl.BlockSpec((ROW_BLOCK, n), lambda i: (i, 0)),
                  pl.BlockSpec((None, SUBLANES, n), lambda i: (_group_of(i, bps, n_batch), 0, 0)),
                  pl.BlockSpec((1, n), lambda i: (0, 0))],
        out_specs=[pl.BlockSpec((ROW_BLOCK, n), lambda i: (i, 0)), pl.BlockSpec((ROW_BLOCK, n), lambda i: (i, 0))],
        out_shape=[jax.ShapeDtypeStruct((rows, n), f32), jax.ShapeDtypeStruct((rows, n), bf16)],
        compiler_params=_cparams(1),
        name="matmul_residual",
    )(a_lat, a_ctx, w, x, mods, nw.reshape(1, n))


def _qkv_gqa_kernel(a_ref, w_ref, qg_ref, kg_ref, cos_ref, sa_ref, sb_ref, o_ref, *, nq_tiles, nk_tiles, scale):
    j = pl.program_id(0)
    acc = jnp.dot(a_ref[...], w_ref[...], preferred_element_type=f32)
    heads = acc.shape[1] // LANES

    def prep(gain, post):
        cos, sa, sb = cos_ref[...], sa_ref[...], sb_ref[...]
        for h in range(heads):
            y = acc[:, h * LANES:(h + 1) * LANES]
            y = y * lax.rsqrt(jnp.mean(y * y, axis=-1, keepdims=True) + EPS) * gain
            y = y * cos + pltpu.roll(y, LANES - 32, 1) * sa + pltpu.roll(y, 32, 1) * sb
            o_ref[:, h * LANES:(h + 1) * LANES] = (y * post).astype(o_ref.dtype)

    @pl.when(j < nq_tiles)
    def _():
        prep(qg_ref[...], scale)

    @pl.when(jnp.logical_and(j >= nq_tiles, j < nq_tiles + nk_tiles))
    def _():
        prep(kg_ref[...], 1.0)

    @pl.when(j >= nq_tiles + nk_tiles)
    def _():
        o_ref[...] = acc.astype(o_ref.dtype)


def _rope_tables(s):
    t = np.arange(s)
    half = LANES // 2
    inv = ROPE_BASE ** (-jnp.arange(0, half, 2, dtype=f32) / half)
    ang_r = (t // GRID_W).astype(np.float32)[:, None] * inv[None, :]
    ang_c = (t % GRID_W).astype(np.float32)[:, None] * inv[None, :]
    ang = jnp.concatenate([ang_r, ang_r, ang_c, ang_c], axis=-1)
    cos, sin = jnp.cos(ang), jnp.sin(ang)
    first = (np.arange(LANES) % half) < (half // 2)
    sa = jnp.where(first[None, :], -sin, 0.0)
    sb = jnp.where(first[None, :], 0.0, sin)
    pad1 = jnp.ones((ROW_BLOCK, LANES), f32)
    pad0 = jnp.zeros((ROW_BLOCK, LANES), f32)
    return (jnp.concatenate([cos, pad1], 0), jnp.concatenate([sa, pad0], 0), jnp.concatenate([sb, pad0], 0))


def _qkv_gqa(h, wqkv, q_gain, k_gain, tables, q_dim, kv_dim, n_blocks, bps, n_batch):
    t, k = h.shape
    n = wqkv.shape[1]
    tn = min(512, kv_dim)
    cos, sa, sb = tables

    def pos_map(j, i):
        return (jnp.where(i < bps * n_batch, i % bps, bps), 0)

    return pl.pallas_call(
        functools.partial(_qkv_gqa_kernel, nq_tiles=q_dim // tn, nk_tiles=kv_dim // tn, scale=SM_SCALE_LOG2),
        grid=(n // tn, n_blocks),
        in_specs=[pl.BlockSpec((ROW_BLOCK, k), lambda j, i: (i, 0)),
                  pl.BlockSpec((k, tn), lambda j, i: (0, j)),
                  pl.BlockSpec((1, LANES), lambda j, i: (0, 0)),
                  pl.BlockSpec((1, LANES), lambda j, i: (0, 0)),
                  pl.BlockSpec((ROW_BLOCK, LANES), pos_map),
                  pl.BlockSpec((ROW_BLOCK, LANES), pos_map),
                  pl.BlockSpec((ROW_BLOCK, LANES), pos_map)],
        out_specs=pl.BlockSpec((ROW_BLOCK, tn), lambda j, i: (i, j)),
        out_shape=jax.ShapeDtypeStruct((t, n), bf16),
        compiler_params=_cparams(2),
        name="qkv_gqa",
    )(h, wqkv, q_gain.reshape(1, LANES), k_gain.reshape(1, LANES), cos, sa, sb)


def _flash_kernel(*refs, groups, n_lat_chunks, ck, scale):
    if n_lat_chunks:
        q_ref, kc_ref, vc_ref, kl_ref, vl_ref, o_ref = refs
    else:
        q_ref, kc_ref, vc_ref, o_ref = refs
    tq = q_ref.shape[0]
    q = jnp.concatenate([q_ref[:, g * LANES:(g + 1) * LANES] for g in range(groups)], axis=0)
    rows = groups * tq

    def step(k, v, m, l, acc):
        s = lax.dot_general(q, k, _NT, preferred_element_type=f32)
        if scale != 1.0:
            s = s * scale
        m_new = jnp.maximum(m, jnp.max(s, axis=-1, keepdims=True))
        p = jnp.exp2(s - m_new)
        alpha = jnp.exp2(m - m_new)
        l = alpha * l + jnp.sum(p, axis=-1, keepdims=True)
        acc = alpha * acc + jnp.dot(p.astype(bf16), v, preferred_element_type=f32)
        return m_new, l, acc

    carry = step(kc_ref[...], vc_ref[...], jnp.full((rows, 1), NEG, f32), jnp.zeros((rows, 1), f32),
                 jnp.zeros((rows, LANES), f32))
    if n_lat_chunks:
        def body(c, carry):
            off = pl.multiple_of(c * ck, ck)
            return step(kl_ref[pl.ds(off, ck), :], vl_ref[pl.ds(off, ck), :], *carry)
        carry = lax.fori_loop(0, n_lat_chunks, body, carry)
    _, l, acc = carry
    o = acc / l
    for g in range(groups):
        o_ref[:, g * LANES:(g + 1) * LANES] = o[g * tq:(g + 1) * tq].astype(o_ref.dtype)


def _gqa_lat_kernel(q_ref, kc_ref, vc_ref, kl_ref, vl_ref, o_ref, *scratch, groups, n_sub):
    tq = q_ref.shape[0] // n_sub
    vcx_ref, vlx_ref = scratch[4 * n_sub:]

    @pl.when(pl.program_id(2) == 0)
    def _():
        for src, dst in ((vc_ref, vcx_ref), (vl_ref, vlx_ref)):
            dst[:, :LANES] = src[...]
            dst[:, LANES:] = jnp.ones((src.shape[0], LANES), bf16)

    for h in range(n_sub):
        sc_ref, sl_ref, pc_ref, pl_ref = scratch[4 * h:4 * h + 4]
        q = jnp.concatenate([q_ref[h * tq:(h + 1) * tq, g * LANES:(g + 1) * LANES] for g in range(groups)], axis=0)
        sc_ref[...] = lax.dot_general(q, kc_ref[...], _NT, preferred_element_type=f32)
        sl_ref[...] = lax.dot_general(q, kl_ref[...], _NT, preferred_element_type=f32)
        m = jnp.maximum(jnp.max(sc_ref[...], axis=-1, keepdims=True), jnp.max(sl_ref[...], axis=-1, keepdims=True))
        pc_ref[...] = jnp.exp2((sc_ref[...] - m).astype(bf16))
        pl_ref[...] = jnp.exp2((sl_ref[...] - m).astype(bf16))
    for h in range(n_sub):
        pc_ref, pl_ref = scratch[4 * h + 2:4 * h + 4]
        ox = (jnp.dot(pc_ref[...], vcx_ref[...], preferred_element_type=f32)
              + jnp.dot(pl_ref[...], vlx_ref[...], preferred_element_type=f32))
        o = ox[:, :LANES] / ox[:, LANES:]
        for g in range(groups):
            o_ref[h * tq:(h + 1) * tq, g * LANES:(g + 1) * LANES] = o[g * tq:(g + 1) * tq].astype(o_ref.dtype)


def _attn_lat(q, k, v, q_col0, k_col0, v_col0, groups, n_kv, n_batch, s, c, out_cols):
    n_sub = ATTN_Q_SUBBLOCKS
    tq = n_sub * ATTN_Q_ROWS
    nqb = s // tq
    ctx_blk0 = n_batch * s // c
    qw = groups * LANES
    return pl.pallas_call(
        functools.partial(_gqa_lat_kernel, groups=groups, n_sub=n_sub),
        grid=(n_batch, n_kv, nqb),
        in_specs=[pl.BlockSpec((tq, qw), lambda b, h, i: (b * nqb + i, q_col0 // qw + h)),
                  pl.BlockSpec((c, LANES), lambda b, h, i: (ctx_blk0 + b, k_col0 // LANES + h)),
                  pl.BlockSpec((c, LANES), lambda b, h, i: (ctx_blk0 + b, v_col0 // LANES + h)),
                  pl.BlockSpec((s, LANES), lambda b, h, i: (b, k_col0 // LANES + h)),
                  pl.BlockSpec((s, LANES), lambda b, h, i: (b, v_col0 // LANES + h))],
        out_specs=pl.BlockSpec((tq, qw), lambda b, h, i: (b * nqb + i, h)),
        out_shape=jax.ShapeDtypeStruct((n_batch * s, out_cols), bf16),
        scratch_shapes=[pltpu.VMEM((groups * ATTN_Q_ROWS, n), dt)
                        for _ in range(n_sub) for dt in (f32, bf16) for n in (c, s)]
                       + [pltpu.VMEM((c, 2 * LANES), bf16), pltpu.VMEM((s, 2 * LANES), bf16)],
        compiler_params=_cparams(3),
        name="attn_latent",
    )(q, k, v, k, v)


def _attn_ctx(q, k, v, q_col0, k_col0, v_col0, groups, n_kv, n_batch, s, c, scale):
    ctx_blk0 = n_batch * s // c
    qw = groups * LANES
    return pl.pallas_call(
        functools.partial(_flash_kernel, groups=groups, n_lat_chunks=0, ck=0, scale=scale),
        grid=(n_batch, n_kv),
        in_specs=[pl.BlockSpec((c, qw), lambda b, h: (ctx_blk0 + b, q_col0 // qw + h)),
                  pl.BlockSpec((c, LANES), lambda b, h: (ctx_blk0 + b, k_col0 // LANES + h)),
                  pl.BlockSpec((c, LANES), lambda b, h: (ctx_blk0 + b, v_col0 // LANES + h))],
        out_specs=pl.BlockSpec((c, qw), lambda b, h: (b, h)),
        out_shape=jax.ShapeDtypeStruct((n_batch * c, n_kv * qw), bf16),
        compiler_params=_cparams(2),
        name="attn_context",
    )(q, k, v)


def _natten_kernel(q_ref, k_ref, v_ref, kc_ref, vc_ref, bias_ref, o_ref, *, rows, w, wr, scale):
    kc = kc_ref[...]
    vc = vc_ref[...]

    nr = NATTEN_UNROLL

    def body(g, carry):
        base = pl.multiple_of(g * nr * w, nr * w)
        qg = q_ref[pl.ds(base, nr * w), :]
        scg = lax.dot_general(qg, kc, _NT, preferred_element_type=f32) * scale
        offs, sws = [], []
        for i in range(nr):
            r = g * nr + i
            r0 = jnp.clip(r - wr // 2, 0, rows - wr)
            off = pl.multiple_of(r0 * w, w)
            kb = k_ref[pl.ds(off, wr * w), :]
            sws.append(lax.dot_general(qg[i * w:(i + 1) * w], kb, _NT, preferred_element_type=f32) * scale
                       + bias_ref[r - r0])
            offs.append(off)
        pws, pcs, ls = [], [], []
        for i in range(nr):
            sc = scg[i * w:(i + 1) * w]
            m = jnp.maximum(jnp.max(sws[i], axis=-1, keepdims=True), jnp.max(sc, axis=-1, keepdims=True))
            pw = jnp.exp2(sws[i] - m)
            pc = jnp.exp2(sc - m)
            ls.append(jnp.sum(pw, axis=-1, keepdims=True) + jnp.sum(pc, axis=-1, keepdims=True))
            pws.append(pw.astype(bf16))
            pcs.append(pc.astype(bf16))
        ocg = jnp.dot(jnp.concatenate(pcs, axis=0), vc, preferred_element_type=f32)
        for i in range(nr):
            vb = v_ref[pl.ds(offs[i], wr * w), :]
            o = (jnp.dot(pws[i], vb, preferred_element_type=f32) + ocg[i * w:(i + 1) * w]) / ls[i]
            o_ref[pl.ds(pl.multiple_of(base + i * w, w), w), :] = o.astype(o_ref.dtype)
        return carry

    lax.fori_loop(0, rows // nr, body, 0)


def _natten_bias(rpb, rows, wr, win_r, win_c):
    w = GRID_W
    cols = np.arange(w)
    col_start = np.clip(cols - win_c // 2, 0, w - win_c)
    inside = (cols[None, :] >= col_start[:, None]) & (cols[None, :] < col_start[:, None] + win_c)
    rpbp = jnp.pad(rpb, ((0, 0), (0, 0), (w - win_c, w - win_c)))
    p = jnp.stack([rpbp[:, :, w - 1 - qc:2 * w - 1 - qc] for qc in range(w)], axis=2)
    p = jnp.where(inside[None, None], p * LOG2E, NEG)
    tab = jnp.stack([p[:, win_r - 1 - d:win_r - 1 - d + wr] for d in range(wr)], axis=1)
    return jnp.transpose(tab, (0, 1, 3, 2, 4)).reshape(rpb.shape[0], wr, w, wr * w)


def _natten_lat(qkv, bias, n_heads, n_batch, s, c, d_model):
    t = qkv.shape[0]
    rows = s // GRID_W
    wr = bias.shape[1]
    ctx_blk0 = n_batch * s // c
    hq, hk, hv = 0, d_model // LANES, 2 * d_model // LANES
    return pl.pallas_call(
        functools.partial(_natten_kernel, rows=rows, w=GRID_W, wr=wr, scale=SM_SCALE_LOG2),
        grid=(n_batch, n_heads),
        in_specs=[pl.BlockSpec((s, LANES), lambda b, h: (b, hq + h)),
                  pl.BlockSpec((s, LANES), lambda b, h: (b, hk + h)),
                  pl.BlockSpec((s, LANES), lambda b, h: (b, hv + h)),
                  pl.BlockSpec((c, LANES), lambda b, h: (ctx_blk0 + b, hk + h)),
                  pl.BlockSpec((c, LANES), lambda b, h: (ctx_blk0 + b, hv + h)),
                  pl.BlockSpec((None, wr, GRID_W, wr * GRID_W), lambda b, h: (h, 0, 0, 0))],
        out_specs=pl.BlockSpec((s, LANES), lambda b, h: (b, h)),
        out_shape=jax.ShapeDtypeStruct((n_batch * s, d_model), bf16),
        compiler_params=_cparams(2),
        name="natten_latent",
    )(qkv, qkv, qkv, qkv, qkv, bias)


def _pool_kernel(h_ref, hp_ref, hn_ref, x_ref, w_ref, ls_ref, m_ref, nw_ref, o_ref, f_ref, pad_ref, *,
                 bps_lat, bps_ctx, n_lat_blocks, s, c, gate_row, shift_row, scale_row):
    i = pl.program_id(0)
    tm = h_ref.shape[0]
    is_lat = i < n_lat_blocks
    blk = jnp.where(is_lat, i % bps_lat, (i - n_lat_blocks) % bps_ctx)
    nblk = jnp.where(is_lat, bps_lat, bps_ctx)
    length = jnp.where(is_lat, s, c)
    pad_ref[0:POOL_HALO, :] = jnp.where(blk == 0, 0.0, hp_ref[...])
    pad_ref[POOL_HALO:POOL_HALO + tm, :] = h_ref[...]
    pad_ref[POOL_HALO + tm:2 * POOL_HALO + tm, :] = jnp.where(blk == nblk - 1, 0.0, hn_ref[...])
    pos = blk * tm + lax.broadcasted_iota(jnp.int32, (tm, 1), 0)
    pg = w_ref.shape[1]
    for g, win in enumerate(POOL_WINDOWS):
        cs = slice(g * pg, (g + 1) * pg)
        lo_off, hi_off = win // 2, win - win // 2
        acc = pad_ref[POOL_HALO - lo_off:POOL_HALO - lo_off + tm, cs]
        for k in range(-lo_off + 1, hi_off):
            acc = acc + pad_ref[POOL_HALO + k:POOL_HALO + k + tm, cs]
        cnt = jnp.minimum(pos + hi_off, length) - jnp.maximum(pos - lo_off, 0)
        y = acc * (1.0 / cnt.astype(f32)) - h_ref[:, cs]
        z = jnp.dot(y.astype(bf16), w_ref[g], preferred_element_type=f32) * ls_ref[:, cs]
        o_ref[:, cs] = x_ref[:, cs] + m_ref[gate_row:gate_row + 1, cs] * z
    f_ref[...] = _norm_mod_value(o_ref[...], nw_ref[...], m_ref, shift_row, scale_row).astype(f_ref.dtype)


def _pool(h, x, pool_w, pool_scale, mods, nw, gate_row, shift_row, scale_row, n_batch, s, c):
    t, d = h.shape
    tm = POOL_ROW_BLOCK
    nb = t // tm
    bps_lat, bps_ctx = s // tm, c // tm
    hb = tm // POOL_HALO
    last_halo = t // POOL_HALO - 1
    return pl.pallas_call(
        functools.partial(_pool_kernel, bps_lat=bps_lat, bps_ctx=bps_ctx, n_lat_blocks=n_batch * bps_lat,
                          s=s, c=c, gate_row=gate_row, shift_row=shift_row, scale_row=scale_row),
        grid=(nb,),
        in_specs=[pl.BlockSpec((tm, d), lambda i: (i, 0)),
                  pl.BlockSpec((POOL_HALO, d), lambda i: (jnp.maximum(i * hb - 1, 0), 0)),
                  pl.BlockSpec((POOL_HALO, d), lambda i: (jnp.minimum((i + 1) * hb, last_halo), 0)),
                  pl.BlockSpec((tm, d), lambda i: (i, 0)),
                  pl.BlockSpec(pool_w.shape, lambda i: (0, 0, 0)),
                  pl.BlockSpec((1, d), lambda i: (0, 0)),
                  pl.BlockSpec((None, SUBLANES, d), lambda i: (_group_of(i, bps_lat, n_batch), 0, 0)),
                  pl.BlockSpec((1, d), lambda i: (0, 0))],
        out_specs=[pl.BlockSpec((tm, d), lambda i: (i, 0)), pl.BlockSpec((tm, d), lambda i: (i, 0))],
        out_shape=[jax.ShapeDtypeStruct((t, d), f32), jax.ShapeDtypeStruct((t, d), bf16)],
        scratch_shapes=[pltpu.VMEM((tm + 2 * POOL_HALO, d), f32)],
        compiler_params=_cparams(1),
        name="pool_mixer",
    )(h, h, h, x, pool_w, pool_scale.reshape(1, d), mods, nw.reshape(1, d))


def _topk_rows(chains, tb):
    rows = lax.broadcasted_iota(jnp.int32, (PEER_TOPK, tb), 0)

    def body(k, carry):
        out = []
        for (s_ref, order), (vals, idxs) in zip(chains, carry):
            s = s_ref[...]
            m = jnp.max(s, axis=0, keepdims=True)
            am = jnp.min(jnp.where(s == m, order, float(2 ** 23)), axis=0, keepdims=True)
            s_ref[...] = jnp.where(order == am, NEG, s)
            out.append((jnp.where(rows == k, m, vals), jnp.where(rows == k, am, idxs)))
        return tuple(out)

    zero = jnp.zeros((PEER_TOPK, tb), f32)
    res = lax.fori_loop(0, PEER_TOPK, body, tuple((zero, zero) for _ in chains))
    return [(vals, idxs.astype(jnp.int32)) for vals, idxs in res]


def _cand_blocks():
    blocks = [(0, PEER_TOPK)] + [(a, SUBLANES) for a in range(1, SUBLANES)]
    assert all((a + 1) * (nb + 1) > PEER_TOPK for a, nb in blocks[1:]) and 2 * SUBLANES == PEER_TOPK
    return blocks


def _cast_chunks(n_rows, steps):
    n = 1 << (steps.bit_length() - 1)
    assert n_rows % n == 0
    return n, n_rows // n


def _peer_topk_kernel(q_ref, sk_ref, w_ref, e_ref, g_ref, wb_ref, s1_ref, s2_ref, cand_ref, es_ref, gs_ref, *,
                      n_heads, nkeys):
    tb = q_ref.shape[0]
    wb_ref[...] = w_ref[...].astype(bf16)
    key_order = lax.broadcasted_iota(jnp.int32, (nkeys, tb), 0).astype(f32)
    blocks = _cand_blocks()
    n_mid = (len(blocks) - 1) * SUBLANES
    r = lax.broadcasted_iota(jnp.int32, cand_ref.shape, 0)
    rm = r - PEER_TOPK
    mid = (lax.shift_right_logical(rm, SUBLANES.bit_length() - 1) + 1) * PEER_TOPK + lax.bitwise_and(rm, SUBLANES - 1)
    tail = (rm - n_mid + SUBLANES) * PEER_TOPK
    cand_order = jnp.where(r < PEER_TOPK, r, jnp.where(rm < n_mid, mid, tail)).astype(f32)

    def scores(h):
        col = pl.multiple_of(h * 2 * LANES, 2 * LANES)
        s1_ref[...] = lax.dot_general(sk_ref[0], q_ref[:, pl.ds(col, LANES)], _NT, preferred_element_type=f32)
        s2_ref[...] = lax.dot_general(sk_ref[1], q_ref[:, pl.ds(col + LANES, LANES)], _NT, preferred_element_type=f32)

    sub_chains = [(s1_ref, key_order), (s2_ref, key_order)]
    scores(0)
    (v1, i1), (v2, i2) = _topk_rows(sub_chains, tb)

    def head(h, carry):
        v1, i1, v2, i2 = carry
        r0 = 0
        for a, nb in blocks:
            cand_ref[r0:r0 + nb, :] = v1[a:a + 1, :] + v2[:nb]
            r0 += nb
        cand_ref[r0:r0 + SUBLANES, :] = v1[SUBLANES:] + v2[0:1, :]
        scores(jnp.minimum(h + 1, n_heads - 1))
        nxt1, nxt2, (sc, ci) = _topk_rows(sub_chains + [(cand_ref, cand_order)], tb)
        ca = lax.shift_right_logical(ci, PEER_TOPK.bit_length() - 1)
        cb = lax.bitwise_and(ci, PEER_TOPK - 1)
        e1 = jnp.zeros((PEER_TOPK, tb), jnp.int32)
        e2 = jnp.zeros((PEER_TOPK, tb), jnp.int32)
        for a in range(PEER_TOPK):
            e1 = jnp.where(ca == a, i1[a:a + 1, :], e1)
            e2 = jnp.where(cb == a, i2[a:a + 1, :], e2)
        p = jnp.exp(sc - sc[0:1, :])
        gate = p / jnp.sum(p, axis=0, keepdims=True)
        row = pl.multiple_of(h * PEER_TOPK, PEER_TOPK)
        es_ref[pl.ds(row, PEER_TOPK), :] = e1 * nkeys + e2
        gs_ref[pl.ds(row, PEER_TOPK), :] = gate
        return nxt1 + nxt2

    lax.fori_loop(0, n_heads, head, (v1, i1, v2, i2))
    e_ref[...] = es_ref[...].T
    g_ref[...] = gs_ref[...].T


def _peer_topk(q, sub_keys, n_blocks_rows, w_all, layer):
    t, qd = q.shape
    nkeys = sub_keys.shape[1]
    n_heads = qd // (2 * LANES)
    slots = n_heads * PEER_TOPK
    tb = TOPK_TOKENS
    steps = n_blocks_rows * ROW_BLOCK // tb
    ne, d = w_all.shape[1:]
    n_chunks, chunk = _cast_chunks(ne, steps)
    return pl.pallas_call(
        functools.partial(_peer_topk_kernel, n_heads=n_heads, nkeys=nkeys),
        grid=(steps,),
        in_specs=[pl.BlockSpec((tb, qd), lambda i: (i, 0)),
                  pl.BlockSpec(sub_keys.shape, lambda i: (0, 0, 0)),
                  pl.BlockSpec((None, chunk, d), lambda i: (layer, jnp.minimum(i, n_chunks - 1), 0))],
        out_specs=[pl.BlockSpec((tb, slots), lambda i: (i, 0)),
                   pl.BlockSpec((tb, slots), lambda i: (i, 0)),
                   pl.BlockSpec((chunk, d), lambda i: (jnp.minimum(i, n_chunks - 1), 0))],
        out_shape=[jax.ShapeDtypeStruct((n_blocks_rows * ROW_BLOCK, slots), jnp.int32),
                   jax.ShapeDtypeStruct((n_blocks_rows * ROW_BLOCK, slots), f32),
                   jax.ShapeDtypeStruct((ne, d), bf16)],
        scratch_shapes=[pltpu.VMEM((nkeys, tb), f32), pltpu.VMEM((nkeys, tb), f32),
                        pltpu.VMEM((sum(nb for _, nb in _cand_blocks()) + SUBLANES, tb), f32),
                        pltpu.VMEM((slots, tb), jnp.int32), pltpu.VMEM((slots, tb), f32)],
        compiler_params=_cparams(1),
        name="peer_topk",
    )(q, sub_keys, w_all)


def _peer_gates_kernel(e_ref, g_ref, w_ref, o_ref, wb_ref, *, nkeys):
    tb, slots = e_ref.shape
    wb_ref[...] = w_ref[...].astype(bf16)
    iota = lax.broadcasted_iota(jnp.int32, (nkeys, slots), 0)

    def body(t, carry):
        e = e_ref[pl.ds(t, 1), :]
        g = g_ref[pl.ds(t, 1), :]
        e1 = lax.shift_right_logical(e, nkeys.bit_length() - 1)
        e2 = lax.bitwise_and(e, nkeys - 1)
        w1 = jnp.where(e1 == iota, g, 0.0).astype(bf16)
        o2 = jnp.where(e2 == iota, 1.0, 0.0).astype(bf16)
        res = lax.dot_general(w1, o2, _NT, preferred_element_type=f32)
        bits = lax.bitcast_convert_type(res.astype(bf16).astype(f32), jnp.uint32)
        packed = jnp.bitwise_or(jnp.right_shift(bits[:nkeys // 2], jnp.uint32(16)), bits[nkeys // 2:])
        o_ref[:, t] = packed.reshape(nkeys // (2 * SUBLANES), SUBLANES, nkeys)
        return carry

    lax.fori_loop(0, tb, body, 0, unroll=GATES_UNROLL)


def _peer_gates(e, g, nkeys, n_blocks_rows, w_all, layer):
    t, slots = e.shape
    tb = GATES_TOKENS
    nblk = nkeys // (2 * SUBLANES)
    steps = n_blocks_rows * ROW_BLOCK // tb
    ne, d = w_all.shape[1:]
    n_chunks, chunk = _cast_chunks(ne, steps)
    return pl.pallas_call(
        functools.partial(_peer_gates_kernel, nkeys=nkeys),
        grid=(steps,),
        in_specs=[pl.BlockSpec((tb, slots), lambda i: (i, 0)),
                  pl.BlockSpec((tb, slots), lambda i: (i, 0)),
                  pl.BlockSpec((None, chunk, d), lambda i: (layer, jnp.minimum(i, n_chunks - 1), 0))],
        out_specs=[pl.BlockSpec((nblk, tb, SUBLANES, nkeys), lambda i: (0, i, 0, 0)),
                   pl.BlockSpec((chunk, d), lambda i: (jnp.minimum(i, n_chunks - 1), 0))],
        out_shape=[jax.ShapeDtypeStruct((nblk, t, SUBLANES, nkeys), jnp.uint32),
                   jax.ShapeDtypeStruct((ne, d), bf16)],
        compiler_params=_cparams(1),
        name="peer_gates",
    )(e, g, w_all)


def _gelu_tanh(x):
    k0 = -2.0 * 0.7978845608028654 * 1.4426950408889634
    return x / (1.0 + jnp.exp2(x * (k0 + (k0 * 0.044715) * (x * x))))


def _peer_dense_kernel(f_ref, u_ref, v_ref, g_ref, x_ref, m_ref, fw_ref, o_ref, acc_ref, a_ref, *,
                       gate_row, nkeys, final_norm):
    j = pl.program_id(1)

    @pl.when(j == 0)
    def _():
        acc_ref[...] = jnp.zeros_like(acc_ref)

    slot = j % 2
    a_ref[slot] = lax.dot_general(f_ref[...], u_ref[...], _NT, preferred_element_type=f32)
    tb = f_ref.shape[0]
    n_first = g_ref.shape[0] // tb
    shift = jnp.where(j < pl.num_programs(1) // 2, 16, 0).astype(jnp.uint32)
    parts = []
    for k in range(n_first):
        word = g_ref[pl.ds(k, tb, stride=n_first), :]
        gbits = jnp.bitwise_and(jnp.left_shift(word, shift), jnp.uint32(0xFFFF0000))
        gk = lax.bitcast_convert_type(gbits, f32)
        parts.append((_gelu_tanh(a_ref[slot, :, k * nkeys:(k + 1) * nkeys]) * gk).astype(bf16))
    ga = jnp.concatenate(parts, axis=1)
    acc_ref[...] += jnp.dot(ga, v_ref[...], preferred_element_type=f32)

    @pl.when(j == pl.num_programs(1) - 1)
    def _():
        y = x_ref[...] + m_ref[gate_row:gate_row + 1, :] * acc_ref[...]
        if final_norm:
            y = y * lax.rsqrt(jnp.mean(y * y, axis=-1, keepdims=True) + EPS) * fw_ref[...]
        o_ref[...] = y


def _peer_dense(f, u, v, gates, x, mods, gate_row, n_blocks, bps, n_batch, final_w=None):
    t, d = f.shape
    ne = v.shape[0]
    nkeys = gates.shape[3]
    te = SUBLANES * nkeys
    n_gblk = gates.shape[0]
    assert ne // te == 2 * n_gblk
    gates = gates.reshape(n_gblk, gates.shape[1] * SUBLANES, nkeys)
    return pl.pallas_call(
        functools.partial(_peer_dense_kernel, gate_row=gate_row, nkeys=nkeys, final_norm=final_w is not None),
        grid=(n_blocks, ne // te),
        in_specs=[pl.BlockSpec((ROW_BLOCK, d), lambda i, j: (i, 0)),
                  pl.BlockSpec((te, d), lambda i, j: (j, 0)),
                  pl.BlockSpec((te, d), lambda i, j: (j, 0)),
                  pl.BlockSpec((None, ROW_BLOCK * SUBLANES, nkeys), lambda i, j: (j % n_gblk, i, 0)),
                  pl.BlockSpec((ROW_BLOCK, d), lambda i, j: (i, 0)),
                  pl.BlockSpec((None, SUBLANES, d), lambda i, j: (_group_of(i, bps, n_batch), 0, 0)),
                  pl.BlockSpec((1, d), lambda i, j: (0, 0))],
        out_specs=pl.BlockSpec((ROW_BLOCK, d), lambda i, j: (i, 0)),
        out_shape=jax.ShapeDtypeStruct((t, d), f32),
        scratch_shapes=[pltpu.VMEM((ROW_BLOCK, d), f32), pltpu.VMEM((2, ROW_BLOCK, te), f32)],
        compiler_params=_cparams(2),
        name="peer_dense",
    )(f, u, v, gates, x, mods, (jnp.ones((d,), f32) if final_w is None else final_w).reshape(1, d))


def kernel(x, c, ctx, c_ctx, mod_w, mod_b, norm_w, final_norm_w, a_wqkv, a_q_gain, a_k_gain, a_wo,
           b_wqkv, b_rpb, b_wo, pool_w, pool_scale, peer_wq, peer_sub_keys, peer_u, peer_v):
    n_batch, s, d = x.shape
    c_len = ctx.shape[1]
    depth = mod_w.shape[0]
    assert s % ROW_BLOCK == 0 and (n_batch * c_len) % ROW_BLOCK == 0 and n_batch * c_len <= s
    assert s % c_len == 0 and c_len % POOL_ROW_BLOCK == 0 and s % GRID_W == 0
    assert a_q_gain.shape[1] == LANES and peer_sub_keys.shape[2] == LANES and peer_sub_keys.shape[3] == LANES
    assert PEER_TOPK & (PEER_TOPK - 1) == 0 and peer_sub_keys.shape[2] & (peer_sub_keys.shape[2] - 1) == 0
    assert depth % N_MIXERS != 0

    bps = s // ROW_BLOCK
    nb_lat = n_batch * bps
    nb_all = nb_lat + n_batch * c_len // ROW_BLOCK
    t_lat = n_batch * s
    q_dim = a_wo.shape[1]
    kv_dim = (a_wqkv.shape[2] - q_dim) // 2
    n_kv = kv_dim // LANES
    groups = q_dim // kv_dim
    b_heads = b_rpb.shape[1]
    win_r, win_c = (b_rpb.shape[2] + 1) // 2, (b_rpb.shape[3] + 1) // 2
    nkeys = peer_sub_keys.shape[2]
    scale = SM_SCALE_LOG2

    xs = jnp.concatenate([x.reshape(t_lat, d), ctx.reshape(n_batch * c_len, d)], axis=0)
    cvec = jnp.concatenate([c, c_ctx[None, :], jnp.zeros((SUBLANES - n_batch - 1, d), f32)], axis=0)
    mods_all = _mods(cvec, mod_w, mod_b)
    tables = _rope_tables(s)

    for i in range(depth):
        last = i == depth - 1
        kind, j = i % N_MIXERS, i // N_MIXERS
        nb = nb_lat if last else nb_all
        mods = mods_all[i, :n_batch + 1].reshape(n_batch + 1, N_MOD, d)
        mods = jnp.pad(mods, ((0, 0), (0, SUBLANES - N_MOD), (0, 0)))

        if kind == 2:
            h = _norm_mod(xs, norm_w[i, 0], mods, 0, 1, nb_all, bps, n_batch, f32)
            xs, f = _pool(h, xs, pool_w[j].astype(bf16), pool_scale[j], mods, norm_w[i, 1], 2, 3, 4, n_batch, s, c_len)
        else:
            h = _norm_mod(xs, norm_w[i, 0], mods, 0, 1, nb_all, bps, n_batch, bf16)
            if kind == 0:
                qkv = _qkv_gqa(h, a_wqkv[j].astype(bf16), a_q_gain[j], a_k_gain[j], tables, q_dim, kv_dim,
                               nb_all, bps, n_batch)
                o = _attn_lat(qkv, qkv, qkv, 0, q_dim, q_dim + kv_dim, groups, n_kv, n_batch, s, c_len, q_dim)
                if not last:
                    o_ctx = _attn_ctx(qkv, qkv, qkv, 0, q_dim, q_dim + kv_dim, groups, n_kv, n_batch, s, c_len, 1.0)
                wo = a_wo[j]
            else:
                qkv = _mm(h, b_wqkv[j].astype(bf16), nb_all, _col_tile(3 * d), bf16)
                rows = s // GRID_W
                bias = _natten_bias(b_rpb[j], rows, min(win_r, rows), win_r, win_c)
                o = _natten_lat(qkv, bias, b_heads, n_batch, s, c_len, d)
                if not last:
                    o_ctx = _attn_ctx(qkv, qkv, qkv, 0, d, 2 * d, 1, b_heads, n_batch, s, c_len, scale)
                wo = b_wo[j]
            if last:
                o_ctx = o
            xs, f = _mm_res(o, o_ctx, wo.astype(bf16), xs, mods, norm_w[i, 1], 2, 3, 4, nb, bps, n_batch)

        qp = _mm(f, peer_wq[i].astype(bf16), nb, _col_tile(peer_wq.shape[2]), f32)
        e, g, u_bf = _peer_topk(qp, peer_sub_keys[i], nb, peer_u, i)
        gates, v_bf = _peer_gates(e, g, nkeys, nb, peer_v, i)
        xs = _peer_dense(f, u_bf, v_bf, gates, xs, mods, 5, nb, bps, n_batch, final_norm_w if last else None)

    return xs.reshape(n_batch, s, d)
```

```python
import functools

import numpy as np
import jax
import jax.numpy as jnp
from jax import lax
from jax.experimental import pallas as pl
from jax.experimental.pallas import tpu as pltpu

GRID_W = 64
ROPE_BASE = 10000.0
POOL_WINDOWS = (2, 4, 8, 16)
PEER_TOPK = 16
EPS = 1e-6
N_MIXERS = 3
N_MOD = 6

LANES = 128
SUBLANES = 8
ROW_BLOCK = 512
POOL_ROW_BLOCK = 256
POOL_HALO = 8
MM_COL_TILE = 2048
ATTN_Q_ROWS = 128
ATTN_Q_SUBBLOCKS = 2
NATTEN_UNROLL = 16
TOPK_TOKENS = 512
GATES_TOKENS = 256
GATES_UNROLL = 64
VMEM_LIMIT = 56 * 1024 * 1024
NEG = -1e30
LOG2E = 1.4426950408889634
SM_SCALE_LOG2 = float(LANES) ** -0.5 * LOG2E

f32 = jnp.float32
bf16 = jnp.bfloat16
_NT = (((1,), (1,)), ((), ()))


def _cparams(n_axes):
    return pltpu.CompilerParams(dimension_semantics=("arbitrary",) * n_axes, vmem_limit_bytes=VMEM_LIMIT)


def _group_of(i, blocks_per_seq, n_batch):
    return jnp.minimum(i // blocks_per_seq, n_batch)


def _mods_kernel(cv_ref, w_ref, b_ref, o_ref):
    cv = cv_ref[...]
    a = (cv / (1.0 + jnp.exp(-cv))).astype(bf16)
    o_ref[...] = jnp.dot(a, w_ref[...].astype(bf16), preferred_element_type=f32) + b_ref[...]


def _mods(cvec, mod_w, mod_b):
    depth, d, n = mod_w.shape
    tn = 1024
    return pl.pallas_call(
        _mods_kernel,
        grid=(depth, n // tn),
        in_specs=[pl.BlockSpec((SUBLANES, d), lambda l, j: (0, 0)),
                  pl.BlockSpec((None, d, tn), lambda l, j: (l, 0, j)),
                  pl.BlockSpec((None, 1, tn), lambda l, j: (l, 0, j))],
        out_specs=pl.BlockSpec((None, SUBLANES, tn), lambda l, j: (l, 0, j)),
        out_shape=jax.ShapeDtypeStruct((depth, SUBLANES, n), f32),
        compiler_params=_cparams(2),
        name="adaln_mods",
    )(cvec, mod_w, mod_b.reshape(depth, 1, n))


def _norm_mod_value(x, nw, m_ref, shift_row, scale_row):
    y = x * lax.rsqrt(jnp.mean(x * x, axis=-1, keepdims=True) + EPS) * nw
    return y * (1.0 + m_ref[scale_row:scale_row + 1, :]) + m_ref[shift_row:shift_row + 1, :]


def _norm_mod_kernel(x_ref, nw_ref, m_ref, o_ref, *, shift_row, scale_row):
    o_ref[...] = _norm_mod_value(x_ref[...], nw_ref[...], m_ref, shift_row, scale_row).astype(o_ref.dtype)


def _norm_mod(x, nw, mods, shift_row, scale_row, n_blocks, bps, n_batch, out_dtype):
    t, d = x.shape
    return pl.pallas_call(
        functools.partial(_norm_mod_kernel, shift_row=shift_row, scale_row=scale_row),
        grid=(n_blocks,),
        in_specs=[pl.BlockSpec((ROW_BLOCK, d), lambda i: (i, 0)),
                  pl.BlockSpec((1, d), lambda i: (0, 0)),
                  pl.BlockSpec((None, SUBLANES, d), lambda i: (_group_of(i, bps, n_batch), 0, 0))],
        out_specs=pl.BlockSpec((ROW_BLOCK, d), lambda i: (i, 0)),
        out_shape=jax.ShapeDtypeStruct((n_blocks * ROW_BLOCK, d), out_dtype),
        compiler_params=_cparams(1),
        name="norm_mod",
    )(x, nw.reshape(1, d), mods)


def _mm_kernel(a_ref, w_ref, o_ref):
    o_ref[...] = jnp.dot(a_ref[...], w_ref[...], preferred_element_type=f32).astype(o_ref.dtype)


def _col_tile(n):
    return max(t for t in range(LANES, min(n, MM_COL_TILE) + 1, LANES) if n % t == 0)


def _mm(a, w, n_blocks, tn, out_dtype):
    t, k = a.shape
    n = w.shape[1]
    return pl.pallas_call(
        _mm_kernel,
        grid=(n // tn, n_blocks),
        in_specs=[pl.BlockSpec((ROW_BLOCK, k), lambda j, i: (i, 0)),
                  pl.BlockSpec((k, tn), lambda j, i: (0, j))],
        out_specs=pl.BlockSpec((ROW_BLOCK, tn), lambda j, i: (i, j)),
        out_shape=jax.ShapeDtypeStruct((n_blocks * ROW_BLOCK, n), out_dtype),
        compiler_params=_cparams(2),
        name="matmul",
    )(a, w)


def _mm_res_kernel(al_ref, ac_ref, w_ref, x_ref, m_ref, nw_ref, o_ref, f_ref, *,
                   gate_row, shift_row, scale_row, n_lat_blocks):
    a = jnp.where(pl.program_id(0) < n_lat_blocks, al_ref[...], ac_ref[...])
    y = x_ref[...] + m_ref[gate_row:gate_row + 1, :] * jnp.dot(a, w_ref[...], preferred_element_type=f32)
    o_ref[...] = y
    f_ref[...] = _norm_mod_value(y, nw_ref[...], m_ref, shift_row, scale_row).astype(f_ref.dtype)


def _mm_res(a_lat, a_ctx, w, x, mods, nw, gate_row, shift_row, scale_row, n_blocks, bps, n_batch):
    k = a_lat.shape[1]
    n = w.shape[1]
    rows = n_blocks * ROW_BLOCK
    n_lat_blocks = a_lat.shape[0] // ROW_BLOCK
    assert n_blocks == n_lat_blocks or n_blocks == n_lat_blocks + a_ctx.shape[0] // ROW_BLOCK
    return pl.pallas_call(
        functools.partial(_mm_res_kernel, gate_row=gate_row, shift_row=shift_row, scale_row=scale_row,
                          n_lat_blocks=n_lat_blocks),
        grid=(n_blocks,),
        in_specs=[pl.BlockSpec((ROW_BLOCK, k), lambda i: (jnp.minimum(i, n_lat_blocks - 1), 0)),
                  pl.BlockSpec((ROW_BLOCK, k), lambda i: (jnp.maximum(i - n_lat_blocks, 0), 0)),
                  pl.BlockSpec((k, n), lambda i: (0, 0)),
                  pl.BlockSpec((ROW_BLOCK, n), lambda i: (i, 0)),
                  pl.BlockSpec((None, SUBLANES, n), lambda i: (_group_of(i, bps, n_batch), 0, 0)),
                  pl.BlockSpec((1, n), lambda i: (0, 0))],
        out_specs=[pl.BlockSpec((ROW_BLOCK, n), lambda i: (i, 0)), pl.BlockSpec((ROW_BLOCK, n), lambda i: (i, 0))],
        out_shape=[jax.ShapeDtypeStruct((rows, n), f32), jax.ShapeDtypeStruct((rows, n), bf16)],
        compiler_params=_cparams(1),
        name="matmul_residual",
    )(a_lat, a_ctx, w, x, mods, nw.reshape(1, n))


def _qkv_gqa_kernel(a_ref, w_ref, qg_ref, kg_ref, cos_ref, sa_ref, sb_ref, o_ref, *, nq_tiles, nk_tiles, scale):
    j = pl.program_id(0)
    acc = jnp.dot(a_ref[...], w_ref[...], preferred_element_type=f32)
    heads = acc.shape[1] // LANES

    def prep(gain, post):
        cos, sa, sb = cos_ref[...], sa_ref[...], sb_ref[...]
        for h in range(heads):
            y = acc[:, h * LANES:(h + 1) * LANES]
            y = y * lax.rsqrt(jnp.mean(y * y, axis=-1, keepdims=True) + EPS) * gain
            y = y * cos + pltpu.roll(y, LANES - 32, 1) * sa + pltpu.roll(y, 32, 1) * sb
            o_ref[:, h * LANES:(h + 1) * LANES] = (y * post).astype(o_ref.dtype)

    @pl.when(j < nq_tiles)
    def _():
        prep(qg_ref[...], scale)

    @pl.when(jnp.logical_and(j >= nq_tiles, j < nq_tiles + nk_tiles))
    def _():
        prep(kg_ref[...], 1.0)

    @pl.when(j >= nq_tiles + nk_tiles)
    def _():
        o_ref[...] = acc.astype(o_ref.dtype)


def _rope_tables(s):
    t = np.arange(s)
    half = LANES // 2
    inv = ROPE_BASE ** (-jnp.arange(0, half, 2, dtype=f32) / half)
    ang_r = (t // GRID_W).astype(np.float32)[:, None] * inv[None, :]
    ang_c = (t % GRID_W).astype(np.float32)[:, None] * inv[None, :]
    ang = jnp.concatenate([ang_r, ang_r, ang_c, ang_c], axis=-1)
    cos, sin = jnp.cos(ang), jnp.sin(ang)
    first = (np.arange(LANES) % half) < (half // 2)
    sa = jnp.where(first[None, :], -sin, 0.0)
    sb = jnp.where(first[None, :], 0.0, sin)
    pad1 = jnp.ones((ROW_BLOCK, LANES), f32)
    pad0 = jnp.zeros((ROW_BLOCK, LANES), f32)
    return (jnp.concatenate([cos, pad1], 0), jnp.concatenate([sa, pad0], 0), jnp.concatenate([sb, pad0], 0))


def _qkv_gqa(h, wqkv, q_gain, k_gain, tables, q_dim, kv_dim, n_blocks, bps, n_batch):
    t, k = h.shape
    n = wqkv.shape[1]
    tn = min(512, kv_dim)
    cos, sa, sb = tables

    def pos_map(j, i):
        return (jnp.where(i < bps * n_batch, i % bps, bps), 0)

    return pl.pallas_call(
        functools.partial(_qkv_gqa_kernel, nq_tiles=q_dim // tn, nk_tiles=kv_dim // tn, scale=SM_SCALE_LOG2),
        grid=(n // tn, n_blocks),
        in_specs=[pl.BlockSpec((ROW_BLOCK, k), lambda j, i: (i, 0)),
                  pl.BlockSpec((k, tn), lambda j, i: (0, j)),
                  pl.BlockSpec((1, LANES), lambda j, i: (0, 0)),
                  pl.BlockSpec((1, LANES), lambda j, i: (0, 0)),
                  pl.BlockSpec((ROW_BLOCK, LANES), pos_map),
                  pl.BlockSpec((ROW_BLOCK, LANES), pos_map),
                  pl.BlockSpec((ROW_BLOCK, LANES), pos_map)],
        out_specs=pl.BlockSpec((ROW_BLOCK, tn), lambda j, i: (i, j)),
        out_shape=jax.ShapeDtypeStruct((t, n), bf16),
        compiler_params=_cparams(2),
        name="qkv_gqa",
    )(h, wqkv, q_gain.reshape(1, LANES), k_gain.reshape(1, LANES), cos, sa, sb)


def _flash_kernel(*refs, groups, n_lat_chunks, ck, scale):
    if n_lat_chunks:
        q_ref, kc_ref, vc_ref, kl_ref, vl_ref, o_ref = refs
    else:
        q_ref, kc_ref, vc_ref, o_ref = refs
    tq = q_ref.shape[0]
    q = jnp.concatenate([q_ref[:, g * LANES:(g + 1) * LANES] for g in range(groups)], axis=0)
    rows = groups * tq

    def step(k, v, m, l, acc):
        s = lax.dot_general(q, k, _NT, preferred_element_type=f32)
        if scale != 1.0:
            s = s * scale
        m_new = jnp.maximum(m, jnp.max(s, axis=-1, keepdims=True))
        p = jnp.exp2(s - m_new)
        alpha = jnp.exp2(m - m_new)
        l = alpha * l + jnp.sum(p, axis=-1, keepdims=True)
        acc = alpha * acc + jnp.dot(p.astype(bf16), v, preferred_element_type=f32)
        return m_new, l, acc

    carry = step(kc_ref[...], vc_ref[...], jnp.full((rows, 1), NEG, f32), jnp.zeros((rows, 1), f32),
                 jnp.zeros((rows, LANES), f32))
    if n_lat_chunks:
        def body(c, carry):
            off = pl.multiple_of(c * ck, ck)
            return step(kl_ref[pl.ds(off, ck), :], vl_ref[pl.ds(off, ck), :], *carry)
        carry = lax.fori_loop(0, n_lat_chunks, body, carry)
    _, l, acc = carry
    o = acc / l
    for g in range(groups):
        o_ref[:, g * LANES:(g + 1) * LANES] = o[g * tq:(g + 1) * tq].astype(o_ref.dtype)


def _gqa_lat_kernel(q_ref, kc_ref, vc_ref, kl_ref, vl_ref, o_ref, *scratch, groups, n_sub):
    tq = q_ref.shape[0] // n_sub
    vcx_ref, vlx_ref = scratch[4 * n_sub:]

    @pl.when(pl.program_id(2) == 0)
    def _():
        for src, dst in ((vc_ref, vcx_ref), (vl_ref, vlx_ref)):
            dst[:, :LANES] = src[...]
            dst[:, LANES:] = jnp.ones((src.shape[0], LANES), bf16)

    for h in range(n_sub):
        sc_ref, sl_ref, pc_ref, pl_ref = scratch[4 * h:4 * h + 4]
        q = jnp.concatenate([q_ref[h * tq:(h + 1) * tq, g * LANES:(g + 1) * LANES] for g in range(groups)], axis=0)
        sc_ref[...] = lax.dot_general(q, kc_ref[...], _NT, preferred_element_type=f32)
        sl_ref[...] = lax.dot_general(q, kl_ref[...], _NT, preferred_element_type=f32)
        m = jnp.maximum(jnp.max(sc_ref[...], axis=-1, keepdims=True), jnp.max(sl_ref[...], axis=-1, keepdims=True))
        pc_ref[...] = jnp.exp2((sc_ref[...] - m).astype(bf16))
        pl_ref[...] = jnp.exp2((sl_ref[...] - m).astype(bf16))
    for h in range(n_sub):
        pc_ref, pl_ref = scratch[4 * h + 2:4 * h + 4]
        ox = (jnp.dot(pc_ref[...], vcx_ref[...], preferred_element_type=f32)
              + jnp.dot(pl_ref[...], vlx_ref[...], preferred_element_type=f32))
        o = ox[:, :LANES] / ox[:, LANES:]
        for g in range(groups):
            o_ref[h * tq:(h + 1) * tq, g * LANES:(g + 1) * LANES] = o[g * tq:(g + 1) * tq].astype(o_ref.dtype)


def _attn_lat(q, k, v, q_col0, k_col0, v_col0, groups, n_kv, n_batch, s, c, out_cols):
    n_sub = ATTN_Q_SUBBLOCKS
    tq = n_sub * ATTN_Q_ROWS
    nqb = s // tq
    ctx_blk0 = n_batch * s // c
    qw = groups * LANES
    return pl.pallas_call(
        functools.partial(_gqa_lat_kernel, groups=groups, n_sub=n_sub),
        grid=(n_batch, n_kv, nqb),
        in_specs=[pl.BlockSpec((tq, qw), lambda b, h, i: (b * nqb + i, q_col0 // qw + h)),
                  pl.BlockSpec((c, LANES), lambda b, h, i: (ctx_blk0 + b, k_col0 // LANES + h)),
                  pl.BlockSpec((c, LANES), lambda b, h, i: (ctx_blk0 + b, v_col0 // LANES + h)),
                  pl.BlockSpec((s, LANES), lambda b, h, i: (b, k_col0 // LANES + h)),
                  pl.BlockSpec((s, LANES), lambda b, h, i: (b, v_col0 // LANES + h))],
        out_specs=pl.BlockSpec((tq, qw), lambda b, h, i: (b * nqb + i, h)),
        out_shape=jax.ShapeDtypeStruct((n_batch * s, out_cols), bf16),
        scratch_shapes=[pltpu.VMEM((groups * ATTN_Q_ROWS, n), dt)
                        for _ in range(n_sub) for dt in (f32, bf16) for n in (c, s)]
                       + [pltpu.VMEM((c, 2 * LANES), bf16), pltpu.VMEM((s, 2 * LANES), bf16)],
        compiler_params=_cparams(3),
        name="attn_latent",
    )(q, k, v, k, v)


def _attn_ctx(q, k, v, q_col0, k_col0, v_col0, groups, n_kv, n_batch, s, c, scale):
    ctx_blk0 = n_batch * s // c
    qw = groups * LANES
    return pl.pallas_call(
        functools.partial(_flash_kernel, groups=groups, n_lat_chunks=0, ck=0, scale=scale),
        grid=(n_batch, n_kv),
        in_specs=[pl.BlockSpec((c, qw), lambda b, h: (ctx_blk0 + b, q_col0 // qw + h)),
                  pl.BlockSpec((c, LANES), lambda b, h: (ctx_blk0 + b, k_col0 // LANES + h)),
                  pl.BlockSpec((c, LANES), lambda b, h: (ctx_blk0 + b, v_col0 // LANES + h))],
        out_specs=pl.BlockSpec((c, qw), lambda b, h: (b, h)),
        out_shape=jax.ShapeDtypeStruct((n_batch * c, n_kv * qw), bf16),
        compiler_params=_cparams(2),
        name="attn_context",
    )(q, k, v)


def _natten_kernel(q_ref, k_ref, v_ref, kc_ref, vc_ref, bias_ref, o_ref, *, rows, w, wr, scale):
    kc = kc_ref[...]
    vc = vc_ref[...]

    nr = NATTEN_UNROLL

    def body(g, carry):
        base = pl.multiple_of(g * nr * w, nr * w)
        qg = q_ref[pl.ds(base, nr * w), :]
        scg = lax.dot_general(qg, kc, _NT, preferred_element_type=f32) * scale
        offs, sws = [], []
        for i in range(nr):
            r = g * nr + i
            r0 = jnp.clip(r - wr // 2, 0, rows - wr)
            off = pl.multiple_of(r0 * w, w)
            kb = k_ref[pl.ds(off, wr * w), :]
            sws.append(lax.dot_general(qg[i * w:(i + 1) * w], kb, _NT, preferred_element_type=f32) * scale
                       + bias_ref[r - r0])
            offs.append(off)
        pws, pcs, ls = [], [], []
        for i in range(nr):
            sc = scg[i * w:(i + 1) * w]
            m = jnp.maximum(jnp.max(sws[i], axis=-1, keepdims=True), jnp.max(sc, axis=-1, keepdims=True))
            pw = jnp.exp2(sws[i] - m)
            pc = jnp.exp2(sc - m)
            ls.append(jnp.sum(pw, axis=-1, keepdims=True) + jnp.sum(pc, axis=-1, keepdims=True))
            pws.append(pw.astype(bf16))
            pcs.append(pc.astype(bf16))
        ocg = jnp.dot(jnp.concatenate(pcs, axis=0), vc, preferred_element_type=f32)
        for i in range(nr):
            vb = v_ref[pl.ds(offs[i], wr * w), :]
            o = (jnp.dot(pws[i], vb, preferred_element_type=f32) + ocg[i * w:(i + 1) * w]) / ls[i]
            o_ref[pl.ds(pl.multiple_of(base + i * w, w), w), :] = o.astype(o_ref.dtype)
        return carry

    lax.fori_loop(0, rows // nr, body, 0)


def _natten_bias(rpb, rows, wr, win_r, win_c):
    w = GRID_W
    cols = np.arange(w)
    col_start = np.clip(cols - win_c // 2, 0, w - win_c)
    inside = (cols[None, :] >= col_start[:, None]) & (cols[None, :] < col_start[:, None] + win_c)
    rpbp = jnp.pad(rpb, ((0, 0), (0, 0), (w - win_c, w - win_c)))
    p = jnp.stack([rpbp[:, :, w - 1 - qc:2 * w - 1 - qc] for qc in range(w)], axis=2)
    p = jnp.where(inside[None, None], p * LOG2E, NEG)
    tab = jnp.stack([p[:, win_r - 1 - d:win_r - 1 - d + wr] for d in range(wr)], axis=1)
    return jnp.transpose(tab, (0, 1, 3, 2, 4)).reshape(rpb.shape[0], wr, w, wr * w)


def _natten_lat(qkv, bias, n_heads, n_batch, s, c, d_model):
    t = qkv.shape[0]
    rows = s // GRID_W
    wr = bias.shape[1]
    ctx_blk0 = n_batch * s // c
    hq, hk, hv = 0, d_model // LANES, 2 * d_model // LANES
    return pl.pallas_call(
        functools.partial(_natten_kernel, rows=rows, w=GRID_W, wr=wr, scale=SM_SCALE_LOG2),
        grid=(n_batch, n_heads),
        in_specs=[pl.BlockSpec((s, LANES), lambda b, h: (b, hq + h)),
                  pl.BlockSpec((s, LANES), lambda b, h: (b, hk + h)),
                  pl.BlockSpec((s, LANES), lambda b, h: (b, hv + h)),
                  pl.BlockSpec((c, LANES), lambda b, h: (ctx_blk0 + b, hk + h)),
                  pl.BlockSpec((c, LANES), lambda b, h: (ctx_blk0 + b, hv + h)),
                  pl.BlockSpec((None, wr, GRID_W, wr * GRID_W), lambda b, h: (h, 0, 0, 0))],
        out_specs=pl.BlockSpec((s, LANES), lambda b, h: (b, h)),
        out_shape=jax.ShapeDtypeStruct((n_batch * s, d_model), bf16),
        compiler_params=_cparams(2),
        name="natten_latent",
    )(qkv, qkv, qkv, qkv, qkv, bias)


def _pool_kernel(h_ref, hp_ref, hn_ref, x_ref, w_ref, ls_ref, m_ref, nw_ref, o_ref, f_ref, pad_ref, *,
                 bps_lat, bps_ctx, n_lat_blocks, s, c, gate_row, shift_row, scale_row):
    i = pl.program_id(0)
    tm = h_ref.shape[0]
    is_lat = i < n_lat_blocks
    blk = jnp.where(is_lat, i % bps_lat, (i - n_lat_blocks) % bps_ctx)
    nblk = jnp.where(is_lat, bps_lat, bps_ctx)
    length = jnp.where(is_lat, s, c)
    pad_ref[0:POOL_HALO, :] = jnp.where(blk == 0, 0.0, hp_ref[...])
    pad_ref[POOL_HALO:POOL_HALO + tm, :] = h_ref[...]
    pad_ref[POOL_HALO + tm:2 * POOL_HALO + tm, :] = jnp.where(blk == nblk - 1, 0.0, hn_ref[...])
    pos = blk * tm + lax.broadcasted_iota(jnp.int32, (tm, 1), 0)
    pg = w_ref.shape[1]
    for g, win in enumerate(POOL_WINDOWS):
        cs = slice(g * pg, (g + 1) * pg)
        lo_off, hi_off = win // 2, win - win // 2
        acc = pad_ref[POOL_HALO - lo_off:POOL_HALO - lo_off + tm, cs]
        for k in range(-lo_off + 1, hi_off):
            acc = acc + pad_ref[POOL_HALO + k:POOL_HALO + k + tm, cs]
        cnt = jnp.minimum(pos + hi_off, length) - jnp.maximum(pos - lo_off, 0)
        y = acc * (1.0 / cnt.astype(f32)) - h_ref[:, cs]
        z = jnp.dot(y.astype(bf16), w_ref[g], preferred_element_type=f32) * ls_ref[:, cs]
        o_ref[:, cs] = x_ref[:, cs] + m_ref[gate_row:gate_row + 1, cs] * z
    f_ref[...] = _norm_mod_value(o_ref[...], nw_ref[...], m_ref, shift_row, scale_row).astype(f_ref.dtype)


def _pool(h, x, pool_w, pool_scale, mods, nw, gate_row, shift_row, scale_row, n_batch, s, c):
    t, d = h.shape
    tm = POOL_ROW_BLOCK
    nb = t // tm
    bps_lat, bps_ctx = s // tm, c // tm
    hb = tm // POOL_HALO
    last_halo = t // POOL_HALO - 1
    return pl.pallas_call(
        functools.partial(_pool_kernel, bps_lat=bps_lat, bps_ctx=bps_ctx, n_lat_blocks=n_batch * bps_lat,
                          s=s, c=c, gate_row=gate_row, shift_row=shift_row, scale_row=scale_row),
        grid=(nb,),
        in_specs=[pl.BlockSpec((tm, d), lambda i: (i, 0)),
                  pl.BlockSpec((POOL_HALO, d), lambda i: (jnp.maximum(i * hb - 1, 0), 0)),
                  pl.BlockSpec((POOL_HALO, d), lambda i: (jnp.minimum((i + 1) * hb, last_halo), 0)),
                  pl.BlockSpec((tm, d), lambda i: (i, 0)),
                  pl.BlockSpec(pool_w.shape, lambda i: (0, 0, 0)),
                  pl.BlockSpec((1, d), lambda i: (0, 0)),
                  pl.BlockSpec((None, SUBLANES, d), lambda i: (_group_of(i, bps_lat, n_batch), 0, 0)),
                  pl.BlockSpec((1, d), lambda i: (0, 0))],
        out_specs=[pl.BlockSpec((tm, d), lambda i: (i, 0)), pl.BlockSpec((tm, d), lambda i: (i, 0))],
        out_shape=[jax.ShapeDtypeStruct((t, d), f32), jax.ShapeDtypeStruct((t, d), bf16)],
        scratch_shapes=[pltpu.VMEM((tm + 2 * POOL_HALO, d), f32)],
        compiler_params=_cparams(1),
        name="pool_mixer",
    )(h, h, h, x, pool_w, pool_scale.reshape(1, d), mods, nw.reshape(1, d))


def _topk_rows(chains, tb):
    rows = lax.broadcasted_iota(jnp.int32, (PEER_TOPK, tb), 0)

    def body(k, carry):
        out = []
        for (s_ref, order), (vals, idxs) in zip(chains, carry):
            s = s_ref[...]
            m = jnp.max(s, axis=0, keepdims=True)
            am = jnp.min(jnp.where(s == m, order, float(2 ** 23)), axis=0, keepdims=True)
            s_ref[...] = jnp.where(order == am, NEG, s)
            out.append((jnp.where(rows == k, m, vals), jnp.where(rows == k, am, idxs)))
        return tuple(out)

    zero = jnp.zeros((PEER_TOPK, tb), f32)
    res = lax.fori_loop(0, PEER_TOPK, body, tuple((zero, zero) for _ in chains))
    return [(vals, idxs.astype(jnp.int32)) for vals, idxs in res]


def _cand_blocks():
    blocks = [(0, PEER_TOPK)] + [(a, SUBLANES) for a in range(1, SUBLANES)]
    assert all((a + 1) * (nb + 1) > PEER_TOPK for a, nb in blocks[1:]) and 2 * SUBLANES == PEER_TOPK
    return blocks


def _cast_chunks(n_rows, steps):
    n = 1 << (steps.bit_length() - 1)
    assert n_rows % n == 0
    return n, n_rows // n


def _peer_topk_kernel(q_ref, sk_ref, w_ref, e_ref, g_ref, wb_ref, s1_ref, s2_ref, cand_ref, es_ref, gs_ref, *,
                      n_heads, nkeys):
    tb = q_ref.shape[0]
    wb_ref[...] = w_ref[...].astype(bf16)
    key_order = lax.broadcasted_iota(jnp.int32, (nkeys, tb), 0).astype(f32)
    blocks = _cand_blocks()
    n_mid = (len(blocks) - 1) * SUBLANES
    r = lax.broadcasted_iota(jnp.int32, cand_ref.shape, 0)
    rm = r - PEER_TOPK
    mid = (lax.shift_right_logical(rm, SUBLANES.bit_length() - 1) + 1) * PEER_TOPK + lax.bitwise_and(rm, SUBLANES - 1)
    tail = (rm - n_mid + SUBLANES) * PEER_TOPK
    cand_order = jnp.where(r < PEER_TOPK, r, jnp.where(rm < n_mid, mid, tail)).astype(f32)

    def scores(h):
        col = pl.multiple_of(h * 2 * LANES, 2 * LANES)
        s1_ref[...] = lax.dot_general(sk_ref[0], q_ref[:, pl.ds(col, LANES)], _NT, preferred_element_type=f32)
        s2_ref[...] = lax.dot_general(sk_ref[1], q_ref[:, pl.ds(col + LANES, LANES)], _NT, preferred_element_type=f32)

    sub_chains = [(s1_ref, key_order), (s2_ref, key_order)]
    scores(0)
    (v1, i1), (v2, i2) = _topk_rows(sub_chains, tb)

    def head(h, carry):
        v1, i1, v2, i2 = carry
        r0 = 0
        for a, nb in blocks:
            cand_ref[r0:r0 + nb, :] = v1[a:a + 1, :] + v2[:nb]
            r0 += nb
        cand_ref[r0:r0 + SUBLANES, :] = v1[SUBLANES:] + v2[0:1, :]
        scores(jnp.minimum(h + 1, n_heads - 1))
        nxt1, nxt2, (sc, ci) = _topk_rows(sub_chains + [(cand_ref, cand_order)], tb)
        ca = lax.shift_right_logical(ci, PEER_TOPK.bit_length() - 1)
        cb = lax.bitwise_and(ci, PEER_TOPK - 1)
        e1 = jnp.zeros((PEER_TOPK, tb), jnp.int32)
        e2 = jnp.zeros((PEER_TOPK, tb), jnp.int32)
        for a in range(PEER_TOPK):
            e1 = jnp.where(ca == a, i1[a:a + 1, :], e1)
            e2 = jnp.where(cb == a, i2[a:a + 1, :], e2)
        p = jnp.exp(sc - sc[0:1, :])
        gate = p / jnp.sum(p, axis=0, keepdims=True)
        row = pl.multiple_of(h * PEER_TOPK, PEER_TOPK)
        es_ref[pl.ds(row, PEER_TOPK), :] = e1 * nkeys + e2
        gs_ref[pl.ds(row, PEER_TOPK), :] = gate
        return nxt1 + nxt2

    lax.fori_loop(0, n_heads, head, (v1, i1, v2, i2))
    e_ref[...] = es_ref[...].T
    g_ref[...] = gs_ref[...].T


def _peer_topk(q, sub_keys, n_blocks_rows, w_all, layer):
    t, qd = q.shape
    nkeys = sub_keys.shape[1]
    n_heads = qd // (2 * LANES)
    slots = n_heads * PEER_TOPK
    tb = TOPK_TOKENS
    steps = n_blocks_rows * ROW_BLOCK // tb
    ne, d = w_all.shape[1:]
    n_chunks, chunk = _cast_chunks(ne, steps)
    return pl.pallas_call(
        functools.partial(_peer_topk_kernel, n_heads=n_heads, nkeys=nkeys),
        grid=(steps,),
        in_specs=[pl.BlockSpec((tb, qd), lambda i: (i, 0)),
                  pl.BlockSpec(sub_keys.shape, lambda i: (0, 0, 0)),
                  pl.BlockSpec((None, chunk, d), lambda i: (layer, jnp.minimum(i, n_chunks - 1), 0))],
        out_specs=[pl.BlockSpec((tb, slots), lambda i: (i, 0)),
                   pl.BlockSpec((tb, slots), lambda i: (i, 0)),
                   pl.BlockSpec((chunk, d), lambda i: (jnp.minimum(i, n_chunks - 1), 0))],
        out_shape=[jax.ShapeDtypeStruct((n_blocks_rows * ROW_BLOCK, slots), jnp.int32),
                   jax.ShapeDtypeStruct((n_blocks_rows * ROW_BLOCK, slots), f32),
                   jax.ShapeDtypeStruct((ne, d), bf16)],
        scratch_shapes=[pltpu.VMEM((nkeys, tb), f32), pltpu.VMEM((nkeys, tb), f32),
                        pltpu.VMEM((sum(nb for _, nb in _cand_blocks()) + SUBLANES, tb), f32),
                        pltpu.VMEM((slots, tb), jnp.int32), pltpu.VMEM((slots, tb), f32)],
        compiler_params=_cparams(1),
        name="peer_topk",
    )(q, sub_keys, w_all)


def _peer_gates_kernel(e_ref, g_ref, w_ref, o_ref, wb_ref, *, nkeys):
    tb, slots = e_ref.shape
    wb_ref[...] = w_ref[...].astype(bf16)
    iota = lax.broadcasted_iota(jnp.int32, (nkeys, slots), 0)

    def body(t, carry):
        e = e_ref[pl.ds(t, 1), :]
        g = g_ref[pl.ds(t, 1), :]
        e1 = lax.shift_right_logical(e, nkeys.bit_length() - 1)
        e2 = lax.bitwise_and(e, nkeys - 1)
        w1 = jnp.where(e1 == iota, g, 0.0).astype(bf16)
        o2 = jnp.where(e2 == iota, 1.0, 0.0).astype(bf16)
        res = lax.dot_general(w1, o2, _NT, preferred_element_type=f32)
        bits = lax.bitcast_convert_type(res.astype(bf16).astype(f32), jnp.uint32)
        packed = jnp.bitwise_or(jnp.right_shift(bits[:nkeys // 2], jnp.uint32(16)), bits[nkeys // 2:])
        o_ref[:, t] = packed.reshape(nkeys // (2 * SUBLANES), SUBLANES, nkeys)
        return carry

    lax.fori_loop(0, tb, body, 0, unroll=GATES_UNROLL)


def _peer_gates(e, g, nkeys, n_blocks_rows, w_all, layer):
    t, slots = e.shape
    tb = GATES_TOKENS
    nblk = nkeys // (2 * SUBLANES)
    steps = n_blocks_rows * ROW_BLOCK // tb
    ne, d = w_all.shape[1:]
    n_chunks, chunk = _cast_chunks(ne, steps)
    return pl.pallas_call(
        functools.partial(_peer_gates_kernel, nkeys=nkeys),
        grid=(steps,),
        in_specs=[pl.BlockSpec((tb, slots), lambda i: (i, 0)),
                  pl.BlockSpec((tb, slots), lambda i: (i, 0)),
                  pl.BlockSpec((None, chunk, d), lambda i: (layer, jnp.minimum(i, n_chunks - 1), 0))],
        out_specs=[pl.BlockSpec((nblk, tb, SUBLANES, nkeys), lambda i: (0, i, 0, 0)),
                   pl.BlockSpec((chunk, d), lambda i: (jnp.minimum(i, n_chunks - 1), 0))],
        out_shape=[jax.ShapeDtypeStruct((nblk, t, SUBLANES, nkeys), jnp.uint32),
                   jax.ShapeDtypeStruct((ne, d), bf16)],
        compiler_params=_cparams(1),
        name="peer_gates",
    )(e, g, w_all)


def _gelu_tanh(x):
    k0 = -2.0 * 0.7978845608028654 * 1.4426950408889634
    return x / (1.0 + jnp.exp2(x * (k0 + (k0 * 0.044715) * (x * x))))


def _peer_dense_kernel(f_ref, u_ref, v_ref, g_ref, x_ref, m_ref, fw_ref, o_ref, acc_ref, a_ref, *,
                       gate_row, nkeys, final_norm):
    j = pl.program_id(1)

    @pl.when(j == 0)
    def _():
        acc_ref[...] = jnp.zeros_like(acc_ref)

    slot = j % 2
    a_ref[slot] = lax.dot_general(f_ref[...], u_ref[...], _NT, preferred_element_type=f32)
    tb = f_ref.shape[0]
    n_first = g_ref.shape[0] // tb
    shift = jnp.where(j < pl.num_programs(1) // 2, 16, 0).astype(jnp.uint32)
    parts = []
    for k in range(n_first):
        word = g_ref[pl.ds(k, tb, stride=n_first), :]
        gbits = jnp.bitwise_and(jnp.left_shift(word, shift), jnp.uint32(0xFFFF0000))
        gk = lax.bitcast_convert_type(gbits, f32)
        parts.append((_gelu_tanh(a_ref[slot, :, k * nkeys:(k + 1) * nkeys]) * gk).astype(bf16))
    ga = jnp.concatenate(parts, axis=1)
    acc_ref[...] += jnp.dot(ga, v_ref[...], preferred_element_type=f32)

    @pl.when(j == pl.num_programs(1) - 1)
    def _():
        y = x_ref[...] + m_ref[gate_row:gate_row + 1, :] * acc_ref[...]
        if final_norm:
            y = y * lax.rsqrt(jnp.mean(y * y, axis=-1, keepdims=True) + EPS) * fw_ref[...]
        o_ref[...] = y


def _peer_dense(f, u, v, gates, x, mods, gate_row, n_blocks, bps, n_batch, final_w=None):
    t, d = f.shape
    ne = v.shape[0]
    nkeys = gates.shape[3]
    te = SUBLANES * nkeys
    n_gblk = gates.shape[0]
    assert ne // te == 2 * n_gblk
    gates = gates.reshape(n_gblk, gates.shape[1] * SUBLANES, nkeys)
    return pl.pallas_call(
        functools.partial(_peer_dense_kernel, gate_row=gate_row, nkeys=nkeys, final_norm=final_w is not None),
        grid=(n_blocks, ne // te),
        in_specs=[pl.BlockSpec((ROW_BLOCK, d), lambda i, j: (i, 0)),
                  pl.BlockSpec((te, d), lambda i, j: (j, 0)),
                  pl.BlockSpec((te, d), lambda i, j: (j, 0)),
                  pl.BlockSpec((None, ROW_BLOCK * SUBLANES, nkeys), lambda i, j: (j % n_gblk, i, 0)),
                  pl.BlockSpec((ROW_BLOCK, d), lambda i, j: (i, 0)),
                  pl.BlockSpec((None, SUBLANES, d), lambda i, j: (_group_of(i, bps, n_batch), 0, 0)),
                  pl.BlockSpec((1, d), lambda i, j: (0, 0))],
        out_specs=pl.BlockSpec((ROW_BLOCK, d), lambda i, j: (i, 0)),
        out_shape=jax.ShapeDtypeStruct((t, d), f32),
        scratch_shapes=[pltpu.VMEM((ROW_BLOCK, d), f32), pltpu.VMEM((2, ROW_BLOCK, te), f32)],
        compiler_params=_cparams(2),
        name="peer_dense",
    )(f, u, v, gates, x, mods, (jnp.ones((d,), f32) if final_w is None else final_w).reshape(1, d))


def kernel(x, c, ctx, c_ctx, mod_w, mod_b, norm_w, final_norm_w, a_wqkv, a_q_gain, a_k_gain, a_wo,
           b_wqkv, b_rpb, b_wo, pool_w, pool_scale, peer_wq, peer_sub_keys, peer_u, peer_v):
    n_batch, s, d = x.shape
    c_len = ctx.shape[1]
    depth = mod_w.shape[0]
    assert s % ROW_BLOCK == 0 and (n_batch * c_len) % ROW_BLOCK == 0 and n_batch * c_len <= s
    assert s % c_len == 0 and c_len % POOL_ROW_BLOCK == 0 and s % GRID_W == 0
    assert a_q_gain.shape[1] == LANES and peer_sub_keys.shape[2] == LANES and peer_sub_keys.shape[3] == LANES
    assert PEER_TOPK & (PEER_TOPK - 1) == 0 and peer_sub_keys.shape[2] & (peer_sub_keys.shape[2] - 1) == 0
    assert depth % N_MIXERS != 0

    bps = s // ROW_BLOCK
    nb_lat = n_batch * bps
    nb_all = nb_lat + n_batch * c_len // ROW_BLOCK
    t_lat = n_batch * s
    q_dim = a_wo.shape[1]
    kv_dim = (a_wqkv.shape[2] - q_dim) // 2
    n_kv = kv_dim // LANES
    groups = q_dim // kv_dim
    b_heads = b_rpb.shape[1]
    win_r, win_c = (b_rpb.shape[2] + 1) // 2, (b_rpb.shape[3] + 1) // 2
    nkeys = peer_sub_keys.shape[2]
    scale = SM_SCALE_LOG2

    xs = jnp.concatenate([x.reshape(t_lat, d), ctx.reshape(n_batch * c_len, d)], axis=0)
    cvec = jnp.concatenate([c, c_ctx[None, :], jnp.zeros((SUBLANES - n_batch - 1, d), f32)], axis=0)
    mods_all = _mods(cvec, mod_w, mod_b)
    tables = _rope_tables(s)

    for i in range(depth):
        last = i == depth - 1
        kind, j = i % N_MIXERS, i // N_MIXERS
        nb = nb_lat if last else nb_all
        mods = mods_all[i, :n_batch + 1].reshape(n_batch + 1, N_MOD, d)
        mods = jnp.pad(mods, ((0, 0), (0, SUBLANES - N_MOD), (0, 0)))

        if kind == 2:
            h = _norm_mod(xs, norm_w[i, 0], mods, 0, 1, nb_all, bps, n_batch, f32)
            xs, f = _pool(h, xs, pool_w[j].astype(bf16), pool_scale[j], mods, norm_w[i, 1], 2, 3, 4, n_batch, s, c_len)
        else:
            h = _norm_mod(xs, norm_w[i, 0], mods, 0, 1, nb_all, bps, n_batch, bf16)
            if kind == 0:
                qkv = _qkv_gqa(h, a_wqkv[j].astype(bf16), a_q_gain[j], a_k_gain[j], tables, q_dim, kv_dim,
                               nb_all, bps, n_batch)
                o = _attn_lat(qkv, qkv, qkv, 0, q_dim, q_dim + kv_dim, groups, n_kv, n_batch, s, c_len, q_dim)
                if not last:
                    o_ctx = _attn_ctx(qkv, qkv, qkv, 0, q_dim, q_dim + kv_dim, groups, n_kv, n_batch, s, c_len, 1.0)
                wo = a_wo[j]
            else:
                qkv = _mm(h, b_wqkv[j].astype(bf16), nb_all, _col_tile(3 * d), bf16)
                rows = s // GRID_W
                bias = _natten_bias(b_rpb[j], rows, min(win_r, rows), win_r, win_c)
                o = _natten_lat(qkv, bias, b_heads, n_batch, s, c_len, d)
                if not last:
                    o_ctx = _attn_ctx(qkv, qkv, qkv, 0, d, 2 * d, 1, b_heads, n_batch, s, c_len, scale)
                wo = b_wo[j]
            if last:
                o_ctx = o
            xs, f = _mm_res(o, o_ctx, wo.astype(bf16), xs, mods, norm_w[i, 1], 2, 3, 4, nb, bps, n_batch)

        qp = _mm(f, peer_wq[i].astype(bf16), nb, _col_tile(peer_wq.shape[2]), f32)
        e, g, u_bf = _peer_topk(qp, peer_sub_keys[i], nb, peer_u, i)
        gates, v_bf = _peer_gates(e, g, nkeys, nb, peer_v, i)
        xs = _peer_dense(f, u_bf, v_bf, gates, xs, mods, 5, nb, bps, n_batch, final_norm_w if last else None)

    return xs.reshape(n_batch, s, d)
```

```python
import functools

import numpy as np
import jax
import jax.numpy as jnp
from jax import lax
from jax.experimental import pallas as pl
from jax.experimental.pallas import tpu as pltpu

GRID_W = 64
ROPE_BASE = 10000.0
POOL_WINDOWS = (2, 4, 8, 16)
PEER_TOPK = 16
EPS = 1e-6
N_MIXERS = 3
N_MOD = 6

LANES = 128
SUBLANES = 8
ROW_BLOCK = 512
POOL_ROW_BLOCK = 256
POOL_HALO = 8
MM_COL_TILE = 2048
ATTN_Q_ROWS = 128
ATTN_Q_SUBBLOCKS = 2
NATTEN_UNROLL = 16
TOPK_TOKENS = 512
GATES_TOKENS = 256
GATES_UNROLL = 64
VMEM_LIMIT = 56 * 1024 * 1024
NEG = -1e30
LOG2E = 1.4426950408889634
SM_SCALE_LOG2 = float(LANES) ** -0.5 * LOG2E

f32 = jnp.float32
bf16 = jnp.bfloat16
_NT = (((1,), (1,)), ((), ()))


def _cparams(n_axes):
    return pltpu.CompilerParams(dimension_semantics=("arbitrary",) * n_axes, vmem_limit_bytes=VMEM_LIMIT)


def _group_of(i, blocks_per_seq, n_batch):
    return jnp.minimum(i // blocks_per_seq, n_batch)


def _mods_kernel(cv_ref, w_ref, b_ref, o_ref):
    cv = cv_ref[...]
    a = (cv / (1.0 + jnp.exp(-cv))).astype(bf16)
    o_ref[...] = jnp.dot(a, w_ref[...].astype(bf16), preferred_element_type=f32) + b_ref[...]


def _mods(cvec, mod_w, mod_b):
    depth, d, n = mod_w.shape
    tn = 1024
    return pl.pallas_call(
        _mods_kernel,
        grid=(depth, n // tn),
        in_specs=[pl.BlockSpec((SUBLANES, d), lambda l, j: (0, 0)),
                  pl.BlockSpec((None, d, tn), lambda l, j: (l, 0, j)),
                  pl.BlockSpec((None, 1, tn), lambda l, j: (l, 0, j))],
        out_specs=pl.BlockSpec((None, SUBLANES, tn), lambda l, j: (l, 0, j)),
        out_shape=jax.ShapeDtypeStruct((depth, SUBLANES, n), f32),
        compiler_params=_cparams(2),
        name="adaln_mods",
    )(cvec, mod_w, mod_b.reshape(depth, 1, n))


def _norm_mod_value(x, nw, m_ref, shift_row, scale_row):
    y = x * lax.rsqrt(jnp.mean(x * x, axis=-1, keepdims=True) + EPS) * nw
    return y * (1.0 + m_ref[scale_row:scale_row + 1, :]) + m_ref[shift_row:shift_row + 1, :]


def _norm_mod_kernel(x_ref, nw_ref, m_ref, o_ref, *, shift_row, scale_row):
    o_ref[...] = _norm_mod_value(x_ref[...], nw_ref[...], m_ref, shift_row, scale_row).astype(o_ref.dtype)


def _norm_mod(x, nw, mods, shift_row, scale_row, n_blocks, bps, n_batch, out_dtype):
    t, d = x.shape
    return pl.pallas_call(
        functools.partial(_norm_mod_kernel, shift_row=shift_row, scale_row=scale_row),
        grid=(n_blocks,),
        in_specs=[pl.BlockSpec((ROW_BLOCK, d), lambda i: (i, 0)),
                  pl.BlockSpec((1, d), lambda i: (0, 0)),
                  pl.BlockSpec((None, SUBLANES, d), lambda i: (_group_of(i, bps, n_batch), 0, 0))],
        out_specs=pl.BlockSpec((ROW_BLOCK, d), lambda i: (i, 0)),
        out_shape=jax.ShapeDtypeStruct((n_blocks * ROW_BLOCK, d), out_dtype),
        compiler_params=_cparams(1),
        name="norm_mod",
    )(x, nw.reshape(1, d), mods)


def _mm_kernel(a_ref, w_ref, o_ref):
    o_ref[...] = jnp.dot(a_ref[...], w_ref[...], preferred_element_type=f32).astype(o_ref.dtype)


def _col_tile(n):
    return max(t for t in range(LANES, min(n, MM_COL_TILE) + 1, LANES) if n % t == 0)


def _mm(a, w, n_blocks, tn, out_dtype):
    t, k = a.shape
    n = w.shape[1]
    return pl.pallas_call(
        _mm_kernel,
        grid=(n // tn, n_blocks),
        in_specs=[pl.BlockSpec((ROW_BLOCK, k), lambda j, i: (i, 0)),
                  pl.BlockSpec((k, tn), lambda j, i: (0, j))],
        out_specs=pl.BlockSpec((ROW_BLOCK, tn), lambda j, i: (i, j)),
        out_shape=jax.ShapeDtypeStruct((n_blocks * ROW_BLOCK, n), out_dtype),
        compiler_params=_cparams(2),
        name="matmul",
    )(a, w)


def _mm_res_kernel(al_ref, ac_ref, w_ref, x_ref, m_ref, nw_ref, o_ref, f_ref, *,
                   gate_row, shift_row, scale_row, n_lat_blocks):
    a = jnp.where(pl.program_id(0) < n_lat_blocks, al_ref[...], ac_ref[...])
    y = x_ref[...] + m_ref[gate_row:gate_row + 1, :] * jnp.dot(a, w_ref[...], preferred_element_type=f32)
    o_ref[...] = y
    f_ref[...] = _norm_mod_value(y, nw_ref[...], m_ref, shift_row, scale_row).astype(f_ref.dtype)


def _mm_res(a_lat, a_ctx, w, x, mods, nw, gate_row, shift_row, scale_row, n_blocks, bps, n_batch):
    k = a_lat.shape[1]
    n = w.shape[1]
    rows = n_blocks * ROW_BLOCK
    n_lat_blocks = a_lat.shape[0] // ROW_BLOCK
    assert n_blocks == n_lat_blocks or n_blocks == n_lat_blocks + a_ctx.shape[0] // ROW_BLOCK
    return pl.pallas_call(
        functools.partial(_mm_res_kernel, gate_row=gate_row, shift_row=shift_row, scale_row=scale_row,
                          n_lat_blocks=n_lat_blocks),
        grid=(n_blocks,),
        in_specs=[pl.BlockSpec((ROW_BLOCK, k), lambda i: (jnp.minimum(i, n_lat_blocks - 1), 0)),
                  pl.BlockSpec((ROW_BLOCK, k), lambda i: (jnp.maximum(i - n_lat_blocks, 0), 0)),
                  pl.BlockSpec((k, n), lambda i: (0, 0)),
                  pl.BlockSpec((ROW_BLOCK, n), lambda i: (i, 0)),
                  pl.BlockSpec((None, SUBLANES, n), lambda i: (_group_of(i, bps, n_batch), 0, 0)),
                  pl.BlockSpec((1, n), lambda i: (0, 0))],
        out_specs=[pl.BlockSpec((ROW_BLOCK, n), lambda i: (i, 0)), pl.BlockSpec((ROW_BLOCK, n), lambda i: (i, 0))],
        out_shape=[jax.ShapeDtypeStruct((rows, n), f32), jax.ShapeDtypeStruct((rows, n), bf16)],
        compiler_params=_cparams(1),
        name="matmul_residual",
    )(a_lat, a_ctx, w, x, mods, nw.reshape(1, n))


def _qkv_gqa_kernel(a_ref, w_ref, qg_ref, kg_ref, cos_ref, sin_ref, o_ref, *, nq_tiles, nk_tiles, scale):
    j = pl.program_id(0)
    acc = jnp.dot(a_ref[...], w_ref[...], preferred_element_type=f32)
    heads = acc.shape[1] // LANES

    def prep(gain, post):
        cos, sin = cos_ref[...], sin_ref[...]
        for h in range(heads):
            y = acc[:, h * LANES:(h + 1) * LANES]
            y = y * lax.rsqrt(jnp.mean(y * y, axis=-1, keepdims=True) + EPS) * gain
            y = y * cos + pltpu.roll(y, LANES // 2, 1) * sin
            o_ref[:, h * LANES:(h + 1) * LANES] = (y * post).astype(o_ref.dtype)

    @pl.when(j < nq_tiles)
    def _():
        prep(qg_ref[...], scale)

    @pl.when(jnp.logical_and(j >= nq_tiles, j < nq_tiles + nk_tiles))
    def _():
        prep(kg_ref[...], 1.0)

    @pl.when(j >= nq_tiles + nk_tiles)
    def _():
        o_ref[...] = acc.astype(o_ref.dtype)


def _rope_head_perm(a):
    quarter = LANES // 4
    shape = a.shape
    a = a.reshape(shape[:-1] + (shape[-1] // LANES, 2, 2, quarter))
    return jnp.swapaxes(a, -3, -2).reshape(shape)


def _rope_tables(s):
    t = np.arange(s)
    half = LANES // 2
    inv = ROPE_BASE ** (-jnp.arange(0, half, 2, dtype=f32) / half)
    ang_r = (t // GRID_W).astype(np.float32)[:, None] * inv[None, :]
    ang_c = (t % GRID_W).astype(np.float32)[:, None] * inv[None, :]
    ang = jnp.concatenate([ang_r, ang_c], axis=-1)
    cos = jnp.concatenate([jnp.cos(ang), jnp.cos(ang)], axis=-1)
    sin = jnp.concatenate([-jnp.sin(ang), jnp.sin(ang)], axis=-1)
    pad1 = jnp.ones((ROW_BLOCK, LANES), f32)
    pad0 = jnp.zeros((ROW_BLOCK, LANES), f32)
    return jnp.concatenate([cos, pad1], 0), jnp.concatenate([sin, pad0], 0)


def _qkv_gqa(h, wqkv, q_gain, k_gain, tables, q_dim, kv_dim, n_blocks, bps, n_batch):
    t, k = h.shape
    n = wqkv.shape[1]
    tn = min(512, kv_dim)
    cos, sin = tables
    wqkv = jnp.concatenate([_rope_head_perm(wqkv[:, :q_dim + kv_dim]), wqkv[:, q_dim + kv_dim:]], axis=1)
    q_gain, k_gain = _rope_head_perm(q_gain), _rope_head_perm(k_gain)

    def pos_map(j, i):
        return (jnp.where(i < bps * n_batch, i % bps, bps), 0)

    return pl.pallas_call(
        functools.partial(_qkv_gqa_kernel, nq_tiles=q_dim // tn, nk_tiles=kv_dim // tn, scale=SM_SCALE_LOG2),
        grid=(n // tn, n_blocks),
        in_specs=[pl.BlockSpec((ROW_BLOCK, k), lambda j, i: (i, 0)),
                  pl.BlockSpec((k, tn), lambda j, i: (0, j)),
                  pl.BlockSpec((1, LANES), lambda j, i: (0, 0)),
                  pl.BlockSpec((1, LANES), lambda j, i: (0, 0)),
                  pl.BlockSpec((ROW_BLOCK, LANES), pos_map),
                  pl.BlockSpec((ROW_BLOCK, LANES), pos_map)],
        out_specs=pl.BlockSpec((ROW_BLOCK, tn), lambda j, i: (i, j)),
        out_shape=jax.ShapeDtypeStruct((t, n), bf16),
        compiler_params=_cparams(2),
        name="qkv_gqa",
    )(h, wqkv, q_gain.reshape(1, LANES), k_gain.reshape(1, LANES), cos, sin)


def _flash_kernel(*refs, groups, n_lat_chunks, ck, scale):
    if n_lat_chunks:
        q_ref, kc_ref, vc_ref, kl_ref, vl_ref, o_ref = refs
    else:
        q_ref, kc_ref, vc_ref, o_ref = refs
    tq = q_ref.shape[0]
    q = jnp.concatenate([q_ref[:, g * LANES:(g + 1) * LANES] for g in range(groups)], axis=0)
    rows = groups * tq

    def step(k, v, m, l, acc):
        s = lax.dot_general(q, k, _NT, preferred_element_type=f32)
        if scale != 1.0:
            s = s * scale
        m_new = jnp.maximum(m, jnp.max(s, axis=-1, keepdims=True))
        p = jnp.exp2(s - m_new)
        alpha = jnp.exp2(m - m_new)
        l = alpha * l + jnp.sum(p, axis=-1, keepdims=True)
        acc = alpha * acc + jnp.dot(p.astype(bf16), v, preferred_element_type=f32)
        return m_new, l, acc

    carry = step(kc_ref[...], vc_ref[...], jnp.full((rows, 1), NEG, f32), jnp.zeros((rows, 1), f32),
                 jnp.zeros((rows, LANES), f32))
    if n_lat_chunks:
        def body(c, carry):
            off = pl.multiple_of(c * ck, ck)
            return step(kl_ref[pl.ds(off, ck), :], vl_ref[pl.ds(off, ck), :], *carry)
        carry = lax.fori_loop(0, n_lat_chunks, body, carry)
    _, l, acc = carry
    o = acc / l
    for g in range(groups):
        o_ref[:, g * LANES:(g + 1) * LANES] = o[g * tq:(g + 1) * tq].astype(o_ref.dtype)


def _gqa_lat_kernel(q_ref, kc_ref, vc_ref, kl_ref, vl_ref, o_ref, *scratch, groups, n_sub):
    tq = q_ref.shape[0] // n_sub
    vcx_ref, vlx_ref = scratch[4 * n_sub:]

    @pl.when(pl.program_id(2) == 0)
    def _():
        for src, dst in ((vc_ref, vcx_ref), (vl_ref, vlx_ref)):
            dst[:, :LANES] = src[...]
            dst[:, LANES:] = jnp.ones((src.shape[0], LANES), bf16)

    for h in range(n_sub):
        sc_ref, sl_ref, pc_ref, pl_ref = scratch[4 * h:4 * h + 4]
        q = jnp.concatenate([q_ref[h * tq:(h + 1) * tq, g * LANES:(g + 1) * LANES] for g in range(groups)], axis=0)
        sc_ref[...] = lax.dot_general(q, kc_ref[...], _NT, preferred_element_type=f32)
        sl_ref[...] = lax.dot_general(q, kl_ref[...], _NT, preferred_element_type=f32)
        m = jnp.maximum(jnp.max(sc_ref[...], axis=-1, keepdims=True), jnp.max(sl_ref[...], axis=-1, keepdims=True))
        pc_ref[...] = jnp.exp2((sc_ref[...] - m).astype(bf16))
        pl_ref[...] = jnp.exp2((sl_ref[...] - m).astype(bf16))
    for h in range(n_sub):
        pc_ref, pl_ref = scratch[4 * h + 2:4 * h + 4]
        ox = (jnp.dot(pc_ref[...], vcx_ref[...], preferred_element_type=f32)
              + jnp.dot(pl_ref[...], vlx_ref[...], preferred_element_type=f32))
        o = ox[:, :LANES] / ox[:, LANES:]
        for g in range(groups):
            o_ref[h * tq:(h + 1) * tq, g * LANES:(g + 1) * LANES] = o[g * tq:(g + 1) * tq].astype(o_ref.dtype)


def _attn_lat(q, k, v, q_col0, k_col0, v_col0, groups, n_kv, n_batch, s, c, out_cols):
    n_sub = ATTN_Q_SUBBLOCKS
    tq = n_sub * ATTN_Q_ROWS
    nqb = s // tq
    ctx_blk0 = n_batch * s // c
    qw = groups * LANES
    return pl.pallas_call(
        functools.partial(_gqa_lat_kernel, groups=groups, n_sub=n_sub),
        grid=(n_batch, n_kv, nqb),
        in_specs=[pl.BlockSpec((tq, qw), lambda b, h, i: (b * nqb + i, q_col0 // qw + h)),
                  pl.BlockSpec((c, LANES), lambda b, h, i: (ctx_blk0 + b, k_col0 // LANES + h)),
                  pl.BlockSpec((c, LANES), lambda b, h, i: (ctx_blk0 + b, v_col0 // LANES + h)),
                  pl.BlockSpec((s, LANES), lambda b, h, i: (b, k_col0 // LANES + h)),
                  pl.BlockSpec((s, LANES), lambda b, h, i: (b, v_col0 // LANES + h))],
        out_specs=pl.BlockSpec((tq, qw), lambda b, h, i: (b * nqb + i, h)),
        out_shape=jax.ShapeDtypeStruct((n_batch * s, out_cols), bf16),
        scratch_shapes=[pltpu.VMEM((groups * ATTN_Q_ROWS, n), dt)
                        for _ in range(n_sub) for dt in (f32, bf16) for n in (c, s)]
                       + [pltpu.VMEM((c, 2 * LANES), bf16), pltpu.VMEM((s, 2 * LANES), bf16)],
        compiler_params=_cparams(3),
        name="attn_latent",
    )(q, k, v, k, v)


def _attn_ctx(q, k, v, q_col0, k_col0, v_col0, groups, n_kv, n_batch, s, c, scale):
    ctx_blk0 = n_batch * s // c
    qw = groups * LANES
    return pl.pallas_call(
        functools.partial(_flash_kernel, groups=groups, n_lat_chunks=0, ck=0, scale=scale),
        grid=(n_batch, n_kv),
        in_specs=[pl.BlockSpec((c, qw), lambda b, h: (ctx_blk0 + b, q_col0 // qw + h)),
                  pl.BlockSpec((c, LANES), lambda b, h: (ctx_blk0 + b, k_col0 // LANES + h)),
                  pl.BlockSpec((c, LANES), lambda b, h: (ctx_blk0 + b, v_col0 // LANES + h))],
        out_specs=pl.BlockSpec((c, qw), lambda b, h: (b, h)),
        out_shape=jax.ShapeDtypeStruct((n_batch * c, n_kv * qw), bf16),
        compiler_params=_cparams(2),
        name="attn_context",
    )(q, k, v)


def _natten_kernel(q_ref, k_ref, v_ref, kc_ref, vc_ref, bias_ref, o_ref, *, rows, w, wr, scale):
    kc = kc_ref[...]
    vc = vc_ref[...]

    nr = NATTEN_UNROLL

    def body(g, carry):
        base = pl.multiple_of(g * nr * w, nr * w)
        qg = q_ref[pl.ds(base, nr * w), :]
        scg = lax.dot_general(qg, kc, _NT, preferred_element_type=f32) * scale
        offs, sws = [], []
        for i in range(nr):
            r = g * nr + i
            r0 = jnp.clip(r - wr // 2, 0, rows - wr)
            off = pl.multiple_of(r0 * w, w)
            kb = k_ref[pl.ds(off, wr * w), :]
            sws.append(lax.dot_general(qg[i * w:(i + 1) * w], kb, _NT, preferred_element_type=f32) * scale
                       + bias_ref[r - r0])
            offs.append(off)
        pws, pcs, ls = [], [], []
        for i in range(nr):
            sc = scg[i * w:(i + 1) * w]
            m = jnp.maximum(jnp.max(sws[i], axis=-1, keepdims=True), jnp.max(sc, axis=-1, keepdims=True))
            pw = jnp.exp2(sws[i] - m)
            pc = jnp.exp2(sc - m)
            ls.append(jnp.sum(pw, axis=-1, keepdims=True) + jnp.sum(pc, axis=-1, keepdims=True))
            pws.append(pw.astype(bf16))
            pcs.append(pc.astype(bf16))
        ocg = jnp.dot(jnp.concatenate(pcs, axis=0), vc, preferred_element_type=f32)
        for i in range(nr):
            vb = v_ref[pl.ds(offs[i], wr * w), :]
            o = (jnp.dot(pws[i], vb, preferred_element_type=f32) + ocg[i * w:(i + 1) * w]) / ls[i]
            o_ref[pl.ds(pl.multiple_of(base + i * w, w), w), :] = o.astype(o_ref.dtype)
        return carry

    lax.fori_loop(0, rows // nr, body, 0)


def _natten_bias(rpb, rows, wr, win_r, win_c):
    w = GRID_W
    cols = np.arange(w)
    col_start = np.clip(cols - win_c // 2, 0, w - win_c)
    inside = (cols[None, :] >= col_start[:, None]) & (cols[None, :] < col_start[:, None] + win_c)
    rpbp = jnp.pad(rpb, ((0, 0), (0, 0), (w - win_c, w - win_c)))
    p = jnp.stack([rpbp[:, :, w - 1 - qc:2 * w - 1 - qc] for qc in range(w)], axis=2)
    p = jnp.where(inside[None, None], p * LOG2E, NEG)
    tab = jnp.stack([p[:, win_r - 1 - d:win_r - 1 - d + wr] for d in range(wr)], axis=1)
    return jnp.transpose(tab, (0, 1, 3, 2, 4)).reshape(rpb.shape[0], wr, w, wr * w)


def _natten_lat(qkv, bias, n_heads, n_batch, s, c, d_model):
    t = qkv.shape[0]
    rows = s // GRID_W
    wr = bias.shape[1]
    ctx_blk0 = n_batch * s // c
    hq, hk, hv = 0, d_model // LANES, 2 * d_model // LANES
    return pl.pallas_call(
        functools.partial(_natten_kernel, rows=rows, w=GRID_W, wr=wr, scale=SM_SCALE_LOG2),
        grid=(n_batch, n_heads),
        in_specs=[pl.BlockSpec((s, LANES), lambda b, h: (b, hq + h)),
                  pl.BlockSpec((s, LANES), lambda b, h: (b, hk + h)),
                  pl.BlockSpec((s, LANES), lambda b, h: (b, hv + h)),
                  pl.BlockSpec((c, LANES), lambda b, h: (ctx_blk0 + b, hk + h)),
                  pl.BlockSpec((c, LANES), lambda b, h: (ctx_blk0 + b, hv + h)),
                  pl.BlockSpec((None, wr, GRID_W, wr * GRID_W), lambda b, h: (h, 0, 0, 0))],
        out_specs=pl.BlockSpec((s, LANES), lambda b, h: (b, h)),
        out_shape=jax.ShapeDtypeStruct((n_batch * s, d_model), bf16),
        compiler_params=_cparams(2),
        name="natten_latent",
    )(qkv, qkv, qkv, qkv, qkv, bias)


def _pool_kernel(h_ref, hp_ref, hn_ref, x_ref, w_ref, ls_ref, m_ref, nw_ref, o_ref, f_ref, pad_ref, *,
                 bps_lat, bps_ctx, n_lat_blocks, s, c, gate_row, shift_row, scale_row):
    i = pl.program_id(0)
    tm = h_ref.shape[0]
    is_lat = i < n_lat_blocks
    blk = jnp.where(is_lat, i % bps_lat, (i - n_lat_blocks) % bps_ctx)
    nblk = jnp.where(is_lat, bps_lat, bps_ctx)
    length = jnp.where(is_lat, s, c)
    pad_ref[0:POOL_HALO, :] = jnp.where(blk == 0, 0.0, hp_ref[...])
    pad_ref[POOL_HALO:POOL_HALO + tm, :] = h_ref[...]
    pad_ref[POOL_HALO + tm:2 * POOL_HALO + tm, :] = jnp.where(blk == nblk - 1, 0.0, hn_ref[...])
    pos = blk * tm + lax.broadcasted_iota(jnp.int32, (tm, 1), 0)
    pg = w_ref.shape[1]
    for g, win in enumerate(POOL_WINDOWS):
        cs = slice(g * pg, (g + 1) * pg)
        lo_off, hi_off = win // 2, win - win // 2
        acc = pad_ref[POOL_HALO - lo_off:POOL_HALO - lo_off + tm, cs]
        for k in range(-lo_off + 1, hi_off):
            acc = acc + pad_ref[POOL_HALO + k:POOL_HALO + k + tm, cs]
        cnt = jnp.minimum(pos + hi_off, length) - jnp.maximum(pos - lo_off, 0)
        y = acc * (1.0 / cnt.astype(f32)) - h_ref[:, cs]
        z = jnp.dot(y.astype(bf16), w_ref[g], preferred_element_type=f32) * ls_ref[:, cs]
        o_ref[:, cs] = x_ref[:, cs] + m_ref[gate_row:gate_row + 1, cs] * z
    f_ref[...] = _norm_mod_value(o_ref[...], nw_ref[...], m_ref, shift_row, scale_row).astype(f_ref.dtype)


def _pool(h, x, pool_w, pool_scale, mods, nw, gate_row, shift_row, scale_row, n_batch, s, c):
    t, d = h.shape
    tm = POOL_ROW_BLOCK
    nb = t // tm
    bps_lat, bps_ctx = s // tm, c // tm
    hb = tm // POOL_HALO
    last_halo = t // POOL_HALO - 1
    return pl.pallas_call(
        functools.partial(_pool_kernel, bps_lat=bps_lat, bps_ctx=bps_ctx, n_lat_blocks=n_batch * bps_lat,
                          s=s, c=c, gate_row=gate_row, shift_row=shift_row, scale_row=scale_row),
        grid=(nb,),
        in_specs=[pl.BlockSpec((tm, d), lambda i: (i, 0)),
                  pl.BlockSpec((POOL_HALO, d), lambda i: (jnp.maximum(i * hb - 1, 0), 0)),
                  pl.BlockSpec((POOL_HALO, d), lambda i: (jnp.minimum((i + 1) * hb, last_halo), 0)),
                  pl.BlockSpec((tm, d), lambda i: (i, 0)),
                  pl.BlockSpec(pool_w.shape, lambda i: (0, 0, 0)),
                  pl.BlockSpec((1, d), lambda i: (0, 0)),
                  pl.BlockSpec((None, SUBLANES, d), lambda i: (_group_of(i, bps_lat, n_batch), 0, 0)),
                  pl.BlockSpec((1, d), lambda i: (0, 0))],
        out_specs=[pl.BlockSpec((tm, d), lambda i: (i, 0)), pl.BlockSpec((tm, d), lambda i: (i, 0))],
        out_shape=[jax.ShapeDtypeStruct((t, d), f32), jax.ShapeDtypeStruct((t, d), bf16)],
        scratch_shapes=[pltpu.VMEM((tm + 2 * POOL_HALO, d), f32)],
        compiler_params=_cparams(1),
        name="pool_mixer",
    )(h, h, h, x, pool_w, pool_scale.reshape(1, d), mods, nw.reshape(1, d))


def _topk_rows(chains, tb):
    rows = lax.broadcasted_iota(jnp.int32, (PEER_TOPK, tb), 0)

    def body(k, carry):
        out = []
        for (s_ref, order), (vals, idxs) in zip(chains, carry):
            s = s_ref[...]
            m = jnp.max(s, axis=0, keepdims=True)
            am = jnp.min(jnp.where(s == m, order, float(2 ** 23)), axis=0, keepdims=True)
            s_ref[...] = jnp.where(order == am, NEG, s)
            out.append((jnp.where(rows == k, m, vals), jnp.where(rows == k, am, idxs)))
        return tuple(out)

    zero = jnp.zeros((PEER_TOPK, tb), f32)
    res = lax.fori_loop(0, PEER_TOPK, body, tuple((zero, zero) for _ in chains))
    return [(vals, idxs.astype(jnp.int32)) for vals, idxs in res]


def _cand_blocks():
    blocks = [(0, PEER_TOPK)] + [(a, SUBLANES) for a in range(1, SUBLANES)]
    assert all((a + 1) * (nb + 1) > PEER_TOPK for a, nb in blocks[1:]) and 2 * SUBLANES == PEER_TOPK
    return blocks


def _cast_chunks(n_rows, steps):
    n = 1 << (steps.bit_length() - 1)
    assert n_rows % n == 0
    return n, n_rows // n


def _peer_topk_kernel(q_ref, sk_ref, w_ref, e_ref, g_ref, wb_ref, s1_ref, s2_ref, cand_ref, es_ref, gs_ref, *,
                      n_heads, nkeys):
    tb = q_ref.shape[0]
    wb_ref[...] = w_ref[...].astype(bf16)
    key_order = lax.broadcasted_iota(jnp.int32, (nkeys, tb), 0).astype(f32)
    blocks = _cand_blocks()
    n_mid = (len(blocks) - 1) * SUBLANES
    r = lax.broadcasted_iota(jnp.int32, cand_ref.shape, 0)
    rm = r - PEER_TOPK
    mid = (lax.shift_right_logical(rm, SUBLANES.bit_length() - 1) + 1) * PEER_TOPK + lax.bitwise_and(rm, SUBLANES - 1)
    tail = (rm - n_mid + SUBLANES) * PEER_TOPK
    cand_order = jnp.where(r < PEER_TOPK, r, jnp.where(rm < n_mid, mid, tail)).astype(f32)

    def scores(h):
        col = pl.multiple_of(h * 2 * LANES, 2 * LANES)
        s1_ref[...] = lax.dot_general(sk_ref[0], q_ref[:, pl.ds(col, LANES)], _NT, preferred_element_type=f32)
        s2_ref[...] = lax.dot_general(sk_ref[1], q_ref[:, pl.ds(col + LANES, LANES)], _NT, preferred_element_type=f32)

    sub_chains = [(s1_ref, key_order), (s2_ref, key_order)]
    scores(0)
    (v1, i1), (v2, i2) = _topk_rows(sub_chains, tb)

    def head(h, carry):
        v1, i1, v2, i2 = carry
        r0 = 0
        for a, nb in blocks:
            cand_ref[r0:r0 + nb, :] = v1[a:a + 1, :] + v2[:nb]
            r0 += nb
        cand_ref[r0:r0 + SUBLANES, :] = v1[SUBLANES:] + v2[0:1, :]
        scores(jnp.minimum(h + 1, n_heads - 1))
        nxt1, nxt2, (sc, ci) = _topk_rows(sub_chains + [(cand_ref, cand_order)], tb)
        ca = lax.shift_right_logical(ci, PEER_TOPK.bit_length() - 1)
        cb = lax.bitwise_and(ci, PEER_TOPK - 1)
        e1 = jnp.zeros((PEER_TOPK, tb), jnp.int32)
        e2 = jnp.zeros((PEER_TOPK, tb), jnp.int32)
        for a in range(PEER_TOPK):
            e1 = jnp.where(ca == a, i1[a:a + 1, :], e1)
            e2 = jnp.where(cb == a, i2[a:a + 1, :], e2)
        p = jnp.exp(sc - sc[0:1, :])
        gate = p / jnp.sum(p, axis=0, keepdims=True)
        row = pl.multiple_of(h * PEER_TOPK, PEER_TOPK)
        es_ref[pl.ds(row, PEER_TOPK), :] = e1 * nkeys + e2
        gs_ref[pl.ds(row, PEER_TOPK), :] = gate
        return nxt1 + nxt2

    lax.fori_loop(0, n_heads, head, (v1, i1, v2, i2))
    e_ref[...] = es_ref[...].T
    g_ref[...] = gs_ref[...].T


def _peer_topk(q, sub_keys, n_blocks_rows, w_all, layer):
    t, qd = q.shape
    nkeys = sub_keys.shape[1]
    n_heads = qd // (2 * LANES)
    slots = n_heads * PEER_TOPK
    tb = TOPK_TOKENS
    steps = n_blocks_rows * ROW_BLOCK // tb
    ne, d = w_all.shape[1:]
    n_chunks, chunk = _cast_chunks(ne, steps)
    return pl.pallas_call(
        functools.partial(_peer_topk_kernel, n_heads=n_heads, nkeys=nkeys),
        grid=(steps,),
        in_specs=[pl.BlockSpec((tb, qd), lambda i: (i, 0)),
                  pl.BlockSpec(sub_keys.shape, lambda i: (0, 0, 0)),
                  pl.BlockSpec((None, chunk, d), lambda i: (layer, jnp.minimum(i, n_chunks - 1), 0))],
        out_specs=[pl.BlockSpec((tb, slots), lambda i: (i, 0)),
                   pl.BlockSpec((tb, slots), lambda i: (i, 0)),
                   pl.BlockSpec((chunk, d), lambda i: (jnp.minimum(i, n_chunks - 1), 0))],
        out_shape=[jax.ShapeDtypeStruct((n_blocks_rows * ROW_BLOCK, slots), jnp.int32),
                   jax.ShapeDtypeStruct((n_blocks_rows * ROW_BLOCK, slots), f32),
                   jax.ShapeDtypeStruct((ne, d), bf16)],
        scratch_shapes=[pltpu.VMEM((nkeys, tb), f32), pltpu.VMEM((nkeys, tb), f32),
                        pltpu.VMEM((sum(nb for _, nb in _cand_blocks()) + SUBLANES, tb), f32),
                        pltpu.VMEM((slots, tb), jnp.int32), pltpu.VMEM((slots, tb), f32)],
        compiler_params=_cparams(1),
        name="peer_topk",
    )(q, sub_keys, w_all)


def _peer_gates_kernel(e_ref, g_ref, w_ref, o_ref, wb_ref, *, nkeys):
    tb, slots = e_ref.shape
    wb_ref[...] = w_ref[...].astype(bf16)
    iota = lax.broadcasted_iota(jnp.int32, (nkeys, slots), 0)

    def body(t, carry):
        e = e_ref[pl.ds(t, 1), :]
        g = g_ref[pl.ds(t, 1), :]
        e1 = lax.shift_right_logical(e, nkeys.bit_length() - 1)
        e2 = lax.bitwise_and(e, nkeys - 1)
        w1 = jnp.where(e1 == iota, g, 0.0).astype(bf16)
        o2 = jnp.where(e2 == iota, 1.0, 0.0).astype(bf16)
        res = lax.dot_general(w1, o2, _NT, preferred_element_type=f32)
        bits = lax.bitcast_convert_type(res.astype(bf16).astype(f32), jnp.uint32)
        packed = jnp.bitwise_or(jnp.right_shift(bits[:nkeys // 2], jnp.uint32(16)), bits[nkeys // 2:])
        o_ref[:, t] = packed.reshape(nkeys // (2 * SUBLANES), SUBLANES, nkeys)
        return carry

    lax.fori_loop(0, tb, body, 0, unroll=GATES_UNROLL)


def _peer_gates(e, g, nkeys, n_blocks_rows, w_all, layer):
    t, slots = e.shape
    tb = GATES_TOKENS
    nblk = nkeys // (2 * SUBLANES)
    steps = n_blocks_rows * ROW_BLOCK // tb
    ne, d = w_all.shape[1:]
    n_chunks, chunk = _cast_chunks(ne, steps)
    return pl.pallas_call(
        functools.partial(_peer_gates_kernel, nkeys=nkeys),
        grid=(steps,),
        in_specs=[pl.BlockSpec((tb, slots), lambda i: (i, 0)),
                  pl.BlockSpec((tb, slots), lambda i: (i, 0)),
                  pl.BlockSpec((None, chunk, d), lambda i: (layer, jnp.minimum(i, n_chunks - 1), 0))],
        out_specs=[pl.BlockSpec((nblk, tb, SUBLANES, nkeys), lambda i: (0, i, 0, 0)),
                   pl.BlockSpec((chunk, d), lambda i: (jnp.minimum(i, n_chunks - 1), 0))],
        out_shape=[jax.ShapeDtypeStruct((nblk, t, SUBLANES, nkeys), jnp.uint32),
                   jax.ShapeDtypeStruct((ne, d), bf16)],
        compiler_params=_cparams(1),
        name="peer_gates",
    )(e, g, w_all)


def _gelu_tanh(x):
    k0 = -2.0 * 0.7978845608028654 * 1.4426950408889634
    return x / (1.0 + jnp.exp2(x * (k0 + (k0 * 0.044715) * (x * x))))


def _peer_dense_kernel(f_ref, u_ref, v_ref, g_ref, x_ref, m_ref, fw_ref, o_ref, acc_ref, a_ref, *,
                       gate_row, nkeys, final_norm):
    j = pl.program_id(1)

    @pl.when(j == 0)
    def _():
        acc_ref[...] = jnp.zeros_like(acc_ref)

    slot = j % 2
    a_ref[slot] = lax.dot_general(f_ref[...], u_ref[...], _NT, preferred_element_type=f32)
    tb = f_ref.shape[0]
    n_first = g_ref.shape[0] // tb
    shift = jnp.where(j < pl.num_programs(1) // 2, 16, 0).astype(jnp.uint32)
    parts = []
    for k in range(n_first):
        word = g_ref[pl.ds(k, tb, stride=n_first), :]
        gbits = jnp.bitwise_and(jnp.left_shift(word, shift), jnp.uint32(0xFFFF0000))
        gk = lax.bitcast_convert_type(gbits, f32)
        parts.append((_gelu_tanh(a_ref[slot, :, k * nkeys:(k + 1) * nkeys]) * gk).astype(bf16))
    ga = jnp.concatenate(parts, axis=1)
    acc_ref[...] += jnp.dot(ga, v_ref[...], preferred_element_type=f32)

    @pl.when(j == pl.num_programs(1) - 1)
    def _():
        y = x_ref[...] + m_ref[gate_row:gate_row + 1, :] * acc_ref[...]
        if final_norm:
            y = y * lax.rsqrt(jnp.mean(y * y, axis=-1, keepdims=True) + EPS) * fw_ref[...]
        o_ref[...] = y


def _peer_dense(f, u, v, gates, x, mods, gate_row, n_blocks, bps, n_batch, final_w=None):
    t, d = f.shape
    ne = v.shape[0]
    nkeys = gates.shape[3]
    te = SUBLANES * nkeys
    n_gblk = gates.shape[0]
    assert ne // te == 2 * n_gblk
    gates = gates.reshape(n_gblk, gates.shape[1] * SUBLANES, nkeys)
    return pl.pallas_call(
        functools.partial(_peer_dense_kernel, gate_row=gate_row, nkeys=nkeys, final_norm=final_w is not None),
        grid=(n_blocks, ne // te),
        in_specs=[pl.BlockSpec((ROW_BLOCK, d), lambda i, j: (i, 0)),
                  pl.BlockSpec((te, d), lambda i, j: (j, 0)),
                  pl.BlockSpec((te, d), lambda i, j: (j, 0)),
                  pl.BlockSpec((None, ROW_BLOCK * SUBLANES, nkeys), lambda i, j: (j % n_gblk, i, 0)),
                  pl.BlockSpec((ROW_BLOCK, d), lambda i, j: (i, 0)),
                  pl.BlockSpec((None, SUBLANES, d), lambda i, j: (_group_of(i, bps, n_batch), 0, 0)),
                  pl.BlockSpec((1, d), lambda i, j: (0, 0))],
        out_specs=pl.BlockSpec((ROW_BLOCK, d), lambda i, j: (i, 0)),
        out_shape=jax.ShapeDtypeStruct((t, d), f32),
        scratch_shapes=[pltpu.VMEM((ROW_BLOCK, d), f32), pltpu.VMEM((2, ROW_BLOCK, te), f32)],
        compiler_params=_cparams(2),
        name="peer_dense",
    )(f, u, v, gates, x, mods, (jnp.ones((d,), f32) if final_w is None else final_w).reshape(1, d))


def kernel(x, c, ctx, c_ctx, mod_w, mod_b, norm_w, final_norm_w, a_wqkv, a_q_gain, a_k_gain, a_wo,
           b_wqkv, b_rpb, b_wo, pool_w, pool_scale, peer_wq, peer_sub_keys, peer_u, peer_v):
    n_batch, s, d = x.shape
    c_len = ctx.shape[1]
    depth = mod_w.shape[0]
    assert s % ROW_BLOCK == 0 and (n_batch * c_len) % ROW_BLOCK == 0 and n_batch * c_len <= s
    assert s % c_len == 0 and c_len % POOL_ROW_BLOCK == 0 and s % GRID_W == 0
    assert a_q_gain.shape[1] == LANES and peer_sub_keys.shape[2] == LANES and peer_sub_keys.shape[3] == LANES
    assert PEER_TOPK & (PEER_TOPK - 1) == 0 and peer_sub_keys.shape[2] & (peer_sub_keys.shape[2] - 1) == 0
    assert depth % N_MIXERS != 0

    bps = s // ROW_BLOCK
    nb_lat = n_batch * bps
    nb_all = nb_lat + n_batch * c_len // ROW_BLOCK
    t_lat = n_batch * s
    q_dim = a_wo.shape[1]
    kv_dim = (a_wqkv.shape[2] - q_dim) // 2
    n_kv = kv_dim // LANES
    groups = q_dim // kv_dim
    b_heads = b_rpb.shape[1]
    win_r, win_c = (b_rpb.shape[2] + 1) // 2, (b_rpb.shape[3] + 1) // 2
    nkeys = peer_sub_keys.shape[2]
    scale = SM_SCALE_LOG2

    xs = jnp.concatenate([x.reshape(t_lat, d), ctx.reshape(n_batch * c_len, d)], axis=0)
    cvec = jnp.concatenate([c, c_ctx[None, :], jnp.zeros((SUBLANES - n_batch - 1, d), f32)], axis=0)
    mods_all = _mods(cvec, mod_w, mod_b)
    tables = _rope_tables(s)

    for i in range(depth):
        last = i == depth - 1
        kind, j = i % N_MIXERS, i // N_MIXERS
        nb = nb_lat if last else nb_all
        mods = mods_all[i, :n_batch + 1].reshape(n_batch + 1, N_MOD, d)
        mods = jnp.pad(mods, ((0, 0), (0, SUBLANES - N_MOD), (0, 0)))

        if kind == 2:
            h = _norm_mod(xs, norm_w[i, 0], mods, 0, 1, nb_all, bps, n_batch, f32)
            xs, f = _pool(h, xs, pool_w[j].astype(bf16), pool_scale[j], mods, norm_w[i, 1], 2, 3, 4, n_batch, s, c_len)
        else:
            h = _norm_mod(xs, norm_w[i, 0], mods, 0, 1, nb_all, bps, n_batch, bf16)
            if kind == 0:
                qkv = _qkv_gqa(h, a_wqkv[j].astype(bf16), a_q_gain[j], a_k_gain[j], tables, q_dim, kv_dim,
                               nb_all, bps, n_batch)
                o = _attn_lat(qkv, qkv, qkv, 0, q_dim, q_dim + kv_dim, groups, n_kv, n_batch, s, c_len, q_dim)
                if not last:
                    o_ctx = _attn_ctx(qkv, qkv, qkv, 0, q_dim, q_dim + kv_dim, groups, n_kv, n_batch, s, c_len, 1.0)
                wo = a_wo[j]
            else:
                qkv = _mm(h, b_wqkv[j].astype(bf16), nb_all, _col_tile(3 * d), bf16)
                rows = s // GRID_W
                bias = _natten_bias(b_rpb[j], rows, min(win_r, rows), win_r, win_c)
                o = _natten_lat(qkv, bias, b_heads, n_batch, s, c_len, d)
                if not last:
                    o_ctx = _attn_ctx(qkv, qkv, qkv, 0, d, 2 * d, 1, b_heads, n_batch, s, c_len, scale)
                wo = b_wo[j]
            if last:
                o_ctx = o
            xs, f = _mm_res(o, o_ctx, wo.astype(bf16), xs, mods, norm_w[i, 1], 2, 3, 4, nb, bps, n_batch)

        qp = _mm(f, peer_wq[i].astype(bf16), nb, _col_tile(peer_wq.shape[2]), f32)
        e, g, u_bf = _peer_topk(qp, peer_sub_keys[i], nb, peer_u, i)
        gates, v_bf = _peer_gates(e, g, nkeys, nb, peer_v, i)
        xs = _peer_dense(f, u_bf, v_bf, gates, xs, mods, 5, nb, bps, n_batch, final_norm_w if last else None)

    return xs.reshape(n_batch, s, d)
```

```python
import functools

import numpy as np
import jax
import jax.numpy as jnp
from jax import lax
from jax.experimental import pallas as pl
from jax.experimental.pallas import tpu as pltpu

GRID_W = 64
ROPE_BASE = 10000.0
POOL_WINDOWS = (2, 4, 8, 16)
PEER_TOPK = 16
EPS = 1e-6
N_MIXERS = 3
N_MOD = 6

LANES = 128
SUBLANES = 8
ROW_BLOCK = 512
POOL_ROW_BLOCK = 256
POOL_HALO = 8
MM_COL_TILE = 2048
ATTN_Q_ROWS = 128
ATTN_Q_SUBBLOCKS = 2
NATTEN_UNROLL = 16
TOPK_TOKENS = 512
GATES_TOKENS = 256
GATES_UNROLL = 128
VMEM_LIMIT = 56 * 1024 * 1024
NEG = -1e30
LOG2E = 1.4426950408889634
SM_SCALE_LOG2 = float(LANES) ** -0.5 * LOG2E

f32 = jnp.float32
bf16 = jnp.bfloat16
_NT = (((1,), (1,)), ((), ()))


def _cparams(n_axes):
    return pltpu.CompilerParams(dimension_semantics=("arbitrary",) * n_axes, vmem_limit_bytes=VMEM_LIMIT)


def _group_of(i, blocks_per_seq, n_batch):
    return jnp.minimum(i // blocks_per_seq, n_batch)


def _mods_kernel(cv_ref, w_ref, b_ref, o_ref):
    cv = cv_ref[...]
    a = (cv / (1.0 + jnp.exp(-cv))).astype(bf16)
    o_ref[...] = jnp.dot(a, w_ref[...].astype(bf16), preferred_element_type=f32) + b_ref[...]


def _mods(cvec, mod_w, mod_b):
    depth, d, n = mod_w.shape
    tn = 1024
    return pl.pallas_call(
        _mods_kernel,
        grid=(depth, n // tn),
        in_specs=[pl.BlockSpec((SUBLANES, d), lambda l, j: (0, 0)),
                  pl.BlockSpec((None, d, tn), lambda l, j: (l, 0, j)),
                  pl.BlockSpec((None, 1, tn), lambda l, j: (l, 0, j))],
        out_specs=pl.BlockSpec((None, SUBLANES, tn), lambda l, j: (l, 0, j)),
        out_shape=jax.ShapeDtypeStruct((depth, SUBLANES, n), f32),
        compiler_params=_cparams(2),
        name="adaln_mods",
    )(cvec, mod_w, mod_b.reshape(depth, 1, n))


def _norm_mod_value(x, nw, m_ref, shift_row, scale_row):
    y = x * lax.rsqrt(jnp.mean(x * x, axis=-1, keepdims=True) + EPS) * nw
    return y * (1.0 + m_ref[scale_row:scale_row + 1, :]) + m_ref[shift_row:shift_row + 1, :]


def _norm_mod_kernel(x_ref, nw_ref, m_ref, o_ref, *, shift_row, scale_row):
    o_ref[...] = _norm_mod_value(x_ref[...], nw_ref[...], m_ref, shift_row, scale_row).astype(o_ref.dtype)


def _norm_mod(x, nw, mods, shift_row, scale_row, n_blocks, bps, n_batch, out_dtype):
    t, d = x.shape
    return pl.pallas_call(
        functools.partial(_norm_mod_kernel, shift_row=shift_row, scale_row=scale_row),
        grid=(n_blocks,),
        in_specs=[pl.BlockSpec((ROW_BLOCK, d), lambda i: (i, 0)),
                  pl.BlockSpec((1, d), lambda i: (0, 0)),
                  pl.BlockSpec((None, SUBLANES, d), lambda i: (_group_of(i, bps, n_batch), 0, 0))],
        out_specs=pl.BlockSpec((ROW_BLOCK, d), lambda i: (i, 0)),
        out_shape=jax.ShapeDtypeStruct((n_blocks * ROW_BLOCK, d), out_dtype),
        compiler_params=_cparams(1),
        name="norm_mod",
    )(x, nw.reshape(1, d), mods)


def _mm_kernel(a_ref, w_ref, o_ref):
    o_ref[...] = jnp.dot(a_ref[...], w_ref[...], preferred_element_type=f32).astype(o_ref.dtype)


def _col_tile(n):
    return max(t for t in range(LANES, min(n, MM_COL_TILE) + 1, LANES) if n % t == 0)


def _mm(a, w, n_blocks, tn, out_dtype):
    t, k = a.shape
    n = w.shape[1]
    return pl.pallas_call(
        _mm_kernel,
        grid=(n // tn, n_blocks),
        in_specs=[pl.BlockSpec((ROW_BLOCK, k), lambda j, i: (i, 0)),
                  pl.BlockSpec((k, tn), lambda j, i: (0, j))],
        out_specs=pl.BlockSpec((ROW_BLOCK, tn), lambda j, i: (i, j)),
        out_shape=jax.ShapeDtypeStruct((n_blocks * ROW_BLOCK, n), out_dtype),
        compiler_params=_cparams(2),
        name="matmul",
    )(a, w)


def _mm_res_kernel(al_ref, ac_ref, w_ref, x_ref, m_ref, nw_ref, o_ref, f_ref, *,
                   gate_row, shift_row, scale_row, n_lat_blocks):
    a = jnp.where(pl.program_id(0) < n_lat_blocks, al_ref[...], ac_ref[...])
    y = x_ref[...] + m_ref[gate_row:gate_row + 1, :] * jnp.dot(a, w_ref[...], preferred_element_type=f32)
    o_ref[...] = y
    f_ref[...] = _norm_mod_value(y, nw_ref[...], m_ref, shift_row, scale_row).astype(f_ref.dtype)


def _mm_res(a_lat, a_ctx, w, x, mods, nw, gate_row, shift_row, scale_row, n_blocks, bps, n_batch):
    k = a_lat.shape[1]
    n = w.shape[1]
    rows = n_blocks * ROW_BLOCK
    n_lat_blocks = a_lat.shape[0] // ROW_BLOCK
    assert n_blocks == n_lat_blocks or n_blocks == n_lat_blocks + a_ctx.shape[0] // ROW_BLOCK
    return pl.pallas_call(
        functools.partial(_mm_res_kernel, gate_row=gate_row, shift_row=shift_row, scale_row=scale_row,
                          n_lat_blocks=n_lat_blocks),
        grid=(n_blocks,),
        in_specs=[pl.BlockSpec((ROW_BLOCK, k), lambda i: (jnp.minimum(i, n_lat_blocks - 1), 0)),
                  pl.BlockSpec((ROW_BLOCK, k), lambda i: (jnp.maximum(i - n_lat_blocks, 0), 0)),
                  pl.BlockSpec((k, n), lambda i: (0, 0)),
                  pl.BlockSpec((ROW_BLOCK, n), lambda i: (i, 0)),
                  pl.BlockSpec((None, SUBLANES, n), lambda i: (_group_of(i, bps, n_batch), 0, 0)),
                  pl.BlockSpec((1, n), lambda i: (0, 0))],
        out_specs=[pl.BlockSpec((ROW_BLOCK, n), lambda i: (i, 0)), pl.BlockSpec((ROW_BLOCK, n), lambda i: (i, 0))],
        out_shape=[jax.ShapeDtypeStruct((rows, n), f32), jax.ShapeDtypeStruct((rows, n), bf16)],
        compiler_params=_cparams(1),
        name="matmul_residual",
    )(a_lat, a_ctx, w, x, mods, nw.reshape(1, n))


def _qkv_gqa_kernel(a_ref, w_ref, qg_ref, kg_ref, cos_ref, sin_ref, o_ref, *, nq_tiles, nk_tiles, scale):
    j = pl.program_id(0)
    acc = jnp.dot(a_ref[...], w_ref[...], preferred_element_type=f32)
    heads = acc.shape[1] // LANES

    def prep(gain, post):
        cos, sin = cos_ref[...], sin_ref[...]
        for h in range(heads):
            y = acc[:, h * LANES:(h + 1) * LANES]
            y = y * lax.rsqrt(jnp.mean(y * y, axis=-1, keepdims=True) + EPS) * gain
            y = y * cos + pltpu.roll(y, LANES // 2, 1) * sin
            o_ref[:, h * LANES:(h + 1) * LANES] = (y * post).astype(o_ref.dtype)

    @pl.when(j < nq_tiles)
    def _():
        prep(qg_ref[...], scale)

    @pl.when(jnp.logical_and(j >= nq_tiles, j < nq_tiles + nk_tiles))
    def _():
        prep(kg_ref[...], 1.0)

    @pl.when(j >= nq_tiles + nk_tiles)
    def _():
        o_ref[...] = acc.astype(o_ref.dtype)


def _rope_head_perm(a):
    quarter = LANES // 4
    shape = a.shape
    a = a.reshape(shape[:-1] + (shape[-1] // LANES, 2, 2, quarter))
    return jnp.swapaxes(a, -3, -2).reshape(shape)


def _rope_tables(s):
    t = np.arange(s)
    half = LANES // 2
    inv = ROPE_BASE ** (-jnp.arange(0, half, 2, dtype=f32) / half)
    ang_r = (t // GRID_W).astype(np.float32)[:, None] * inv[None, :]
    ang_c = (t % GRID_W).astype(np.float32)[:, None] * inv[None, :]
    ang = jnp.concatenate([ang_r, ang_c], axis=-1)
    cos = jnp.concatenate([jnp.cos(ang), jnp.cos(ang)], axis=-1)
    sin = jnp.concatenate([-jnp.sin(ang), jnp.sin(ang)], axis=-1)
    pad1 = jnp.ones((ROW_BLOCK, LANES), f32)
    pad0 = jnp.zeros((ROW_BLOCK, LANES), f32)
    return jnp.concatenate([cos, pad1], 0), jnp.concatenate([sin, pad0], 0)


def _qkv_gqa(h, wqkv, q_gain, k_gain, tables, q_dim, kv_dim, n_blocks, bps, n_batch):
    t, k = h.shape
    n = wqkv.shape[1]
    tn = min(512, kv_dim)
    cos, sin = tables
    wqkv = jnp.concatenate([_rope_head_perm(wqkv[:, :q_dim + kv_dim]), wqkv[:, q_dim + kv_dim:]], axis=1)
    q_gain, k_gain = _rope_head_perm(q_gain), _rope_head_perm(k_gain)

    def pos_map(j, i):
        return (jnp.where(i < bps * n_batch, i % bps, bps), 0)

    return pl.pallas_call(
        functools.partial(_qkv_gqa_kernel, nq_tiles=q_dim // tn, nk_tiles=kv_dim // tn, scale=SM_SCALE_LOG2),
        grid=(n // tn, n_blocks),
        in_specs=[pl.BlockSpec((ROW_BLOCK, k), lambda j, i: (i, 0)),
                  pl.BlockSpec((k, tn), lambda j, i: (0, j)),
                  pl.BlockSpec((1, LANES), lambda j, i: (0, 0)),
                  pl.BlockSpec((1, LANES), lambda j, i: (0, 0)),
                  pl.BlockSpec((ROW_BLOCK, LANES), pos_map),
                  pl.BlockSpec((ROW_BLOCK, LANES), pos_map)],
        out_specs=pl.BlockSpec((ROW_BLOCK, tn), lambda j, i: (i, j)),
        out_shape=jax.ShapeDtypeStruct((t, n), bf16),
        compiler_params=_cparams(2),
        name="qkv_gqa",
    )(h, wqkv, q_gain.reshape(1, LANES), k_gain.reshape(1, LANES), cos, sin)


def _flash_kernel(*refs, groups, n_lat_chunks, ck, scale):
    if n_lat_chunks:
        q_ref, kc_ref, vc_ref, kl_ref, vl_ref, o_ref = refs
    else:
        q_ref, kc_ref, vc_ref, o_ref = refs
    tq = q_ref.shape[0]
    q = jnp.concatenate([q_ref[:, g * LANES:(g + 1) * LANES] for g in range(groups)], axis=0)
    rows = groups * tq

    def step(k, v, m, l, acc):
        s = lax.dot_general(q, k, _NT, preferred_element_type=f32)
        if scale != 1.0:
            s = s * scale
        m_new = jnp.maximum(m, jnp.max(s, axis=-1, keepdims=True))
        p = jnp.exp2(s - m_new)
        alpha = jnp.exp2(m - m_new)
        l = alpha * l + jnp.sum(p, axis=-1, keepdims=True)
        acc = alpha * acc + jnp.dot(p.astype(bf16), v, preferred_element_type=f32)
        return m_new, l, acc

    carry = step(kc_ref[...], vc_ref[...], jnp.full((rows, 1), NEG, f32), jnp.zeros((rows, 1), f32),
                 jnp.zeros((rows, LANES), f32))
    if n_lat_chunks:
        def body(c, carry):
            off = pl.multiple_of(c * ck, ck)
            return step(kl_ref[pl.ds(off, ck), :], vl_ref[pl.ds(off, ck), :], *carry)
        carry = lax.fori_loop(0, n_lat_chunks, body, carry)
    _, l, acc = carry
    o = acc / l
    for g in range(groups):
        o_ref[:, g * LANES:(g + 1) * LANES] = o[g * tq:(g + 1) * tq].astype(o_ref.dtype)


def _gqa_lat_kernel(q_ref, kc_ref, vc_ref, kl_ref, vl_ref, o_ref, *scratch, groups, n_sub):
    tq = q_ref.shape[0] // n_sub
    vcx_ref, vlx_ref = scratch[4 * n_sub:]

    @pl.when(pl.program_id(2) == 0)
    def _():
        for src, dst in ((vc_ref, vcx_ref), (vl_ref, vlx_ref)):
            dst[:, :LANES] = src[...]
            dst[:, LANES:] = jnp.ones((src.shape[0], LANES), bf16)

    for h in range(n_sub):
        sc_ref, sl_ref, pc_ref, pl_ref = scratch[4 * h:4 * h + 4]
        q = jnp.concatenate([q_ref[h * tq:(h + 1) * tq, g * LANES:(g + 1) * LANES] for g in range(groups)], axis=0)
        sc_ref[...] = lax.dot_general(q, kc_ref[...], _NT, preferred_element_type=f32)
        sl_ref[...] = lax.dot_general(q, kl_ref[...], _NT, preferred_element_type=f32)
        m = jnp.maximum(jnp.max(sc_ref[...], axis=-1, keepdims=True), jnp.max(sl_ref[...], axis=-1, keepdims=True))
        pc_ref[...] = jnp.exp2((sc_ref[...] - m).astype(bf16))
        pl_ref[...] = jnp.exp2((sl_ref[...] - m).astype(bf16))
    for h in range(n_sub):
        pc_ref, pl_ref = scratch[4 * h + 2:4 * h + 4]
        ox = (jnp.dot(pc_ref[...], vcx_ref[...], preferred_element_type=f32)
              + jnp.dot(pl_ref[...], vlx_ref[...], preferred_element_type=f32))
        o = ox[:, :LANES] / ox[:, LANES:]
        for g in range(groups):
            o_ref[h * tq:(h + 1) * tq, g * LANES:(g + 1) * LANES] = o[g * tq:(g + 1) * tq].astype(o_ref.dtype)


def _attn_lat(q, k, v, q_col0, k_col0, v_col0, groups, n_kv, n_batch, s, c, out_cols):
    n_sub = ATTN_Q_SUBBLOCKS
    tq = n_sub * ATTN_Q_ROWS
    nqb = s // tq
    ctx_blk0 = n_batch * s // c
    qw = groups * LANES
    return pl.pallas_call(
        functools.partial(_gqa_lat_kernel, groups=groups, n_sub=n_sub),
        grid=(n_batch, n_kv, nqb),
        in_specs=[pl.BlockSpec((tq, qw), lambda b, h, i: (b * nqb + i, q_col0 // qw + h)),
                  pl.BlockSpec((c, LANES), lambda b, h, i: (ctx_blk0 + b, k_col0 // LANES + h)),
                  pl.BlockSpec((c, LANES), lambda b, h, i: (ctx_blk0 + b, v_col0 // LANES + h)),
                  pl.BlockSpec((s, LANES), lambda b, h, i: (b, k_col0 // LANES + h)),
                  pl.BlockSpec((s, LANES), lambda b, h, i: (b, v_col0 // LANES + h))],
        out_specs=pl.BlockSpec((tq, qw), lambda b, h, i: (b * nqb + i, h)),
        out_shape=jax.ShapeDtypeStruct((n_batch * s, out_cols), bf16),
        scratch_shapes=[pltpu.VMEM((groups * ATTN_Q_ROWS, n), dt)
                        for _ in range(n_sub) for dt in (f32, bf16) for n in (c, s)]
                       + [pltpu.VMEM((c, 2 * LANES), bf16), pltpu.VMEM((s, 2 * LANES), bf16)],
        compiler_params=_cparams(3),
        name="attn_latent",
    )(q, k, v, k, v)


def _attn_ctx(q, k, v, q_col0, k_col0, v_col0, groups, n_kv, n_batch, s, c, scale):
    ctx_blk0 = n_batch * s // c
    qw = groups * LANES
    return pl.pallas_call(
        functools.partial(_flash_kernel, groups=groups, n_lat_chunks=0, ck=0, scale=scale),
        grid=(n_batch, n_kv),
        in_specs=[pl.BlockSpec((c, qw), lambda b, h: (ctx_blk0 + b, q_col0 // qw + h)),
                  pl.BlockSpec((c, LANES), lambda b, h: (ctx_blk0 + b, k_col0 // LANES + h)),
                  pl.BlockSpec((c, LANES), lambda b, h: (ctx_blk0 + b, v_col0 // LANES + h))],
        out_specs=pl.BlockSpec((c, qw), lambda b, h: (b, h)),
        out_shape=jax.ShapeDtypeStruct((n_batch * c, n_kv * qw), bf16),
        compiler_params=_cparams(2),
        name="attn_context",
    )(q, k, v)


def _natten_kernel(q_ref, k_ref, v_ref, kc_ref, vc_ref, bias_ref, o_ref, *, rows, w, wr, scale):
    kc = kc_ref[...]
    vc = vc_ref[...]

    nr = NATTEN_UNROLL

    def body(g, carry):
        base = pl.multiple_of(g * nr * w, nr * w)
        qg = q_ref[pl.ds(base, nr * w), :]
        scg = lax.dot_general(qg, kc, _NT, preferred_element_type=f32) * scale
        offs, sws = [], []
        for i in range(nr):
            r = g * nr + i
            r0 = jnp.clip(r - wr // 2, 0, rows - wr)
            off = pl.multiple_of(r0 * w, w)
            kb = k_ref[pl.ds(off, wr * w), :]
            sws.append(lax.dot_general(qg[i * w:(i + 1) * w], kb, _NT, preferred_element_type=f32) * scale
                       + bias_ref[r - r0])
            offs.append(off)
        pws, pcs, ls = [], [], []
        for i in range(nr):
            sc = scg[i * w:(i + 1) * w]
            m = jnp.maximum(jnp.max(sws[i], axis=-1, keepdims=True), jnp.max(sc, axis=-1, keepdims=True))
            pw = jnp.exp2(sws[i] - m)
            pc = jnp.exp2(sc - m)
            ls.append(jnp.sum(pw, axis=-1, keepdims=True) + jnp.sum(pc, axis=-1, keepdims=True))
            pws.append(pw.astype(bf16))
            pcs.append(pc.astype(bf16))
        ocg = jnp.dot(jnp.concatenate(pcs, axis=0), vc, preferred_element_type=f32)
        for i in range(nr):
            vb = v_ref[pl.ds(offs[i], wr * w), :]
            o = (jnp.dot(pws[i], vb, preferred_element_type=f32) + ocg[i * w:(i + 1) * w]) / ls[i]
            o_ref[pl.ds(pl.multiple_of(base + i * w, w), w), :] = o.astype(o_ref.dtype)
        return carry

    lax.fori_loop(0, rows // nr, body, 0)


def _natten_bias(rpb, rows, wr, win_r, win_c):
    w = GRID_W
    cols = np.arange(w)
    col_start = np.clip(cols - win_c // 2, 0, w - win_c)
    inside = (cols[None, :] >= col_start[:, None]) & (cols[None, :] < col_start[:, None] + win_c)
    rpbp = jnp.pad(rpb, ((0, 0), (0, 0), (w - win_c, w - win_c)))
    p = jnp.stack([rpbp[:, :, w - 1 - qc:2 * w - 1 - qc] for qc in range(w)], axis=2)
    p = jnp.where(inside[None, None], p * LOG2E, NEG)
    tab = jnp.stack([p[:, win_r - 1 - d:win_r - 1 - d + wr] for d in range(wr)], axis=1)
    return jnp.transpose(tab, (0, 1, 3, 2, 4)).reshape(rpb.shape[0], wr, w, wr * w)


def _natten_lat(qkv, bias, n_heads, n_batch, s, c, d_model):
    t = qkv.shape[0]
    rows = s // GRID_W
    wr = bias.shape[1]
    ctx_blk0 = n_batch * s // c
    hq, hk, hv = 0, d_model // LANES, 2 * d_model // LANES
    return pl.pallas_call(
        functools.partial(_natten_kernel, rows=rows, w=GRID_W, wr=wr, scale=SM_SCALE_LOG2),
        grid=(n_batch, n_heads),
        in_specs=[pl.BlockSpec((s, LANES), lambda b, h: (b, hq + h)),
                  pl.BlockSpec((s, LANES), lambda b, h: (b, hk + h)),
                  pl.BlockSpec((s, LANES), lambda b, h: (b, hv + h)),
                  pl.BlockSpec((c, LANES), lambda b, h: (ctx_blk0 + b, hk + h)),
                  pl.BlockSpec((c, LANES), lambda b, h: (ctx_blk0 + b, hv + h)),
                  pl.BlockSpec((None, wr, GRID_W, wr * GRID_W), lambda b, h: (h, 0, 0, 0))],
        out_specs=pl.BlockSpec((s, LANES), lambda b, h: (b, h)),
        out_shape=jax.ShapeDtypeStruct((n_batch * s, d_model), bf16),
        compiler_params=_cparams(2),
        name="natten_latent",
    )(qkv, qkv, qkv, qkv, qkv, bias)


def _pool_kernel(h_ref, hp_ref, hn_ref, x_ref, w_ref, ls_ref, m_ref, nw_ref, o_ref, f_ref, pad_ref, *,
                 bps_lat, bps_ctx, n_lat_blocks, s, c, gate_row, shift_row, scale_row):
    i = pl.program_id(0)
    tm = h_ref.shape[0]
    is_lat = i < n_lat_blocks
    blk = jnp.where(is_lat, i % bps_lat, (i - n_lat_blocks) % bps_ctx)
    nblk = jnp.where(is_lat, bps_lat, bps_ctx)
    length = jnp.where(is_lat, s, c)
    pad_ref[0:POOL_HALO, :] = jnp.where(blk == 0, 0.0, hp_ref[...])
    pad_ref[POOL_HALO:POOL_HALO + tm, :] = h_ref[...]
    pad_ref[POOL_HALO + tm:2 * POOL_HALO + tm, :] = jnp.where(blk == nblk - 1, 0.0, hn_ref[...])
    pos = blk * tm + lax.broadcasted_iota(jnp.int32, (tm, 1), 0)
    pg = w_ref.shape[1]
    for g, win in enumerate(POOL_WINDOWS):
        cs = slice(g * pg, (g + 1) * pg)
        lo_off, hi_off = win // 2, win - win // 2
        acc = pad_ref[POOL_HALO - lo_off:POOL_HALO - lo_off + tm, cs]
        for k in range(-lo_off + 1, hi_off):
            acc = acc + pad_ref[POOL_HALO + k:POOL_HALO + k + tm, cs]
        cnt = jnp.minimum(pos + hi_off, length) - jnp.maximum(pos - lo_off, 0)
        y = acc * (1.0 / cnt.astype(f32)) - h_ref[:, cs]
        z = jnp.dot(y.astype(bf16), w_ref[g], preferred_element_type=f32) * ls_ref[:, cs]
        o_ref[:, cs] = x_ref[:, cs] + m_ref[gate_row:gate_row + 1, cs] * z
    f_ref[...] = _norm_mod_value(o_ref[...], nw_ref[...], m_ref, shift_row, scale_row).astype(f_ref.dtype)


def _pool(h, x, pool_w, pool_scale, mods, nw, gate_row, shift_row, scale_row, n_batch, s, c):
    t, d = h.shape
    tm = POOL_ROW_BLOCK
    nb = t // tm
    bps_lat, bps_ctx = s // tm, c // tm
    hb = tm // POOL_HALO
    last_halo = t // POOL_HALO - 1
    return pl.pallas_call(
        functools.partial(_pool_kernel, bps_lat=bps_lat, bps_ctx=bps_ctx, n_lat_blocks=n_batch * bps_lat,
                          s=s, c=c, gate_row=gate_row, shift_row=shift_row, scale_row=scale_row),
        grid=(nb,),
        in_specs=[pl.BlockSpec((tm, d), lambda i: (i, 0)),
                  pl.BlockSpec((POOL_HALO, d), lambda i: (jnp.maximum(i * hb - 1, 0), 0)),
                  pl.BlockSpec((POOL_HALO, d), lambda i: (jnp.minimum((i + 1) * hb, last_halo), 0)),
                  pl.BlockSpec((tm, d), lambda i: (i, 0)),
                  pl.BlockSpec(pool_w.shape, lambda i: (0, 0, 0)),
                  pl.BlockSpec((1, d), lambda i: (0, 0)),
                  pl.BlockSpec((None, SUBLANES, d), lambda i: (_group_of(i, bps_lat, n_batch), 0, 0)),
                  pl.BlockSpec((1, d), lambda i: (0, 0))],
        out_specs=[pl.BlockSpec((tm, d), lambda i: (i, 0)), pl.BlockSpec((tm, d), lambda i: (i, 0))],
        out_shape=[jax.ShapeDtypeStruct((t, d), f32), jax.ShapeDtypeStruct((t, d), bf16)],
        scratch_shapes=[pltpu.VMEM((tm + 2 * POOL_HALO, d), f32)],
        compiler_params=_cparams(1),
        name="pool_mixer",
    )(h, h, h, x, pool_w, pool_scale.reshape(1, d), mods, nw.reshape(1, d))


def _topk_rows(chains, res_ref):
    def body(k, carry):
        for c, (s_ref, order) in enumerate(chains):
            s = s_ref[...]
            m = jnp.max(s, axis=0, keepdims=True)
            am = jnp.min(jnp.where(s == m, order, float(2 ** 23)), axis=0, keepdims=True)
            s_ref[...] = jnp.where(order == am, NEG, s)
            res_ref[c, 0, pl.ds(k, 1), :] = m
            res_ref[c, 1, pl.ds(k, 1), :] = am
        return carry

    lax.fori_loop(0, PEER_TOPK, body, 0)
    return [(res_ref[c, 0], res_ref[c, 1].astype(jnp.int32)) for c in range(len(chains))]


def _cand_blocks():
    blocks = [(0, PEER_TOPK)] + [(a, SUBLANES) for a in range(1, SUBLANES)]
    assert all((a + 1) * (nb + 1) > PEER_TOPK for a, nb in blocks[1:]) and 2 * SUBLANES == PEER_TOPK
    return blocks


def _cast_chunks(n_rows, steps):
    n = 1 << (steps.bit_length() - 1)
    assert n_rows % n == 0
    return n, n_rows // n


def _peer_topk_kernel(q_ref, sk_ref, w_ref, e_ref, g_ref, wb_ref, s1_ref, s2_ref, cand_ref, es_ref, gs_ref, res_ref, *,
                      n_heads, nkeys):
    tb = q_ref.shape[0]
    wb_ref[...] = w_ref[...].astype(bf16)
    key_order = lax.broadcasted_iota(jnp.int32, (nkeys, tb), 0).astype(f32)
    blocks = _cand_blocks()
    n_mid = (len(blocks) - 1) * SUBLANES
    r = lax.broadcasted_iota(jnp.int32, cand_ref.shape, 0)
    rm = r - PEER_TOPK
    mid = (lax.shift_right_logical(rm, SUBLANES.bit_length() - 1) + 1) * PEER_TOPK + lax.bitwise_and(rm, SUBLANES - 1)
    tail = (rm - n_mid + SUBLANES) * PEER_TOPK
    cand_order = jnp.where(r < PEER_TOPK, r, jnp.where(rm < n_mid, mid, tail)).astype(f32)

    def scores(h):
        col = pl.multiple_of(h * 2 * LANES, 2 * LANES)
        s1_ref[...] = lax.dot_general(sk_ref[0], q_ref[:, pl.ds(col, LANES)], _NT, preferred_element_type=f32)
        s2_ref[...] = lax.dot_general(sk_ref[1], q_ref[:, pl.ds(col + LANES, LANES)], _NT, preferred_element_type=f32)

    sub_chains = [(s1_ref, key_order), (s2_ref, key_order)]
    scores(0)
    (v1, i1), (v2, i2) = _topk_rows(sub_chains, res_ref)

    def head(h, carry):
        v1, i1, v2, i2 = carry
        r0 = 0
        for a, nb in blocks:
            cand_ref[r0:r0 + nb, :] = v1[a:a + 1, :] + v2[:nb]
            r0 += nb
        cand_ref[r0:r0 + SUBLANES, :] = v1[SUBLANES:] + v2[0:1, :]
        scores(jnp.minimum(h + 1, n_heads - 1))
        nxt1, nxt2, (sc, ci) = _topk_rows(sub_chains + [(cand_ref, cand_order)], res_ref)
        ca = lax.shift_right_logical(ci, PEER_TOPK.bit_length() - 1)
        cb = lax.bitwise_and(ci, PEER_TOPK - 1)
        e1 = jnp.zeros((PEER_TOPK, tb), jnp.int32)
        e2 = jnp.zeros((PEER_TOPK, tb), jnp.int32)
        for a in range(PEER_TOPK):
            e1 = jnp.where(ca == a, i1[a:a + 1, :], e1)
            e2 = jnp.where(cb == a, i2[a:a + 1, :], e2)
        p = jnp.exp(sc - sc[0:1, :])
        gate = p / jnp.sum(p, axis=0, keepdims=True)
        row = pl.multiple_of(h * PEER_TOPK, PEER_TOPK)
        es_ref[pl.ds(row, PEER_TOPK), :] = e1 * nkeys + e2
        gs_ref[pl.ds(row, PEER_TOPK), :] = gate
        return nxt1 + nxt2

    lax.fori_loop(0, n_heads, head, (v1, i1, v2, i2))
    e_ref[...] = es_ref[...].T
    g_ref[...] = gs_ref[...].T


def _peer_topk(q, sub_keys, n_blocks_rows, w_all, layer):
    t, qd = q.shape
    nkeys = sub_keys.shape[1]
    n_heads = qd // (2 * LANES)
    slots = n_heads * PEER_TOPK
    tb = TOPK_TOKENS
    steps = n_blocks_rows * ROW_BLOCK // tb
    ne, d = w_all.shape[1:]
    n_chunks, chunk = _cast_chunks(ne, steps)
    return pl.pallas_call(
        functools.partial(_peer_topk_kernel, n_heads=n_heads, nkeys=nkeys),
        grid=(steps,),
        in_specs=[pl.BlockSpec((tb, qd), lambda i: (i, 0)),
                  pl.BlockSpec(sub_keys.shape, lambda i: (0, 0, 0)),
                  pl.BlockSpec((None, chunk, d), lambda i: (layer, jnp.minimum(i, n_chunks - 1), 0))],
        out_specs=[pl.BlockSpec((tb, slots), lambda i: (i, 0)),
                   pl.BlockSpec((tb, slots), lambda i: (i, 0)),
                   pl.BlockSpec((chunk, d), lambda i: (jnp.minimum(i, n_chunks - 1), 0))],
        out_shape=[jax.ShapeDtypeStruct((n_blocks_rows * ROW_BLOCK, slots), jnp.int32),
                   jax.ShapeDtypeStruct((n_blocks_rows * ROW_BLOCK, slots), f32),
                   jax.ShapeDtypeStruct((ne, d), bf16)],
        scratch_shapes=[pltpu.VMEM((nkeys, tb), f32), pltpu.VMEM((nkeys, tb), f32),
                        pltpu.VMEM((sum(nb for _, nb in _cand_blocks()) + SUBLANES, tb), f32),
                        pltpu.VMEM((slots, tb), jnp.int32), pltpu.VMEM((slots, tb), f32),
                        pltpu.VMEM((3, 2, PEER_TOPK, tb), f32)],
        compiler_params=_cparams(1),
        name="peer_topk",
    )(q, sub_keys, w_all)


def _peer_gates_kernel(e_ref, g_ref, w_ref, o_ref, wb_ref, *, nkeys):
    tb, slots = e_ref.shape
    wb_ref[...] = w_ref[...].astype(bf16)
    iota = lax.broadcasted_iota(jnp.int32, (nkeys, slots), 0)

    def body(t, carry):
        e = e_ref[pl.ds(t, 1), :]
        g = g_ref[pl.ds(t, 1), :]
        e1 = lax.shift_right_logical(e, nkeys.bit_length() - 1)
        e2 = lax.bitwise_and(e, nkeys - 1)
        w1 = jnp.where(e1 == iota, g, 0.0).astype(bf16)
        o2 = jnp.where(e2 == iota, 1.0, 0.0).astype(bf16)
        res = lax.dot_general(w1, o2, _NT, preferred_element_type=f32)
        bits = lax.bitcast_convert_type(res.astype(bf16).astype(f32), jnp.uint32)
        packed = jnp.bitwise_or(jnp.right_shift(bits[:nkeys // 2], jnp.uint32(16)), bits[nkeys // 2:])
        o_ref[:, t] = packed.reshape(nkeys // (2 * SUBLANES), SUBLANES, nkeys)
        return carry

    lax.fori_loop(0, tb, body, 0, unroll=GATES_UNROLL)


def _peer_gates(e, g, nkeys, n_blocks_rows, w_all, layer):
    t, slots = e.shape
    tb = GATES_TOKENS
    nblk = nkeys // (2 * SUBLANES)
    steps = n_blocks_rows * ROW_BLOCK // tb
    ne, d = w_all.shape[1:]
    n_chunks, chunk = _cast_chunks(ne, steps)
    return pl.pallas_call(
        functools.partial(_peer_gates_kernel, nkeys=nkeys),
        grid=(steps,),
        in_specs=[pl.BlockSpec((tb, slots), lambda i: (i, 0)),
                  pl.BlockSpec((tb, slots), lambda i: (i, 0)),
                  pl.BlockSpec((None, chunk, d), lambda i: (layer, jnp.minimum(i, n_chunks - 1), 0))],
        out_specs=[pl.BlockSpec((nblk, tb, SUBLANES, nkeys), lambda i: (0, i, 0, 0)),
                   pl.BlockSpec((chunk, d), lambda i: (jnp.minimum(i, n_chunks - 1), 0))],
        out_shape=[jax.ShapeDtypeStruct((nblk, t, SUBLANES, nkeys), jnp.uint32),
                   jax.ShapeDtypeStruct((ne, d), bf16)],
        compiler_params=_cparams(1),
        name="peer_gates",
    )(e, g, w_all)


def _gelu_tanh(x):
    k0 = -2.0 * 0.7978845608028654 * 1.4426950408889634
    return x / (1.0 + jnp.exp2(x * (k0 + (k0 * 0.044715) * (x * x))))


def _peer_dense_kernel(f_ref, u_ref, v_ref, g_ref, x_ref, m_ref, fw_ref, o_ref, acc_ref, a_ref, *,
                       gate_row, nkeys, final_norm):
    j = pl.program_id(1)

    @pl.when(j == 0)
    def _():
        acc_ref[...] = jnp.zeros_like(acc_ref)

    slot = j % 2
    a_ref[slot] = lax.dot_general(f_ref[...], u_ref[...], _NT, preferred_element_type=f32)
    tb = f_ref.shape[0]
    n_first = g_ref.shape[0] // tb
    shift = jnp.where(j < pl.num_programs(1) // 2, 16, 0).astype(jnp.uint32)
    parts = []
    for k in range(n_first):
        word = g_ref[pl.ds(k, tb, stride=n_first), :]
        gbits = jnp.bitwise_and(jnp.left_shift(word, shift), jnp.uint32(0xFFFF0000))
        gk = lax.bitcast_convert_type(gbits, f32)
        parts.append((_gelu_tanh(a_ref[slot, :, k * nkeys:(k + 1) * nkeys]) * gk).astype(bf16))
    ga = jnp.concatenate(parts, axis=1)
    acc_ref[...] += jnp.dot(ga, v_ref[...], preferred_element_type=f32)

    @pl.when(j == pl.num_programs(1) - 1)
    def _():
        y = x_ref[...] + m_ref[gate_row:gate_row + 1, :] * acc_ref[...]
        if final_norm:
            y = y * lax.rsqrt(jnp.mean(y * y, axis=-1, keepdims=True) + EPS) * fw_ref[...]
        o_ref[...] = y


def _peer_dense(f, u, v, gates, x, mods, gate_row, n_blocks, bps, n_batch, final_w=None):
    t, d = f.shape
    ne = v.shape[0]
    nkeys = gates.shape[3]
    te = SUBLANES * nkeys
    n_gblk = gates.shape[0]
    assert ne // te == 2 * n_gblk
    gates = gates.reshape(n_gblk, gates.shape[1] * SUBLANES, nkeys)
    return pl.pallas_call(
        functools.partial(_peer_dense_kernel, gate_row=gate_row, nkeys=nkeys, final_norm=final_w is not None),
        grid=(n_blocks, ne // te),
        in_specs=[pl.BlockSpec((ROW_BLOCK, d), lambda i, j: (i, 0)),
                  pl.BlockSpec((te, d), lambda i, j: (j, 0)),
                  pl.BlockSpec((te, d), lambda i, j: (j, 0)),
                  pl.BlockSpec((None, ROW_BLOCK * SUBLANES, nkeys), lambda i, j: (j % n_gblk, i, 0)),
                  pl.BlockSpec((ROW_BLOCK, d), lambda i, j: (i, 0)),
                  pl.BlockSpec((None, SUBLANES, d), lambda i, j: (_group_of(i, bps, n_batch), 0, 0)),
                  pl.BlockSpec((1, d), lambda i, j: (0, 0))],
        out_specs=pl.BlockSpec((ROW_BLOCK, d), lambda i, j: (i, 0)),
        out_shape=jax.ShapeDtypeStruct((t, d), f32),
        scratch_shapes=[pltpu.VMEM((ROW_BLOCK, d), f32), pltpu.VMEM((2, ROW_BLOCK, te), f32)],
        compiler_params=_cparams(2),
        name="peer_dense",
    )(f, u, v, gates, x, mods, (jnp.ones((d,), f32) if final_w is None else final_w).reshape(1, d))


def kernel(x, c, ctx, c_ctx, mod_w, mod_b, norm_w, final_norm_w, a_wqkv, a_q_gain, a_k_gain, a_wo,
           b_wqkv, b_rpb, b_wo, pool_w, pool_scale, peer_wq, peer_sub_keys, peer_u, peer_v):
    n_batch, s, d = x.shape
    c_len = ctx.shape[1]
    depth = mod_w.shape[0]
    assert s % ROW_BLOCK == 0 and (n_batch * c_len) % ROW_BLOCK == 0 and n_batch * c_len <= s
    assert s % c_len == 0 and c_len % POOL_ROW_BLOCK == 0 and s % GRID_W == 0
    assert a_q_gain.shape[1] == LANES and peer_sub_keys.shape[2] == LANES and peer_sub_keys.shape[3] == LANES
    assert PEER_TOPK & (PEER_TOPK - 1) == 0 and peer_sub_keys.shape[2] & (peer_sub_keys.shape[2] - 1) == 0
    assert depth % N_MIXERS != 0

    bps = s // ROW_BLOCK
    nb_lat = n_batch * bps
    nb_all = nb_lat + n_batch * c_len // ROW_BLOCK
    t_lat = n_batch * s
    q_dim = a_wo.shape[1]
    kv_dim = (a_wqkv.shape[2] - q_dim) // 2
    n_kv = kv_dim // LANES
    groups = q_dim // kv_dim
    b_heads = b_rpb.shape[1]
    win_r, win_c = (b_rpb.shape[2] + 1) // 2, (b_rpb.shape[3] + 1) // 2
    nkeys = peer_sub_keys.shape[2]
    scale = SM_SCALE_LOG2

    xs = jnp.concatenate([x.reshape(t_lat, d), ctx.reshape(n_batch * c_len, d)], axis=0)
    cvec = jnp.concatenate([c, c_ctx[None, :], jnp.zeros((SUBLANES - n_batch - 1, d), f32)], axis=0)
    mods_all = _mods(cvec, mod_w, mod_b)
    tables = _rope_tables(s)

    for i in range(depth):
        last = i == depth - 1
        kind, j = i % N_MIXERS, i // N_MIXERS
        nb = nb_lat if last else nb_all
        mods = mods_all[i, :n_batch + 1].reshape(n_batch + 1, N_MOD, d)
        mods = jnp.pad(mods, ((0, 0), (0, SUBLANES - N_MOD), (0, 0)))

        if kind == 2:
            h = _norm_mod(xs, norm_w[i, 0], mods, 0, 1, nb_all, bps, n_batch, f32)
            xs, f = _pool(h, xs, pool_w[j].astype(bf16), pool_scale[j], mods, norm_w[i, 1], 2, 3, 4, n_batch, s, c_len)
        else:
            h = _norm_mod(xs, norm_w[i, 0], mods, 0, 1, nb_all, bps, n_batch, bf16)
            if kind == 0:
                qkv = _qkv_gqa(h, a_wqkv[j].astype(bf16), a_q_gain[j], a_k_gain[j], tables, q_dim, kv_dim,
                               nb_all, bps, n_batch)
                o = _attn_lat(qkv, qkv, qkv, 0, q_dim, q_dim + kv_dim, groups, n_kv, n_batch, s, c_len, q_dim)
                if not last:
                    o_ctx = _attn_ctx(qkv, qkv, qkv, 0, q_dim, q_dim + kv_dim, groups, n_kv, n_batch, s, c_len, 1.0)
                wo = a_wo[j]
            else:
                qkv = _mm(h, b_wqkv[j].astype(bf16), nb_all, _col_tile(3 * d), bf16)
                rows = s // GRID_W
                bias = _natten_bias(b_rpb[j], rows, min(win_r, rows), win_r, win_c)
                o = _natten_lat(qkv, bias, b_heads, n_batch, s, c_len, d)
                if not last:
                    o_ctx = _attn_ctx(qkv, qkv, qkv, 0, d, 2 * d, 1, b_heads, n_batch, s, c_len, scale)
                wo = b_wo[j]
            if last:
                o_ctx = o
            xs, f = _mm_res(o, o_ctx, wo.astype(bf16), xs, mods, norm_w[i, 1], 2, 3, 4, nb, bps, n_batch)

        qp = _mm(f, peer_wq[i].astype(bf16), nb, _col_tile(peer_wq.shape[2]), f32)
        e, g, u_bf = _peer_topk(qp, peer_sub_keys[i], nb, peer_u, i)
        gates, v_bf = _peer_gates(e, g, nkeys, nb, peer_v, i)
        xs = _peer_dense(f, u_bf, v_bf, gates, xs, mods, 5, nb, bps, n_batch, final_norm_w if last else None)

    return xs.reshape(n_batch, s, d)
```

```python
import functools

import numpy as np
import jax
import jax.numpy as jnp
from jax import lax
from jax.experimental import pallas as pl
from jax.experimental.pallas import tpu as pltpu

GRID_W = 64
ROPE_BASE = 10000.0
POOL_WINDOWS = (2, 4, 8, 16)
PEER_TOPK = 16
EPS = 1e-6
N_MIXERS = 3
N_MOD = 6

LANES = 128
SUBLANES = 8
ROW_BLOCK = 512
POOL_ROW_BLOCK = 256
POOL_HALO = 8
MM_COL_TILE = 2048
ATTN_Q_ROWS = 128
ATTN_Q_SUBBLOCKS = 2
NATTEN_UNROLL = 16
TOPK_TOKENS = 512
GATES_TOKENS = 256
GATES_UNROLL = 128
VMEM_LIMIT = 56 * 1024 * 1024
NEG = -1e30
LOG2E = 1.4426950408889634
SM_SCALE_LOG2 = float(LANES) ** -0.5 * LOG2E

f32 = jnp.float32
bf16 = jnp.bfloat16
_NT = (((1,), (1,)), ((), ()))


def _cparams(n_axes):
    return pltpu.CompilerParams(dimension_semantics=("arbitrary",) * n_axes, vmem_limit_bytes=VMEM_LIMIT)


def _group_of(i, blocks_per_seq, n_batch):
    return jnp.minimum(i // blocks_per_seq, n_batch)


def _mods_kernel(cv_ref, w_ref, b_ref, o_ref):
    cv = cv_ref[...]
    a = (cv / (1.0 + jnp.exp(-cv))).astype(bf16)
    o_ref[...] = jnp.dot(a, w_ref[...].astype(bf16), preferred_element_type=f32) + b_ref[...]


def _mods(cvec, mod_w, mod_b):
    depth, d, n = mod_w.shape
    tn = 1024
    return pl.pallas_call(
        _mods_kernel,
        grid=(depth, n // tn),
        in_specs=[pl.BlockSpec((SUBLANES, d), lambda l, j: (0, 0)),
                  pl.BlockSpec((None, d, tn), lambda l, j: (l, 0, j)),
                  pl.BlockSpec((None, 1, tn), lambda l, j: (l, 0, j))],
        out_specs=pl.BlockSpec((None, SUBLANES, tn), lambda l, j: (l, 0, j)),
        out_shape=jax.ShapeDtypeStruct((depth, SUBLANES, n), f32),
        compiler_params=_cparams(2),
        name="adaln_mods",
    )(cvec, mod_w, mod_b.reshape(depth, 1, n))


def _norm_mod_value(x, nw, m_ref, shift_row, scale_row):
    y = x * lax.rsqrt(jnp.mean(x * x, axis=-1, keepdims=True) + EPS) * nw
    return y * (1.0 + m_ref[scale_row:scale_row + 1, :]) + m_ref[shift_row:shift_row + 1, :]


def _norm_mod_kernel(x_ref, nw_ref, m_ref, o_ref, *, shift_row, scale_row):
    o_ref[...] = _norm_mod_value(x_ref[...], nw_ref[...], m_ref, shift_row, scale_row).astype(o_ref.dtype)


def _norm_mod(x, nw, mods, shift_row, scale_row, n_blocks, bps, n_batch, out_dtype):
    t, d = x.shape
    return pl.pallas_call(
        functools.partial(_norm_mod_kernel, shift_row=shift_row, scale_row=scale_row),
        grid=(n_blocks,),
        in_specs=[pl.BlockSpec((ROW_BLOCK, d), lambda i: (i, 0)),
                  pl.BlockSpec((1, d), lambda i: (0, 0)),
                  pl.BlockSpec((None, SUBLANES, d), lambda i: (_group_of(i, bps, n_batch), 0, 0))],
        out_specs=pl.BlockSpec((ROW_BLOCK, d), lambda i: (i, 0)),
        out_shape=jax.ShapeDtypeStruct((n_blocks * ROW_BLOCK, d), out_dtype),
        compiler_params=_cparams(1),
        name="norm_mod",
    )(x, nw.reshape(1, d), mods)


def _mm_kernel(a_ref, w_ref, o_ref):
    o_ref[...] = jnp.dot(a_ref[...], w_ref[...], preferred_element_type=f32).astype(o_ref.dtype)


def _col_tile(n):
    return max(t for t in range(LANES, min(n, MM_COL_TILE) + 1, LANES) if n % t == 0)


def _mm(a, w, n_blocks, tn, out_dtype):
    t, k = a.shape
    n = w.shape[1]
    return pl.pallas_call(
        _mm_kernel,
        grid=(n // tn, n_blocks),
        in_specs=[pl.BlockSpec((ROW_BLOCK, k), lambda j, i: (i, 0)),
                  pl.BlockSpec((k, tn), lambda j, i: (0, j))],
        out_specs=pl.BlockSpec((ROW_BLOCK, tn), lambda j, i: (i, j)),
        out_shape=jax.ShapeDtypeStruct((n_blocks * ROW_BLOCK, n), out_dtype),
        compiler_params=_cparams(2),
        name="matmul",
    )(a, w)


def _mm_res_kernel(al_ref, ac_ref, w_ref, x_ref, m_ref, nw_ref, o_ref, f_ref, *,
                   gate_row, shift_row, scale_row, n_lat_blocks):
    a = jnp.where(pl.program_id(0) < n_lat_blocks, al_ref[...], ac_ref[...])
    y = x_ref[...] + m_ref[gate_row:gate_row + 1, :] * jnp.dot(a, w_ref[...], preferred_element_type=f32)
    o_ref[...] = y
    f_ref[...] = _norm_mod_value(y, nw_ref[...], m_ref, shift_row, scale_row).astype(f_ref.dtype)


def _mm_res(a_lat, a_ctx, w, x, mods, nw, gate_row, shift_row, scale_row, n_blocks, bps, n_batch):
    k = a_lat.shape[1]
    n = w.shape[1]
    rows = n_blocks * ROW_BLOCK
    n_lat_blocks = a_lat.shape[0] // ROW_BLOCK
    assert n_blocks == n_lat_blocks or n_blocks == n_lat_blocks + a_ctx.shape[0] // ROW_BLOCK
    return pl.pallas_call(
        functools.partial(_mm_res_kernel, gate_row=gate_row, shift_row=shift_row, scale_row=scale_row,
                          n_lat_blocks=n_lat_blocks),
        grid=(n_blocks,),
        in_specs=[pl.BlockSpec((ROW_BLOCK, k), lambda i: (jnp.minimum(i, n_lat_blocks - 1), 0)),
                  pl.BlockSpec((ROW_BLOCK, k), lambda i: (jnp.maximum(i - n_lat_blocks, 0), 0)),
                  pl.BlockSpec((k, n), lambda i: (0, 0)),
                  pl.BlockSpec((ROW_BLOCK, n), lambda i: (i, 0)),
                  pl.BlockSpec((None, SUBLANES, n), lambda i: (_group_of(i, bps, n_batch), 0, 0)),
                  pl.BlockSpec((1, n), lambda i: (0, 0))],
        out_specs=[pl.BlockSpec((ROW_BLOCK, n), lambda i: (i, 0)), pl.BlockSpec((ROW_BLOCK, n), lambda i: (i, 0))],
        out_shape=[jax.ShapeDtypeStruct((rows, n), f32), jax.ShapeDtypeStruct((rows, n), bf16)],
        compiler_params=_cparams(1),
        name="matmul_residual",
    )(a_lat, a_ctx, w, x, mods, nw.reshape(1, n))


def _qkv_gqa_kernel(a_ref, w_ref, qg_ref, kg_ref, cos_ref, sin_ref, o_ref, *, nq_tiles, nk_tiles, scale):
    j = pl.program_id(0)
    acc = jnp.dot(a_ref[...], w_ref[...], preferred_element_type=f32)
    heads = acc.shape[1] // LANES

    def prep(gain, post):
        cos, sin = cos_ref[...], sin_ref[...]
        for h in range(heads):
            y = acc[:, h * LANES:(h + 1) * LANES]
            y = y * lax.rsqrt(jnp.mean(y * y, axis=-1, keepdims=True) + EPS) * gain
            y = y * cos + pltpu.roll(y, LANES // 2, 1) * sin
            o_ref[:, h * LANES:(h + 1) * LANES] = (y * post).astype(o_ref.dtype)

    @pl.when(j < nq_tiles)
    def _():
        prep(qg_ref[...], scale)

    @pl.when(jnp.logical_and(j >= nq_tiles, j < nq_tiles + nk_tiles))
    def _():
        prep(kg_ref[...], 1.0)

    @pl.when(j >= nq_tiles + nk_tiles)
    def _():
        o_ref[...] = acc.astype(o_ref.dtype)


def _rope_head_perm(a):
    quarter = LANES // 4
    shape = a.shape
    a = a.reshape(shape[:-1] + (shape[-1] // LANES, 2, 2, quarter))
    return jnp.swapaxes(a, -3, -2).reshape(shape)


def _rope_tables(s):
    t = np.arange(s)
    half = LANES // 2
    inv = ROPE_BASE ** (-jnp.arange(0, half, 2, dtype=f32) / half)
    ang_r = (t // GRID_W).astype(np.float32)[:, None] * inv[None, :]
    ang_c = (t % GRID_W).astype(np.float32)[:, None] * inv[None, :]
    ang = jnp.concatenate([ang_r, ang_c], axis=-1)
    cos = jnp.concatenate([jnp.cos(ang), jnp.cos(ang)], axis=-1)
    sin = jnp.concatenate([-jnp.sin(ang), jnp.sin(ang)], axis=-1)
    pad1 = jnp.ones((ROW_BLOCK, LANES), f32)
    pad0 = jnp.zeros((ROW_BLOCK, LANES), f32)
    return jnp.concatenate([cos, pad1], 0), jnp.concatenate([sin, pad0], 0)


def _qkv_gqa(h, wqkv, q_gain, k_gain, tables, q_dim, kv_dim, n_blocks, bps, n_batch):
    t, k = h.shape
    n = wqkv.shape[1]
    tn = min(512, kv_dim)
    cos, sin = tables
    wqkv = jnp.concatenate([_rope_head_perm(wqkv[:, :q_dim + kv_dim]), wqkv[:, q_dim + kv_dim:]], axis=1)
    q_gain, k_gain = _rope_head_perm(q_gain), _rope_head_perm(k_gain)

    def pos_map(j, i):
        return (jnp.where(i < bps * n_batch, i % bps, bps), 0)

    return pl.pallas_call(
        functools.partial(_qkv_gqa_kernel, nq_tiles=q_dim // tn, nk_tiles=kv_dim // tn, scale=SM_SCALE_LOG2),
        grid=(n // tn, n_blocks),
        in_specs=[pl.BlockSpec((ROW_BLOCK, k), lambda j, i: (i, 0)),
                  pl.BlockSpec((k, tn), lambda j, i: (0, j)),
                  pl.BlockSpec((1, LANES), lambda j, i: (0, 0)),
                  pl.BlockSpec((1, LANES), lambda j, i: (0, 0)),
                  pl.BlockSpec((ROW_BLOCK, LANES), pos_map),
                  pl.BlockSpec((ROW_BLOCK, LANES), pos_map)],
        out_specs=pl.BlockSpec((ROW_BLOCK, tn), lambda j, i: (i, j)),
        out_shape=jax.ShapeDtypeStruct((t, n), bf16),
        compiler_params=_cparams(2),
        name="qkv_gqa",
    )(h, wqkv, q_gain.reshape(1, LANES), k_gain.reshape(1, LANES), cos, sin)


def _flash_kernel(*refs, groups, n_lat_chunks, ck, scale):
    if n_lat_chunks:
        q_ref, kc_ref, vc_ref, kl_ref, vl_ref, o_ref = refs
    else:
        q_ref, kc_ref, vc_ref, o_ref = refs
    tq = q_ref.shape[0]
    q = jnp.concatenate([q_ref[:, g * LANES:(g + 1) * LANES] for g in range(groups)], axis=0)
    rows = groups * tq

    def step(k, v, m, l, acc):
        s = lax.dot_general(q, k, _NT, preferred_element_type=f32)
        if scale != 1.0:
            s = s * scale
        m_new = jnp.maximum(m, jnp.max(s, axis=-1, keepdims=True))
        p = jnp.exp2(s - m_new)
        alpha = jnp.exp2(m - m_new)
        l = alpha * l + jnp.sum(p, axis=-1, keepdims=True)
        acc = alpha * acc + jnp.dot(p.astype(bf16), v, preferred_element_type=f32)
        return m_new, l, acc

    carry = step(kc_ref[...], vc_ref[...], jnp.full((rows, 1), NEG, f32), jnp.zeros((rows, 1), f32),
                 jnp.zeros((rows, LANES), f32))
    if n_lat_chunks:
        def body(c, carry):
            off = pl.multiple_of(c * ck, ck)
            return step(kl_ref[pl.ds(off, ck), :], vl_ref[pl.ds(off, ck), :], *carry)
        carry = lax.fori_loop(0, n_lat_chunks, body, carry)
    _, l, acc = carry
    o = acc / l
    for g in range(groups):
        o_ref[:, g * LANES:(g + 1) * LANES] = o[g * tq:(g + 1) * tq].astype(o_ref.dtype)


def _gqa_lat_kernel(q_ref, kc_ref, vc_ref, kl_ref, vl_ref, o_ref, *scratch, groups, n_sub):
    tq = q_ref.shape[0] // n_sub
    vcx_ref, vlx_ref = scratch[4 * n_sub:]

    @pl.when(pl.program_id(2) == 0)
    def _():
        for src, dst in ((vc_ref, vcx_ref), (vl_ref, vlx_ref)):
            dst[:, :LANES] = src[...]
            dst[:, LANES:] = jnp.ones((src.shape[0], LANES), bf16)

    for h in range(n_sub):
        sc_ref, sl_ref, pc_ref, pl_ref = scratch[4 * h:4 * h + 4]
        q = jnp.concatenate([q_ref[h * tq:(h + 1) * tq, g * LANES:(g + 1) * LANES] for g in range(groups)], axis=0)
        sc_ref[...] = lax.dot_general(q, kc_ref[...], _NT, preferred_element_type=f32)
        sl_ref[...] = lax.dot_general(q, kl_ref[...], _NT, preferred_element_type=f32)
        m = jnp.maximum(jnp.max(sc_ref[...], axis=-1, keepdims=True), jnp.max(sl_ref[...], axis=-1, keepdims=True))
        pc_ref[...] = jnp.exp2((sc_ref[...] - m).astype(bf16))
        pl_ref[...] = jnp.exp2((sl_ref[...] - m).astype(bf16))
    for h in range(n_sub):
        pc_ref, pl_ref = scratch[4 * h + 2:4 * h + 4]
        ox = (jnp.dot(pc_ref[...], vcx_ref[...], preferred_element_type=f32)
              + jnp.dot(pl_ref[...], vlx_ref[...], preferred_element_type=f32))
        o = ox[:, :LANES] / ox[:, LANES:]
        for g in range(groups):
            o_ref[h * tq:(h + 1) * tq, g * LANES:(g + 1) * LANES] = o[g * tq:(g + 1) * tq].astype(o_ref.dtype)


def _attn_lat(q, k, v, q_col0, k_col0, v_col0, groups, n_kv, n_batch, s, c, out_cols):
    n_sub = ATTN_Q_SUBBLOCKS
    tq = n_sub * ATTN_Q_ROWS
    nqb = s // tq
    ctx_blk0 = n_batch * s // c
    qw = groups * LANES
    return pl.pallas_call(
        functools.partial(_gqa_lat_kernel, groups=groups, n_sub=n_sub),
        grid=(n_batch, n_kv, nqb),
        in_specs=[pl.BlockSpec((tq, qw), lambda b, h, i: (b * nqb + i, q_col0 // qw + h)),
                  pl.BlockSpec((c, LANES), lambda b, h, i: (ctx_blk0 + b, k_col0 // LANES + h)),
                  pl.BlockSpec((c, LANES), lambda b, h, i: (ctx_blk0 + b, v_col0 // LANES + h)),
                  pl.BlockSpec((s, LANES), lambda b, h, i: (b, k_col0 // LANES + h)),
                  pl.BlockSpec((s, LANES), lambda b, h, i: (b, v_col0 // LANES + h))],
        out_specs=pl.BlockSpec((tq, qw), lambda b, h, i: (b * nqb + i, h)),
        out_shape=jax.ShapeDtypeStruct((n_batch * s, out_cols), bf16),
        scratch_shapes=[pltpu.VMEM((groups * ATTN_Q_ROWS, n), dt)
                        for _ in range(n_sub) for dt in (f32, bf16) for n in (c, s)]
                       + [pltpu.VMEM((c, 2 * LANES), bf16), pltpu.VMEM((s, 2 * LANES), bf16)],
        compiler_params=_cparams(3),
        name="attn_latent",
    )(q, k, v, k, v)


def _attn_ctx(q, k, v, q_col0, k_col0, v_col0, groups, n_kv, n_batch, s, c, scale):
    ctx_blk0 = n_batch * s // c
    qw = groups * LANES
    return pl.pallas_call(
        functools.partial(_flash_kernel, groups=groups, n_lat_chunks=0, ck=0, scale=scale),
        grid=(n_batch, n_kv),
        in_specs=[pl.BlockSpec((c, qw), lambda b, h: (ctx_blk0 + b, q_col0 // qw + h)),
                  pl.BlockSpec((c, LANES), lambda b, h: (ctx_blk0 + b, k_col0 // LANES + h)),
                  pl.BlockSpec((c, LANES), lambda b, h: (ctx_blk0 + b, v_col0 // LANES + h))],
        out_specs=pl.BlockSpec((c, qw), lambda b, h: (b, h)),
        out_shape=jax.ShapeDtypeStruct((n_batch * c, n_kv * qw), bf16),
        compiler_params=_cparams(2),
        name="attn_context",
    )(q, k, v)


def _natten_kernel(q_ref, k_ref, v_ref, kc_ref, vc_ref, bias_ref, o_ref, *, rows, w, wr, scale):
    kc = kc_ref[...]
    vc = vc_ref[...]

    nr = NATTEN_UNROLL

    def body(g, carry):
        base = pl.multiple_of(g * nr * w, nr * w)
        qg = q_ref[pl.ds(base, nr * w), :]
        scg = lax.dot_general(qg, kc, _NT, preferred_element_type=f32) * scale
        offs, sws = [], []
        for i in range(nr):
            r = g * nr + i
            r0 = jnp.clip(r - wr // 2, 0, rows - wr)
            off = pl.multiple_of(r0 * w, w)
            kb = k_ref[pl.ds(off, wr * w), :]
            sws.append(lax.dot_general(qg[i * w:(i + 1) * w], kb, _NT, preferred_element_type=f32) * scale
                       + bias_ref[r - r0])
            offs.append(off)
        pws, pcs, ls = [], [], []
        for i in range(nr):
            sc = scg[i * w:(i + 1) * w]
            m = jnp.maximum(jnp.max(sws[i], axis=-1, keepdims=True), jnp.max(sc, axis=-1, keepdims=True))
            pw = jnp.exp2(sws[i] - m)
            pc = jnp.exp2(sc - m)
            ls.append(jnp.sum(pw, axis=-1, keepdims=True) + jnp.sum(pc, axis=-1, keepdims=True))
            pws.append(pw.astype(bf16))
            pcs.append(pc.astype(bf16))
        ocg = jnp.dot(jnp.concatenate(pcs, axis=0), vc, preferred_element_type=f32)
        for i in range(nr):
            vb = v_ref[pl.ds(offs[i], wr * w), :]
            o = (jnp.dot(pws[i], vb, preferred_element_type=f32) + ocg[i * w:(i + 1) * w]) / ls[i]
            o_ref[pl.ds(pl.multiple_of(base + i * w, w), w), :] = o.astype(o_ref.dtype)
        return carry

    lax.fori_loop(0, rows // nr, body, 0)


def _natten_bias(rpb, rows, wr, win_r, win_c):
    w = GRID_W
    cols = np.arange(w)
    col_start = np.clip(cols - win_c // 2, 0, w - win_c)
    inside = (cols[None, :] >= col_start[:, None]) & (cols[None, :] < col_start[:, None] + win_c)
    rpbp = jnp.pad(rpb, ((0, 0), (0, 0), (w - win_c, w - win_c)))
    p = jnp.stack([rpbp[:, :, w - 1 - qc:2 * w - 1 - qc] for qc in range(w)], axis=2)
    p = jnp.where(inside[None, None], p * LOG2E, NEG)
    tab = jnp.stack([p[:, win_r - 1 - d:win_r - 1 - d + wr] for d in range(wr)], axis=1)
    return jnp.transpose(tab, (0, 1, 3, 2, 4)).reshape(rpb.shape[0], wr, w, wr * w)


def _natten_lat(qkv, bias, n_heads, n_batch, s, c, d_model):
    t = qkv.shape[0]
    rows = s // GRID_W
    wr = bias.shape[1]
    ctx_blk0 = n_batch * s // c
    hq, hk, hv = 0, d_model // LANES, 2 * d_model // LANES
    return pl.pallas_call(
        functools.partial(_natten_kernel, rows=rows, w=GRID_W, wr=wr, scale=SM_SCALE_LOG2),
        grid=(n_batch, n_heads),
        in_specs=[pl.BlockSpec((s, LANES), lambda b, h: (b, hq + h)),
                  pl.BlockSpec((s, LANES), lambda b, h: (b, hk + h)),
                  pl.BlockSpec((s, LANES), lambda b, h: (b, hv + h)),
                  pl.BlockSpec((c, LANES), lambda b, h: (ctx_blk0 + b, hk + h)),
                  pl.BlockSpec((c, LANES), lambda b, h: (ctx_blk0 + b, hv + h)),
                  pl.BlockSpec((None, wr, GRID_W, wr * GRID_W), lambda b, h: (h, 0, 0, 0))],
        out_specs=pl.BlockSpec((s, LANES), lambda b, h: (b, h)),
        out_shape=jax.ShapeDtypeStruct((n_batch * s, d_model), bf16),
        compiler_params=_cparams(2),
        name="natten_latent",
    )(qkv, qkv, qkv, qkv, qkv, bias)


def _pool_kernel(h_ref, hp_ref, hn_ref, x_ref, w_ref, ls_ref, m_ref, nw_ref, o_ref, f_ref, pad_ref, *,
                 bps_lat, bps_ctx, n_lat_blocks, s, c, gate_row, shift_row, scale_row):
    i = pl.program_id(0)
    tm = h_ref.shape[0]
    is_lat = i < n_lat_blocks
    blk = jnp.where(is_lat, i % bps_lat, (i - n_lat_blocks) % bps_ctx)
    nblk = jnp.where(is_lat, bps_lat, bps_ctx)
    length = jnp.where(is_lat, s, c)
    pad_ref[0:POOL_HALO, :] = jnp.where(blk == 0, 0.0, hp_ref[...])
    pad_ref[POOL_HALO:POOL_HALO + tm, :] = h_ref[...]
    pad_ref[POOL_HALO + tm:2 * POOL_HALO + tm, :] = jnp.where(blk == nblk - 1, 0.0, hn_ref[...])
    pos = blk * tm + lax.broadcasted_iota(jnp.int32, (tm, 1), 0)
    pg = w_ref.shape[1]
    for g, win in enumerate(POOL_WINDOWS):
        cs = slice(g * pg, (g + 1) * pg)
        lo_off, hi_off = win // 2, win - win // 2
        acc = pad_ref[POOL_HALO - lo_off:POOL_HALO - lo_off + tm, cs]
        for k in range(-lo_off + 1, hi_off):
            acc = acc + pad_ref[POOL_HALO + k:POOL_HALO + k + tm, cs]
        cnt = jnp.minimum(pos + hi_off, length) - jnp.maximum(pos - lo_off, 0)
        y = acc * (1.0 / cnt.astype(f32)) - h_ref[:, cs]
        z = jnp.dot(y.astype(bf16), w_ref[g], preferred_element_type=f32) * ls_ref[:, cs]
        o_ref[:, cs] = x_ref[:, cs] + m_ref[gate_row:gate_row + 1, cs] * z
    f_ref[...] = _norm_mod_value(o_ref[...], nw_ref[...], m_ref, shift_row, scale_row).astype(f_ref.dtype)


def _pool(h, x, pool_w, pool_scale, mods, nw, gate_row, shift_row, scale_row, n_batch, s, c):
    t, d = h.shape
    tm = POOL_ROW_BLOCK
    nb = t // tm
    bps_lat, bps_ctx = s // tm, c // tm
    hb = tm // POOL_HALO
    last_halo = t // POOL_HALO - 1
    return pl.pallas_call(
        functools.partial(_pool_kernel, bps_lat=bps_lat, bps_ctx=bps_ctx, n_lat_blocks=n_batch * bps_lat,
                          s=s, c=c, gate_row=gate_row, shift_row=shift_row, scale_row=scale_row),
        grid=(nb,),
        in_specs=[pl.BlockSpec((tm, d), lambda i: (i, 0)),
                  pl.BlockSpec((POOL_HALO, d), lambda i: (jnp.maximum(i * hb - 1, 0), 0)),
                  pl.BlockSpec((POOL_HALO, d), lambda i: (jnp.minimum((i + 1) * hb, last_halo), 0)),
                  pl.BlockSpec((tm, d), lambda i: (i, 0)),
                  pl.BlockSpec(pool_w.shape, lambda i: (0, 0, 0)),
                  pl.BlockSpec((1, d), lambda i: (0, 0)),
                  pl.BlockSpec((None, SUBLANES, d), lambda i: (_group_of(i, bps_lat, n_batch), 0, 0)),
                  pl.BlockSpec((1, d), lambda i: (0, 0))],
        out_specs=[pl.BlockSpec((tm, d), lambda i: (i, 0)), pl.BlockSpec((tm, d), lambda i: (i, 0))],
        out_shape=[jax.ShapeDtypeStruct((t, d), f32), jax.ShapeDtypeStruct((t, d), bf16)],
        scratch_shapes=[pltpu.VMEM((tm + 2 * POOL_HALO, d), f32)],
        compiler_params=_cparams(1),
        name="pool_mixer",
    )(h, h, h, x, pool_w, pool_scale.reshape(1, d), mods, nw.reshape(1, d))


def _topk_rows(chains, res_ref):
    def body(k, carry):
        for c, (s_ref, order) in enumerate(chains):
            s = s_ref[...]
            best, best_o = s[:SUBLANES], order[:SUBLANES, :]
            for r in range(SUBLANES, s.shape[0], SUBLANES):
                blk = s[r:r + SUBLANES]
                take = blk > best
                best = jnp.maximum(best, blk)
                best_o = jnp.where(take, order[r:r + SUBLANES, :], best_o)
            m = jnp.max(best, axis=0, keepdims=True)
            am = jnp.min(jnp.where(best == m, best_o, float(2 ** 23)), axis=0, keepdims=True)
            s_ref[...] = jnp.where(order[...] == am, NEG, s)
            res_ref[c, 0, pl.ds(k, 1), :] = m
            res_ref[c, 1, pl.ds(k, 1), :] = am
        return carry

    lax.fori_loop(0, PEER_TOPK, body, 0)
    return [(res_ref[c, 0], res_ref[c, 1].astype(jnp.int32)) for c in range(len(chains))]


def _cand_blocks():
    blocks = [(0, PEER_TOPK)] + [(a, SUBLANES) for a in range(1, SUBLANES)]
    assert all((a + 1) * (nb + 1) > PEER_TOPK for a, nb in blocks[1:]) and 2 * SUBLANES == PEER_TOPK
    return blocks


def _cast_chunks(n_rows, steps):
    n = 1 << (steps.bit_length() - 1)
    assert n_rows % n == 0
    return n, n_rows // n


def _peer_topk_kernel(q_ref, sk_ref, w_ref, e_ref, g_ref, wb_ref, s1_ref, s2_ref, cand_ref, es_ref, gs_ref, res_ref,
                      key_order, cand_order, *, n_heads, nkeys):
    tb = q_ref.shape[0]
    wb_ref[...] = w_ref[...].astype(bf16)
    key_order[...] = lax.broadcasted_iota(jnp.int32, (nkeys, tb), 0).astype(f32)
    blocks = _cand_blocks()
    n_mid = (len(blocks) - 1) * SUBLANES
    r = lax.broadcasted_iota(jnp.int32, cand_ref.shape, 0)
    rm = r - PEER_TOPK
    mid = (lax.shift_right_logical(rm, SUBLANES.bit_length() - 1) + 1) * PEER_TOPK + lax.bitwise_and(rm, SUBLANES - 1)
    tail = (rm - n_mid + SUBLANES) * PEER_TOPK
    cand_order[...] = jnp.where(r < PEER_TOPK, r, jnp.where(rm < n_mid, mid, tail)).astype(f32)

    def scores(h):
        col = pl.multiple_of(h * 2 * LANES, 2 * LANES)
        s1_ref[...] = lax.dot_general(sk_ref[0], q_ref[:, pl.ds(col, LANES)], _NT, preferred_element_type=f32)
        s2_ref[...] = lax.dot_general(sk_ref[1], q_ref[:, pl.ds(col + LANES, LANES)], _NT, preferred_element_type=f32)

    sub_chains = [(s1_ref, key_order), (s2_ref, key_order)]
    scores(0)
    (v1, i1), (v2, i2) = _topk_rows(sub_chains, res_ref)

    def head(h, carry):
        v1, i1, v2, i2 = carry
        r0 = 0
        for a, nb in blocks:
            cand_ref[r0:r0 + nb, :] = v1[a:a + 1, :] + v2[:nb]
            r0 += nb
        cand_ref[r0:r0 + SUBLANES, :] = v1[SUBLANES:] + v2[0:1, :]
        scores(jnp.minimum(h + 1, n_heads - 1))
        nxt1, nxt2, (sc, ci) = _topk_rows(sub_chains + [(cand_ref, cand_order)], res_ref)
        ca = lax.shift_right_logical(ci, PEER_TOPK.bit_length() - 1)
        cb = lax.bitwise_and(ci, PEER_TOPK - 1)
        e1 = jnp.zeros((PEER_TOPK, tb), jnp.int32)
        e2 = jnp.zeros((PEER_TOPK, tb), jnp.int32)
        for a in range(PEER_TOPK):
            e1 = jnp.where(ca == a, i1[a:a + 1, :], e1)
            e2 = jnp.where(cb == a, i2[a:a + 1, :], e2)
        p = jnp.exp(sc - sc[0:1, :])
        gate = p / jnp.sum(p, axis=0, keepdims=True)
        row = pl.multiple_of(h * PEER_TOPK, PEER_TOPK)
        es_ref[pl.ds(row, PEER_TOPK), :] = e1 * nkeys + e2
        gs_ref[pl.ds(row, PEER_TOPK), :] = gate
        return nxt1 + nxt2

    lax.fori_loop(0, n_heads, head, (v1, i1, v2, i2))
    e_ref[...] = es_ref[...].T
    g_ref[...] = gs_ref[...].T


def _peer_topk(q, sub_keys, n_blocks_rows, w_all, layer):
    t, qd = q.shape
    nkeys = sub_keys.shape[1]
    n_heads = qd // (2 * LANES)
    slots = n_heads * PEER_TOPK
    tb = TOPK_TOKENS
    steps = n_blocks_rows * ROW_BLOCK // tb
    ne, d = w_all.shape[1:]
    n_chunks, chunk = _cast_chunks(ne, steps)
    n_cand = sum(nb for _, nb in _cand_blocks()) + SUBLANES
    return pl.pallas_call(
        functools.partial(_peer_topk_kernel, n_heads=n_heads, nkeys=nkeys),
        grid=(steps,),
        in_specs=[pl.BlockSpec((tb, qd), lambda i: (i, 0)),
                  pl.BlockSpec(sub_keys.shape, lambda i: (0, 0, 0)),
                  pl.BlockSpec((None, chunk, d), lambda i: (layer, jnp.minimum(i, n_chunks - 1), 0))],
        out_specs=[pl.BlockSpec((tb, slots), lambda i: (i, 0)),
                   pl.BlockSpec((tb, slots), lambda i: (i, 0)),
                   pl.BlockSpec((chunk, d), lambda i: (jnp.minimum(i, n_chunks - 1), 0))],
        out_shape=[jax.ShapeDtypeStruct((n_blocks_rows * ROW_BLOCK, slots), jnp.int32),
                   jax.ShapeDtypeStruct((n_blocks_rows * ROW_BLOCK, slots), f32),
                   jax.ShapeDtypeStruct((ne, d), bf16)],
        scratch_shapes=[pltpu.VMEM((nkeys, tb), f32), pltpu.VMEM((nkeys, tb), f32),
                        pltpu.VMEM((n_cand, tb), f32),
                        pltpu.VMEM((slots, tb), jnp.int32), pltpu.VMEM((slots, tb), f32),
                        pltpu.VMEM((3, 2, PEER_TOPK, tb), f32),
                        pltpu.VMEM((nkeys, tb), f32), pltpu.VMEM((n_cand, tb), f32)],
        compiler_params=_cparams(1),
        name="peer_topk",
    )(q, sub_keys, w_all)


def _peer_gates_kernel(e_ref, g_ref, w_ref, o_ref, wb_ref, *, nkeys):
    tb, slots = e_ref.shape
    wb_ref[...] = w_ref[...].astype(bf16)
    iota = lax.broadcasted_iota(jnp.int32, (nkeys, slots), 0)

    def body(t, carry):
        e = e_ref[pl.ds(t, 1), :]
        g = g_ref[pl.ds(t, 1), :]
        e1 = lax.shift_right_logical(e, nkeys.bit_length() - 1)
        e2 = lax.bitwise_and(e, nkeys - 1)
        w1 = jnp.where(e1 == iota, g, 0.0).astype(bf16)
        o2 = jnp.where(e2 == iota, 1.0, 0.0).astype(bf16)
        res = lax.dot_general(w1, o2, _NT, preferred_element_type=f32)
        bits = lax.bitcast_convert_type(res.astype(bf16).astype(f32), jnp.uint32)
        packed = jnp.bitwise_or(jnp.right_shift(bits[:nkeys // 2], jnp.uint32(16)), bits[nkeys // 2:])
        o_ref[:, t] = packed.reshape(nkeys // (2 * SUBLANES), SUBLANES, nkeys)
        return carry

    lax.fori_loop(0, tb, body, 0, unroll=GATES_UNROLL)


def _peer_gates(e, g, nkeys, n_blocks_rows, w_all, layer):
    t, slots = e.shape
    tb = GATES_TOKENS
    nblk = nkeys // (2 * SUBLANES)
    steps = n_blocks_rows * ROW_BLOCK // tb
    ne, d = w_all.shape[1:]
    n_chunks, chunk = _cast_chunks(ne, steps)
    return pl.pallas_call(
        functools.partial(_peer_gates_kernel, nkeys=nkeys),
        grid=(steps,),
        in_specs=[pl.BlockSpec((tb, slots), lambda i: (i, 0)),
                  pl.BlockSpec((tb, slots), lambda i: (i, 0)),
                  pl.BlockSpec((None, chunk, d), lambda i: (layer, jnp.minimum(i, n_chunks - 1), 0))],
        out_specs=[pl.BlockSpec((nblk, tb, SUBLANES, nkeys), lambda i: (0, i, 0, 0)),
                   pl.BlockSpec((chunk, d), lambda i: (jnp.minimum(i, n_chunks - 1), 0))],
        out_shape=[jax.ShapeDtypeStruct((nblk, t, SUBLANES, nkeys), jnp.uint32),
                   jax.ShapeDtypeStruct((ne, d), bf16)],
        compiler_params=_cparams(1),
        name="peer_gates",
    )(e, g, w_all)


def _gelu_tanh(x):
    k0 = -2.0 * 0.7978845608028654 * 1.4426950408889634
    return x / (1.0 + jnp.exp2(x * (k0 + (k0 * 0.044715) * (x * x))))


def _peer_dense_kernel(f_ref, u_ref, v_ref, g_ref, x_ref, m_ref, fw_ref, o_ref, acc_ref, a_ref, *,
                       gate_row, nkeys, final_norm):
    j = pl.program_id(1)

    @pl.when(j == 0)
    def _():
        acc_ref[...] = jnp.zeros_like(acc_ref)

    slot = j % 2
    a_ref[slot] = lax.dot_general(f_ref[...], u_ref[...], _NT, preferred_element_type=f32)
    tb = f_ref.shape[0]
    n_first = g_ref.shape[0] // tb
    shift = jnp.where(j < pl.num_programs(1) // 2, 16, 0).astype(jnp.uint32)
    parts = []
    for k in range(n_first):
        word = g_ref[pl.ds(k, tb, stride=n_first), :]
        gbits = jnp.bitwise_and(jnp.left_shift(word, shift), jnp.uint32(0xFFFF0000))
        gk = lax.bitcast_convert_type(gbits, f32)
        parts.append((_gelu_tanh(a_ref[slot, :, k * nkeys:(k + 1) * nkeys]) * gk).astype(bf16))
    ga = jnp.concatenate(parts, axis=1)
    acc_ref[...] += jnp.dot(ga, v_ref[...], preferred_element_type=f32)

    @pl.when(j == pl.num_programs(1) - 1)
    def _():
        y = x_ref[...] + m_ref[gate_row:gate_row + 1, :] * acc_ref[...]
        if final_norm:
            y = y * lax.rsqrt(jnp.mean(y * y, axis=-1, keepdims=True) + EPS) * fw_ref[...]
        o_ref[...] = y


def _peer_dense(f, u, v, gates, x, mods, gate_row, n_blocks, bps, n_batch, final_w=None):
    t, d = f.shape
    ne = v.shape[0]
    nkeys = gates.shape[3]
    te = SUBLANES * nkeys
    n_gblk = gates.shape[0]
    assert ne // te == 2 * n_gblk
    gates = gates.reshape(n_gblk, gates.shape[1] * SUBLANES, nkeys)
    return pl.pallas_call(
        functools.partial(_peer_dense_kernel, gate_row=gate_row, nkeys=nkeys, final_norm=final_w is not None),
        grid=(n_blocks, ne // te),
        in_specs=[pl.BlockSpec((ROW_BLOCK, d), lambda i, j: (i, 0)),
                  pl.BlockSpec((te, d), lambda i, j: (j, 0)),
                  pl.BlockSpec((te, d), lambda i, j: (j, 0)),
                  pl.BlockSpec((None, ROW_BLOCK * SUBLANES, nkeys), lambda i, j: (j % n_gblk, i, 0)),
                  pl.BlockSpec((ROW_BLOCK, d), lambda i, j: (i, 0)),
                  pl.BlockSpec((None, SUBLANES, d), lambda i, j: (_group_of(i, bps, n_batch), 0, 0)),
                  pl.BlockSpec((1, d), lambda i, j: (0, 0))],
        out_specs=pl.BlockSpec((ROW_BLOCK, d), lambda i, j: (i, 0)),
        out_shape=jax.ShapeDtypeStruct((t, d), f32),
        scratch_shapes=[pltpu.VMEM((ROW_BLOCK, d), f32), pltpu.VMEM((2, ROW_BLOCK, te), f32)],
        compiler_params=_cparams(2),
        name="peer_dense",
    )(f, u, v, gates, x, mods, (jnp.ones((d,), f32) if final_w is None else final_w).reshape(1, d))


def kernel(x, c, ctx, c_ctx, mod_w, mod_b, norm_w, final_norm_w, a_wqkv, a_q_gain, a_k_gain, a_wo,
           b_wqkv, b_rpb, b_wo, pool_w, pool_scale, peer_wq, peer_sub_keys, peer_u, peer_v):
    n_batch, s, d = x.shape
    c_len = ctx.shape[1]
    depth = mod_w.shape[0]
    assert s % ROW_BLOCK == 0 and (n_batch * c_len) % ROW_BLOCK == 0 and n_batch * c_len <= s
    assert s % c_len == 0 and c_len % POOL_ROW_BLOCK == 0 and s % GRID_W == 0
    assert a_q_gain.shape[1] == LANES and peer_sub_keys.shape[2] == LANES and peer_sub_keys.shape[3] == LANES
    assert PEER_TOPK & (PEER_TOPK - 1) == 0 and peer_sub_keys.shape[2] & (peer_sub_keys.shape[2] - 1) == 0
    assert depth % N_MIXERS != 0

    bps = s // ROW_BLOCK
    nb_lat = n_batch * bps
    nb_all = nb_lat + n_batch * c_len // ROW_BLOCK
    t_lat = n_batch * s
    q_dim = a_wo.shape[1]
    kv_dim = (a_wqkv.shape[2] - q_dim) // 2
    n_kv = kv_dim // LANES
    groups = q_dim // kv_dim
    b_heads = b_rpb.shape[1]
    win_r, win_c = (b_rpb.shape[2] + 1) // 2, (b_rpb.shape[3] + 1) // 2
    nkeys = peer_sub_keys.shape[2]
    scale = SM_SCALE_LOG2

    xs = jnp.concatenate([x.reshape(t_lat, d), ctx.reshape(n_batch * c_len, d)], axis=0)
    cvec = jnp.concatenate([c, c_ctx[None, :], jnp.zeros((SUBLANES - n_batch - 1, d), f32)], axis=0)
    mods_all = _mods(cvec, mod_w, mod_b)
    tables = _rope_tables(s)

    for i in range(depth):
        last = i == depth - 1
        kind, j = i % N_MIXERS, i // N_MIXERS
        nb = nb_lat if last else nb_all
        mods = mods_all[i, :n_batch + 1].reshape(n_batch + 1, N_MOD, d)
        mods = jnp.pad(mods, ((0, 0), (0, SUBLANES - N_MOD), (0, 0)))

        if kind == 2:
            h = _norm_mod(xs, norm_w[i, 0], mods, 0, 1, nb_all, bps, n_batch, f32)
            xs, f = _pool(h, xs, pool_w[j].astype(bf16), pool_scale[j], mods, norm_w[i, 1], 2, 3, 4, n_batch, s, c_len)
        else:
            h = _norm_mod(xs, norm_w[i, 0], mods, 0, 1, nb_all, bps, n_batch, bf16)
            if kind == 0:
                qkv = _qkv_gqa(h, a_wqkv[j].astype(bf16), a_q_gain[j], a_k_gain[j], tables, q_dim, kv_dim,
                               nb_all, bps, n_batch)
                o = _attn_lat(qkv, qkv, qkv, 0, q_dim, q_dim + kv_dim, groups, n_kv, n_batch, s, c_len, q_dim)
                if not last:
                    o_ctx = _attn_ctx(qkv, qkv, qkv, 0, q_dim, q_dim + kv_dim, groups, n_kv, n_batch, s, c_len, 1.0)
                wo = a_wo[j]
            else:
                qkv = _mm(h, b_wqkv[j].astype(bf16), nb_all, _col_tile(3 * d), bf16)
                rows = s // GRID_W
                bias = _natten_bias(b_rpb[j], rows, min(win_r, rows), win_r, win_c)
                o = _natten_lat(qkv, bias, b_heads, n_batch, s, c_len, d)
                if not last:
                    o_ctx = _attn_ctx(qkv, qkv, qkv, 0, d, 2 * d, 1, b_heads, n_batch, s, c_len, scale)
                wo = b_wo[j]
            if last:
                o_ctx = o
            xs, f = _mm_res(o, o_ctx, wo.astype(bf16), xs, mods, norm_w[i, 1], 2, 3, 4, nb, bps, n_batch)

        qp = _mm(f, peer_wq[i].astype(bf16), nb, _col_tile(peer_wq.shape[2]), f32)
        e, g, u_bf = _peer_topk(qp, peer_sub_keys[i], nb, peer_u, i)
        gates, v_bf = _peer_gates(e, g, nkeys, nb, peer_v, i)
        xs = _peer_dense(f, u_bf, v_bf, gates, xs, mods, 5, nb, bps, n_batch, final_norm_w if last else None)

    return xs.reshape(n_batch, s, d)
```

```python
import functools

import numpy as np
import jax
import jax.numpy as jnp
from jax import lax
from jax.experimental import pallas as pl
from jax.experimental.pallas import tpu as pltpu

GRID_W = 64
ROPE_BASE = 10000.0
POOL_WINDOWS = (2, 4, 8, 16)
PEER_TOPK = 16
EPS = 1e-6
N_MIXERS = 3
N_MOD = 6

LANES = 128
SUBLANES = 8
ROW_BLOCK = 512
POOL_ROW_BLOCK = 256
POOL_HALO = 8
MM_COL_TILE = 2048
ATTN_Q_ROWS = 128
ATTN_Q_SUBBLOCKS = 2
NATTEN_UNROLL = 16
TOPK_TOKENS = 512
GATES_TOKENS = 256
GATES_UNROLL = 128
VMEM_LIMIT = 56 * 1024 * 1024
NEG = -1e30
LOG2E = 1.4426950408889634
SM_SCALE_LOG2 = float(LANES) ** -0.5 * LOG2E

f32 = jnp.float32
bf16 = jnp.bfloat16
_NT = (((1,), (1,)), ((), ()))


def _cparams(n_axes):
    return pltpu.CompilerParams(dimension_semantics=("arbitrary",) * n_axes, vmem_limit_bytes=VMEM_LIMIT)


def _group_of(i, blocks_per_seq, n_batch):
    return jnp.minimum(i // blocks_per_seq, n_batch)


def _mods_kernel(cv_ref, w_ref, b_ref, o_ref):
    cv = cv_ref[...]
    a = (cv / (1.0 + jnp.exp(-cv))).astype(bf16)
    o_ref[...] = jnp.dot(a, w_ref[...].astype(bf16), preferred_element_type=f32) + b_ref[...]


def _mods(cvec, mod_w, mod_b):
    depth, d, n = mod_w.shape
    tn = 1024
    return pl.pallas_call(
        _mods_kernel,
        grid=(depth, n // tn),
        in_specs=[pl.BlockSpec((SUBLANES, d), lambda l, j: (0, 0)),
                  pl.BlockSpec((None, d, tn), lambda l, j: (l, 0, j)),
                  pl.BlockSpec((None, 1, tn), lambda l, j: (l, 0, j))],
        out_specs=pl.BlockSpec((None, SUBLANES, tn), lambda l, j: (l, 0, j)),
        out_shape=jax.ShapeDtypeStruct((depth, SUBLANES, n), f32),
        compiler_params=_cparams(2),
        name="adaln_mods",
    )(cvec, mod_w, mod_b.reshape(depth, 1, n))


def _norm_mod_value(x, nw, m_ref, shift_row, scale_row):
    y = x * lax.rsqrt(jnp.mean(x * x, axis=-1, keepdims=True) + EPS) * nw
    return y * (1.0 + m_ref[scale_row:scale_row + 1, :]) + m_ref[shift_row:shift_row + 1, :]


def _norm_mod_kernel(x_ref, nw_ref, m_ref, o_ref, *, shift_row, scale_row):
    o_ref[...] = _norm_mod_value(x_ref[...], nw_ref[...], m_ref, shift_row, scale_row).astype(o_ref.dtype)


def _norm_mod(x, nw, mods, shift_row, scale_row, n_blocks, bps, n_batch, out_dtype):
    t, d = x.shape
    return pl.pallas_call(
        functools.partial(_norm_mod_kernel, shift_row=shift_row, scale_row=scale_row),
        grid=(n_blocks,),
        in_specs=[pl.BlockSpec((ROW_BLOCK, d), lambda i: (i, 0)),
                  pl.BlockSpec((1, d), lambda i: (0, 0)),
                  pl.BlockSpec((None, SUBLANES, d), lambda i: (_group_of(i, bps, n_batch), 0, 0))],
        out_specs=pl.BlockSpec((ROW_BLOCK, d), lambda i: (i, 0)),
        out_shape=jax.ShapeDtypeStruct((n_blocks * ROW_BLOCK, d), out_dtype),
        compiler_params=_cparams(1),
        name="norm_mod",
    )(x, nw.reshape(1, d), mods)


def _mm_kernel(a_ref, w_ref, o_ref):
    o_ref[...] = jnp.dot(a_ref[...], w_ref[...], preferred_element_type=f32).astype(o_ref.dtype)


def _col_tile(n):
    return max(t for t in range(LANES, min(n, MM_COL_TILE) + 1, LANES) if n % t == 0)


def _mm(a, w, n_blocks, tn, out_dtype):
    t, k = a.shape
    n = w.shape[1]
    return pl.pallas_call(
        _mm_kernel,
        grid=(n // tn, n_blocks),
        in_specs=[pl.BlockSpec((ROW_BLOCK, k), lambda j, i: (i, 0)),
                  pl.BlockSpec((k, tn), lambda j, i: (0, j))],
        out_specs=pl.BlockSpec((ROW_BLOCK, tn), lambda j, i: (i, j)),
        out_shape=jax.ShapeDtypeStruct((n_blocks * ROW_BLOCK, n), out_dtype),
        compiler_params=_cparams(2),
        name="matmul",
    )(a, w)


def _mm_res_kernel(al_ref, ac_ref, w_ref, x_ref, m_ref, nw_ref, o_ref, f_ref, *,
                   gate_row, shift_row, scale_row, n_lat_blocks):
    a = jnp.where(pl.program_id(0) < n_lat_blocks, al_ref[...], ac_ref[...])
    y = x_ref[...] + m_ref[gate_row:gate_row + 1, :] * jnp.dot(a, w_ref[...], preferred_element_type=f32)
    o_ref[...] = y
    f_ref[...] = _norm_mod_value(y, nw_ref[...], m_ref, shift_row, scale_row).astype(f_ref.dtype)


def _mm_res(a_lat, a_ctx, w, x, mods, nw, gate_row, shift_row, scale_row, n_blocks, bps, n_batch):
    k = a_lat.shape[1]
    n = w.shape[1]
    rows = n_blocks * ROW_BLOCK
    n_lat_blocks = a_lat.shape[0] // ROW_BLOCK
    assert n_blocks == n_lat_blocks or n_blocks == n_lat_blocks + a_ctx.shape[0] // ROW_BLOCK
    return pl.pallas_call(
        functools.partial(_mm_res_kernel, gate_row=gate_row, shift_row=shift_row, scale_row=scale_row,
                          n_lat_blocks=n_lat_blocks),
        grid=(n_blocks,),
        in_specs=[pl.BlockSpec((ROW_BLOCK, k), lambda i: (jnp.minimum(i, n_lat_blocks - 1), 0)),
                  pl.BlockSpec((ROW_BLOCK, k), lambda i: (jnp.maximum(i - n_lat_blocks, 0), 0)),
                  pl.BlockSpec((k, n), lambda i: (0, 0)),
                  pl.BlockSpec((ROW_BLOCK, n), lambda i: (i, 0)),
                  pl.BlockSpec((None, SUBLANES, n), lambda i: (_group_of(i, bps, n_batch), 0, 0)),
                  pl.BlockSpec((1, n), lambda i: (0, 0))],
        out_specs=[pl.BlockSpec((ROW_BLOCK, n), lambda i: (i, 0)), pl.BlockSpec((ROW_BLOCK, n), lambda i: (i, 0))],
        out_shape=[jax.ShapeDtypeStruct((rows, n), f32), jax.ShapeDtypeStruct((rows, n), bf16)],
        compiler_params=_cparams(1),
        name="matmul_residual",
    )(a_lat, a_ctx, w, x, mods, nw.reshape(1, n))


def _qkv_gqa_kernel(a_ref, w_ref, qg_ref, kg_ref, cos_ref, sin_ref, o_ref, *, nq_tiles, nk_tiles, scale):
    j = pl.program_id(0)
    acc = jnp.dot(a_ref[...], w_ref[...], preferred_element_type=f32)
    heads = acc.shape[1] // LANES

    def prep(gain, post):
        cos, sin = cos_ref[...], sin_ref[...]
        for h in range(heads):
            y = acc[:, h * LANES:(h + 1) * LANES]
            y = y * lax.rsqrt(jnp.mean(y * y, axis=-1, keepdims=True) + EPS) * gain
            y = y * cos + pltpu.roll(y, LANES // 2, 1) * sin
            o_ref[:, h * LANES:(h + 1) * LANES] = (y * post).astype(o_ref.dtype)

    @pl.when(j < nq_tiles)
    def _():
        prep(qg_ref[...], scale)

    @pl.when(jnp.logical_and(j >= nq_tiles, j < nq_tiles + nk_tiles))
    def _():
        prep(kg_ref[...], 1.0)

    @pl.when(j >= nq_tiles + nk_tiles)
    def _():
        o_ref[...] = acc.astype(o_ref.dtype)


def _rope_head_perm(a):
    quarter = LANES // 4
    shape = a.shape
    a = a.reshape(shape[:-1] + (shape[-1] // LANES, 2, 2, quarter))
    return jnp.swapaxes(a, -3, -2).reshape(shape)


def _rope_tables(s):
    t = np.arange(s)
    half = LANES // 2
    inv = ROPE_BASE ** (-jnp.arange(0, half, 2, dtype=f32) / half)
    ang_r = (t // GRID_W).astype(np.float32)[:, None] * inv[None, :]
    ang_c = (t % GRID_W).astype(np.float32)[:, None] * inv[None, :]
    ang = jnp.concatenate([ang_r, ang_c], axis=-1)
    cos = jnp.concatenate([jnp.cos(ang), jnp.cos(ang)], axis=-1)
    sin = jnp.concatenate([-jnp.sin(ang), jnp.sin(ang)], axis=-1)
    pad1 = jnp.ones((ROW_BLOCK, LANES), f32)
    pad0 = jnp.zeros((ROW_BLOCK, LANES), f32)
    return jnp.concatenate([cos, pad1], 0), jnp.concatenate([sin, pad0], 0)


def _qkv_gqa(h, wqkv, q_gain, k_gain, tables, q_dim, kv_dim, n_blocks, bps, n_batch):
    t, k = h.shape
    n = wqkv.shape[1]
    tn = min(512, kv_dim)
    cos, sin = tables
    wqkv = jnp.concatenate([_rope_head_perm(wqkv[:, :q_dim + kv_dim]), wqkv[:, q_dim + kv_dim:]], axis=1)
    q_gain, k_gain = _rope_head_perm(q_gain), _rope_head_perm(k_gain)

    def pos_map(j, i):
        return (jnp.where(i < bps * n_batch, i % bps, bps), 0)

    return pl.pallas_call(
        functools.partial(_qkv_gqa_kernel, nq_tiles=q_dim // tn, nk_tiles=kv_dim // tn, scale=SM_SCALE_LOG2),
        grid=(n // tn, n_blocks),
        in_specs=[pl.BlockSpec((ROW_BLOCK, k), lambda j, i: (i, 0)),
                  pl.BlockSpec((k, tn), lambda j, i: (0, j)),
                  pl.BlockSpec((1, LANES), lambda j, i: (0, 0)),
                  pl.BlockSpec((1, LANES), lambda j, i: (0, 0)),
                  pl.BlockSpec((ROW_BLOCK, LANES), pos_map),
                  pl.BlockSpec((ROW_BLOCK, LANES), pos_map)],
        out_specs=pl.BlockSpec((ROW_BLOCK, tn), lambda j, i: (i, j)),
        out_shape=jax.ShapeDtypeStruct((t, n), bf16),
        compiler_params=_cparams(2),
        name="qkv_gqa",
    )(h, wqkv, q_gain.reshape(1, LANES), k_gain.reshape(1, LANES), cos, sin)


def _flash_kernel(*refs, groups, n_lat_chunks, ck, scale):
    if n_lat_chunks:
        q_ref, kc_ref, vc_ref, kl_ref, vl_ref, o_ref = refs
    else:
        q_ref, kc_ref, vc_ref, o_ref = refs
    tq = q_ref.shape[0]
    q = jnp.concatenate([q_ref[:, g * LANES:(g + 1) * LANES] for g in range(groups)], axis=0)
    rows = groups * tq

    def step(k, v, m, l, acc):
        s = lax.dot_general(q, k, _NT, preferred_element_type=f32)
        if scale != 1.0:
            s = s * scale
        m_new = jnp.maximum(m, jnp.max(s, axis=-1, keepdims=True))
        p = jnp.exp2(s - m_new)
        alpha = jnp.exp2(m - m_new)
        l = alpha * l + jnp.sum(p, axis=-1, keepdims=True)
        acc = alpha * acc + jnp.dot(p.astype(bf16), v, preferred_element_type=f32)
        return m_new, l, acc

    carry = step(kc_ref[...], vc_ref[...], jnp.full((rows, 1), NEG, f32), jnp.zeros((rows, 1), f32),
                 jnp.zeros((rows, LANES), f32))
    if n_lat_chunks:
        def body(c, carry):
            off = pl.multiple_of(c * ck, ck)
            return step(kl_ref[pl.ds(off, ck), :], vl_ref[pl.ds(off, ck), :], *carry)
        carry = lax.fori_loop(0, n_lat_chunks, body, carry)
    _, l, acc = carry
    o = acc / l
    for g in range(groups):
        o_ref[:, g * LANES:(g + 1) * LANES] = o[g * tq:(g + 1) * tq].astype(o_ref.dtype)


def _gqa_lat_kernel(q_ref, kc_ref, vc_ref, kl_ref, vl_ref, o_ref, *scratch, groups, n_sub):
    tq = q_ref.shape[0] // n_sub
    vcx_ref, vlx_ref = scratch[4 * n_sub:]

    @pl.when(pl.program_id(2) == 0)
    def _():
        for src, dst in ((vc_ref, vcx_ref), (vl_ref, vlx_ref)):
            dst[:, :LANES] = src[...]
            dst[:, LANES:] = jnp.ones((src.shape[0], LANES), bf16)

    for h in range(n_sub):
        sc_ref, sl_ref, pc_ref, pl_ref = scratch[4 * h:4 * h + 4]
        q = jnp.concatenate([q_ref[h * tq:(h + 1) * tq, g * LANES:(g + 1) * LANES] for g in range(groups)], axis=0)
        sc_ref[...] = lax.dot_general(q, kc_ref[...], _NT, preferred_element_type=f32)
        sl_ref[...] = lax.dot_general(q, kl_ref[...], _NT, preferred_element_type=f32)
        m = jnp.maximum(jnp.max(sc_ref[...], axis=-1, keepdims=True), jnp.max(sl_ref[...], axis=-1, keepdims=True))
        pc_ref[...] = jnp.exp2((sc_ref[...] - m).astype(bf16))
        pl_ref[...] = jnp.exp2((sl_ref[...] - m).astype(bf16))
    for h in range(n_sub):
        pc_ref, pl_ref = scratch[4 * h + 2:4 * h + 4]
        ox = (jnp.dot(pc_ref[...], vcx_ref[...], preferred_element_type=f32)
              + jnp.dot(pl_ref[...], vlx_ref[...], preferred_element_type=f32))
        o = ox[:, :LANES] / ox[:, LANES:]
        for g in range(groups):
            o_ref[h * tq:(h + 1) * tq, g * LANES:(g + 1) * LANES] = o[g * tq:(g + 1) * tq].astype(o_ref.dtype)


def _attn_lat(q, k, v, q_col0, k_col0, v_col0, groups, n_kv, n_batch, s, c, out_cols):
    n_sub = ATTN_Q_SUBBLOCKS
    tq = n_sub * ATTN_Q_ROWS
    nqb = s // tq
    ctx_blk0 = n_batch * s // c
    qw = groups * LANES
    return pl.pallas_call(
        functools.partial(_gqa_lat_kernel, groups=groups, n_sub=n_sub),
        grid=(n_batch, n_kv, nqb),
        in_specs=[pl.BlockSpec((tq, qw), lambda b, h, i: (b * nqb + i, q_col0 // qw + h)),
                  pl.BlockSpec((c, LANES), lambda b, h, i: (ctx_blk0 + b, k_col0 // LANES + h)),
                  pl.BlockSpec((c, LANES), lambda b, h, i: (ctx_blk0 + b, v_col0 // LANES + h)),
                  pl.BlockSpec((s, LANES), lambda b, h, i: (b, k_col0 // LANES + h)),
                  pl.BlockSpec((s, LANES), lambda b, h, i: (b, v_col0 // LANES + h))],
        out_specs=pl.BlockSpec((tq, qw), lambda b, h, i: (b * nqb + i, h)),
        out_shape=jax.ShapeDtypeStruct((n_batch * s, out_cols), bf16),
        scratch_shapes=[pltpu.VMEM((groups * ATTN_Q_ROWS, n), dt)
                        for _ in range(n_sub) for dt in (f32, bf16) for n in (c, s)]
                       + [pltpu.VMEM((c, 2 * LANES), bf16), pltpu.VMEM((s, 2 * LANES), bf16)],
        compiler_params=_cparams(3),
        name="attn_latent",
    )(q, k, v, k, v)


def _attn_ctx(q, k, v, q_col0, k_col0, v_col0, groups, n_kv, n_batch, s, c, scale):
    ctx_blk0 = n_batch * s // c
    qw = groups * LANES
    return pl.pallas_call(
        functools.partial(_flash_kernel, groups=groups, n_lat_chunks=0, ck=0, scale=scale),
        grid=(n_batch, n_kv),
        in_specs=[pl.BlockSpec((c, qw), lambda b, h: (ctx_blk0 + b, q_col0 // qw + h)),
                  pl.BlockSpec((c, LANES), lambda b, h: (ctx_blk0 + b, k_col0 // LANES + h)),
                  pl.BlockSpec((c, LANES), lambda b, h: (ctx_blk0 + b, v_col0 // LANES + h))],
        out_specs=pl.BlockSpec((c, qw), lambda b, h: (b, h)),
        out_shape=jax.ShapeDtypeStruct((n_batch * c, n_kv * qw), bf16),
        compiler_params=_cparams(2),
        name="attn_context",
    )(q, k, v)


def _natten_kernel(q_ref, k_ref, v_ref, kc_ref, vc_ref, bias_ref, o_ref, *, rows, w, wr, scale):
    kc = kc_ref[...]
    vc = vc_ref[...]

    nr = NATTEN_UNROLL

    def body(g, carry):
        base = pl.multiple_of(g * nr * w, nr * w)
        qg = q_ref[pl.ds(base, nr * w), :]
        scg = lax.dot_general(qg, kc, _NT, preferred_element_type=f32) * scale
        offs, sws = [], []
        for i in range(nr):
            r = g * nr + i
            r0 = jnp.clip(r - wr // 2, 0, rows - wr)
            off = pl.multiple_of(r0 * w, w)
            kb = k_ref[pl.ds(off, wr * w), :]
            sws.append(lax.dot_general(qg[i * w:(i + 1) * w], kb, _NT, preferred_element_type=f32) * scale
                       + bias_ref[r - r0])
            offs.append(off)
        pws, pcs, ls = [], [], []
        for i in range(nr):
            sc = scg[i * w:(i + 1) * w]
            m = jnp.maximum(jnp.max(sws[i], axis=-1, keepdims=True), jnp.max(sc, axis=-1, keepdims=True))
            pw = jnp.exp2(sws[i] - m)
            pc = jnp.exp2(sc - m)
            ls.append(jnp.sum(pw, axis=-1, keepdims=True) + jnp.sum(pc, axis=-1, keepdims=True))
            pws.append(pw.astype(bf16))
            pcs.append(pc.astype(bf16))
        ocg = jnp.dot(jnp.concatenate(pcs, axis=0), vc, preferred_element_type=f32)
        for i in range(nr):
            vb = v_ref[pl.ds(offs[i], wr * w), :]
            o = (jnp.dot(pws[i], vb, preferred_element_type=f32) + ocg[i * w:(i + 1) * w]) / ls[i]
            o_ref[pl.ds(pl.multiple_of(base + i * w, w), w), :] = o.astype(o_ref.dtype)
        return carry

    lax.fori_loop(0, rows // nr, body, 0)


def _natten_bias(rpb, rows, wr, win_r, win_c):
    w = GRID_W
    cols = np.arange(w)
    col_start = np.clip(cols - win_c // 2, 0, w - win_c)
    inside = (cols[None, :] >= col_start[:, None]) & (cols[None, :] < col_start[:, None] + win_c)
    rpbp = jnp.pad(rpb, ((0, 0), (0, 0), (w - win_c, w - win_c)))
    p = jnp.stack([rpbp[:, :, w - 1 - qc:2 * w - 1 - qc] for qc in range(w)], axis=2)
    p = jnp.where(inside[None, None], p * LOG2E, NEG)
    tab = jnp.stack([p[:, win_r - 1 - d:win_r - 1 - d + wr] for d in range(wr)], axis=1)
    return jnp.transpose(tab, (0, 1, 3, 2, 4)).reshape(rpb.shape[0], wr, w, wr * w)


def _natten_lat(qkv, bias, n_heads, n_batch, s, c, d_model):
    t = qkv.shape[0]
    rows = s // GRID_W
    wr = bias.shape[1]
    ctx_blk0 = n_batch * s // c
    hq, hk, hv = 0, d_model // LANES, 2 * d_model // LANES
    return pl.pallas_call(
        functools.partial(_natten_kernel, rows=rows, w=GRID_W, wr=wr, scale=SM_SCALE_LOG2),
        grid=(n_batch, n_heads),
        in_specs=[pl.BlockSpec((s, LANES), lambda b, h: (b, hq + h)),
                  pl.BlockSpec((s, LANES), lambda b, h: (b, hk + h)),
                  pl.BlockSpec((s, LANES), lambda b, h: (b, hv + h)),
                  pl.BlockSpec((c, LANES), lambda b, h: (ctx_blk0 + b, hk + h)),
                  pl.BlockSpec((c, LANES), lambda b, h: (ctx_blk0 + b, hv + h)),
                  pl.BlockSpec((None, wr, GRID_W, wr * GRID_W), lambda b, h: (h, 0, 0, 0))],
        out_specs=pl.BlockSpec((s, LANES), lambda b, h: (b, h)),
        out_shape=jax.ShapeDtypeStruct((n_batch * s, d_model), bf16),
        compiler_params=_cparams(2),
        name="natten_latent",
    )(qkv, qkv, qkv, qkv, qkv, bias)


def _pool_kernel(h_ref, hp_ref, hn_ref, x_ref, w_ref, ls_ref, m_ref, nw_ref, o_ref, f_ref, pad_ref, *,
                 bps_lat, bps_ctx, n_lat_blocks, s, c, gate_row, shift_row, scale_row):
    i = pl.program_id(0)
    tm = h_ref.shape[0]
    is_lat = i < n_lat_blocks
    blk = jnp.where(is_lat, i % bps_lat, (i - n_lat_blocks) % bps_ctx)
    nblk = jnp.where(is_lat, bps_lat, bps_ctx)
    length = jnp.where(is_lat, s, c)
    pad_ref[0:POOL_HALO, :] = jnp.where(blk == 0, 0.0, hp_ref[...])
    pad_ref[POOL_HALO:POOL_HALO + tm, :] = h_ref[...]
    pad_ref[POOL_HALO + tm:2 * POOL_HALO + tm, :] = jnp.where(blk == nblk - 1, 0.0, hn_ref[...])
    pos = blk * tm + lax.broadcasted_iota(jnp.int32, (tm, 1), 0)
    pg = w_ref.shape[1]
    for g, win in enumerate(POOL_WINDOWS):
        cs = slice(g * pg, (g + 1) * pg)
        lo_off, hi_off = win // 2, win - win // 2
        acc = pad_ref[POOL_HALO - lo_off:POOL_HALO - lo_off + tm, cs]
        for k in range(-lo_off + 1, hi_off):
            acc = acc + pad_ref[POOL_HALO + k:POOL_HALO + k + tm, cs]
        cnt = jnp.minimum(pos + hi_off, length) - jnp.maximum(pos - lo_off, 0)
        y = acc * (1.0 / cnt.astype(f32)) - h_ref[:, cs]
        z = jnp.dot(y.astype(bf16), w_ref[g], preferred_element_type=f32) * ls_ref[:, cs]
        o_ref[:, cs] = x_ref[:, cs] + m_ref[gate_row:gate_row + 1, cs] * z
    f_ref[...] = _norm_mod_value(o_ref[...], nw_ref[...], m_ref, shift_row, scale_row).astype(f_ref.dtype)


def _pool(h, x, pool_w, pool_scale, mods, nw, gate_row, shift_row, scale_row, n_batch, s, c):
    t, d = h.shape
    tm = POOL_ROW_BLOCK
    nb = t // tm
    bps_lat, bps_ctx = s // tm, c // tm
    hb = tm // POOL_HALO
    last_halo = t // POOL_HALO - 1
    return pl.pallas_call(
        functools.partial(_pool_kernel, bps_lat=bps_lat, bps_ctx=bps_ctx, n_lat_blocks=n_batch * bps_lat,
                          s=s, c=c, gate_row=gate_row, shift_row=shift_row, scale_row=scale_row),
        grid=(nb,),
        in_specs=[pl.BlockSpec((tm, d), lambda i: (i, 0)),
                  pl.BlockSpec((POOL_HALO, d), lambda i: (jnp.maximum(i * hb - 1, 0), 0)),
                  pl.BlockSpec((POOL_HALO, d), lambda i: (jnp.minimum((i + 1) * hb, last_halo), 0)),
                  pl.BlockSpec((tm, d), lambda i: (i, 0)),
                  pl.BlockSpec(pool_w.shape, lambda i: (0, 0, 0)),
                  pl.BlockSpec((1, d), lambda i: (0, 0)),
                  pl.BlockSpec((None, SUBLANES, d), lambda i: (_group_of(i, bps_lat, n_batch), 0, 0)),
                  pl.BlockSpec((1, d), lambda i: (0, 0))],
        out_specs=[pl.BlockSpec((tm, d), lambda i: (i, 0)), pl.BlockSpec((tm, d), lambda i: (i, 0))],
        out_shape=[jax.ShapeDtypeStruct((t, d), f32), jax.ShapeDtypeStruct((t, d), bf16)],
        scratch_shapes=[pltpu.VMEM((tm + 2 * POOL_HALO, d), f32)],
        compiler_params=_cparams(1),
        name="pool_mixer",
    )(h, h, h, x, pool_w, pool_scale.reshape(1, d), mods, nw.reshape(1, d))


def _topk_rows(chains, res_ref):
    def body(k, carry):
        for c, (s_ref, order) in enumerate(chains):
            s = s_ref[...]
            best, best_o = s[:SUBLANES], order[:SUBLANES, :]
            for r in range(SUBLANES, s.shape[0], SUBLANES):
                blk = s[r:r + SUBLANES]
                take = blk > best
                best = jnp.maximum(best, blk)
                best_o = jnp.where(take, order[r:r + SUBLANES, :], best_o)
            m = jnp.max(best, axis=0, keepdims=True)
            am = jnp.min(jnp.where(best == m, best_o, float(2 ** 23)), axis=0, keepdims=True)
            s_ref[...] = jnp.where(order[...] == am, NEG, s)
            res_ref[c, 0, pl.ds(k, 1), :] = m
            res_ref[c, 1, pl.ds(k, 1), :] = am
        return carry

    lax.fori_loop(0, PEER_TOPK, body, 0)
    return [(res_ref[c, 0], res_ref[c, 1].astype(jnp.int32)) for c in range(len(chains))]


def _cand_blocks():
    blocks = [(0, PEER_TOPK)] + [(a, SUBLANES) for a in range(1, SUBLANES)]
    assert all((a + 1) * (nb + 1) > PEER_TOPK for a, nb in blocks[1:]) and 2 * SUBLANES == PEER_TOPK
    return blocks


def _cast_chunks(n_rows, steps):
    n = 1 << (steps.bit_length() - 1)
    assert n_rows % n == 0
    return n, n_rows // n


def _peer_topk_kernel(q_ref, sk_ref, w_ref, e_ref, g_ref, wb_ref, s1_ref, s2_ref, cand_ref, es_ref, gs_ref, res_ref,
                      key_order, cand_order, *, n_heads, nkeys):
    tb = q_ref.shape[0]
    wb_ref[...] = w_ref[...].astype(bf16)
    key_order[...] = lax.broadcasted_iota(jnp.int32, (nkeys, tb), 0).astype(f32)
    blocks = _cand_blocks()
    n_mid = (len(blocks) - 1) * SUBLANES
    r = lax.broadcasted_iota(jnp.int32, cand_ref.shape, 0)
    rm = r - PEER_TOPK
    mid = (lax.shift_right_logical(rm, SUBLANES.bit_length() - 1) + 1) * PEER_TOPK + lax.bitwise_and(rm, SUBLANES - 1)
    tail = (rm - n_mid + SUBLANES) * PEER_TOPK
    cand_order[...] = jnp.where(r < PEER_TOPK, r, jnp.where(rm < n_mid, mid, tail)).astype(f32)

    def scores(h):
        col = pl.multiple_of(h * 2 * LANES, 2 * LANES)
        s1_ref[...] = lax.dot_general(sk_ref[0], q_ref[:, pl.ds(col, LANES)], _NT, preferred_element_type=f32)
        s2_ref[...] = lax.dot_general(sk_ref[1], q_ref[:, pl.ds(col + LANES, LANES)], _NT, preferred_element_type=f32)

    sub_chains = [(s1_ref, key_order), (s2_ref, key_order)]
    scores(0)
    (v1, i1), (v2, i2) = _topk_rows(sub_chains, res_ref)

    def head(h, carry, with_next):
        v1, i1, v2, i2 = carry
        r0 = 0
        for a, nb in blocks:
            cand_ref[r0:r0 + nb, :] = v1[a:a + 1, :] + v2[:nb]
            r0 += nb
        cand_ref[r0:r0 + SUBLANES, :] = v1[SUBLANES:] + v2[0:1, :]
        if with_next:
            scores(h + 1)
            nxt1, nxt2, (sc, ci) = _topk_rows(sub_chains + [(cand_ref, cand_order)], res_ref)
        else:
            nxt1, nxt2 = (v1, i1), (v2, i2)
            ((sc, ci),) = _topk_rows([(cand_ref, cand_order)], res_ref)
        ca = lax.shift_right_logical(ci, PEER_TOPK.bit_length() - 1)
        cb = lax.bitwise_and(ci, PEER_TOPK - 1)
        e1 = jnp.zeros((PEER_TOPK, tb), jnp.int32)
        e2 = jnp.zeros((PEER_TOPK, tb), jnp.int32)
        for a in range(PEER_TOPK):
            e1 = jnp.where(ca == a, i1[a:a + 1, :], e1)
            e2 = jnp.where(cb == a, i2[a:a + 1, :], e2)
        p = jnp.exp(sc - sc[0:1, :])
        gate = p / jnp.sum(p, axis=0, keepdims=True)
        row = h * PEER_TOPK if isinstance(h, int) else pl.multiple_of(h * PEER_TOPK, PEER_TOPK)
        es_ref[pl.ds(row, PEER_TOPK), :] = e1 * nkeys + e2
        gs_ref[pl.ds(row, PEER_TOPK), :] = gate
        return nxt1 + nxt2

    carry = lax.fori_loop(0, n_heads - 1, functools.partial(head, with_next=True), (v1, i1, v2, i2))
    head(n_heads - 1, carry, with_next=False)
    e_ref[...] = es_ref[...].T
    g_ref[...] = gs_ref[...].T


def _peer_topk(q, sub_keys, n_blocks_rows, w_all, layer):
    t, qd = q.shape
    nkeys = sub_keys.shape[1]
    n_heads = qd // (2 * LANES)
    slots = n_heads * PEER_TOPK
    tb = TOPK_TOKENS
    steps = n_blocks_rows * ROW_BLOCK // tb
    ne, d = w_all.shape[1:]
    n_chunks, chunk = _cast_chunks(ne, steps)
    n_cand = sum(nb for _, nb in _cand_blocks()) + SUBLANES
    return pl.pallas_call(
        functools.partial(_peer_topk_kernel, n_heads=n_heads, nkeys=nkeys),
        grid=(steps,),
        in_specs=[pl.BlockSpec((tb, qd), lambda i: (i, 0)),
                  pl.BlockSpec(sub_keys.shape, lambda i: (0, 0, 0)),
                  pl.BlockSpec((None, chunk, d), lambda i: (layer, jnp.minimum(i, n_chunks - 1), 0))],
        out_specs=[pl.BlockSpec((tb, slots), lambda i: (i, 0)),
                   pl.BlockSpec((tb, slots), lambda i: (i, 0)),
                   pl.BlockSpec((chunk, d), lambda i: (jnp.minimum(i, n_chunks - 1), 0))],
        out_shape=[jax.ShapeDtypeStruct((n_blocks_rows * ROW_BLOCK, slots), jnp.int32),
                   jax.ShapeDtypeStruct((n_blocks_rows * ROW_BLOCK, slots), f32),
                   jax.ShapeDtypeStruct((ne, d), bf16)],
        scratch_shapes=[pltpu.VMEM((nkeys, tb), f32), pltpu.VMEM((nkeys, tb), f32),
                        pltpu.VMEM((n_cand, tb), f32),
                        pltpu.VMEM((slots, tb), jnp.int32), pltpu.VMEM((slots, tb), f32),
                        pltpu.VMEM((3, 2, PEER_TOPK, tb), f32),
                        pltpu.VMEM((nkeys, tb), f32), pltpu.VMEM((n_cand, tb), f32)],
        compiler_params=_cparams(1),
        name="peer_topk",
    )(q, sub_keys, w_all)


def _peer_gates_kernel(e_ref, g_ref, w_ref, o_ref, wb_ref, *, nkeys):
    tb, slots = e_ref.shape
    wb_ref[...] = w_ref[...].astype(bf16)
    iota = lax.broadcasted_iota(jnp.int32, (nkeys, slots), 0)

    def body(t, carry):
        e = e_ref[pl.ds(t, 1), :]
        g = g_ref[pl.ds(t, 1), :]
        e1 = lax.shift_right_logical(e, nkeys.bit_length() - 1)
        e2 = lax.bitwise_and(e, nkeys - 1)
        w1 = jnp.where(e1 == iota, g, 0.0).astype(bf16)
        o2 = jnp.where(e2 == iota, 1.0, 0.0).astype(bf16)
        res = lax.dot_general(w1, o2, _NT, preferred_element_type=f32)
        bits = lax.bitcast_convert_type(res.astype(bf16).astype(f32), jnp.uint32)
        packed = jnp.bitwise_or(jnp.right_shift(bits[:nkeys // 2], jnp.uint32(16)), bits[nkeys // 2:])
        o_ref[:, t] = packed.reshape(nkeys // (2 * SUBLANES), SUBLANES, nkeys)
        return carry

    lax.fori_loop(0, tb, body, 0, unroll=GATES_UNROLL)


def _peer_gates(e, g, nkeys, n_blocks_rows, w_all, layer):
    t, slots = e.shape
    tb = GATES_TOKENS
    nblk = nkeys // (2 * SUBLANES)
    steps = n_blocks_rows * ROW_BLOCK // tb
    ne, d = w_all.shape[1:]
    n_chunks, chunk = _cast_chunks(ne, steps)
    return pl.pallas_call(
        functools.partial(_peer_gates_kernel, nkeys=nkeys),
        grid=(steps,),
        in_specs=[pl.BlockSpec((tb, slots), lambda i: (i, 0)),
                  pl.BlockSpec((tb, slots), lambda i: (i, 0)),
                  pl.BlockSpec((None, chunk, d), lambda i: (layer, jnp.minimum(i, n_chunks - 1), 0))],
        out_specs=[pl.BlockSpec((nblk, tb, SUBLANES, nkeys), lambda i: (0, i, 0, 0)),
                   pl.BlockSpec((chunk, d), lambda i: (jnp.minimum(i, n_chunks - 1), 0))],
        out_shape=[jax.ShapeDtypeStruct((nblk, t, SUBLANES, nkeys), jnp.uint32),
                   jax.ShapeDtypeStruct((ne, d), bf16)],
        compiler_params=_cparams(1),
        name="peer_gates",
    )(e, g, w_all)


def _gelu_tanh(x):
    k0 = -2.0 * 0.7978845608028654 * 1.4426950408889634
    return x / (1.0 + jnp.exp2(x * (k0 + (k0 * 0.044715) * (x * x))))


def _peer_dense_kernel(f_ref, u_ref, v_ref, g_ref, x_ref, m_ref, fw_ref, o_ref, acc_ref, a_ref, *,
                       gate_row, nkeys, final_norm):
    j = pl.program_id(1)

    @pl.when(j == 0)
    def _():
        acc_ref[...] = jnp.zeros_like(acc_ref)

    slot = j % 2
    a_ref[slot] = lax.dot_general(f_ref[...], u_ref[...], _NT, preferred_element_type=f32)
    tb = f_ref.shape[0]
    n_first = g_ref.shape[0] // tb
    shift = jnp.where(j < pl.num_programs(1) // 2, 16, 0).astype(jnp.uint32)
    parts = []
    for k in range(n_first):
        word = g_ref[pl.ds(k, tb, stride=n_first), :]
        gbits = jnp.bitwise_and(jnp.left_shift(word, shift), jnp.uint32(0xFFFF0000))
        gk = lax.bitcast_convert_type(gbits, f32)
        parts.append((_gelu_tanh(a_ref[slot, :, k * nkeys:(k + 1) * nkeys]) * gk).astype(bf16))
    ga = jnp.concatenate(parts, axis=1)
    acc_ref[...] += jnp.dot(ga, v_ref[...], preferred_element_type=f32)

    @pl.when(j == pl.num_programs(1) - 1)
    def _():
        y = x_ref[...] + m_ref[gate_row:gate_row + 1, :] * acc_ref[...]
        if final_norm:
            y = y * lax.rsqrt(jnp.mean(y * y, axis=-1, keepdims=True) + EPS) * fw_ref[...]
        o_ref[...] = y


def _peer_dense(f, u, v, gates, x, mods, gate_row, n_blocks, bps, n_batch, final_w=None):
    t, d = f.shape
    ne = v.shape[0]
    nkeys = gates.shape[3]
    te = SUBLANES * nkeys
    n_gblk = gates.shape[0]
    assert ne // te == 2 * n_gblk
    gates = gates.reshape(n_gblk, gates.shape[1] * SUBLANES, nkeys)
    return pl.pallas_call(
        functools.partial(_peer_dense_kernel, gate_row=gate_row, nkeys=nkeys, final_norm=final_w is not None),
        grid=(n_blocks, ne // te),
        in_specs=[pl.BlockSpec((ROW_BLOCK, d), lambda i, j: (i, 0)),
                  pl.BlockSpec((te, d), lambda i, j: (j, 0)),
                  pl.BlockSpec((te, d), lambda i, j: (j, 0)),
                  pl.BlockSpec((None, ROW_BLOCK * SUBLANES, nkeys), lambda i, j: (j % n_gblk, i, 0)),
                  pl.BlockSpec((ROW_BLOCK, d), lambda i, j: (i, 0)),
                  pl.BlockSpec((None, SUBLANES, d), lambda i, j: (_group_of(i, bps, n_batch), 0, 0)),
                  pl.BlockSpec((1, d), lambda i, j: (0, 0))],
        out_specs=pl.BlockSpec((ROW_BLOCK, d), lambda i, j: (i, 0)),
        out_shape=jax.ShapeDtypeStruct((t, d), f32),
        scratch_shapes=[pltpu.VMEM((ROW_BLOCK, d), f32), pltpu.VMEM((2, ROW_BLOCK, te), f32)],
        compiler_params=_cparams(2),
        name="peer_dense",
    )(f, u, v, gates, x, mods, (jnp.ones((d,), f32) if final_w is None else final_w).reshape(1, d))


def kernel(x, c, ctx, c_ctx, mod_w, mod_b, norm_w, final_norm_w, a_wqkv, a_q_gain, a_k_gain, a_wo,
           b_wqkv, b_rpb, b_wo, pool_w, pool_scale, peer_wq, peer_sub_keys, peer_u, peer_v):
    n_batch, s, d = x.shape
    c_len = ctx.shape[1]
    depth = mod_w.shape[0]
    assert s % ROW_BLOCK == 0 and (n_batch * c_len) % ROW_BLOCK == 0 and n_batch * c_len <= s
    assert s % c_len == 0 and c_len % POOL_ROW_BLOCK == 0 and s % GRID_W == 0
    assert a_q_gain.shape[1] == LANES and peer_sub_keys.shape[2] == LANES and peer_sub_keys.shape[3] == LANES
    assert PEER_TOPK & (PEER_TOPK - 1) == 0 and peer_sub_keys.shape[2] & (peer_sub_keys.shape[2] - 1) == 0
    assert depth % N_MIXERS != 0

    bps = s // ROW_BLOCK
    nb_lat = n_batch * bps
    nb_all = nb_lat + n_batch * c_len // ROW_BLOCK
    t_lat = n_batch * s
    q_dim = a_wo.shape[1]
    kv_dim = (a_wqkv.shape[2] - q_dim) // 2
    n_kv = kv_dim // LANES
    groups = q_dim // kv_dim
    b_heads = b_rpb.shape[1]
    win_r, win_c = (b_rpb.shape[2] + 1) // 2, (b_rpb.shape[3] + 1) // 2
    nkeys = peer_sub_keys.shape[2]
    scale = SM_SCALE_LOG2

    xs = jnp.concatenate([x.reshape(t_lat, d), ctx.reshape(n_batch * c_len, d)], axis=0)
    cvec = jnp.concatenate([c, c_ctx[None, :], jnp.zeros((SUBLANES - n_batch - 1, d), f32)], axis=0)
    mods_all = _mods(cvec, mod_w, mod_b)
    tables = _rope_tables(s)

    for i in range(depth):
        last = i == depth - 1
        kind, j = i % N_MIXERS, i // N_MIXERS
        nb = nb_lat if last else nb_all
        mods = mods_all[i, :n_batch + 1].reshape(n_batch + 1, N_MOD, d)
        mods = jnp.pad(mods, ((0, 0), (0, SUBLANES - N_MOD), (0, 0)))

        if kind == 2:
            h = _norm_mod(xs, norm_w[i, 0], mods, 0, 1, nb_all, bps, n_batch, f32)
            xs, f = _pool(h, xs, pool_w[j].astype(bf16), pool_scale[j], mods, norm_w[i, 1], 2, 3, 4, n_batch, s, c_len)
        else:
            h = _norm_mod(xs, norm_w[i, 0], mods, 0, 1, nb_all, bps, n_batch, bf16)
            if kind == 0:
                qkv = _qkv_gqa(h, a_wqkv[j].astype(bf16), a_q_gain[j], a_k_gain[j], tables, q_dim, kv_dim,
                               nb_all, bps, n_batch)
                o = _attn_lat(qkv, qkv, qkv, 0, q_dim, q_dim + kv_dim, groups, n_kv, n_batch, s, c_len, q_dim)
                if not last:
                    o_ctx = _attn_ctx(qkv, qkv, qkv, 0, q_dim, q_dim + kv_dim, groups, n_kv, n_batch, s, c_len, 1.0)
                wo = a_wo[j]
            else:
                qkv = _mm(h, b_wqkv[j].astype(bf16), nb_all, _col_tile(3 * d), bf16)
                rows = s // GRID_W
                bias = _natten_bias(b_rpb[j], rows, min(win_r, rows), win_r, win_c)
                o = _natten_lat(qkv, bias, b_heads, n_batch, s, c_len, d)
                if not last:
                    o_ctx = _attn_ctx(qkv, qkv, qkv, 0, d, 2 * d, 1, b_heads, n_batch, s, c_len, scale)
                wo = b_wo[j]
            if last:
                o_ctx = o
            xs, f = _mm_res(o, o_ctx, wo.astype(bf16), xs, mods, norm_w[i, 1], 2, 3, 4, nb, bps, n_batch)

        qp = _mm(f, peer_wq[i].astype(bf16), nb, _col_tile(peer_wq.shape[2]), f32)
        e, g, u_bf = _peer_topk(qp, peer_sub_keys[i], nb, peer_u, i)
        gates, v_bf = _peer_gates(e, g, nkeys, nb, peer_v, i)
        xs = _peer_dense(f, u_bf, v_bf, gates, xs, mods, 5, nb, bps, n_batch, final_norm_w if last else None)

    return xs.reshape(n_batch, s, d)
```

```python
import functools

import numpy as np
import jax
import jax.numpy as jnp
from jax import lax
from jax.experimental import pallas as pl
from jax.experimental.pallas import tpu as pltpu

GRID_W = 64
ROPE_BASE = 10000.0
POOL_WINDOWS = (2, 4, 8, 16)
PEER_TOPK = 16
EPS = 1e-6
N_MIXERS = 3
N_MOD = 6

LANES = 128
SUBLANES = 8
ROW_BLOCK = 512
POOL_ROW_BLOCK = 256
POOL_HALO = 8
MM_COL_TILE = 2048
ATTN_Q_ROWS = 128
ATTN_Q_SUBBLOCKS = 2
NATTEN_UNROLL = 16
TOPK_TOKENS = 512
GATES_TOKENS = 256
GATES_UNROLL = 128
VMEM_LIMIT = 56 * 1024 * 1024
NEG = -1e30
LOG2E = 1.4426950408889634
SM_SCALE_LOG2 = float(LANES) ** -0.5 * LOG2E

f32 = jnp.float32
bf16 = jnp.bfloat16
_NT = (((1,), (1,)), ((), ()))


def _cparams(n_axes):
    return pltpu.CompilerParams(dimension_semantics=("arbitrary",) * n_axes, vmem_limit_bytes=VMEM_LIMIT)


def _group_of(i, blocks_per_seq, n_batch):
    return jnp.minimum(i // blocks_per_seq, n_batch)


def _mods_kernel(cv_ref, w_ref, b_ref, o_ref):
    cv = cv_ref[...]
    a = (cv / (1.0 + jnp.exp(-cv))).astype(bf16)
    o_ref[...] = jnp.dot(a, w_ref[...].astype(bf16), preferred_element_type=f32) + b_ref[...]


def _mods(cvec, mod_w, mod_b):
    depth, d, n = mod_w.shape
    tn = 1024
    return pl.pallas_call(
        _mods_kernel,
        grid=(depth, n // tn),
        in_specs=[pl.BlockSpec((SUBLANES, d), lambda l, j: (0, 0)),
                  pl.BlockSpec((None, d, tn), lambda l, j: (l, 0, j)),
                  pl.BlockSpec((None, 1, tn), lambda l, j: (l, 0, j))],
        out_specs=pl.BlockSpec((None, SUBLANES, tn), lambda l, j: (l, 0, j)),
        out_shape=jax.ShapeDtypeStruct((depth, SUBLANES, n), f32),
        compiler_params=_cparams(2),
        name="adaln_mods",
    )(cvec, mod_w, mod_b.reshape(depth, 1, n))


def _norm_mod_value(x, nw, m_ref, shift_row, scale_row):
    y = x * lax.rsqrt(jnp.mean(x * x, axis=-1, keepdims=True) + EPS) * nw
    return y * (1.0 + m_ref[scale_row:scale_row + 1, :]) + m_ref[shift_row:shift_row + 1, :]


def _row_specs(x, width):
    if not isinstance(x, tuple):
        return [pl.BlockSpec((ROW_BLOCK, width), lambda i: (i, 0))], [x], None
    n_lat = x[0].shape[0] // ROW_BLOCK
    return ([pl.BlockSpec((ROW_BLOCK, width), lambda i: (jnp.minimum(i, n_lat - 1), 0)),
             pl.BlockSpec((ROW_BLOCK, width), lambda i: (jnp.maximum(i - n_lat, 0), 0))], list(x), n_lat)


def _pick_rows(refs, n_lat):
    if n_lat is None:
        return refs[0][...]
    return jnp.where(pl.program_id(0) < n_lat, refs[0][...], refs[1][...])


def _norm_mod_kernel(*refs, shift_row, scale_row, n_lat):
    *x_refs, nw_ref, m_ref, o_ref = refs
    o_ref[...] = _norm_mod_value(_pick_rows(x_refs, n_lat), nw_ref[...], m_ref, shift_row, scale_row).astype(o_ref.dtype)


def _norm_mod(x, nw, mods, shift_row, scale_row, n_blocks, bps, n_batch, out_dtype):
    d = nw.shape[0]
    x_specs, x_ops, n_lat = _row_specs(x, d)
    return pl.pallas_call(
        functools.partial(_norm_mod_kernel, shift_row=shift_row, scale_row=scale_row, n_lat=n_lat),
        grid=(n_blocks,),
        in_specs=x_specs + [pl.BlockSpec((1, d), lambda i: (0, 0)),
                            pl.BlockSpec((None, SUBLANES, d), lambda i: (_group_of(i, bps, n_batch), 0, 0))],
        out_specs=pl.BlockSpec((ROW_BLOCK, d), lambda i: (i, 0)),
        out_shape=jax.ShapeDtypeStruct((n_blocks * ROW_BLOCK, d), out_dtype),
        compiler_params=_cparams(1),
        name="norm_mod",
    )(*x_ops, nw.reshape(1, d), mods)


def _mm_kernel(a_ref, w_ref, o_ref):
    o_ref[...] = jnp.dot(a_ref[...], w_ref[...], preferred_element_type=f32).astype(o_ref.dtype)


def _col_tile(n):
    return max(t for t in range(LANES, min(n, MM_COL_TILE) + 1, LANES) if n % t == 0)


def _mm(a, w, n_blocks, tn, out_dtype):
    t, k = a.shape
    n = w.shape[1]
    return pl.pallas_call(
        _mm_kernel,
        grid=(n // tn, n_blocks),
        in_specs=[pl.BlockSpec((ROW_BLOCK, k), lambda j, i: (i, 0)),
                  pl.BlockSpec((k, tn), lambda j, i: (0, j))],
        out_specs=pl.BlockSpec((ROW_BLOCK, tn), lambda j, i: (i, j)),
        out_shape=jax.ShapeDtypeStruct((n_blocks * ROW_BLOCK, n), out_dtype),
        compiler_params=_cparams(2),
        name="matmul",
    )(a, w)


def _mm_res_kernel(*refs, gate_row, shift_row, scale_row, n_lat_blocks, x_n_lat):
    al_ref, ac_ref, w_ref, *x_refs, m_ref, nw_ref, o_ref, f_ref = refs
    a = jnp.where(pl.program_id(0) < n_lat_blocks, al_ref[...], ac_ref[...])
    y = (_pick_rows(x_refs, x_n_lat)
         + m_ref[gate_row:gate_row + 1, :] * jnp.dot(a, w_ref[...], preferred_element_type=f32))
    o_ref[...] = y
    f_ref[...] = _norm_mod_value(y, nw_ref[...], m_ref, shift_row, scale_row).astype(f_ref.dtype)


def _mm_res(a_lat, a_ctx, w, x, mods, nw, gate_row, shift_row, scale_row, n_blocks, bps, n_batch):
    k = a_lat.shape[1]
    n = w.shape[1]
    rows = n_blocks * ROW_BLOCK
    n_lat_blocks = a_lat.shape[0] // ROW_BLOCK
    assert n_blocks == n_lat_blocks or n_blocks == n_lat_blocks + a_ctx.shape[0] // ROW_BLOCK
    x_specs, x_ops, x_n_lat = _row_specs(x, n)
    return pl.pallas_call(
        functools.partial(_mm_res_kernel, gate_row=gate_row, shift_row=shift_row, scale_row=scale_row,
                          n_lat_blocks=n_lat_blocks, x_n_lat=x_n_lat),
        grid=(n_blocks,),
        in_specs=[pl.BlockSpec((ROW_BLOCK, k), lambda i: (jnp.minimum(i, n_lat_blocks - 1), 0)),
                  pl.BlockSpec((ROW_BLOCK, k), lambda i: (jnp.maximum(i - n_lat_blocks, 0), 0)),
                  pl.BlockSpec((k, n), lambda i: (0, 0))]
                 + x_specs
                 + [pl.BlockSpec((None, SUBLANES, n), lambda i: (_group_of(i, bps, n_batch), 0, 0)),
                    pl.BlockSpec((1, n), lambda i: (0, 0))],
        out_specs=[pl.BlockSpec((ROW_BLOCK, n), lambda i: (i, 0)), pl.BlockSpec((ROW_BLOCK, n), lambda i: (i, 0))],
        out_shape=[jax.ShapeDtypeStruct((rows, n), f32), jax.ShapeDtypeStruct((rows, n), bf16)],
        compiler_params=_cparams(1),
        name="matmul_residual",
    )(a_lat, a_ctx, w, *x_ops, mods, nw.reshape(1, n))


def _qkv_gqa_kernel(a_ref, w_ref, qg_ref, kg_ref, cos_ref, sin_ref, o_ref, *, nq_tiles, nk_tiles, scale):
    j = pl.program_id(0)
    acc = jnp.dot(a_ref[...], w_ref[...], preferred_element_type=f32)
    heads = acc.shape[1] // LANES

    def prep(gain, post):
        cos, sin = cos_ref[...], sin_ref[...]
        for h in range(heads):
            y = acc[:, h * LANES:(h + 1) * LANES]
            y = y * lax.rsqrt(jnp.mean(y * y, axis=-1, keepdims=True) + EPS) * gain
            y = y * cos + pltpu.roll(y, LANES // 2, 1) * sin
            o_ref[:, h * LANES:(h + 1) * LANES] = (y * post).astype(o_ref.dtype)

    @pl.when(j < nq_tiles)
    def _():
        prep(qg_ref[...], scale)

    @pl.when(jnp.logical_and(j >= nq_tiles, j < nq_tiles + nk_tiles))
    def _():
        prep(kg_ref[...], 1.0)

    @pl.when(j >= nq_tiles + nk_tiles)
    def _():
        o_ref[...] = acc.astype(o_ref.dtype)


def _rope_head_perm(a):
    quarter = LANES // 4
    shape = a.shape
    a = a.reshape(shape[:-1] + (shape[-1] // LANES, 2, 2, quarter))
    return jnp.swapaxes(a, -3, -2).reshape(shape)


def _rope_tables(s):
    t = np.arange(s)
    half = LANES // 2
    inv = ROPE_BASE ** (-jnp.arange(0, half, 2, dtype=f32) / half)
    ang_r = (t // GRID_W).astype(np.float32)[:, None] * inv[None, :]
    ang_c = (t % GRID_W).astype(np.float32)[:, None] * inv[None, :]
    ang = jnp.concatenate([ang_r, ang_c], axis=-1)
    cos = jnp.concatenate([jnp.cos(ang), jnp.cos(ang)], axis=-1)
    sin = jnp.concatenate([-jnp.sin(ang), jnp.sin(ang)], axis=-1)
    pad1 = jnp.ones((ROW_BLOCK, LANES), f32)
    pad0 = jnp.zeros((ROW_BLOCK, LANES), f32)
    return jnp.concatenate([cos, pad1], 0), jnp.concatenate([sin, pad0], 0)


def _qkv_gqa(h, wqkv, q_gain, k_gain, tables, q_dim, kv_dim, n_blocks, bps, n_batch):
    t, k = h.shape
    n = wqkv.shape[1]
    tn = min(512, kv_dim)
    cos, sin = tables
    wqkv = jnp.concatenate([_rope_head_perm(wqkv[:, :q_dim + kv_dim]), wqkv[:, q_dim + kv_dim:]], axis=1)
    q_gain, k_gain = _rope_head_perm(q_gain), _rope_head_perm(k_gain)

    def pos_map(j, i):
        return (jnp.where(i < bps * n_batch, i % bps, bps), 0)

    return pl.pallas_call(
        functools.partial(_qkv_gqa_kernel, nq_tiles=q_dim // tn, nk_tiles=kv_dim // tn, scale=SM_SCALE_LOG2),
        grid=(n // tn, n_blocks),
        in_specs=[pl.BlockSpec((ROW_BLOCK, k), lambda j, i: (i, 0)),
                  pl.BlockSpec((k, tn), lambda j, i: (0, j)),
                  pl.BlockSpec((1, LANES), lambda j, i: (0, 0)),
                  pl.BlockSpec((1, LANES), lambda j, i: (0, 0)),
                  pl.BlockSpec((ROW_BLOCK, LANES), pos_map),
                  pl.BlockSpec((ROW_BLOCK, LANES), pos_map)],
        out_specs=pl.BlockSpec((ROW_BLOCK, tn), lambda j, i: (i, j)),
        out_shape=jax.ShapeDtypeStruct((t, n), bf16),
        compiler_params=_cparams(2),
        name="qkv_gqa",
    )(h, wqkv, q_gain.reshape(1, LANES), k_gain.reshape(1, LANES), cos, sin)


def _flash_kernel(*refs, groups, n_lat_chunks, ck, scale):
    if n_lat_chunks:
        q_ref, kc_ref, vc_ref, kl_ref, vl_ref, o_ref = refs
    else:
        q_ref, kc_ref, vc_ref, o_ref = refs
    tq = q_ref.shape[0]
    q = jnp.concatenate([q_ref[:, g * LANES:(g + 1) * LANES] for g in range(groups)], axis=0)
    rows = groups * tq

    def step(k, v, m, l, acc):
        s = lax.dot_general(q, k, _NT, preferred_element_type=f32)
        if scale != 1.0:
            s = s * scale
        m_new = jnp.maximum(m, jnp.max(s, axis=-1, keepdims=True))
        p = jnp.exp2(s - m_new)
        alpha = jnp.exp2(m - m_new)
        l = alpha * l + jnp.sum(p, axis=-1, keepdims=True)
        acc = alpha * acc + jnp.dot(p.astype(bf16), v, preferred_element_type=f32)
        return m_new, l, acc

    carry = step(kc_ref[...], vc_ref[...], jnp.full((rows, 1), NEG, f32), jnp.zeros((rows, 1), f32),
                 jnp.zeros((rows, LANES), f32))
    if n_lat_chunks:
        def body(c, carry):
            off = pl.multiple_of(c * ck, ck)
            return step(kl_ref[pl.ds(off, ck), :], vl_ref[pl.ds(off, ck), :], *carry)
        carry = lax.fori_loop(0, n_lat_chunks, body, carry)
    _, l, acc = carry
    o = acc / l
    for g in range(groups):
        o_ref[:, g * LANES:(g + 1) * LANES] = o[g * tq:(g + 1) * tq].astype(o_ref.dtype)


def _gqa_lat_kernel(q_ref, kc_ref, vc_ref, kl_ref, vl_ref, o_ref, *scratch, groups, n_sub):
    tq = q_ref.shape[0] // n_sub
    vcx_ref, vlx_ref = scratch[4 * n_sub:]

    @pl.when(pl.program_id(2) == 0)
    def _():
        for src, dst in ((vc_ref, vcx_ref), (vl_ref, vlx_ref)):
            dst[:, :LANES] = src[...]
            dst[:, LANES:] = jnp.ones((src.shape[0], LANES), bf16)

    for h in range(n_sub):
        sc_ref, sl_ref, pc_ref, pl_ref = scratch[4 * h:4 * h + 4]
        q = jnp.concatenate([q_ref[h * tq:(h + 1) * tq, g * LANES:(g + 1) * LANES] for g in range(groups)], axis=0)
        sc_ref[...] = lax.dot_general(q, kc_ref[...], _NT, preferred_element_type=f32)
        sl_ref[...] = lax.dot_general(q, kl_ref[...], _NT, preferred_element_type=f32)
        m = jnp.maximum(jnp.max(sc_ref[...], axis=-1, keepdims=True), jnp.max(sl_ref[...], axis=-1, keepdims=True))
        pc_ref[...] = jnp.exp2((sc_ref[...] - m).astype(bf16))
        pl_ref[...] = jnp.exp2((sl_ref[...] - m).astype(bf16))
    for h in range(n_sub):
        pc_ref, pl_ref = scratch[4 * h + 2:4 * h + 4]
        ox = (jnp.dot(pc_ref[...], vcx_ref[...], preferred_element_type=f32)
              + jnp.dot(pl_ref[...], vlx_ref[...], preferred_element_type=f32))
        o = ox[:, :LANES] / ox[:, LANES:]
        for g in range(groups):
            o_ref[h * tq:(h + 1) * tq, g * LANES:(g + 1) * LANES] = o[g * tq:(g + 1) * tq].astype(o_ref.dtype)


def _attn_lat(q, k, v, q_col0, k_col0, v_col0, groups, n_kv, n_batch, s, c, out_cols):
    n_sub = ATTN_Q_SUBBLOCKS
    tq = n_sub * ATTN_Q_ROWS
    nqb = s // tq
    ctx_blk0 = n_batch * s // c
    qw = groups * LANES
    return pl.pallas_call(
        functools.partial(_gqa_lat_kernel, groups=groups, n_sub=n_sub),
        grid=(n_batch, n_kv, nqb),
        in_specs=[pl.BlockSpec((tq, qw), lambda b, h, i: (b * nqb + i, q_col0 // qw + h)),
                  pl.BlockSpec((c, LANES), lambda b, h, i: (ctx_blk0 + b, k_col0 // LANES + h)),
                  pl.BlockSpec((c, LANES), lambda b, h, i: (ctx_blk0 + b, v_col0 // LANES + h)),
                  pl.BlockSpec((s, LANES), lambda b, h, i: (b, k_col0 // LANES + h)),
                  pl.BlockSpec((s, LANES), lambda b, h, i: (b, v_col0 // LANES + h))],
        out_specs=pl.BlockSpec((tq, qw), lambda b, h, i: (b * nqb + i, h)),
        out_shape=jax.ShapeDtypeStruct((n_batch * s, out_cols), bf16),
        scratch_shapes=[pltpu.VMEM((groups * ATTN_Q_ROWS, n), dt)
                        for _ in range(n_sub) for dt in (f32, bf16) for n in (c, s)]
                       + [pltpu.VMEM((c, 2 * LANES), bf16), pltpu.VMEM((s, 2 * LANES), bf16)],
        compiler_params=_cparams(3),
        name="attn_latent",
    )(q, k, v, k, v)


def _attn_ctx(q, k, v, q_col0, k_col0, v_col0, groups, n_kv, n_batch, s, c, scale):
    ctx_blk0 = n_batch * s // c
    qw = groups * LANES
    return pl.pallas_call(
        functools.partial(_flash_kernel, groups=groups, n_lat_chunks=0, ck=0, scale=scale),
        grid=(n_batch, n_kv),
        in_specs=[pl.BlockSpec((c, qw), lambda b, h: (ctx_blk0 + b, q_col0 // qw + h)),
                  pl.BlockSpec((c, LANES), lambda b, h: (ctx_blk0 + b, k_col0 // LANES + h)),
                  pl.BlockSpec((c, LANES), lambda b, h: (ctx_blk0 + b, v_col0 // LANES + h))],
        out_specs=pl.BlockSpec((c, qw), lambda b, h: (b, h)),
        out_shape=jax.ShapeDtypeStruct((n_batch * c, n_kv * qw), bf16),
        compiler_params=_cparams(2),
        name="attn_context",
    )(q, k, v)


def _natten_kernel(q_ref, k_ref, v_ref, kc_ref, vc_ref, bias_ref, o_ref, *, rows, w, wr, scale):
    kc = kc_ref[...]
    vc = vc_ref[...]

    nr = NATTEN_UNROLL

    def body(g, carry):
        base = pl.multiple_of(g * nr * w, nr * w)
        qg = q_ref[pl.ds(base, nr * w), :]
        scg = lax.dot_general(qg, kc, _NT, preferred_element_type=f32) * scale
        offs, sws = [], []
        for i in range(nr):
            r = g * nr + i
            r0 = jnp.clip(r - wr // 2, 0, rows - wr)
            off = pl.multiple_of(r0 * w, w)
            kb = k_ref[pl.ds(off, wr * w), :]
            sws.append(lax.dot_general(qg[i * w:(i + 1) * w], kb, _NT, preferred_element_type=f32) * scale
                       + bias_ref[r - r0])
            offs.append(off)
        pws, pcs, ls = [], [], []
        for i in range(nr):
            sc = scg[i * w:(i + 1) * w]
            m = jnp.maximum(jnp.max(sws[i], axis=-1, keepdims=True), jnp.max(sc, axis=-1, keepdims=True))
            pw = jnp.exp2(sws[i] - m)
            pc = jnp.exp2(sc - m)
            ls.append(jnp.sum(pw, axis=-1, keepdims=True) + jnp.sum(pc, axis=-1, keepdims=True))
            pws.append(pw.astype(bf16))
            pcs.append(pc.astype(bf16))
        ocg = jnp.dot(jnp.concatenate(pcs, axis=0), vc, preferred_element_type=f32)
        for i in range(nr):
            vb = v_ref[pl.ds(offs[i], wr * w), :]
            o = (jnp.dot(pws[i], vb, preferred_element_type=f32) + ocg[i * w:(i + 1) * w]) / ls[i]
            o_ref[pl.ds(pl.multiple_of(base + i * w, w), w), :] = o.astype(o_ref.dtype)
        return carry

    lax.fori_loop(0, rows // nr, body, 0)


def _natten_bias(rpb, rows, wr, win_r, win_c):
    w = GRID_W
    cols = np.arange(w)
    col_start = np.clip(cols - win_c // 2, 0, w - win_c)
    inside = (cols[None, :] >= col_start[:, None]) & (cols[None, :] < col_start[:, None] + win_c)
    rpbp = jnp.pad(rpb, ((0, 0), (0, 0), (w - win_c, w - win_c)))
    p = jnp.stack([rpbp[:, :, w - 1 - qc:2 * w - 1 - qc] for qc in range(w)], axis=2)
    p = jnp.where(inside[None, None], p * LOG2E, NEG)
    tab = jnp.stack([p[:, win_r - 1 - d:win_r - 1 - d + wr] for d in range(wr)], axis=1)
    return jnp.transpose(tab, (0, 1, 3, 2, 4)).reshape(rpb.shape[0], wr, w, wr * w)


def _natten_lat(qkv, bias, n_heads, n_batch, s, c, d_model):
    t = qkv.shape[0]
    rows = s // GRID_W
    wr = bias.shape[1]
    ctx_blk0 = n_batch * s // c
    hq, hk, hv = 0, d_model // LANES, 2 * d_model // LANES
    return pl.pallas_call(
        functools.partial(_natten_kernel, rows=rows, w=GRID_W, wr=wr, scale=SM_SCALE_LOG2),
        grid=(n_batch, n_heads),
        in_specs=[pl.BlockSpec((s, LANES), lambda b, h: (b, hq + h)),
                  pl.BlockSpec((s, LANES), lambda b, h: (b, hk + h)),
                  pl.BlockSpec((s, LANES), lambda b, h: (b, hv + h)),
                  pl.BlockSpec((c, LANES), lambda b, h: (ctx_blk0 + b, hk + h)),
                  pl.BlockSpec((c, LANES), lambda b, h: (ctx_blk0 + b, hv + h)),
                  pl.BlockSpec((None, wr, GRID_W, wr * GRID_W), lambda b, h: (h, 0, 0, 0))],
        out_specs=pl.BlockSpec((s, LANES), lambda b, h: (b, h)),
        out_shape=jax.ShapeDtypeStruct((n_batch * s, d_model), bf16),
        compiler_params=_cparams(2),
        name="natten_latent",
    )(qkv, qkv, qkv, qkv, qkv, bias)


def _pool_kernel(h_ref, hp_ref, hn_ref, x_ref, w_ref, ls_ref, m_ref, nw_ref, o_ref, f_ref, pad_ref, *,
                 bps_lat, bps_ctx, n_lat_blocks, s, c, gate_row, shift_row, scale_row):
    i = pl.program_id(0)
    tm = h_ref.shape[0]
    is_lat = i < n_lat_blocks
    blk = jnp.where(is_lat, i % bps_lat, (i - n_lat_blocks) % bps_ctx)
    nblk = jnp.where(is_lat, bps_lat, bps_ctx)
    length = jnp.where(is_lat, s, c)
    pad_ref[0:POOL_HALO, :] = jnp.where(blk == 0, 0.0, hp_ref[...])
    pad_ref[POOL_HALO:POOL_HALO + tm, :] = h_ref[...]
    pad_ref[POOL_HALO + tm:2 * POOL_HALO + tm, :] = jnp.where(blk == nblk - 1, 0.0, hn_ref[...])
    pos = blk * tm + lax.broadcasted_iota(jnp.int32, (tm, 1), 0)
    pg = w_ref.shape[1]
    for g, win in enumerate(POOL_WINDOWS):
        cs = slice(g * pg, (g + 1) * pg)
        lo_off, hi_off = win // 2, win - win // 2
        acc = pad_ref[POOL_HALO - lo_off:POOL_HALO - lo_off + tm, cs]
        for k in range(-lo_off + 1, hi_off):
            acc = acc + pad_ref[POOL_HALO + k:POOL_HALO + k + tm, cs]
        cnt = jnp.minimum(pos + hi_off, length) - jnp.maximum(pos - lo_off, 0)
        y = acc * (1.0 / cnt.astype(f32)) - h_ref[:, cs]
        z = jnp.dot(y.astype(bf16), w_ref[g], preferred_element_type=f32) * ls_ref[:, cs]
        o_ref[:, cs] = x_ref[:, cs] + m_ref[gate_row:gate_row + 1, cs] * z
    f_ref[...] = _norm_mod_value(o_ref[...], nw_ref[...], m_ref, shift_row, scale_row).astype(f_ref.dtype)


def _pool(h, x, pool_w, pool_scale, mods, nw, gate_row, shift_row, scale_row, n_batch, s, c):
    t, d = h.shape
    tm = POOL_ROW_BLOCK
    nb = t // tm
    bps_lat, bps_ctx = s // tm, c // tm
    hb = tm // POOL_HALO
    last_halo = t // POOL_HALO - 1
    return pl.pallas_call(
        functools.partial(_pool_kernel, bps_lat=bps_lat, bps_ctx=bps_ctx, n_lat_blocks=n_batch * bps_lat,
                          s=s, c=c, gate_row=gate_row, shift_row=shift_row, scale_row=scale_row),
        grid=(nb,),
        in_specs=[pl.BlockSpec((tm, d), lambda i: (i, 0)),
                  pl.BlockSpec((POOL_HALO, d), lambda i: (jnp.maximum(i * hb - 1, 0), 0)),
                  pl.BlockSpec((POOL_HALO, d), lambda i: (jnp.minimum((i + 1) * hb, last_halo), 0)),
                  pl.BlockSpec((tm, d), lambda i: (i, 0)),
                  pl.BlockSpec(pool_w.shape, lambda i: (0, 0, 0)),
                  pl.BlockSpec((1, d), lambda i: (0, 0)),
                  pl.BlockSpec((None, SUBLANES, d), lambda i: (_group_of(i, bps_lat, n_batch), 0, 0)),
                  pl.BlockSpec((1, d), lambda i: (0, 0))],
        out_specs=[pl.BlockSpec((tm, d), lambda i: (i, 0)), pl.BlockSpec((tm, d), lambda i: (i, 0))],
        out_shape=[jax.ShapeDtypeStruct((t, d), f32), jax.ShapeDtypeStruct((t, d), bf16)],
        scratch_shapes=[pltpu.VMEM((tm + 2 * POOL_HALO, d), f32)],
        compiler_params=_cparams(1),
        name="pool_mixer",
    )(h, h, h, x, pool_w, pool_scale.reshape(1, d), mods, nw.reshape(1, d))


def _topk_rows(chains, res_ref):
    def body(k, carry):
        for c, (s_ref, order) in enumerate(chains):
            s = s_ref[...]
            best, best_o = s[:SUBLANES], order[:SUBLANES, :]
            for r in range(SUBLANES, s.shape[0], SUBLANES):
                blk = s[r:r + SUBLANES]
                take = blk > best
                best = jnp.maximum(best, blk)
                best_o = jnp.where(take, order[r:r + SUBLANES, :], best_o)
            m = jnp.max(best, axis=0, keepdims=True)
            am = jnp.min(jnp.where(best == m, best_o, float(2 ** 23)), axis=0, keepdims=True)
            s_ref[...] = jnp.where(order[...] == am, NEG, s)
            res_ref[c, 0, pl.ds(k, 1), :] = m
            res_ref[c, 1, pl.ds(k, 1), :] = am
        return carry

    lax.fori_loop(0, PEER_TOPK, body, 0)
    return [(res_ref[c, 0], res_ref[c, 1].astype(jnp.int32)) for c in range(len(chains))]


def _cand_blocks():
    blocks = [(0, PEER_TOPK)] + [(a, SUBLANES) for a in range(1, SUBLANES)]
    assert all((a + 1) * (nb + 1) > PEER_TOPK for a, nb in blocks[1:]) and 2 * SUBLANES == PEER_TOPK
    return blocks


def _cast_chunks(n_rows, steps):
    n = 1 << (steps.bit_length() - 1)
    assert n_rows % n == 0
    return n, n_rows // n


def _peer_topk_kernel(q_ref, sk_ref, w_ref, e_ref, g_ref, wb_ref, s1_ref, s2_ref, cand_ref, es_ref, gs_ref, res_ref,
                      key_order, cand_order, *, n_heads, nkeys):
    tb = q_ref.shape[0]
    wb_ref[...] = w_ref[...].astype(bf16)
    key_order[...] = lax.broadcasted_iota(jnp.int32, (nkeys, tb), 0).astype(f32)
    blocks = _cand_blocks()
    n_mid = (len(blocks) - 1) * SUBLANES
    r = lax.broadcasted_iota(jnp.int32, cand_ref.shape, 0)
    rm = r - PEER_TOPK
    mid = (lax.shift_right_logical(rm, SUBLANES.bit_length() - 1) + 1) * PEER_TOPK + lax.bitwise_and(rm, SUBLANES - 1)
    tail = (rm - n_mid + SUBLANES) * PEER_TOPK
    cand_order[...] = jnp.where(r < PEER_TOPK, r, jnp.where(rm < n_mid, mid, tail)).astype(f32)

    def scores(h):
        col = pl.multiple_of(h * 2 * LANES, 2 * LANES)
        s1_ref[...] = lax.dot_general(sk_ref[0], q_ref[:, pl.ds(col, LANES)], _NT, preferred_element_type=f32)
        s2_ref[...] = lax.dot_general(sk_ref[1], q_ref[:, pl.ds(col + LANES, LANES)], _NT, preferred_element_type=f32)

    sub_chains = [(s1_ref, key_order), (s2_ref, key_order)]
    scores(0)
    (v1, i1), (v2, i2) = _topk_rows(sub_chains, res_ref)

    def head(h, carry, with_next):
        v1, i1, v2, i2 = carry
        r0 = 0
        for a, nb in blocks:
            cand_ref[r0:r0 + nb, :] = v1[a:a + 1, :] + v2[:nb]
            r0 += nb
        cand_ref[r0:r0 + SUBLANES, :] = v1[SUBLANES:] + v2[0:1, :]
        if with_next:
            scores(h + 1)
            nxt1, nxt2, (sc, ci) = _topk_rows(sub_chains + [(cand_ref, cand_order)], res_ref)
        else:
            nxt1, nxt2 = (v1, i1), (v2, i2)
            ((sc, ci),) = _topk_rows([(cand_ref, cand_order)], res_ref)
        ca = lax.shift_right_logical(ci, PEER_TOPK.bit_length() - 1)
        cb = lax.bitwise_and(ci, PEER_TOPK - 1)
        e1 = jnp.zeros((PEER_TOPK, tb), jnp.int32)
        e2 = jnp.zeros((PEER_TOPK, tb), jnp.int32)
        for a in range(PEER_TOPK):
            e1 = jnp.where(ca == a, i1[a:a + 1, :], e1)
            e2 = jnp.where(cb == a, i2[a:a + 1, :], e2)
        p = jnp.exp(sc - sc[0:1, :])
        gate = p / jnp.sum(p, axis=0, keepdims=True)
        row = h * PEER_TOPK if isinstance(h, int) else pl.multiple_of(h * PEER_TOPK, PEER_TOPK)
        es_ref[pl.ds(row, PEER_TOPK), :] = e1 * nkeys + e2
        gs_ref[pl.ds(row, PEER_TOPK), :] = gate
        return nxt1 + nxt2

    carry = lax.fori_loop(0, n_heads - 1, functools.partial(head, with_next=True), (v1, i1, v2, i2))
    head(n_heads - 1, carry, with_next=False)
    e_ref[...] = es_ref[...].T
    g_ref[...] = gs_ref[...].T


def _peer_topk(q, sub_keys, n_blocks_rows, w_all, layer):
    t, qd = q.shape
    nkeys = sub_keys.shape[1]
    n_heads = qd // (2 * LANES)
    slots = n_heads * PEER_TOPK
    tb = TOPK_TOKENS
    steps = n_blocks_rows * ROW_BLOCK // tb
    ne, d = w_all.shape[1:]
    n_chunks, chunk = _cast_chunks(ne, steps)
    n_cand = sum(nb for _, nb in _cand_blocks()) + SUBLANES
    return pl.pallas_call(
        functools.partial(_peer_topk_kernel, n_heads=n_heads, nkeys=nkeys),
        grid=(steps,),
        in_specs=[pl.BlockSpec((tb, qd), lambda i: (i, 0)),
                  pl.BlockSpec(sub_keys.shape, lambda i: (0, 0, 0)),
                  pl.BlockSpec((None, chunk, d), lambda i: (layer, jnp.minimum(i, n_chunks - 1), 0))],
        out_specs=[pl.BlockSpec((tb, slots), lambda i: (i, 0)),
                   pl.BlockSpec((tb, slots), lambda i: (i, 0)),
                   pl.BlockSpec((chunk, d), lambda i: (jnp.minimum(i, n_chunks - 1), 0))],
        out_shape=[jax.ShapeDtypeStruct((n_blocks_rows * ROW_BLOCK, slots), jnp.int32),
                   jax.ShapeDtypeStruct((n_blocks_rows * ROW_BLOCK, slots), f32),
                   jax.ShapeDtypeStruct((ne, d), bf16)],
        scratch_shapes=[pltpu.VMEM((nkeys, tb), f32), pltpu.VMEM((nkeys, tb), f32),
                        pltpu.VMEM((n_cand, tb), f32),
                        pltpu.VMEM((slots, tb), jnp.int32), pltpu.VMEM((slots, tb), f32),
                        pltpu.VMEM((3, 2, PEER_TOPK, tb), f32),
                        pltpu.VMEM((nkeys, tb), f32), pltpu.VMEM((n_cand, tb), f32)],
        compiler_params=_cparams(1),
        name="peer_topk",
    )(q, sub_keys, w_all)


def _peer_gates_kernel(e_ref, g_ref, w_ref, o_ref, wb_ref, *, nkeys):
    tb, slots = e_ref.shape
    wb_ref[...] = w_ref[...].astype(bf16)
    iota = lax.broadcasted_iota(jnp.int32, (nkeys, slots), 0)

    def body(t, carry):
        e = e_ref[pl.ds(t, 1), :]
        g = g_ref[pl.ds(t, 1), :]
        e1 = lax.shift_right_logical(e, nkeys.bit_length() - 1)
        e2 = lax.bitwise_and(e, nkeys - 1)
        w1 = jnp.where(e1 == iota, g, 0.0).astype(bf16)
        o2 = jnp.where(e2 == iota, 1.0, 0.0).astype(bf16)
        res = lax.dot_general(w1, o2, _NT, preferred_element_type=f32)
        bits = lax.bitcast_convert_type(res.astype(bf16).astype(f32), jnp.uint32)
        packed = jnp.bitwise_or(jnp.right_shift(bits[:nkeys // 2], jnp.uint32(16)), bits[nkeys // 2:])
        o_ref[:, t] = packed.reshape(nkeys // (2 * SUBLANES), SUBLANES, nkeys)
        return carry

    lax.fori_loop(0, tb, body, 0, unroll=GATES_UNROLL)


def _peer_gates(e, g, nkeys, n_blocks_rows, w_all, layer):
    t, slots = e.shape
    tb = GATES_TOKENS
    nblk = nkeys // (2 * SUBLANES)
    steps = n_blocks_rows * ROW_BLOCK // tb
    ne, d = w_all.shape[1:]
    n_chunks, chunk = _cast_chunks(ne, steps)
    return pl.pallas_call(
        functools.partial(_peer_gates_kernel, nkeys=nkeys),
        grid=(steps,),
        in_specs=[pl.BlockSpec((tb, slots), lambda i: (i, 0)),
                  pl.BlockSpec((tb, slots), lambda i: (i, 0)),
                  pl.BlockSpec((None, chunk, d), lambda i: (layer, jnp.minimum(i, n_chunks - 1), 0))],
        out_specs=[pl.BlockSpec((nblk, tb, SUBLANES, nkeys), lambda i: (0, i, 0, 0)),
                   pl.BlockSpec((chunk, d), lambda i: (jnp.minimum(i, n_chunks - 1), 0))],
        out_shape=[jax.ShapeDtypeStruct((nblk, t, SUBLANES, nkeys), jnp.uint32),
                   jax.ShapeDtypeStruct((ne, d), bf16)],
        compiler_params=_cparams(1),
        name="peer_gates",
    )(e, g, w_all)


def _gelu_tanh(x):
    k0 = -2.0 * 0.7978845608028654 * 1.4426950408889634
    return x / (1.0 + jnp.exp2(x * (k0 + (k0 * 0.044715) * (x * x))))


def _peer_dense_kernel(f_ref, u_ref, v_ref, g_ref, x_ref, m_ref, fw_ref, o_ref, acc_ref, a_ref, *,
                       gate_row, nkeys, final_norm):
    j = pl.program_id(1)

    @pl.when(j == 0)
    def _():
        acc_ref[...] = jnp.zeros_like(acc_ref)

    slot = j % 2
    a_ref[slot] = lax.dot_general(f_ref[...], u_ref[...], _NT, preferred_element_type=f32)
    tb = f_ref.shape[0]
    n_first = g_ref.shape[0] // tb
    shift = jnp.where(j < pl.num_programs(1) // 2, 16, 0).astype(jnp.uint32)
    parts = []
    for k in range(n_first):
        word = g_ref[pl.ds(k, tb, stride=n_first), :]
        gbits = jnp.bitwise_and(jnp.left_shift(word, shift), jnp.uint32(0xFFFF0000))
        gk = lax.bitcast_convert_type(gbits, f32)
        parts.append((_gelu_tanh(a_ref[slot, :, k * nkeys:(k + 1) * nkeys]) * gk).astype(bf16))
    ga = jnp.concatenate(parts, axis=1)
    acc_ref[...] += jnp.dot(ga, v_ref[...], preferred_element_type=f32)

    @pl.when(j == pl.num_programs(1) - 1)
    def _():
        y = x_ref[...] + m_ref[gate_row:gate_row + 1, :] * acc_ref[...]
        if final_norm:
            y = y * lax.rsqrt(jnp.mean(y * y, axis=-1, keepdims=True) + EPS) * fw_ref[...]
        o_ref[...] = y


def _peer_dense(f, u, v, gates, x, mods, gate_row, n_blocks, bps, n_batch, final_w=None):
    t, d = f.shape
    ne = v.shape[0]
    nkeys = gates.shape[3]
    te = SUBLANES * nkeys
    n_gblk = gates.shape[0]
    assert ne // te == 2 * n_gblk
    gates = gates.reshape(n_gblk, gates.shape[1] * SUBLANES, nkeys)
    return pl.pallas_call(
        functools.partial(_peer_dense_kernel, gate_row=gate_row, nkeys=nkeys, final_norm=final_w is not None),
        grid=(n_blocks, ne // te),
        in_specs=[pl.BlockSpec((ROW_BLOCK, d), lambda i, j: (i, 0)),
                  pl.BlockSpec((te, d), lambda i, j: (j, 0)),
                  pl.BlockSpec((te, d), lambda i, j: (j, 0)),
                  pl.BlockSpec((None, ROW_BLOCK * SUBLANES, nkeys), lambda i, j: (j % n_gblk, i, 0)),
                  pl.BlockSpec((ROW_BLOCK, d), lambda i, j: (i, 0)),
                  pl.BlockSpec((None, SUBLANES, d), lambda i, j: (_group_of(i, bps, n_batch), 0, 0)),
                  pl.BlockSpec((1, d), lambda i, j: (0, 0))],
        out_specs=pl.BlockSpec((ROW_BLOCK, d), lambda i, j: (i, 0)),
        out_shape=jax.ShapeDtypeStruct((t, d), f32),
        scratch_shapes=[pltpu.VMEM((ROW_BLOCK, d), f32), pltpu.VMEM((2, ROW_BLOCK, te), f32)],
        compiler_params=_cparams(2),
        name="peer_dense",
    )(f, u, v, gates, x, mods, (jnp.ones((d,), f32) if final_w is None else final_w).reshape(1, d))


def kernel(x, c, ctx, c_ctx, mod_w, mod_b, norm_w, final_norm_w, a_wqkv, a_q_gain, a_k_gain, a_wo,
           b_wqkv, b_rpb, b_wo, pool_w, pool_scale, peer_wq, peer_sub_keys, peer_u, peer_v):
    n_batch, s, d = x.shape
    c_len = ctx.shape[1]
    depth = mod_w.shape[0]
    assert s % ROW_BLOCK == 0 and (n_batch * c_len) % ROW_BLOCK == 0 and n_batch * c_len <= s
    assert s % c_len == 0 and c_len % POOL_ROW_BLOCK == 0 and s % GRID_W == 0
    assert a_q_gain.shape[1] == LANES and peer_sub_keys.shape[2] == LANES and peer_sub_keys.shape[3] == LANES
    assert PEER_TOPK & (PEER_TOPK - 1) == 0 and peer_sub_keys.shape[2] & (peer_sub_keys.shape[2] - 1) == 0
    assert depth % N_MIXERS != 0

    bps = s // ROW_BLOCK
    nb_lat = n_batch * bps
    nb_all = nb_lat + n_batch * c_len // ROW_BLOCK
    t_lat = n_batch * s
    q_dim = a_wo.shape[1]
    kv_dim = (a_wqkv.shape[2] - q_dim) // 2
    n_kv = kv_dim // LANES
    groups = q_dim // kv_dim
    b_heads = b_rpb.shape[1]
    win_r, win_c = (b_rpb.shape[2] + 1) // 2, (b_rpb.shape[3] + 1) // 2
    nkeys = peer_sub_keys.shape[2]
    scale = SM_SCALE_LOG2

    xs = (x.reshape(t_lat, d), ctx.reshape(n_batch * c_len, d))
    cvec = jnp.concatenate([c, c_ctx[None, :], jnp.zeros((SUBLANES - n_batch - 1, d), f32)], axis=0)
    mods_all = _mods(cvec, mod_w, mod_b)
    tables = _rope_tables(s)

    for i in range(depth):
        last = i == depth - 1
        kind, j = i % N_MIXERS, i // N_MIXERS
        nb = nb_lat if last else nb_all
        mods = mods_all[i, :n_batch + 1].reshape(n_batch + 1, N_MOD, d)
        mods = jnp.pad(mods, ((0, 0), (0, SUBLANES - N_MOD), (0, 0)))

        if kind == 2:
            h = _norm_mod(xs, norm_w[i, 0], mods, 0, 1, nb_all, bps, n_batch, f32)
            xs, f = _pool(h, xs, pool_w[j].astype(bf16), pool_scale[j], mods, norm_w[i, 1], 2, 3, 4, n_batch, s, c_len)
        else:
            h = _norm_mod(xs, norm_w[i, 0], mods, 0, 1, nb_all, bps, n_batch, bf16)
            if kind == 0:
                qkv = _qkv_gqa(h, a_wqkv[j].astype(bf16), a_q_gain[j], a_k_gain[j], tables, q_dim, kv_dim,
                               nb_all, bps, n_batch)
                o = _attn_lat(qkv, qkv, qkv, 0, q_dim, q_dim + kv_dim, groups, n_kv, n_batch, s, c_len, q_dim)
                if not last:
                    o_ctx = _attn_ctx(qkv, qkv, qkv, 0, q_dim, q_dim + kv_dim, groups, n_kv, n_batch, s, c_len, 1.0)
                wo = a_wo[j]
            else:
                qkv = _mm(h, b_wqkv[j].astype(bf16), nb_all, _col_tile(3 * d), bf16)
                rows = s // GRID_W
                bias = _natten_bias(b_rpb[j], rows, min(win_r, rows), win_r, win_c)
                o = _natten_lat(qkv, bias, b_heads, n_batch, s, c_len, d)
                if not last:
                    o_ctx = _attn_ctx(qkv, qkv, qkv, 0, d, 2 * d, 1, b_heads, n_batch, s, c_len, scale)
                wo = b_wo[j]
            if last:
                o_ctx = o
            xs, f = _mm_res(o, o_ctx, wo.astype(bf16), xs, mods, norm_w[i, 1], 2, 3, 4, nb, bps, n_batch)

        qp = _mm(f, peer_wq[i].astype(bf16), nb, _col_tile(peer_wq.shape[2]), f32)
        e, g, u_bf = _peer_topk(qp, peer_sub_keys[i], nb, peer_u, i)
        gates, v_bf = _peer_gates(e, g, nkeys, nb, peer_v, i)
        xs = _peer_dense(f, u_bf, v_bf, gates, xs, mods, 5, nb, bps, n_batch, final_norm_w if last else None)

    return xs.reshape(n_batch, s, d)
```
